```python
import math
import jax, jax.numpy as jnp
from jax import lax
import numpy as np

D_MODEL = 2048
BATCH = 4
SEQ = 2048
DEPTH = 1

CHUNK = 64
Q_BLOCK = 128
EPS = 1e-6

RET_HEADS = 8
RET_DK = 128
RET_DV = 128
RET_WIDTH = RET_HEADS * RET_DV
ROPE_BASE = 10000.0

DIFF_HEADS = 8
DIFF_DK = 64
DIFF_DV = 2 * DIFF_DK
DIFF_WIDTH = DIFF_HEADS * DIFF_DV

MIX_WIDTH = RET_WIDTH + DIFF_WIDTH

PROJ_SIZES = (RET_HEADS * RET_DK, RET_HEADS * RET_DK, RET_WIDTH, RET_WIDTH,
              DIFF_HEADS * 2 * DIFF_DK, DIFF_HEADS * 2 * DIFF_DK, DIFF_WIDTH)
PROJ_COLS = 7168

NUM_BUCKETS = 32
MAX_DISTANCE = 128

N_EXPERTS = 32
TOP_K = 4
D_EXPERT = D_MODEL
SWIGLU_LIMIT = 7.0
SWIGLU_ALPHA = 1.702

kernel_name = 'hymba_retnet_diffattn_gptoss_moe_block'


def rms_norm(x, g):
    xf = x.astype(jnp.float32)
    y = xf * lax.rsqrt(jnp.mean(xf * xf, axis=-1, keepdims=True) + EPS)
    return (y * g.astype(jnp.float32)).astype(x.dtype)


def split_points():
    pts, acc = [], 0
    for s in PROJ_SIZES[:-1]:
        acc += s
        pts.append(acc)
    return pts


def rotary(x, pos):
    d = x.shape[-1]
    inv_freq = ROPE_BASE ** (-jnp.arange(0, d, 2, dtype=jnp.float32) / d)
    ang = pos.astype(jnp.float32)[:, None] * inv_freq[None, :]
    cos = jnp.cos(ang)[None, :, None, :]
    sin = jnp.sin(ang)[None, :, None, :]
    xf = x.astype(jnp.float32)
    x1, x2 = xf[..., 0::2], xf[..., 1::2]
    out = jnp.stack([x1 * cos - x2 * sin, x2 * cos + x1 * sin], axis=-1)
    return out.reshape(x.shape).astype(x.dtype)


def retention(q, k, v):
    B, S, H, Dk = q.shape
    Dv = v.shape[-1]
    nc = S // CHUNK
    log_gamma = jnp.log1p(-(2.0 ** (-5.0 - jnp.arange(H, dtype=jnp.float32))))
    qc = q.reshape(B, nc, CHUNK, H, Dk).astype(jnp.float32) * (Dk ** -0.5)
    kc = k.reshape(B, nc, CHUNK, H, Dk).astype(jnp.float32)
    vc = v.reshape(B, nc, CHUNK, H, Dv).astype(jnp.float32)
    idx = jnp.arange(CHUNK, dtype=jnp.float32)
    d_intra = jnp.exp(log_gamma[:, None, None] * jnp.abs(idx[:, None] - idx[None, :]))
    s = jnp.einsum('bcnhd,bcmhd->bchnm', qc, kc) * d_intra
    intra = jnp.einsum('bchnm,bcmhe->bcnhe', s, vc)
    k_decay = jnp.exp(log_gamma[:, None] * (CHUNK - 1 - idx)[None, :])
    kv = jnp.einsum('bcmhd,hm,bcmhe->bchde', kc, k_decay, vc)
    chunk_decay = jnp.exp(log_gamma * CHUNK)[None, :, None, None]

    def step(state, kv_c):
        return state * chunk_decay + kv_c, state

    init = jnp.zeros((B, H, Dk, Dv), jnp.float32)
    _, prev = lax.scan(step, init, jnp.moveaxis(kv, 1, 0))
    prev = jnp.moveaxis(prev, 0, 1)
    q_decay = jnp.exp(log_gamma[:, None] * (idx + 1.0)[None, :])
    cross = jnp.einsum('bcnhd,hn,bchde->bcnhe', qc, q_decay, prev)
    return (intra + cross).reshape(B, S, H, Dv)


def t5_bucket(rel):
    nb = NUM_BUCKETS // 2
    max_exact = nb // 2
    base = jnp.where(rel > 0, nb, 0)
    n = jnp.abs(rel)
    large = max_exact + (jnp.log(jnp.maximum(n, 1).astype(jnp.float32) / max_exact)
                         / math.log(MAX_DISTANCE / max_exact) * (nb - max_exact)).astype(jnp.int32)
    large = jnp.minimum(large, nb - 1)
    return base + jnp.where(n < max_exact, n, large)


def diff_attention(q1, q2, k1, k2, v, rel_table, lam):
    S = q1.shape[1]
    scale = DIFF_DK ** -0.5
    pos = jnp.arange(S, dtype=jnp.int32)
    vf = v.astype(jnp.float32)
    outs = []
    for qb in range(S // Q_BLOCK):
        q0, end = qb * Q_BLOCK, (qb + 1) * Q_BLOCK
        qpos, kpos = pos[q0:end], pos[:end]
        bias = jnp.transpose(rel_table[t5_bucket(kpos[None, :] - qpos[:, None])], (2, 0, 1))
        bias = bias.astype(jnp.float32)
        mask = (kpos[None, :] // CHUNK) <= (qpos[:, None] // CHUNK)

        def probs(q, k):
            s = jnp.einsum('bqhd,bkhd->bhqk', q[:, q0:end], k[:, :end]).astype(jnp.float32)
            s = jnp.where(mask, s * scale + bias, -jnp.inf)
            return jax.nn.softmax(s, axis=-1)

        a = probs(q1, k1) - lam * probs(q2, k2)
        outs.append(jnp.einsum('bhqk,bkhe->bqhe', a, vf[:, :end]))
    return jnp.concatenate(outs, axis=1)


def moe(h, w_router, b_router, w_in, b_in, w_out, b_out):
    B, S, D = h.shape
    x2 = h.reshape(B * S, D)
    logits = (x2 @ w_router + b_router).astype(jnp.float32)
    top_val, top_idx = lax.top_k(logits, TOP_K)
    gate = jax.nn.softmax(top_val, axis=-1)
    combine = jnp.sum(jax.nn.one_hot(top_idx, N_EXPERTS, dtype=jnp.float32) * gate[..., None], axis=1)
    out = jnp.zeros((B * S, D), jnp.float32)
    for e in range(N_EXPERTS):
        hh = x2 @ w_in[e] + b_in[e]
        g, u = hh[:, :D_EXPERT], hh[:, D_EXPERT:]
        g = jnp.minimum(g, SWIGLU_LIMIT)
        u = jnp.clip(u, -SWIGLU_LIMIT, SWIGLU_LIMIT)
        act = (u + 1.0) * (g * jax.nn.sigmoid(SWIGLU_ALPHA * g))
        out = out + combine[:, e:e + 1] * (act @ w_out[e] + b_out[e]).astype(jnp.float32)
    return out.reshape(B, S, D).astype(h.dtype)


def setup_inputs(seed: int = 0) -> dict:
    key = jax.random.key(seed)
    ks = jax.random.split(key, 24)
    f32 = jnp.float32

    def nrm(k, shape, scale):
        return jax.random.normal(k, shape, f32) * scale

    def gain(k, shape):
        return 1.0 + 0.01 * jax.random.normal(k, shape, f32)

    return {
        'x': nrm(ks[0], (BATCH, SEQ, D_MODEL), 1.0),
        'norm_mix_g': gain(ks[1], (DEPTH, D_MODEL)),
        'w_mix_in': nrm(ks[2], (DEPTH, D_MODEL, PROJ_COLS), D_MODEL ** -0.5),
        'ret_gn_g': gain(ks[3], (DEPTH, RET_HEADS, RET_DV)),
        'q_norm_g': gain(ks[4], (DEPTH, DIFF_DK)),
        'k_norm_g': gain(ks[5], (DEPTH, DIFF_DK)),
        'lambda_q1': nrm(ks[6], (DEPTH, DIFF_DK), 0.1),
        'lambda_k1': nrm(ks[7], (DEPTH, DIFF_DK), 0.1),
        'lambda_q2': nrm(ks[8], (DEPTH, DIFF_DK), 0.1),
        'lambda_k2': nrm(ks[9], (DEPTH, DIFF_DK), 0.1),
        'diff_subln_g': gain(ks[10], (DEPTH, DIFF_DV)),
        'rel_bias_table': nrm(ks[11], (NUM_BUCKETS, DIFF_HEADS), 0.5),
        'w_mix_out': nrm(ks[12], (DEPTH, MIX_WIDTH, D_MODEL), MIX_WIDTH ** -0.5),
        'norm_ffn_g': gain(ks[13], (DEPTH, D_MODEL)),
        'w_router': nrm(ks[14], (DEPTH, D_MODEL, N_EXPERTS), D_MODEL ** -0.5),
        'b_router': nrm(ks[15], (DEPTH, N_EXPERTS), 0.01),
        'w_exp_in': nrm(ks[16], (DEPTH, N_EXPERTS, D_MODEL, 2 * D_EXPERT), D_MODEL ** -0.5),
        'b_exp_in': nrm(ks[17], (DEPTH, N_EXPERTS, 2 * D_EXPERT), 0.02),
        'w_exp_out': nrm(ks[18], (DEPTH, N_EXPERTS, D_EXPERT, D_MODEL), D_EXPERT ** -0.5),
        'b_exp_out': nrm(ks[19], (DEPTH, N_EXPERTS, D_MODEL), 0.02),
    }


def reference(x, norm_mix_g, w_mix_in, ret_gn_g, q_norm_g, k_norm_g, lambda_q1, lambda_k1,
              lambda_q2, lambda_k2, diff_subln_g, rel_bias_table, w_mix_out, norm_ffn_g,
              w_router, b_router, w_exp_in, b_exp_in, w_exp_out, b_exp_out):
    B, S, _ = x.shape
    pos = jnp.arange(S, dtype=jnp.int32)
    for l in range(DEPTH):
        lam_init = 0.8 - 0.6 * math.exp(-0.3 * l)
        h = rms_norm(x, norm_mix_g[l])
        proj = h @ w_mix_in[l]
        rq, rk, rv, rg, dq, dk, dv = jnp.split(proj, split_points(), axis=-1)

        rq = rotary(rq.reshape(B, S, RET_HEADS, RET_DK), pos)
        rk = rotary(rk.reshape(B, S, RET_HEADS, RET_DK), pos)
        ret = retention(rq, rk, rv.reshape(B, S, RET_HEADS, RET_DV)).astype(h.dtype)
        ret = rms_norm(ret, ret_gn_g[l]).reshape(B, S, RET_WIDTH)
        y_ret = jax.nn.silu(rg) * ret

        dq = rms_norm(dq.reshape(B, S, DIFF_HEADS, 2, DIFF_DK), q_norm_g[l])
        dk = rms_norm(dk.reshape(B, S, DIFF_HEADS, 2, DIFF_DK), k_norm_g[l])
        lam = (jnp.exp(jnp.sum(lambda_q1[l] * lambda_k1[l]).astype(jnp.float32))
               - jnp.exp(jnp.sum(lambda_q2[l] * lambda_k2[l]).astype(jnp.float32)) + lam_init)
        att = diff_attention(dq[..., 0, :], dq[..., 1, :], dk[..., 0, :], dk[..., 1, :],
                             dv.reshape(B, S, DIFF_HEADS, DIFF_DV), rel_bias_table, lam)
        att = rms_norm(att.astype(h.dtype), diff_subln_g[l]) * (1.0 - lam_init)
        y_diff = att.reshape(B, S, DIFF_WIDTH)

        x = x + jnp.concatenate([y_ret, y_diff], axis=-1) @ w_mix_out[l]

        h2 = rms_norm(x, norm_ffn_g[l])
        x = x + moe(h2, w_router[l], b_router[l], w_exp_in[l], b_exp_in[l],
                    w_exp_out[l], b_exp_out[l])
    return x
```

```python
import functools
import math

import jax
import jax.numpy as jnp
from jax import lax
from jax.experimental import pallas as pl
from jax.experimental.pallas import tpu as pltpu

F32 = jnp.float32
BF16 = jnp.bfloat16
I32 = jnp.int32
U32 = jnp.uint32

EPS = 1e-6
CHUNK = 64
RET_HEADS = 8
DIFF_HEADS = 8
HEAD_W = 128
DIFF_DK = 64
ROPE_BASE = 10000.0
NUM_BUCKETS = 32
MAX_DISTANCE = 128
TOP_K = 4
SWIGLU_LIMIT = 7.0
SWIGLU_ALPHA = 1.702
LAM_INIT = 0.8 - 0.6 * math.exp(-0.3 * 0)

LANES = 128
NEG_BIG = -1e30
VMEM_LIMIT = 56 * 1024 * 1024

SEQ_BLK = 256
ROW_TILE = 256
GROUP_TILES = 8
F_CHUNK = 256


def _cparams(sem, vmem=VMEM_LIMIT):
    return pltpu.CompilerParams(dimension_semantics=sem, vmem_limit_bytes=vmem)


def _rmsnorm_kernel(x_ref, g_ref, o_ref):
    x = x_ref[...]
    ms = jnp.mean(x * x, axis=-1, keepdims=True)
    o_ref[...] = (x * lax.rsqrt(ms + EPS) * g_ref[...]).astype(o_ref.dtype)


def _rmsnorm_call(x2, g, tm=512):
    t, d = x2.shape
    return pl.pallas_call(
        _rmsnorm_kernel,
        grid=(t // tm,),
        in_specs=[pl.BlockSpec((tm, d), lambda i: (i, 0)),
                  pl.BlockSpec((1, d), lambda i: (0, 0))],
        out_specs=pl.BlockSpec((tm, d), lambda i: (i, 0)),
        out_shape=jax.ShapeDtypeStruct((t, d), BF16),
        compiler_params=_cparams(("arbitrary",)),
        name="rmsnorm_in",
    )(x2, g.reshape(1, d))


def _inproj_kernel(h_ref, w_ref, o_ref, wb_ref, *, rows_per_cast):
    @pl.when(pl.program_id(1) == 0)
    def _():
        d = w_ref.shape[0]
        for c in range(d // rows_per_cast):
            sl = slice(c * rows_per_cast, (c + 1) * rows_per_cast)
            wb_ref[sl, :] = w_ref[sl, :].astype(BF16)

    acc = jnp.dot(h_ref[...], wb_ref[...], preferred_element_type=F32)
    for j in range(o_ref.shape[0]):
        o_ref[j] = acc[:, j * LANES:(j + 1) * LANES].astype(o_ref.dtype)


def _inproj_call(h, w, tm=512, tn=1024):
    t, d = h.shape
    n = w.shape[1]
    return pl.pallas_call(
        functools.partial(_inproj_kernel, rows_per_cast=min(256, d)),
        grid=(n // tn, t // tm),
        in_specs=[pl.BlockSpec((tm, d), lambda j, i: (i, 0)),
                  pl.BlockSpec((d, tn), lambda j, i: (0, j))],
        out_specs=pl.BlockSpec((tn // LANES, tm, LANES), lambda j, i: (j, i, 0)),
        out_shape=jax.ShapeDtypeStruct((n // LANES, t, LANES), BF16),
        scratch_shapes=[pltpu.VMEM((d, tn), BF16)],
        compiler_params=_cparams(("arbitrary", "arbitrary")),
        name="in_proj",
    )(h, w)


def _ret_kernel(lg_ref, q_ref, k_ref, v_ref, g_ref, cos_ref, sin_ref, gn_ref, o_ref, *, blk, nblk):
    dk = q_ref.shape[-1]
    lg = lg_ref[pl.program_id(1)]
    row = lax.broadcasted_iota(I32, (blk, blk), 0)
    col = lax.broadcasted_iota(I32, (blk, blk), 1)
    dist = jnp.abs(row - col).astype(F32)
    visible = (col // CHUNK) <= (row // CHUNK)
    dmask = jnp.where(visible, jnp.exp(lg * dist), 0.0)
    rr = lax.broadcasted_iota(I32, (blk, dk), 0).astype(F32)
    qdec = jnp.exp(lg * (rr + 1.0))
    kdec = jnp.exp(lg * (blk - 1.0 - rr))
    bdec = jnp.exp(lg * jnp.full((1, HEAD_W), float(blk), F32))
    even = (lax.broadcasted_iota(I32, (blk, dk), 1) & 1) == 0
    scale = dk ** -0.5

    def body(i, state):
        rows = pl.ds(pl.multiple_of(i * blk, blk), blk)
        cos = cos_ref[rows, :]
        sin = sin_ref[rows, :]

        def rot(x):
            partner = jnp.where(even, pltpu.roll(x, dk - 1, 1), pltpu.roll(x, 1, 1))
            return x * cos + partner * sin

        qr = rot(q_ref[rows, :].astype(F32)) * scale
        kr = rot(k_ref[rows, :].astype(F32))
        v = v_ref[rows, :]
        s = lax.dot_general(qr.astype(BF16), kr.astype(BF16), (((1,), (1,)), ((), ())),
                            preferred_element_type=F32) * dmask
        out = jnp.dot(s.astype(BF16), v, preferred_element_type=F32)
        out = out + jnp.dot((qr * qdec).astype(BF16), state.astype(BF16),
                            preferred_element_type=F32)
        kv = lax.dot_general((kr * kdec).astype(BF16), v, (((0,), (0,)), ((), ())),
                             preferred_element_type=F32)
        state = state * bdec + kv
        ms = jnp.mean(out * out, axis=-1, keepdims=True)
        normed = out * lax.rsqrt(ms + EPS) * gn_ref[...]
        g = g_ref[rows, :].astype(F32)
        o_ref[rows, :] = (g * jax.nn.sigmoid(g) * normed).astype(o_ref.dtype)
        return state

    lax.fori_loop(0, nblk, body, jnp.zeros((dk, HEAD_W), F32))


def _ret_call(proj, cos, sin, log_gamma, gn_g, batch, seq, blk=SEQ_BLK):
    nh = RET_HEADS

    def head_spec(base):
        return pl.BlockSpec((None, seq, LANES), lambda b, h: (base + h, b, 0))

    return pl.pallas_call(
        functools.partial(_ret_kernel, blk=blk, nblk=seq // blk),
        grid=(batch, nh),
        in_specs=[pl.BlockSpec(memory_space=pltpu.SMEM),
                  head_spec(0), head_spec(nh), head_spec(2 * nh), head_spec(3 * nh),
                  pl.BlockSpec((seq, LANES), lambda b, h: (0, 0)),
                  pl.BlockSpec((seq, LANES), lambda b, h: (0, 0)),
                  pl.BlockSpec((None, 1, LANES), lambda b, h: (h, 0, 0))],
        out_specs=pl.BlockSpec((None, seq, LANES), lambda b, h: (h, b, 0)),
        out_shape=jax.ShapeDtypeStruct((nh, batch * seq, LANES), BF16),
        compiler_params=_cparams(("arbitrary", "arbitrary")),
        name="retention",
    )(log_gamma, proj, proj, proj, proj, cos, sin, gn_g.reshape(nh, 1, LANES))


def _diff_kernel(tbl_ref, q_ref, k_ref, v_ref, qg_ref, kg_ref, lam_ref, sg_ref, bidx_ref, o_ref,
                 qz_s, kn_s, bias_s, *, blk, nblk):
    h = pl.program_id(0)
    b = pl.program_id(1)
    far_bucket = NUM_BUCKETS // 2 - 1

    @pl.when(b == 0)
    def _build_bias():
        row = lax.broadcasted_iota(I32, (blk, blk), 0)
        col = lax.broadcasted_iota(I32, (blk, blk), 1)
        visible = (col // CHUNK) <= (row // CHUNK)
        for d in range(2):
            idx = bidx_ref[d]
            bias = jnp.zeros((blk, blk), F32)
            for bucket in range(NUM_BUCKETS):
                bias = jnp.where(idx == bucket, tbl_ref[bucket, h], bias)
            if d == 0:
                bias = jnp.where(visible, bias, NEG_BIG)
            bias_s[d, 0:blk, :] = bias
            bias_s[d, blk:2 * blk, :] = bias

    lo = lax.broadcasted_iota(I32, (blk, HEAD_W), 1) < DIFF_DK
    scale = DIFF_DK ** -0.5

    def half_norm(x, g):
        x2 = x * x
        s_lo = jnp.sum(jnp.where(lo, x2, 0.0), axis=-1, keepdims=True)
        s_hi = jnp.sum(jnp.where(lo, 0.0, x2), axis=-1, keepdims=True)
        ms = jnp.where(lo, s_lo, s_hi) * (1.0 / DIFF_DK)
        return x * lax.rsqrt(ms + EPS) * g

    def prep(i, carry):
        rows = pl.ds(pl.multiple_of(i * blk, blk), blk)
        qn = half_norm(q_ref[rows, :].astype(F32), qg_ref[...]) * scale
        kn = half_norm(k_ref[rows, :].astype(F32), kg_ref[...])
        base = pl.multiple_of(i * 2 * blk, 2 * blk)
        qz_s[pl.ds(base, blk), :] = jnp.where(lo, qn, 0.0).astype(BF16)
        qz_s[pl.ds(base + blk, blk), :] = jnp.where(lo, 0.0, qn).astype(BF16)
        kn_s[rows, :] = kn.astype(BF16)
        return carry

    lax.fori_loop(0, nblk, prep, 0)

    lam = (jnp.exp(jnp.sum(lam_ref[0:1, :] * lam_ref[1:2, :], axis=-1, keepdims=True))
           - jnp.exp(jnp.sum(lam_ref[2:3, :] * lam_ref[3:4, :], axis=-1, keepdims=True))
           + LAM_INIT)
    c_far = tbl_ref[far_bucket, h]

    def qblock(i, carry):
        qz = qz_s[pl.ds(pl.multiple_of(i * 2 * blk, 2 * blk), 2 * blk), :]

        def step(j, st, bias):
            m, l, acc = st
            keys = pl.ds(pl.multiple_of(j * blk, blk), blk)
            s = lax.dot_general(qz, kn_s[keys, :], (((1,), (1,)), ((), ())),
                                preferred_element_type=F32) + bias
            m_new = jnp.maximum(m, jnp.max(s, axis=-1, keepdims=True))
            alpha = jnp.exp(m - m_new)
            p = jnp.exp(s - m_new)
            l = alpha * l + jnp.sum(p, axis=-1, keepdims=True)
            acc = alpha * acc + jnp.dot(p.astype(BF16), v_ref[keys, :],
                                        preferred_element_type=F32)
            return m_new, l, acc

        st = (jnp.full((2 * blk, 1), NEG_BIG, F32), jnp.zeros((2 * blk, 1), F32),
              jnp.zeros((2 * blk, HEAD_W), F32))
        st = lax.fori_loop(0, i - 1, lambda j, c: step(j, c, c_far), st)
        st = lax.cond(i >= 1, lambda c: step(i - 1, c, bias_s[1]), lambda c: c, st)
        _, l, acc = step(i, st, bias_s[0])
        o = acc / l
        att = o[0:blk, :] - lam * o[blk:2 * blk, :]
        ms = jnp.mean(att * att, axis=-1, keepdims=True)
        y = att * lax.rsqrt(ms + EPS) * sg_ref[...] * (1.0 - LAM_INIT)
        o_ref[pl.ds(pl.multiple_of(i * blk, blk), blk), :] = y.astype(o_ref.dtype)
        return carry

    lax.fori_loop(0, nblk, qblock, 0)


def _t5_bucket(rel):
    nb = NUM_BUCKETS // 2
    max_exact = nb // 2
    base = jnp.where(rel > 0, nb, 0)
    n = jnp.abs(rel)
    large = max_exact + (jnp.log(jnp.maximum(n, 1).astype(jnp.float32) / max_exact)
                         / math.log(MAX_DISTANCE / max_exact) * (nb - max_exact)).astype(jnp.int32)
    large = jnp.minimum(large, nb - 1)
    return base + jnp.where(n < max_exact, n, large)


def _diff_call(proj, rel_table, qg, kg, lam_vecs, sg, batch, seq, blk=SEQ_BLK):
    nh = DIFF_HEADS
    first = 4 * RET_HEADS
    r = jnp.arange(blk, dtype=I32)
    rel0 = r[None, :] - r[:, None]
    bidx = jnp.stack([_t5_bucket(rel0), _t5_bucket(rel0 - blk)]).astype(I32)

    def head_spec(base):
        return pl.BlockSpec((None, seq, LANES), lambda h, b: (base + h, b, 0))

    def vec_spec():
        return pl.BlockSpec((1, LANES), lambda h, b: (0, 0))

    return pl.pallas_call(
        functools.partial(_diff_kernel, blk=blk, nblk=seq // blk),
        grid=(nh, batch),
        in_specs=[pl.BlockSpec(memory_space=pltpu.SMEM),
                  head_spec(first), head_spec(first + nh), head_spec(first + 2 * nh),
                  vec_spec(), vec_spec(),
                  pl.BlockSpec((4, DIFF_DK), lambda h, b: (0, 0)),
                  vec_spec(),
                  pl.BlockSpec((2, blk, blk), lambda h, b: (0, 0, 0))],
        out_specs=pl.BlockSpec((None, seq, LANES), lambda h, b: (h, b, 0)),
        out_shape=jax.ShapeDtypeStruct((nh, batch * seq, LANES), BF16),
        scratch_shapes=[pltpu.VMEM((2 * seq, LANES), BF16),
                        pltpu.VMEM((seq, LANES), BF16),
                        pltpu.VMEM((2, 2 * blk, blk), F32)],
        compiler_params=_cparams(("arbitrary", "arbitrary")),
        name="diff_attn",
    )(rel_table, proj, proj, proj,
      jnp.tile(qg, 2).reshape(1, LANES), jnp.tile(kg, 2).reshape(1, LANES),
      lam_vecs, sg.reshape(1, LANES), bidx)


def _outproj_kernel(yr_ref, yd_ref, x_ref, w_ref, g_ref, wr_ref, br_ref,
                    x1_ref, h2p_ref, lt_ref, wb_ref, *, rows_per_cast):
    @pl.when(pl.program_id(0) == 0)
    def _():
        for c in range(w_ref.shape[0] // rows_per_cast):
            sl = slice(c * rows_per_cast, (c + 1) * rows_per_cast)
            wb_ref[sl, :] = w_ref[sl, :].astype(BF16)

    y = jnp.concatenate([yr_ref[j] for j in range(yr_ref.shape[0])]
                        + [yd_ref[j] for j in range(yd_ref.shape[0])], axis=-1)
    x1 = x_ref[...] + jnp.dot(y, wb_ref[...], preferred_element_type=F32)
    x1_ref[...] = x1
    ms = jnp.mean(x1 * x1, axis=-1, keepdims=True)
    h2 = x1 * lax.rsqrt(ms + EPS) * g_ref[...]
    lt_ref[...] = lax.dot_general(wr_ref[...], h2, (((1,), (1,)), ((), ())),
                                  precision=lax.Precision.HIGHEST,
                                  preferred_element_type=F32) + br_ref[...]
    half = h2.shape[1] // 2
    h2p_ref[...] = pltpu.pack_elementwise([h2[:, :half], h2[:, half:]], packed_dtype=BF16)


def _outproj_call(y_ret, y_diff, x2, w_out, g, w_router, b_router, tm=256):
    t, d = x2.shape
    ne = w_router.shape[1]
    nhr, nhd = y_ret.shape[0], y_diff.shape[0]
    return pl.pallas_call(
        functools.partial(_outproj_kernel, rows_per_cast=min(256, d)),
        grid=(t // tm,),
        in_specs=[pl.BlockSpec((nhr, tm, LANES), lambda i: (0, i, 0)),
                  pl.BlockSpec((nhd, tm, LANES), lambda i: (0, i, 0)),
                  pl.BlockSpec((tm, d), lambda i: (i, 0)),
                  pl.BlockSpec((d, d), lambda i: (0, 0), pipeline_mode=pl.Buffered(1)),
                  pl.BlockSpec((1, d), lambda i: (0, 0)),
                  pl.BlockSpec((ne, d), lambda i: (0, 0)),
                  pl.BlockSpec((ne, 1), lambda i: (0, 0))],
        out_specs=[pl.BlockSpec((tm, d), lambda i: (i, 0)),
                   pl.BlockSpec((tm, d // 2), lambda i: (i, 0)),
                   pl.BlockSpec((ne, tm), lambda i: (0, i))],
        out_shape=[jax.ShapeDtypeStruct((t, d), F32),
                   jax.ShapeDtypeStruct((t, d // 2), U32),
                   jax.ShapeDtypeStruct((ne, t), F32)],
        scratch_shapes=[pltpu.VMEM((d, d), BF16)],
        compiler_params=_cparams(("arbitrary",)),
        name="out_proj_router",
    )(y_ret, y_diff, x2, w_out, g.reshape(1, d), w_router.T, b_router.reshape(ne, 1))


def _route_kernel(lt_ref, pos_ref, gate_ref, cnt_ref, nt_ref, ot_ref, idx_s, rank_s, *, tb, row_tile):
    ne, t = lt_ref.shape
    e_iota = lax.broadcasted_iota(I32, (ne, tb), 0)
    upper = (lax.broadcasted_iota(I32, (tb, tb), 0)
             < lax.broadcasted_iota(I32, (tb, tb), 1)).astype(BF16)

    def pass_a(i, running):
        cols = pl.ds(pl.multiple_of(i * tb, tb), tb)
        l = lt_ref[:, cols]
        tops, hots = [], []
        for k in range(TOP_K):
            m = jnp.max(l, axis=0, keepdims=True)
            idx = jnp.min(jnp.where(l == m, e_iota, ne), axis=0, keepdims=True)
            hot = e_iota == idx
            l = jnp.where(hot, -jnp.inf, l)
            idx_s[k:k + 1, cols] = idx
            tops.append(m)
            hots.append(hot)
        exps = [jnp.exp(m - tops[0]) for m in tops]
        denom = exps[0] + exps[1] + exps[2] + exps[3]
        for k in range(TOP_K):
            gate_ref[k:k + 1, cols] = exps[k] / denom
        hot_all = jnp.zeros((ne, tb), F32)
        for hot in hots:
            hot_all = hot_all + hot.astype(F32)
        before = running + jnp.dot(hot_all.astype(BF16), upper, preferred_element_type=F32)
        for k in range(TOP_K):
            rank_s[k:k + 1, cols] = jnp.sum(jnp.where(hots[k], before, 0.0), axis=0, keepdims=True)
        return running + jnp.sum(hot_all, axis=1, keepdims=True)

    cnt = lax.fori_loop(0, t // tb, pass_a, jnp.zeros((ne, 1), F32))
    ntile = jnp.floor((cnt + (row_tile - 1.0)) * (1.0 / row_tile))
    lower = (lax.broadcasted_iota(I32, (ne, ne), 1)
             < lax.broadcasted_iota(I32, (ne, ne), 0)).astype(BF16)
    otile = jnp.dot(lower, jnp.broadcast_to(ntile, (ne, LANES)).astype(BF16),
                    preferred_element_type=F32)
    cnt_ref[...] = jnp.broadcast_to(cnt, (ne, LANES)).astype(I32)
    nt_ref[...] = jnp.broadcast_to(ntile, (ne, LANES)).astype(I32)
    ot_ref[...] = otile.astype(I32)
    off_rows = otile[:, 0:1] * float(row_tile)

    def pass_b(i, carry):
        cols = pl.ds(pl.multiple_of(i * tb, tb), tb)
        for k in range(TOP_K):
            hot = e_iota == idx_s[k:k + 1, cols]
            off = jnp.sum(jnp.where(hot, off_rows, 0.0), axis=0, keepdims=True)
            pos_ref[k:k + 1, cols] = (rank_s[k:k + 1, cols] + off).astype(I32)
        return carry

    lax.fori_loop(0, t // tb, pass_b, 0)


def _route_call(logits_t, tb=256, row_tile=ROW_TILE):
    ne, t = logits_t.shape
    return pl.pallas_call(
        functools.partial(_route_kernel, tb=tb, row_tile=row_tile),
        out_shape=[jax.ShapeDtypeStruct((TOP_K, t), I32),
                   jax.ShapeDtypeStruct((TOP_K, t), F32),
                   jax.ShapeDtypeStruct((ne, LANES), I32),
                   jax.ShapeDtypeStruct((ne, LANES), I32),
                   jax.ShapeDtypeStruct((ne, LANES), I32)],
        scratch_shapes=[pltpu.VMEM((TOP_K, t), I32), pltpu.VMEM((TOP_K, t), F32)],
        compiler_params=_cparams(None),
        name="route",
    )(logits_t)


_PAD_PIECES = (128, 64, 32, 16, 8)
_SUBLANES = 8


def _dispatch_kernel(pos_ref, pad_start_ref, pad_n_ref, src_ref, dst_ref, zero_s, sem, zsem,
                     *, tb, ne):
    i = pl.program_id(0)
    nsteps = pl.num_programs(0)

    def pad_copies(e):
        n = pad_n_ref[e]
        head = n & (_SUBLANES - 1)
        start = pad_start_ref[e]
        out = []
        for r in range(_SUBLANES - 1):
            out.append((r < head, pltpu.make_async_copy(
                zero_s.at[pl.ds(0, 1)], dst_ref.at[pl.ds(start + r, 1)], zsem)))
        body = n - head
        for piece in _PAD_PIECES:
            at = pl.multiple_of(start + head + (body & ~(2 * piece - 1)), _SUBLANES)
            out.append(((body & piece) != 0, pltpu.make_async_copy(
                zero_s.at[pl.ds(0, piece)], dst_ref.at[pl.ds(at, piece)], zsem)))
        return out

    @pl.when(i == 0)
    def _zero_pads():
        zero_s[...] = jnp.zeros(zero_s.shape, zero_s.dtype)

        def issue(e, c):
            for cond, cp in pad_copies(e):
                @pl.when(cond)
                def _():
                    cp.start()
            return c

        def drain(e, c):
            for cond, cp in pad_copies(e):
                @pl.when(cond)
                def _():
                    cp.wait()
            return c

        lax.fori_loop(0, ne, issue, 0)
        lax.fori_loop(0, ne, drain, 0)

    slot = i % 2

    def issue_rows(tl, c):
        tok = i * tb + tl
        for k in range(TOP_K):
            pltpu.make_async_copy(src_ref.at[pl.ds(tok, 1)], dst_ref.at[pl.ds(pos_ref[k, tl], 1)],
                                  sem.at[slot]).start()
        return c

    lax.fori_loop(0, tb, issue_rows, 0, unroll=4)

    def wait_step(s):
        pltpu.make_async_copy(src_ref.at[pl.ds(0, TOP_K * tb)], dst_ref.at[pl.ds(0, TOP_K * tb)],
                              sem.at[s]).wait()

    @pl.when(i > 0)
    def _():
        wait_step(1 - slot)

    @pl.when(i == nsteps - 1)
    def _():
        wait_step(slot)


def _dispatch_call(pos, pad_start, pad_n, h2p, n_slots, tb=512):
    t, w = h2p.shape
    ne = pad_n.shape[0]
    return pl.pallas_call(
        functools.partial(_dispatch_kernel, tb=tb, ne=ne),
        grid=(t // tb,),
        in_specs=[pl.BlockSpec((TOP_K, tb), lambda i: (0, i), memory_space=pltpu.SMEM),
                  pl.BlockSpec(memory_space=pltpu.SMEM),
                  pl.BlockSpec(memory_space=pltpu.SMEM),
                  pl.BlockSpec(memory_space=pl.ANY)],
        out_specs=pl.BlockSpec(memory_space=pl.ANY),
        out_shape=jax.ShapeDtypeStruct((n_slots, w), U32),
        scratch_shapes=[pltpu.VMEM((_PAD_PIECES[0], w), U32),
                        pltpu.SemaphoreType.DMA((2,)),
                        pltpu.SemaphoreType.DMA(())],
        compiler_params=pltpu.CompilerParams(dimension_semantics=("arbitrary",),
                                             vmem_limit_bytes=VMEM_LIMIT, has_side_effects=True),
        name="dispatch",
    )(pos, pad_start, pad_n, h2p)


def _moe_kernel(we_ref, wt_ref, wn_ref, xs_ref, wg_ref, wu_ref, bg_ref, bu_ref, wo_ref, bo_ref,
                ys_ref, xu_s, acc_s, wgu_s, wo_s, yst_s, xsem, ysem, *, tmx, fc, nj):
    w = pl.program_id(0)
    j = pl.program_id(1)
    nt = wn_ref[w]
    t0 = wt_ref[w]
    half = xu_s.shape[-1]

    def tile_rows(r):
        return pl.ds(pl.multiple_of((t0 + r) * tmx, tmx), tmx)

    def x_copy(r):
        return pltpu.make_async_copy(xs_ref.at[tile_rows(r)], xu_s.at[r], xsem)

    def y_copy(r, slot):
        return pltpu.make_async_copy(yst_s.at[slot], ys_ref.at[tile_rows(r)], ysem.at[slot])

    @pl.when(nt > 0)
    def _work():
        @pl.when(j == 0)
        def _load_rows():
            def start(r, c):
                x_copy(r).start()
                return c

            def finish(r, c):
                x_copy(r).wait()
                acc_s[r] = jnp.zeros(acc_s.shape[1:], F32)
                return c

            lax.fori_loop(0, nt, start, 0)
            lax.fori_loop(0, nt, finish, 0)

        wgu_s[:, 0:fc] = wg_ref[...].astype(BF16)
        wgu_s[:, fc:2 * fc] = wu_ref[...].astype(BF16)
        wo_s[...] = wo_ref[...].astype(BF16)
        bias_gu = jnp.concatenate([bg_ref[...], bu_ref[...]], axis=-1)

        def tile(r, c):
            xw = xu_s[r]
            x_lo = pltpu.unpack_elementwise(xw, index=0, packed_dtype=BF16,
                                            unpacked_dtype=F32).astype(BF16)
            x_hi = pltpu.unpack_elementwise(xw, index=1, packed_dtype=BF16,
                                            unpacked_dtype=F32).astype(BF16)
            hh = (jnp.dot(x_lo, wgu_s[0:half, :], preferred_element_type=F32)
                  + jnp.dot(x_hi, wgu_s[half:2 * half, :], preferred_element_type=F32) + bias_gu)
            gg = jnp.minimum(hh[:, 0:fc], SWIGLU_LIMIT)
            uu = jnp.clip(hh[:, fc:2 * fc], -SWIGLU_LIMIT, SWIGLU_LIMIT)
            act = (uu + 1.0) * (gg * jax.nn.sigmoid(SWIGLU_ALPHA * gg))
            acc_s[r] += jnp.dot(act.astype(BF16), wo_s[...], preferred_element_type=F32)
            return c

        lax.fori_loop(0, nt, tile, 0)

        @pl.when(j == nj - 1)
        def _store_rows():
            def emit(r, c):
                slot = r % 2

                @pl.when(r >= 2)
                def _():
                    y_copy(r - 2, slot).wait()

                y = acc_s[r] + bo_ref[...]
                yst_s[slot] = pltpu.pack_elementwise([y[:, :half], y[:, half:]], packed_dtype=BF16)
                y_copy(r, slot).start()
                return c

            lax.fori_loop(0, nt, emit, 0)

            @pl.when(nt >= 2)
            def _():
                y_copy(nt - 2, nt % 2).wait()

            y_copy(nt - 1, (nt - 1) % 2).wait()


def _moe_call(work_e, work_t0, work_nt, xs, w_in, b_in, w_out, b_out,
              tmx=ROW_TILE, group=GROUP_TILES, fc=F_CHUNK):
    ne, d, de2 = w_in.shape
    de = de2 // 2
    nj = de // fc
    n_work = work_e.shape[0]
    n_slots, half = xs.shape

    def jj(w, j, wn):
        return jnp.where(wn[w] > 0, j, nj - 1)

    grid_spec = pltpu.PrefetchScalarGridSpec(
        num_scalar_prefetch=3,
        grid=(n_work, nj),
        in_specs=[
            pl.BlockSpec(memory_space=pl.ANY),
            pl.BlockSpec((None, d, fc), lambda w, j, we, wt, wn: (we[w], 0, jj(w, j, wn))),
            pl.BlockSpec((None, d, fc), lambda w, j, we, wt, wn: (we[w], 0, nj + jj(w, j, wn))),
            pl.BlockSpec((None, 1, fc), lambda w, j, we, wt, wn: (we[w], 0, jj(w, j, wn))),
            pl.BlockSpec((None, 1, fc), lambda w, j, we, wt, wn: (we[w], 0, nj + jj(w, j, wn))),
            pl.BlockSpec((None, fc, d), lambda w, j, we, wt, wn: (we[w], jj(w, j, wn), 0)),
            pl.BlockSpec((None, 1, d), lambda w, j, we, wt, wn: (we[w], 0, 0)),
        ],
        out_specs=pl.BlockSpec(memory_space=pl.ANY),
        scratch_shapes=[pltpu.VMEM((group, tmx, half), U32),
                        pltpu.VMEM((group, tmx, d), F32),
                        pltpu.VMEM((d, 2 * fc), BF16),
                        pltpu.VMEM((fc, d), BF16),
                        pltpu.VMEM((2, tmx, half), U32),
                        pltpu.SemaphoreType.DMA(()),
                        pltpu.SemaphoreType.DMA((2,))],
    )
    return pl.pallas_call(
        functools.partial(_moe_kernel, tmx=tmx, fc=fc, nj=nj),
        grid_spec=grid_spec,
        out_shape=jax.ShapeDtypeStruct((n_slots, half), U32),
        compiler_params=pltpu.CompilerParams(dimension_semantics=("arbitrary", "arbitrary"),
                                             vmem_limit_bytes=VMEM_LIMIT, has_side_effects=True),
        name="moe_experts",
    )(work_e, work_t0, work_nt, xs, w_in, w_in, b_in.reshape(ne, 1, de2), b_in.reshape(ne, 1, de2),
      w_out, b_out.reshape(ne, 1, d))


def _combine_kernel(pos_ref, x1_ref, gate_ref, ys_ref, o_ref, ybuf, sem, *, tb):
    half = ybuf.shape[-1]

    def issue(tl, c):
        for k in range(TOP_K):
            pltpu.make_async_copy(ys_ref.at[pl.ds(pos_ref[k, tl], 1)], ybuf.at[k, pl.ds(tl, 1)],
                                  sem).start()
        return c

    lax.fori_loop(0, tb, issue, 0, unroll=4)
    for k in range(TOP_K):
        pltpu.make_async_copy(ys_ref.at[pl.ds(0, tb)], ybuf.at[k], sem).wait()

    lo = x1_ref[:, 0:half]
    hi = x1_ref[:, half:2 * half]
    for k in range(TOP_K):
        g = gate_ref[:, k:k + 1]
        yw = ybuf[k]
        lo = lo + g * pltpu.unpack_elementwise(yw, index=0, packed_dtype=BF16, unpacked_dtype=F32)
        hi = hi + g * pltpu.unpack_elementwise(yw, index=1, packed_dtype=BF16, unpacked_dtype=F32)
    o_ref[:, 0:half] = lo
    o_ref[:, half:2 * half] = hi


def _combine_call(pos, x1, gate_tk, ys, tb=256):
    t, d = x1.shape
    half = ys.shape[1]
    return pl.pallas_call(
        functools.partial(_combine_kernel, tb=tb),
        grid=(t // tb,),
        in_specs=[pl.BlockSpec((TOP_K, tb), lambda i: (0, i), memory_space=pltpu.SMEM),
                  pl.BlockSpec((tb, d), lambda i: (i, 0)),
                  pl.BlockSpec((tb, TOP_K), lambda i: (i, 0)),
                  pl.BlockSpec(memory_space=pl.ANY)],
        out_specs=pl.BlockSpec((tb, d), lambda i: (i, 0)),
        out_shape=jax.ShapeDtypeStruct((t, d), F32),
        scratch_shapes=[pltpu.VMEM((TOP_K, tb, half), U32),
                        pltpu.SemaphoreType.DMA(())],
        compiler_params=_cparams(("arbitrary",)),
        name="combine",
    )(pos, x1, gate_tk, ys)


def _work_list(ntile, otile, n_work, group):
    ne = ntile.shape[0]
    items = (ntile + group - 1) // group
    ends = jnp.cumsum(items)
    total = ends[-1]
    w = jnp.arange(n_work, dtype=I32)
    valid = w < total
    e_w = jnp.clip(jnp.searchsorted(ends, w, side="right").astype(I32), 0, ne - 1)
    local = w - (ends - items)[e_w]
    t0_w = otile[e_w] + local * group
    nt_w = jnp.where(valid, jnp.clip(ntile[e_w] - local * group, 0, group), 0)
    e_last = e_w[jnp.maximum(total - 1, 0)]
    return (jnp.where(valid, e_w, e_last).astype(I32), jnp.where(valid, t0_w, 0).astype(I32),
            nt_w.astype(I32))


def _rotary_tables(seq, dk):
    inv_freq = ROPE_BASE ** (-jnp.arange(0, dk, 2, dtype=F32) / dk)
    ang = jnp.arange(seq, dtype=I32).astype(F32)[:, None] * inv_freq[None, :]
    cos = jnp.repeat(jnp.cos(ang), 2, axis=1)
    sin = jnp.stack([-jnp.sin(ang), jnp.sin(ang)], axis=-1).reshape(seq, dk)
    return cos, sin


def kernel(x, norm_mix_g, w_mix_in, ret_gn_g, q_norm_g, k_norm_g, lambda_q1, lambda_k1, lambda_q2, lambda_k2, diff_subln_g, rel_bias_table, w_mix_out, norm_ffn_g, w_router, b_router, w_exp_in, b_exp_in, w_exp_out, b_exp_out):
    batch, seq, d = x.shape
    t = batch * seq
    depth = norm_mix_g.shape[0]
    ne = w_router.shape[-1]
    n_tiles_max = (t * TOP_K) // ROW_TILE + ne
    n_work = ne + (n_tiles_max - ne) // GROUP_TILES
    cos, sin = _rotary_tables(seq, HEAD_W)
    log_gamma = jnp.log1p(-(2.0 ** (-5.0 - jnp.arange(RET_HEADS, dtype=F32))))

    x2 = x.reshape(t, d)
    for l in range(depth):
        h = _rmsnorm_call(x2, norm_mix_g[l])
        proj = _inproj_call(h, w_mix_in[l])
        y_ret = _ret_call(proj, cos, sin, log_gamma, ret_gn_g[l], batch, seq)
        lam_vecs = jnp.stack([lambda_q1[l], lambda_k1[l], lambda_q2[l], lambda_k2[l]])
        y_diff = _diff_call(proj, rel_bias_table, q_norm_g[l], k_norm_g[l], lam_vecs,
                            diff_subln_g[l], batch, seq)
        x1, h2p, logits_t = _outproj_call(y_ret, y_diff, x2, w_mix_out[l], norm_ffn_g[l],
                                          w_router[l], b_router[l])
        pos, gate, cnt, ntile, otile = _route_call(logits_t)
        cnt, ntile, otile = cnt[:, 0], ntile[:, 0], otile[:, 0]
        work_e, work_t0, work_nt = _work_list(ntile, otile, n_work, GROUP_TILES)
        xs = _dispatch_call(pos, otile * ROW_TILE + cnt, ntile * ROW_TILE - cnt, h2p,
                            n_tiles_max * ROW_TILE)
        ys = _moe_call(work_e, work_t0, work_nt, xs, w_exp_in[l], b_exp_in[l], w_exp_out[l],
                       b_exp_out[l])
        x2 = _combine_call(pos, x1, gate.T, ys)
    return x2.reshape(batch, seq, d)
```

```python
import functools
import math

import jax
import jax.numpy as jnp
from jax import lax
from jax.experimental import pallas as pl
from jax.experimental.pallas import tpu as pltpu

F32 = jnp.float32
BF16 = jnp.bfloat16
I32 = jnp.int32
U32 = jnp.uint32

EPS = 1e-6
CHUNK = 64
RET_HEADS = 8
DIFF_HEADS = 8
HEAD_W = 128
DIFF_DK = 64
ROPE_BASE = 10000.0
NUM_BUCKETS = 32
MAX_DISTANCE = 128
TOP_K = 4
SWIGLU_LIMIT = 7.0
SWIGLU_ALPHA = 1.702
LAM_INIT = 0.8 - 0.6 * math.exp(-0.3 * 0)

LANES = 128
NEG_BIG = -1e30
VMEM_LIMIT = 56 * 1024 * 1024

SEQ_BLK = 256
ROW_TILE = 256
GROUP_TILES = 8
F_CHUNK = 256


def _cparams(sem, vmem=VMEM_LIMIT):
    return pltpu.CompilerParams(dimension_semantics=sem, vmem_limit_bytes=vmem)


def _rmsnorm_kernel(x_ref, g_ref, o_ref):
    x = x_ref[...]
    ms = jnp.mean(x * x, axis=-1, keepdims=True)
    o_ref[...] = (x * lax.rsqrt(ms + EPS) * g_ref[...]).astype(o_ref.dtype)


def _rmsnorm_call(x2, g, tm=512):
    t, d = x2.shape
    return pl.pallas_call(
        _rmsnorm_kernel,
        grid=(t // tm,),
        in_specs=[pl.BlockSpec((tm, d), lambda i: (i, 0)),
                  pl.BlockSpec((1, d), lambda i: (0, 0))],
        out_specs=pl.BlockSpec((tm, d), lambda i: (i, 0)),
        out_shape=jax.ShapeDtypeStruct((t, d), BF16),
        compiler_params=_cparams(("arbitrary",)),
        name="rmsnorm_in",
    )(x2, g.reshape(1, d))


def _inproj_kernel(h_ref, w_ref, o_ref, wb_ref, *, rows_per_cast):
    @pl.when(pl.program_id(1) == 0)
    def _():
        d = w_ref.shape[0]
        for c in range(d // rows_per_cast):
            sl = slice(c * rows_per_cast, (c + 1) * rows_per_cast)
            wb_ref[sl, :] = w_ref[sl, :].astype(BF16)

    acc = jnp.dot(h_ref[...], wb_ref[...], preferred_element_type=F32)
    for j in range(o_ref.shape[0]):
        o_ref[j] = acc[:, j * LANES:(j + 1) * LANES].astype(o_ref.dtype)


def _inproj_call(h, w, tm=512, tn=1024):
    t, d = h.shape
    n = w.shape[1]
    return pl.pallas_call(
        functools.partial(_inproj_kernel, rows_per_cast=min(256, d)),
        grid=(n // tn, t // tm),
        in_specs=[pl.BlockSpec((tm, d), lambda j, i: (i, 0)),
                  pl.BlockSpec((d, tn), lambda j, i: (0, j))],
        out_specs=pl.BlockSpec((tn // LANES, tm, LANES), lambda j, i: (j, i, 0)),
        out_shape=jax.ShapeDtypeStruct((n // LANES, t, LANES), BF16),
        scratch_shapes=[pltpu.VMEM((d, tn), BF16)],
        compiler_params=_cparams(("arbitrary", "arbitrary")),
        name="in_proj",
    )(h, w)


def _ret_kernel(lg_ref, q_ref, k_ref, v_ref, g_ref, cos_ref, sin_ref, gn_ref, o_ref, *, blk, nblk):
    dk = q_ref.shape[-1]
    lg = lg_ref[pl.program_id(1)]
    row = lax.broadcasted_iota(I32, (blk, blk), 0)
    col = lax.broadcasted_iota(I32, (blk, blk), 1)
    dist = jnp.abs(row - col).astype(F32)
    visible = (col // CHUNK) <= (row // CHUNK)
    dmask = jnp.where(visible, jnp.exp(lg * dist), 0.0)
    rr = lax.broadcasted_iota(I32, (blk, dk), 0).astype(F32)
    qdec = jnp.exp(lg * (rr + 1.0))
    kdec = jnp.exp(lg * (blk - 1.0 - rr))
    bdec = jnp.exp(lg * jnp.full((1, HEAD_W), float(blk), F32))
    even = (lax.broadcasted_iota(I32, (blk, dk), 1) & 1) == 0
    scale = dk ** -0.5

    def body(i, state):
        rows = pl.ds(pl.multiple_of(i * blk, blk), blk)
        cos = cos_ref[rows, :]
        sin = sin_ref[rows, :]

        def rot(x):
            partner = jnp.where(even, pltpu.roll(x, dk - 1, 1), pltpu.roll(x, 1, 1))
            return x * cos + partner * sin

        qr = rot(q_ref[rows, :].astype(F32)) * scale
        kr = rot(k_ref[rows, :].astype(F32))
        v = v_ref[rows, :]
        s = lax.dot_general(qr.astype(BF16), kr.astype(BF16), (((1,), (1,)), ((), ())),
                            preferred_element_type=F32) * dmask
        out = jnp.dot(s.astype(BF16), v, preferred_element_type=F32)
        out = out + jnp.dot((qr * qdec).astype(BF16), state.astype(BF16),
                            preferred_element_type=F32)
        kv = lax.dot_general((kr * kdec).astype(BF16), v, (((0,), (0,)), ((), ())),
                             preferred_element_type=F32)
        state = state * bdec + kv
        ms = jnp.mean(out * out, axis=-1, keepdims=True)
        normed = out * lax.rsqrt(ms + EPS) * gn_ref[...]
        g = g_ref[rows, :].astype(F32)
        o_ref[rows, :] = (g * jax.nn.sigmoid(g) * normed).astype(o_ref.dtype)
        return state

    lax.fori_loop(0, nblk, body, jnp.zeros((dk, HEAD_W), F32))


def _ret_call(proj, cos, sin, log_gamma, gn_g, batch, seq, blk=SEQ_BLK):
    nh = RET_HEADS

    def head_spec(base):
        return pl.BlockSpec((None, seq, LANES), lambda b, h: (base + h, b, 0))

    return pl.pallas_call(
        functools.partial(_ret_kernel, blk=blk, nblk=seq // blk),
        grid=(batch, nh),
        in_specs=[pl.BlockSpec(memory_space=pltpu.SMEM),
                  head_spec(0), head_spec(nh), head_spec(2 * nh), head_spec(3 * nh),
                  pl.BlockSpec((seq, LANES), lambda b, h: (0, 0)),
                  pl.BlockSpec((seq, LANES), lambda b, h: (0, 0)),
                  pl.BlockSpec((None, 1, LANES), lambda b, h: (h, 0, 0))],
        out_specs=pl.BlockSpec((None, seq, LANES), lambda b, h: (h, b, 0)),
        out_shape=jax.ShapeDtypeStruct((nh, batch * seq, LANES), BF16),
        compiler_params=_cparams(("arbitrary", "arbitrary")),
        name="retention",
    )(log_gamma, proj, proj, proj, proj, cos, sin, gn_g.reshape(nh, 1, LANES))


def _diff_kernel(tbl_ref, q_ref, k_ref, v_ref, qg_ref, kg_ref, lam_ref, sg_ref, bidx_ref, o_ref,
                 qz_s, kn_s, bias_s, *, blk, nblk):
    h = pl.program_id(0)
    b = pl.program_id(1)
    far_bucket = NUM_BUCKETS // 2 - 1

    @pl.when(b == 0)
    def _build_bias():
        row = lax.broadcasted_iota(I32, (blk, blk), 0)
        col = lax.broadcasted_iota(I32, (blk, blk), 1)
        visible = (col // CHUNK) <= (row // CHUNK)
        for d in range(2):
            idx = bidx_ref[d]
            bias = jnp.zeros((blk, blk), F32)
            for bucket in range(NUM_BUCKETS):
                bias = jnp.where(idx == bucket, tbl_ref[bucket, h], bias)
            if d == 0:
                bias = jnp.where(visible, bias, NEG_BIG)
            bias_s[d, 0:blk, :] = bias
            bias_s[d, blk:2 * blk, :] = bias

    lo = lax.broadcasted_iota(I32, (blk, HEAD_W), 1) < DIFF_DK
    scale = DIFF_DK ** -0.5

    def half_norm(x, g):
        x2 = x * x
        s_lo = jnp.sum(jnp.where(lo, x2, 0.0), axis=-1, keepdims=True)
        s_hi = jnp.sum(jnp.where(lo, 0.0, x2), axis=-1, keepdims=True)
        ms = jnp.where(lo, s_lo, s_hi) * (1.0 / DIFF_DK)
        return x * lax.rsqrt(ms + EPS) * g

    def prep(i, carry):
        rows = pl.ds(pl.multiple_of(i * blk, blk), blk)
        qn = half_norm(q_ref[rows, :].astype(F32), qg_ref[...]) * scale
        kn = half_norm(k_ref[rows, :].astype(F32), kg_ref[...])
        base = pl.multiple_of(i * 2 * blk, 2 * blk)
        qz_s[pl.ds(base, blk), :] = jnp.where(lo, qn, 0.0).astype(BF16)
        qz_s[pl.ds(base + blk, blk), :] = jnp.where(lo, 0.0, qn).astype(BF16)
        kn_s[rows, :] = kn.astype(BF16)
        return carry

    lax.fori_loop(0, nblk, prep, 0)

    lam = (jnp.exp(jnp.sum(lam_ref[0:1, :] * lam_ref[1:2, :], axis=-1, keepdims=True))
           - jnp.exp(jnp.sum(lam_ref[2:3, :] * lam_ref[3:4, :], axis=-1, keepdims=True))
           + LAM_INIT)
    c_far = tbl_ref[far_bucket, h]

    def qblock(i, carry):
        qz = qz_s[pl.ds(pl.multiple_of(i * 2 * blk, 2 * blk), 2 * blk), :]

        def step(j, st, bias):
            m, l, acc = st
            keys = pl.ds(pl.multiple_of(j * blk, blk), blk)
            s = lax.dot_general(qz, kn_s[keys, :], (((1,), (1,)), ((), ())),
                                preferred_element_type=F32) + bias
            m_new = jnp.maximum(m, jnp.max(s, axis=-1, keepdims=True))
            alpha = jnp.exp(m - m_new)
            p = jnp.exp(s - m_new)
            l = alpha * l + jnp.sum(p, axis=-1, keepdims=True)
            acc = alpha * acc + jnp.dot(p.astype(BF16), v_ref[keys, :],
                                        preferred_element_type=F32)
            return m_new, l, acc

        st = (jnp.full((2 * blk, 1), NEG_BIG, F32), jnp.zeros((2 * blk, 1), F32),
              jnp.zeros((2 * blk, HEAD_W), F32))
        st = lax.fori_loop(0, i - 1, lambda j, c: step(j, c, c_far), st)
        st = lax.cond(i >= 1, lambda c: step(i - 1, c, bias_s[1]), lambda c: c, st)
        _, l, acc = step(i, st, bias_s[0])
        o = acc / l
        att = o[0:blk, :] - lam * o[blk:2 * blk, :]
        ms = jnp.mean(att * att, axis=-1, keepdims=True)
        y = att * lax.rsqrt(ms + EPS) * sg_ref[...] * (1.0 - LAM_INIT)
        o_ref[pl.ds(pl.multiple_of(i * blk, blk), blk), :] = y.astype(o_ref.dtype)
        return carry

    lax.fori_loop(0, nblk, qblock, 0)


def _t5_bucket(rel):
    nb = NUM_BUCKETS // 2
    max_exact = nb // 2
    base = jnp.where(rel > 0, nb, 0)
    n = jnp.abs(rel)
    large = max_exact + (jnp.log(jnp.maximum(n, 1).astype(jnp.float32) / max_exact)
                         / math.log(MAX_DISTANCE / max_exact) * (nb - max_exact)).astype(jnp.int32)
    large = jnp.minimum(large, nb - 1)
    return base + jnp.where(n < max_exact, n, large)


def _diff_call(proj, rel_table, qg, kg, lam_vecs, sg, batch, seq, blk=SEQ_BLK):
    nh = DIFF_HEADS
    first = 4 * RET_HEADS
    r = jnp.arange(blk, dtype=I32)
    rel0 = r[None, :] - r[:, None]
    bidx = jnp.stack([_t5_bucket(rel0), _t5_bucket(rel0 - blk)]).astype(I32)

    def head_spec(base):
        return pl.BlockSpec((None, seq, LANES), lambda h, b: (base + h, b, 0))

    def vec_spec():
        return pl.BlockSpec((1, LANES), lambda h, b: (0, 0))

    return pl.pallas_call(
        functools.partial(_diff_kernel, blk=blk, nblk=seq // blk),
        grid=(nh, batch),
        in_specs=[pl.BlockSpec(memory_space=pltpu.SMEM),
                  head_spec(first), head_spec(first + nh), head_spec(first + 2 * nh),
                  vec_spec(), vec_spec(),
                  pl.BlockSpec((4, DIFF_DK), lambda h, b: (0, 0)),
                  vec_spec(),
                  pl.BlockSpec((2, blk, blk), lambda h, b: (0, 0, 0))],
        out_specs=pl.BlockSpec((None, seq, LANES), lambda h, b: (h, b, 0)),
        out_shape=jax.ShapeDtypeStruct((nh, batch * seq, LANES), BF16),
        scratch_shapes=[pltpu.VMEM((2 * seq, LANES), BF16),
                        pltpu.VMEM((seq, LANES), BF16),
                        pltpu.VMEM((2, 2 * blk, blk), F32)],
        compiler_params=_cparams(("arbitrary", "arbitrary")),
        name="diff_attn",
    )(rel_table, proj, proj, proj,
      jnp.tile(qg, 2).reshape(1, LANES), jnp.tile(kg, 2).reshape(1, LANES),
      lam_vecs, sg.reshape(1, LANES), bidx)


def _outproj_kernel(yr_ref, yd_ref, x_ref, w_ref, g_ref, wr_ref, br_ref,
                    x1_ref, h2p_ref, lt_ref, wb_ref, *, rows_per_cast):
    @pl.when(pl.program_id(0) == 0)
    def _():
        for c in range(w_ref.shape[0] // rows_per_cast):
            sl = slice(c * rows_per_cast, (c + 1) * rows_per_cast)
            wb_ref[sl, :] = w_ref[sl, :].astype(BF16)

    y = jnp.concatenate([yr_ref[j] for j in range(yr_ref.shape[0])]
                        + [yd_ref[j] for j in range(yd_ref.shape[0])], axis=-1)
    x1 = x_ref[...] + jnp.dot(y, wb_ref[...], preferred_element_type=F32)
    x1_ref[...] = x1
    ms = jnp.mean(x1 * x1, axis=-1, keepdims=True)
    h2 = x1 * lax.rsqrt(ms + EPS) * g_ref[...]
    lt_ref[...] = lax.dot_general(wr_ref[...], h2, (((1,), (1,)), ((), ())),
                                  precision=lax.Precision.HIGHEST,
                                  preferred_element_type=F32) + br_ref[...]
    half = h2.shape[1] // 2
    h2p_ref[...] = pltpu.pack_elementwise([h2[:, :half], h2[:, half:]], packed_dtype=BF16)


def _outproj_call(y_ret, y_diff, x2, w_out, g, w_router, b_router, tm=256):
    t, d = x2.shape
    ne = w_router.shape[1]
    nhr, nhd = y_ret.shape[0], y_diff.shape[0]
    return pl.pallas_call(
        functools.partial(_outproj_kernel, rows_per_cast=min(256, d)),
        grid=(t // tm,),
        in_specs=[pl.BlockSpec((nhr, tm, LANES), lambda i: (0, i, 0)),
                  pl.BlockSpec((nhd, tm, LANES), lambda i: (0, i, 0)),
                  pl.BlockSpec((tm, d), lambda i: (i, 0)),
                  pl.BlockSpec((d, d), lambda i: (0, 0), pipeline_mode=pl.Buffered(1)),
                  pl.BlockSpec((1, d), lambda i: (0, 0)),
                  pl.BlockSpec((ne, d), lambda i: (0, 0)),
                  pl.BlockSpec((ne, 1), lambda i: (0, 0))],
        out_specs=[pl.BlockSpec((tm, d), lambda i: (i, 0)),
                   pl.BlockSpec((tm, d // 2), lambda i: (i, 0)),
                   pl.BlockSpec((ne, tm), lambda i: (0, i))],
        out_shape=[jax.ShapeDtypeStruct((t, d), F32),
                   jax.ShapeDtypeStruct((t, d // 2), U32),
                   jax.ShapeDtypeStruct((ne, t), F32)],
        scratch_shapes=[pltpu.VMEM((d, d), BF16)],
        compiler_params=_cparams(("arbitrary",)),
        name="out_proj_router",
    )(y_ret, y_diff, x2, w_out, g.reshape(1, d), w_router.T, b_router.reshape(ne, 1))


def _route_kernel(lt_ref, pos_ref, gate_ref, cnt_ref, nt_ref, ot_ref, idx_s, rank_s, *, tb, row_tile):
    ne, t = lt_ref.shape
    e_iota = lax.broadcasted_iota(I32, (ne, tb), 0)
    upper = (lax.broadcasted_iota(I32, (tb, tb), 0)
             < lax.broadcasted_iota(I32, (tb, tb), 1)).astype(BF16)

    def pass_a(i, running):
        cols = pl.ds(pl.multiple_of(i * tb, tb), tb)
        l = lt_ref[:, cols]
        tops, hots = [], []
        for k in range(TOP_K):
            m = jnp.max(l, axis=0, keepdims=True)
            idx = jnp.min(jnp.where(l == m, e_iota, ne), axis=0, keepdims=True)
            hot = e_iota == idx
            l = jnp.where(hot, -jnp.inf, l)
            idx_s[k:k + 1, cols] = idx
            tops.append(m)
            hots.append(hot)
        exps = [jnp.exp(m - tops[0]) for m in tops]
        denom = exps[0] + exps[1] + exps[2] + exps[3]
        for k in range(TOP_K):
            gate_ref[k:k + 1, cols] = exps[k] / denom
        hot_all = jnp.zeros((ne, tb), F32)
        for hot in hots:
            hot_all = hot_all + hot.astype(F32)
        before = running + jnp.dot(hot_all.astype(BF16), upper, preferred_element_type=F32)
        for k in range(TOP_K):
            rank_s[k:k + 1, cols] = jnp.sum(jnp.where(hots[k], before, 0.0), axis=0, keepdims=True)
        return running + jnp.sum(hot_all, axis=1, keepdims=True)

    cnt = lax.fori_loop(0, t // tb, pass_a, jnp.zeros((ne, 1), F32))
    ntile = jnp.floor((cnt + (row_tile - 1.0)) * (1.0 / row_tile))
    lower = (lax.broadcasted_iota(I32, (ne, ne), 1)
             < lax.broadcasted_iota(I32, (ne, ne), 0)).astype(BF16)
    otile = jnp.dot(lower, jnp.broadcast_to(ntile, (ne, LANES)).astype(BF16),
                    preferred_element_type=F32)
    cnt_ref[...] = jnp.broadcast_to(cnt, (ne, LANES)).astype(I32)
    nt_ref[...] = jnp.broadcast_to(ntile, (ne, LANES)).astype(I32)
    ot_ref[...] = otile.astype(I32)
    off_rows = otile[:, 0:1] * float(row_tile)

    def pass_b(i, carry):
        cols = pl.ds(pl.multiple_of(i * tb, tb), tb)
        for k in range(TOP_K):
            hot = e_iota == idx_s[k:k + 1, cols]
            off = jnp.sum(jnp.where(hot, off_rows, 0.0), axis=0, keepdims=True)
            pos_ref[k:k + 1, cols] = (rank_s[k:k + 1, cols] + off).astype(I32)
        return carry

    lax.fori_loop(0, t // tb, pass_b, 0)


def _route_call(logits_t, tb=256, row_tile=ROW_TILE):
    ne, t = logits_t.shape
    return pl.pallas_call(
        functools.partial(_route_kernel, tb=tb, row_tile=row_tile),
        out_shape=[jax.ShapeDtypeStruct((TOP_K, t), I32),
                   jax.ShapeDtypeStruct((TOP_K, t), F32),
                   jax.ShapeDtypeStruct((ne, LANES), I32),
                   jax.ShapeDtypeStruct((ne, LANES), I32),
                   jax.ShapeDtypeStruct((ne, LANES), I32)],
        scratch_shapes=[pltpu.VMEM((TOP_K, t), I32), pltpu.VMEM((TOP_K, t), F32)],
        compiler_params=_cparams(None),
        name="route",
    )(logits_t)


_PAD_PIECES = (128, 64, 32, 16, 8)
_SUBLANES = 8


def _dispatch_kernel(pos_ref, pad_start_ref, pad_n_ref, src_ref, dst_ref, zero_s, sem, zsem,
                     *, tb, ne):
    i = pl.program_id(0)

    def pad_copies(e):
        n = pad_n_ref[e]
        head = n & (_SUBLANES - 1)
        start = pad_start_ref[e]
        out = []
        for r in range(_SUBLANES - 1):
            out.append((r < head, pltpu.make_async_copy(
                zero_s.at[pl.ds(0, 1)], dst_ref.at[pl.ds(start + r, 1)], zsem)))
        body = n - head
        for piece in _PAD_PIECES:
            at = pl.multiple_of(start + head + (body & ~(2 * piece - 1)), _SUBLANES)
            out.append(((body & piece) != 0, pltpu.make_async_copy(
                zero_s.at[pl.ds(0, piece)], dst_ref.at[pl.ds(at, piece)], zsem)))
        return out

    @pl.when(i == 0)
    def _zero_pads():
        zero_s[...] = jnp.zeros(zero_s.shape, zero_s.dtype)

        def issue(e, c):
            for cond, cp in pad_copies(e):
                @pl.when(cond)
                def _():
                    cp.start()
            return c

        def drain(e, c):
            for cond, cp in pad_copies(e):
                @pl.when(cond)
                def _():
                    cp.wait()
            return c

        lax.fori_loop(0, ne, issue, 0)
        lax.fori_loop(0, ne, drain, 0)

    def issue_rows(tl, c):
        for k in range(TOP_K):
            pltpu.make_async_copy(src_ref.at[pl.ds(tl, 1)], dst_ref.at[pl.ds(pos_ref[k, tl], 1)],
                                  sem).start()
        return c

    lax.fori_loop(0, tb, issue_rows, 0, unroll=4)
    for k in range(TOP_K):
        pltpu.make_async_copy(src_ref, dst_ref.at[pl.ds(0, tb)], sem).wait()


def _dispatch_call(pos, pad_start, pad_n, h2p, n_slots, tb=512):
    t, w = h2p.shape
    ne = pad_n.shape[0]
    return pl.pallas_call(
        functools.partial(_dispatch_kernel, tb=tb, ne=ne),
        grid=(t // tb,),
        in_specs=[pl.BlockSpec((TOP_K, tb), lambda i: (0, i), memory_space=pltpu.SMEM),
                  pl.BlockSpec(memory_space=pltpu.SMEM),
                  pl.BlockSpec(memory_space=pltpu.SMEM),
                  pl.BlockSpec((tb, w), lambda i: (i, 0))],
        out_specs=pl.BlockSpec(memory_space=pl.ANY),
        out_shape=jax.ShapeDtypeStruct((n_slots, w), U32),
        scratch_shapes=[pltpu.VMEM((_PAD_PIECES[0], w), U32),
                        pltpu.SemaphoreType.DMA(()),
                        pltpu.SemaphoreType.DMA(())],
        compiler_params=pltpu.CompilerParams(dimension_semantics=("arbitrary",),
                                             vmem_limit_bytes=VMEM_LIMIT, has_side_effects=True),
        name="dispatch",
    )(pos, pad_start, pad_n, h2p)


def _moe_kernel(we_ref, wt_ref, wn_ref, xs_ref, wg_ref, wu_ref, bg_ref, bu_ref, wo_ref, bo_ref,
                ys_ref, xu_s, acc_s, wgu_s, wo_s, yst_s, xsem, ysem, *, tmx, fc, nj):
    w = pl.program_id(0)
    j = pl.program_id(1)
    nt = wn_ref[w]
    t0 = wt_ref[w]
    half = xu_s.shape[-1]

    def tile_rows(r):
        return pl.ds(pl.multiple_of((t0 + r) * tmx, tmx), tmx)

    def x_copy(r):
        return pltpu.make_async_copy(xs_ref.at[tile_rows(r)], xu_s.at[r], xsem)

    def y_copy(r, slot):
        return pltpu.make_async_copy(yst_s.at[slot], ys_ref.at[tile_rows(r)], ysem.at[slot])

    @pl.when(nt > 0)
    def _work():
        @pl.when(j == 0)
        def _load_rows():
            def start(r, c):
                x_copy(r).start()
                return c

            def finish(r, c):
                x_copy(r).wait()
                acc_s[r] = jnp.zeros(acc_s.shape[1:], F32)
                return c

            lax.fori_loop(0, nt, start, 0)
            lax.fori_loop(0, nt, finish, 0)

        wgu_s[:, 0:fc] = wg_ref[...].astype(BF16)
        wgu_s[:, fc:2 * fc] = wu_ref[...].astype(BF16)
        wo_s[...] = wo_ref[...].astype(BF16)
        bias_gu = jnp.concatenate([bg_ref[...], bu_ref[...]], axis=-1)

        def tile(r, c):
            xw = xu_s[r]
            x_lo = pltpu.unpack_elementwise(xw, index=0, packed_dtype=BF16,
                                            unpacked_dtype=F32).astype(BF16)
            x_hi = pltpu.unpack_elementwise(xw, index=1, packed_dtype=BF16,
                                            unpacked_dtype=F32).astype(BF16)
            hh = (jnp.dot(x_lo, wgu_s[0:half, :], preferred_element_type=F32)
                  + jnp.dot(x_hi, wgu_s[half:2 * half, :], preferred_element_type=F32) + bias_gu)
            gg = jnp.minimum(hh[:, 0:fc], SWIGLU_LIMIT)
            uu = jnp.clip(hh[:, fc:2 * fc], -SWIGLU_LIMIT, SWIGLU_LIMIT)
            act = (uu + 1.0) * (gg * jax.nn.sigmoid(SWIGLU_ALPHA * gg))
            acc_s[r] += jnp.dot(act.astype(BF16), wo_s[...], preferred_element_type=F32)
            return c

        lax.fori_loop(0, nt, tile, 0)

        @pl.when(j == nj - 1)
        def _store_rows():
            def emit(r, c):
                slot = r % 2

                @pl.when(r >= 2)
                def _():
                    y_copy(r - 2, slot).wait()

                y = acc_s[r] + bo_ref[...]
                yst_s[slot] = pltpu.pack_elementwise([y[:, :half], y[:, half:]], packed_dtype=BF16)
                y_copy(r, slot).start()
                return c

            lax.fori_loop(0, nt, emit, 0)

            @pl.when(nt >= 2)
            def _():
                y_copy(nt - 2, nt % 2).wait()

            y_copy(nt - 1, (nt - 1) % 2).wait()


def _moe_call(work_e, work_t0, work_nt, xs, w_in, b_in, w_out, b_out,
              tmx=ROW_TILE, group=GROUP_TILES, fc=F_CHUNK):
    ne, d, de2 = w_in.shape
    de = de2 // 2
    nj = de // fc
    n_work = work_e.shape[0]
    n_slots, half = xs.shape

    def jj(w, j, wn):
        return jnp.where(wn[w] > 0, j, nj - 1)

    grid_spec = pltpu.PrefetchScalarGridSpec(
        num_scalar_prefetch=3,
        grid=(n_work, nj),
        in_specs=[
            pl.BlockSpec(memory_space=pl.ANY),
            pl.BlockSpec((None, d, fc), lambda w, j, we, wt, wn: (we[w], 0, jj(w, j, wn))),
            pl.BlockSpec((None, d, fc), lambda w, j, we, wt, wn: (we[w], 0, nj + jj(w, j, wn))),
            pl.BlockSpec((None, 1, fc), lambda w, j, we, wt, wn: (we[w], 0, jj(w, j, wn))),
            pl.BlockSpec((None, 1, fc), lambda w, j, we, wt, wn: (we[w], 0, nj + jj(w, j, wn))),
            pl.BlockSpec((None, fc, d), lambda w, j, we, wt, wn: (we[w], jj(w, j, wn), 0)),
            pl.BlockSpec((None, 1, d), lambda w, j, we, wt, wn: (we[w], 0, 0)),
        ],
        out_specs=pl.BlockSpec(memory_space=pl.ANY),
        scratch_shapes=[pltpu.VMEM((group, tmx, half), U32),
                        pltpu.VMEM((group, tmx, d), F32),
                        pltpu.VMEM((d, 2 * fc), BF16),
                        pltpu.VMEM((fc, d), BF16),
                        pltpu.VMEM((2, tmx, half), U32),
                        pltpu.SemaphoreType.DMA(()),
                        pltpu.SemaphoreType.DMA((2,))],
    )
    return pl.pallas_call(
        functools.partial(_moe_kernel, tmx=tmx, fc=fc, nj=nj),
        grid_spec=grid_spec,
        out_shape=jax.ShapeDtypeStruct((n_slots, half), U32),
        compiler_params=pltpu.CompilerParams(dimension_semantics=("arbitrary", "arbitrary"),
                                             vmem_limit_bytes=VMEM_LIMIT, has_side_effects=True),
        name="moe_experts",
    )(work_e, work_t0, work_nt, xs, w_in, w_in, b_in.reshape(ne, 1, de2), b_in.reshape(ne, 1, de2),
      w_out, b_out.reshape(ne, 1, d))


def _combine_kernel(pos_ref, x1_ref, gate_ref, ys_ref, o_ref, ybuf, sem, *, tb):
    half = ybuf.shape[-1]

    def issue(tl, c):
        for k in range(TOP_K):
            pltpu.make_async_copy(ys_ref.at[pl.ds(pos_ref[k, tl], 1)], ybuf.at[k, pl.ds(tl, 1)],
                                  sem).start()
        return c

    lax.fori_loop(0, tb, issue, 0, unroll=4)
    for k in range(TOP_K):
        pltpu.make_async_copy(ys_ref.at[pl.ds(0, tb)], ybuf.at[k], sem).wait()

    lo = x1_ref[:, 0:half]
    hi = x1_ref[:, half:2 * half]
    for k in range(TOP_K):
        g = gate_ref[:, k:k + 1]
        yw = ybuf[k]
        lo = lo + g * pltpu.unpack_elementwise(yw, index=0, packed_dtype=BF16, unpacked_dtype=F32)
        hi = hi + g * pltpu.unpack_elementwise(yw, index=1, packed_dtype=BF16, unpacked_dtype=F32)
    o_ref[:, 0:half] = lo
    o_ref[:, half:2 * half] = hi


def _combine_call(pos, x1, gate_tk, ys, tb=256):
    t, d = x1.shape
    half = ys.shape[1]
    return pl.pallas_call(
        functools.partial(_combine_kernel, tb=tb),
        grid=(t // tb,),
        in_specs=[pl.BlockSpec((TOP_K, tb), lambda i: (0, i), memory_space=pltpu.SMEM),
                  pl.BlockSpec((tb, d), lambda i: (i, 0)),
                  pl.BlockSpec((tb, TOP_K), lambda i: (i, 0)),
                  pl.BlockSpec(memory_space=pl.ANY)],
        out_specs=pl.BlockSpec((tb, d), lambda i: (i, 0)),
        out_shape=jax.ShapeDtypeStruct((t, d), F32),
        scratch_shapes=[pltpu.VMEM((TOP_K, tb, half), U32),
                        pltpu.SemaphoreType.DMA(())],
        compiler_params=_cparams(("arbitrary",)),
        name="combine",
    )(pos, x1, gate_tk, ys)


def _work_list(ntile, otile, n_work, group):
    ne = ntile.shape[0]
    items = (ntile + group - 1) // group
    ends = jnp.cumsum(items)
    total = ends[-1]
    w = jnp.arange(n_work, dtype=I32)
    valid = w < total
    e_w = jnp.clip(jnp.searchsorted(ends, w, side="right").astype(I32), 0, ne - 1)
    local = w - (ends - items)[e_w]
    t0_w = otile[e_w] + local * group
    nt_w = jnp.where(valid, jnp.clip(ntile[e_w] - local * group, 0, group), 0)
    e_last = e_w[jnp.maximum(total - 1, 0)]
    return (jnp.where(valid, e_w, e_last).astype(I32), jnp.where(valid, t0_w, 0).astype(I32),
            nt_w.astype(I32))


def _rotary_tables(seq, dk):
    inv_freq = ROPE_BASE ** (-jnp.arange(0, dk, 2, dtype=F32) / dk)
    ang = jnp.arange(seq, dtype=I32).astype(F32)[:, None] * inv_freq[None, :]
    cos = jnp.repeat(jnp.cos(ang), 2, axis=1)
    sin = jnp.stack([-jnp.sin(ang), jnp.sin(ang)], axis=-1).reshape(seq, dk)
    return cos, sin


def kernel(x, norm_mix_g, w_mix_in, ret_gn_g, q_norm_g, k_norm_g, lambda_q1, lambda_k1, lambda_q2, lambda_k2, diff_subln_g, rel_bias_table, w_mix_out, norm_ffn_g, w_router, b_router, w_exp_in, b_exp_in, w_exp_out, b_exp_out):
    batch, seq, d = x.shape
    t = batch * seq
    depth = norm_mix_g.shape[0]
    ne = w_router.shape[-1]
    n_tiles_max = (t * TOP_K) // ROW_TILE + ne
    n_work = ne + (n_tiles_max - ne) // GROUP_TILES
    cos, sin = _rotary_tables(seq, HEAD_W)
    log_gamma = jnp.log1p(-(2.0 ** (-5.0 - jnp.arange(RET_HEADS, dtype=F32))))

    x2 = x.reshape(t, d)
    for l in range(depth):
        h = _rmsnorm_call(x2, norm_mix_g[l])
        proj = _inproj_call(h, w_mix_in[l])
        y_ret = _ret_call(proj, cos, sin, log_gamma, ret_gn_g[l], batch, seq)
        lam_vecs = jnp.stack([lambda_q1[l], lambda_k1[l], lambda_q2[l], lambda_k2[l]])
        y_diff = _diff_call(proj, rel_bias_table, q_norm_g[l], k_norm_g[l], lam_vecs,
                            diff_subln_g[l], batch, seq)
        x1, h2p, logits_t = _outproj_call(y_ret, y_diff, x2, w_mix_out[l], norm_ffn_g[l],
                                          w_router[l], b_router[l])
        pos, gate, cnt, ntile, otile = _route_call(logits_t)
        cnt, ntile, otile = cnt[:, 0], ntile[:, 0], otile[:, 0]
        work_e, work_t0, work_nt = _work_list(ntile, otile, n_work, GROUP_TILES)
        xs = _dispatch_call(pos, otile * ROW_TILE + cnt, ntile * ROW_TILE - cnt, h2p,
                            n_tiles_max * ROW_TILE)
        ys = _moe_call(work_e, work_t0, work_nt, xs, w_exp_in[l], b_exp_in[l], w_exp_out[l],
                       b_exp_out[l])
        x2 = _combine_call(pos, x1, gate.T, ys)
    return x2.reshape(batch, seq, d)
```

```python
import functools
import math

import jax
import jax.numpy as jnp
from jax import lax
from jax.experimental import pallas as pl
from jax.experimental.pallas import tpu as pltpu

F32 = jnp.float32
BF16 = jnp.bfloat16
I32 = jnp.int32
U32 = jnp.uint32

EPS = 1e-6
CHUNK = 64
RET_HEADS = 8
DIFF_HEADS = 8
HEAD_W = 128
DIFF_DK = 64
ROPE_BASE = 10000.0
NUM_BUCKETS = 32
MAX_DISTANCE = 128
TOP_K = 4
SWIGLU_LIMIT = 7.0
SWIGLU_ALPHA = 1.702
LAM_INIT = 0.8 - 0.6 * math.exp(-0.3 * 0)

LANES = 128
NEG_BIG = -1e30
VMEM_LIMIT = 56 * 1024 * 1024

SEQ_BLK = 256
ROW_TILE = 256
GROUP_TILES = 8
F_CHUNK = 256


def _cparams(sem, vmem=VMEM_LIMIT):
    return pltpu.CompilerParams(dimension_semantics=sem, vmem_limit_bytes=vmem)


def _rmsnorm_kernel(x_ref, g_ref, o_ref):
    x = x_ref[...]
    ms = jnp.mean(x * x, axis=-1, keepdims=True)
    o_ref[...] = (x * lax.rsqrt(ms + EPS) * g_ref[...]).astype(o_ref.dtype)


def _rmsnorm_call(x2, g, tm=512):
    t, d = x2.shape
    return pl.pallas_call(
        _rmsnorm_kernel,
        grid=(t // tm,),
        in_specs=[pl.BlockSpec((tm, d), lambda i: (i, 0)),
                  pl.BlockSpec((1, d), lambda i: (0, 0))],
        out_specs=pl.BlockSpec((tm, d), lambda i: (i, 0)),
        out_shape=jax.ShapeDtypeStruct((t, d), BF16),
        compiler_params=_cparams(("arbitrary",)),
        name="rmsnorm_in",
    )(x2, g.reshape(1, d))


def _inproj_kernel(h_ref, w_ref, o_ref, wb_ref, *, rows_per_cast):
    @pl.when(pl.program_id(1) == 0)
    def _():
        d = w_ref.shape[0]
        for c in range(d // rows_per_cast):
            sl = slice(c * rows_per_cast, (c + 1) * rows_per_cast)
            wb_ref[sl, :] = w_ref[sl, :].astype(BF16)

    acc = jnp.dot(h_ref[...], wb_ref[...], preferred_element_type=F32)
    for j in range(o_ref.shape[0]):
        o_ref[j] = acc[:, j * LANES:(j + 1) * LANES].astype(o_ref.dtype)


def _inproj_call(h, w, tm=512, tn=1024):
    t, d = h.shape
    n = w.shape[1]
    return pl.pallas_call(
        functools.partial(_inproj_kernel, rows_per_cast=min(256, d)),
        grid=(n // tn, t // tm),
        in_specs=[pl.BlockSpec((tm, d), lambda j, i: (i, 0)),
                  pl.BlockSpec((d, tn), lambda j, i: (0, j))],
        out_specs=pl.BlockSpec((tn // LANES, tm, LANES), lambda j, i: (j, i, 0)),
        out_shape=jax.ShapeDtypeStruct((n // LANES, t, LANES), BF16),
        scratch_shapes=[pltpu.VMEM((d, tn), BF16)],
        compiler_params=_cparams(("arbitrary", "arbitrary")),
        name="in_proj",
    )(h, w)


def _ret_kernel(lg_ref, q_ref, k_ref, v_ref, g_ref, cos_ref, sin_ref, gn_ref, o_ref, *, blk, nblk):
    dk = q_ref.shape[-1]
    lg = lg_ref[pl.program_id(1)]
    row = lax.broadcasted_iota(I32, (blk, blk), 0)
    col = lax.broadcasted_iota(I32, (blk, blk), 1)
    dist = jnp.abs(row - col).astype(F32)
    visible = (col // CHUNK) <= (row // CHUNK)
    dmask = jnp.where(visible, jnp.exp(lg * dist), 0.0)
    rr = lax.broadcasted_iota(I32, (blk, dk), 0).astype(F32)
    qdec = jnp.exp(lg * (rr + 1.0))
    kdec = jnp.exp(lg * (blk - 1.0 - rr))
    bdec = jnp.exp(lg * jnp.full((1, HEAD_W), float(blk), F32))
    even = (lax.broadcasted_iota(I32, (blk, dk), 1) & 1) == 0
    scale = dk ** -0.5

    def body(i, state):
        rows = pl.ds(pl.multiple_of(i * blk, blk), blk)
        cos = cos_ref[rows, :]
        sin = sin_ref[rows, :]

        def rot(x):
            partner = jnp.where(even, pltpu.roll(x, dk - 1, 1), pltpu.roll(x, 1, 1))
            return x * cos + partner * sin

        qr = rot(q_ref[rows, :].astype(F32)) * scale
        kr = rot(k_ref[rows, :].astype(F32))
        v = v_ref[rows, :]
        s = lax.dot_general(qr.astype(BF16), kr.astype(BF16), (((1,), (1,)), ((), ())),
                            preferred_element_type=F32) * dmask
        out = jnp.dot(s.astype(BF16), v, preferred_element_type=F32)
        out = out + jnp.dot((qr * qdec).astype(BF16), state.astype(BF16),
                            preferred_element_type=F32)
        kv = lax.dot_general((kr * kdec).astype(BF16), v, (((0,), (0,)), ((), ())),
                             preferred_element_type=F32)
        state = state * bdec + kv
        ms = jnp.mean(out * out, axis=-1, keepdims=True)
        normed = out * lax.rsqrt(ms + EPS) * gn_ref[...]
        g = g_ref[rows, :].astype(F32)
        o_ref[rows, :] = (g * jax.nn.sigmoid(g) * normed).astype(o_ref.dtype)
        return state

    lax.fori_loop(0, nblk, body, jnp.zeros((dk, HEAD_W), F32))


def _ret_call(proj, cos, sin, log_gamma, gn_g, batch, seq, blk=SEQ_BLK):
    nh = RET_HEADS

    def head_spec(base):
        return pl.BlockSpec((None, seq, LANES), lambda b, h: (base + h, b, 0))

    return pl.pallas_call(
        functools.partial(_ret_kernel, blk=blk, nblk=seq // blk),
        grid=(batch, nh),
        in_specs=[pl.BlockSpec(memory_space=pltpu.SMEM),
                  head_spec(0), head_spec(nh), head_spec(2 * nh), head_spec(3 * nh),
                  pl.BlockSpec((seq, LANES), lambda b, h: (0, 0)),
                  pl.BlockSpec((seq, LANES), lambda b, h: (0, 0)),
                  pl.BlockSpec((None, 1, LANES), lambda b, h: (h, 0, 0))],
        out_specs=pl.BlockSpec((None, seq, LANES), lambda b, h: (h, b, 0)),
        out_shape=jax.ShapeDtypeStruct((nh, batch * seq, LANES), BF16),
        compiler_params=_cparams(("arbitrary", "arbitrary")),
        name="retention",
    )(log_gamma, proj, proj, proj, proj, cos, sin, gn_g.reshape(nh, 1, LANES))


def _diff_kernel(tbl_ref, q_ref, k_ref, v_ref, qg_ref, kg_ref, lam_ref, sg_ref, bidx_ref, o_ref,
                 qz_s, kn_s, bias_s, s_s, p_s, *, blk, nblk):
    h = pl.program_id(0)
    b = pl.program_id(1)
    far_bucket = NUM_BUCKETS // 2 - 1

    @pl.when(b == 0)
    def _build_bias():
        row = lax.broadcasted_iota(I32, (blk, blk), 0)
        col = lax.broadcasted_iota(I32, (blk, blk), 1)
        visible = (col // CHUNK) <= (row // CHUNK)
        for d in range(2):
            idx = bidx_ref[d]
            bias = jnp.zeros((blk, blk), F32)
            for bucket in range(NUM_BUCKETS):
                bias = jnp.where(idx == bucket, tbl_ref[bucket, h], bias)
            if d == 0:
                bias = jnp.where(visible, bias, NEG_BIG)
            bias_s[d, 0:blk, :] = bias
            bias_s[d, blk:2 * blk, :] = bias

    lo = lax.broadcasted_iota(I32, (blk, HEAD_W), 1) < DIFF_DK
    scale = DIFF_DK ** -0.5
    same_half = ((lax.broadcasted_iota(I32, (HEAD_W, HEAD_W), 0) // DIFF_DK)
                 == (lax.broadcasted_iota(I32, (HEAD_W, HEAD_W), 1) // DIFF_DK)).astype(BF16)

    def half_norm(x, g):
        ms = jnp.dot((x * x).astype(BF16), same_half, preferred_element_type=F32) * (1.0 / DIFF_DK)
        return x * lax.rsqrt(ms + EPS) * g

    def prep(i, carry):
        rows = pl.ds(pl.multiple_of(i * blk, blk), blk)
        qn = half_norm(q_ref[rows, :].astype(F32), qg_ref[...]) * scale
        kn = half_norm(k_ref[rows, :].astype(F32), kg_ref[...])
        base = pl.multiple_of(i * 2 * blk, 2 * blk)
        qz_s[pl.ds(base, blk), :] = jnp.where(lo, qn, 0.0).astype(BF16)
        qz_s[pl.ds(base + blk, blk), :] = jnp.where(lo, 0.0, qn).astype(BF16)
        kn_s[rows, :] = kn.astype(BF16)
        return carry

    lax.fori_loop(0, nblk, prep, 0)

    lam = (jnp.exp(jnp.sum(lam_ref[0:1, :] * lam_ref[1:2, :], axis=-1, keepdims=True))
           - jnp.exp(jnp.sum(lam_ref[2:3, :] * lam_ref[3:4, :], axis=-1, keepdims=True))
           + LAM_INIT)
    c_far = tbl_ref[far_bucket, h]

    def lane_fold(x, op):
        out = x[:, 0:LANES]
        for c in range(1, blk // LANES):
            out = op(out, x[:, c * LANES:(c + 1) * LANES])
        return out

    for i in range(nblk):
        qz = qz_s[i * 2 * blk:(i + 1) * 2 * blk, :]
        m_t = jnp.full((2 * blk, LANES), NEG_BIG, F32)
        for j in range(i + 1):
            keys = slice(j * blk, (j + 1) * blk)
            s = lax.dot_general(qz, kn_s[keys, :], (((1,), (1,)), ((), ())),
                                preferred_element_type=F32)
            s = s + (bias_s[0] if j == i else bias_s[1] if j == i - 1 else c_far)
            s_s[:, keys] = s
            m_t = jnp.maximum(m_t, lane_fold(s, jnp.maximum))
        m = jnp.max(m_t, axis=-1, keepdims=True)
        l_t = jnp.zeros((2 * blk, LANES), F32)
        for j in range(i + 1):
            keys = slice(j * blk, (j + 1) * blk)
            p = jnp.exp(s_s[:, keys] - m)
            l_t = l_t + lane_fold(p, jnp.add)
            p_s[:, keys] = p.astype(BF16)
        l = jnp.sum(l_t, axis=-1, keepdims=True)
        kend = (i + 1) * blk
        o = jnp.dot(p_s[:, 0:kend], v_ref[0:kend, :], preferred_element_type=F32) / l
        att = o[0:blk, :] - lam * o[blk:2 * blk, :]
        ms = jnp.mean(att * att, axis=-1, keepdims=True)
        y = att * lax.rsqrt(ms + EPS) * sg_ref[...] * (1.0 - LAM_INIT)
        o_ref[i * blk:(i + 1) * blk, :] = y.astype(o_ref.dtype)


def _t5_bucket(rel):
    nb = NUM_BUCKETS // 2
    max_exact = nb // 2
    base = jnp.where(rel > 0, nb, 0)
    n = jnp.abs(rel)
    large = max_exact + (jnp.log(jnp.maximum(n, 1).astype(jnp.float32) / max_exact)
                         / math.log(MAX_DISTANCE / max_exact) * (nb - max_exact)).astype(jnp.int32)
    large = jnp.minimum(large, nb - 1)
    return base + jnp.where(n < max_exact, n, large)


def _diff_call(proj, rel_table, qg, kg, lam_vecs, sg, batch, seq, blk=SEQ_BLK):
    nh = DIFF_HEADS
    first = 4 * RET_HEADS
    r = jnp.arange(blk, dtype=I32)
    rel0 = r[None, :] - r[:, None]
    bidx = jnp.stack([_t5_bucket(rel0), _t5_bucket(rel0 - blk)]).astype(I32)

    def head_spec(base):
        return pl.BlockSpec((None, seq, LANES), lambda h, b: (base + h, b, 0))

    def vec_spec():
        return pl.BlockSpec((1, LANES), lambda h, b: (0, 0))

    return pl.pallas_call(
        functools.partial(_diff_kernel, blk=blk, nblk=seq // blk),
        grid=(nh, batch),
        in_specs=[pl.BlockSpec(memory_space=pltpu.SMEM),
                  head_spec(first), head_spec(first + nh), head_spec(first + 2 * nh),
                  vec_spec(), vec_spec(),
                  pl.BlockSpec((4, DIFF_DK), lambda h, b: (0, 0)),
                  vec_spec(),
                  pl.BlockSpec((2, blk, blk), lambda h, b: (0, 0, 0))],
        out_specs=pl.BlockSpec((None, seq, LANES), lambda h, b: (h, b, 0)),
        out_shape=jax.ShapeDtypeStruct((nh, batch * seq, LANES), BF16),
        scratch_shapes=[pltpu.VMEM((2 * seq, LANES), BF16),
                        pltpu.VMEM((seq, LANES), BF16),
                        pltpu.VMEM((2, 2 * blk, blk), F32),
                        pltpu.VMEM((2 * blk, seq), F32),
                        pltpu.VMEM((2 * blk, seq), BF16)],
        compiler_params=_cparams(("arbitrary", "arbitrary")),
        name="diff_attn",
    )(rel_table, proj, proj, proj,
      jnp.tile(qg, 2).reshape(1, LANES), jnp.tile(kg, 2).reshape(1, LANES),
      lam_vecs, sg.reshape(1, LANES), bidx)


def _outproj_kernel(yr_ref, yd_ref, x_ref, w_hbm, g_ref, wr_ref, br_ref,
                    x1_ref, h2p_ref, lt_ref, wb_s, wstage_s, wr2_s, wsem,
                    *, rows_per_cast):
    ne = lt_ref.shape[0]
    nchunk = wb_s.shape[0] // rows_per_cast

    def w_copy(c):
        return pltpu.make_async_copy(w_hbm.at[pl.ds(c * rows_per_cast, rows_per_cast)],
                                     wstage_s.at[c % 2], wsem.at[c % 2])

    @pl.when(pl.program_id(0) == 0)
    def _stage_weights():
        w_copy(0).start()
        for c in range(nchunk):
            if c + 1 < nchunk:
                w_copy(c + 1).start()
            w_copy(c).wait()
            wb_s[c * rows_per_cast:(c + 1) * rows_per_cast, :] = wstage_s[c % 2].astype(BF16)
        wr = wr_ref[...]
        wr_hi = wr.astype(BF16)
        wr2_s[:, 0:LANES] = wr_hi
        wr2_s[:, LANES:2 * LANES] = (wr - wr_hi.astype(F32)).astype(BF16)

    y = jnp.concatenate([yr_ref[j] for j in range(yr_ref.shape[0])]
                        + [yd_ref[j] for j in range(yd_ref.shape[0])], axis=-1)
    x1 = x_ref[...] + jnp.dot(y, wb_s[...], preferred_element_type=F32)
    x1_ref[...] = x1
    ms = jnp.mean(x1 * x1, axis=-1, keepdims=True)
    h2 = x1 * lax.rsqrt(ms + EPS) * g_ref[...]
    h_hi = h2.astype(BF16)
    h_lo = (h2 - h_hi.astype(F32)).astype(BF16)
    parts = (jnp.dot(h_hi, wr2_s[...], preferred_element_type=F32)
             + jnp.dot(h_lo, wr2_s[...], preferred_element_type=F32))
    logits = parts[:, 0:LANES] + parts[:, LANES:2 * LANES]
    lt_ref[...] = logits.T[0:ne, :] + br_ref[...]
    half = h2.shape[1] // 2
    h2p_ref[...] = pltpu.pack_elementwise([h2[:, :half], h2[:, half:]], packed_dtype=BF16)


def _outproj_call(y_ret, y_diff, x2, w_out, g, w_router, b_router, tm=512):
    t, d = x2.shape
    ne = w_router.shape[1]
    nhr, nhd = y_ret.shape[0], y_diff.shape[0]
    rows_per_cast = min(256, d)
    wr_pad = jnp.pad(w_router, ((0, 0), (0, LANES - ne)))
    return pl.pallas_call(
        functools.partial(_outproj_kernel, rows_per_cast=rows_per_cast),
        grid=(t // tm,),
        in_specs=[pl.BlockSpec((nhr, tm, LANES), lambda i: (0, i, 0)),
                  pl.BlockSpec((nhd, tm, LANES), lambda i: (0, i, 0)),
                  pl.BlockSpec((tm, d), lambda i: (i, 0)),
                  pl.BlockSpec(memory_space=pl.ANY),
                  pl.BlockSpec((1, d), lambda i: (0, 0)),
                  pl.BlockSpec((d, LANES), lambda i: (0, 0)),
                  pl.BlockSpec((ne, 1), lambda i: (0, 0))],
        out_specs=[pl.BlockSpec((tm, d), lambda i: (i, 0)),
                   pl.BlockSpec((tm, d // 2), lambda i: (i, 0)),
                   pl.BlockSpec((ne, tm), lambda i: (0, i))],
        out_shape=[jax.ShapeDtypeStruct((t, d), F32),
                   jax.ShapeDtypeStruct((t, d // 2), U32),
                   jax.ShapeDtypeStruct((ne, t), F32)],
        scratch_shapes=[pltpu.VMEM((d, d), BF16),
                        pltpu.VMEM((2, rows_per_cast, d), F32),
                        pltpu.VMEM((d, 2 * LANES), BF16),
                        pltpu.SemaphoreType.DMA((2,))],
        compiler_params=_cparams(("arbitrary",)),
        name="out_proj_router",
    )(y_ret, y_diff, x2, w_out, g.reshape(1, d), wr_pad, b_router.reshape(ne, 1))


def _route_kernel(lt_ref, pos_ref, gate_ref, cnt_ref, nt_ref, ot_ref, idx_s, rank_s, *, tb, row_tile):
    ne, t = lt_ref.shape
    e_iota = lax.broadcasted_iota(I32, (ne, tb), 0)
    upper = (lax.broadcasted_iota(I32, (tb, tb), 0)
             < lax.broadcasted_iota(I32, (tb, tb), 1)).astype(BF16)

    def pass_a(i, running):
        cols = pl.ds(pl.multiple_of(i * tb, tb), tb)
        l = lt_ref[:, cols]
        tops, hots = [], []
        for k in range(TOP_K):
            m = jnp.max(l, axis=0, keepdims=True)
            idx = jnp.min(jnp.where(l == m, e_iota, ne), axis=0, keepdims=True)
            hot = e_iota == idx
            l = jnp.where(hot, -jnp.inf, l)
            idx_s[k:k + 1, cols] = idx
            tops.append(m)
            hots.append(hot)
        exps = [jnp.exp(m - tops[0]) for m in tops]
        denom = exps[0] + exps[1] + exps[2] + exps[3]
        for k in range(TOP_K):
            gate_ref[k:k + 1, cols] = exps[k] / denom
        hot_all = jnp.zeros((ne, tb), F32)
        for hot in hots:
            hot_all = hot_all + hot.astype(F32)
        before = running + jnp.dot(hot_all.astype(BF16), upper, preferred_element_type=F32)
        for k in range(TOP_K):
            rank_s[k:k + 1, cols] = jnp.sum(jnp.where(hots[k], before, 0.0), axis=0, keepdims=True)
        return running + jnp.sum(hot_all, axis=1, keepdims=True)

    cnt = lax.fori_loop(0, t // tb, pass_a, jnp.zeros((ne, 1), F32))
    ntile = jnp.floor((cnt + (row_tile - 1.0)) * (1.0 / row_tile))
    lower = (lax.broadcasted_iota(I32, (ne, ne), 1)
             < lax.broadcasted_iota(I32, (ne, ne), 0)).astype(BF16)
    otile = jnp.dot(lower, jnp.broadcast_to(ntile, (ne, LANES)).astype(BF16),
                    preferred_element_type=F32)
    cnt_ref[...] = jnp.broadcast_to(cnt, (ne, LANES)).astype(I32)
    nt_ref[...] = jnp.broadcast_to(ntile, (ne, LANES)).astype(I32)
    ot_ref[...] = otile.astype(I32)
    off_rows = otile[:, 0:1] * float(row_tile)

    def pass_b(i, carry):
        cols = pl.ds(pl.multiple_of(i * tb, tb), tb)
        for k in range(TOP_K):
            hot = e_iota == idx_s[k:k + 1, cols]
            off = jnp.sum(jnp.where(hot, off_rows, 0.0), axis=0, keepdims=True)
            pos_ref[k:k + 1, cols] = (rank_s[k:k + 1, cols] + off).astype(I32)
        return carry

    lax.fori_loop(0, t // tb, pass_b, 0)


def _route_call(logits_t, tb=256, row_tile=ROW_TILE):
    ne, t = logits_t.shape
    return pl.pallas_call(
        functools.partial(_route_kernel, tb=tb, row_tile=row_tile),
        out_shape=[jax.ShapeDtypeStruct((TOP_K, t), I32),
                   jax.ShapeDtypeStruct((TOP_K, t), F32),
                   jax.ShapeDtypeStruct((ne, LANES), I32),
                   jax.ShapeDtypeStruct((ne, LANES), I32),
                   jax.ShapeDtypeStruct((ne, LANES), I32)],
        scratch_shapes=[pltpu.VMEM((TOP_K, t), I32), pltpu.VMEM((TOP_K, t), F32)],
        compiler_params=_cparams(None),
        name="route",
    )(logits_t)


_PAD_PIECES = (128, 64, 32, 16, 8)
_SUBLANES = 8


def _dispatch_kernel(pos_ref, pad_start_ref, pad_n_ref, src_ref, dst_ref, zero_s, sem, zsem,
                     *, tb, ne):
    i = pl.program_id(0)

    def pad_copies(e):
        n = pad_n_ref[e]
        head = n & (_SUBLANES - 1)
        start = pad_start_ref[e]
        out = []
        for r in range(_SUBLANES - 1):
            out.append((r < head, pltpu.make_async_copy(
                zero_s.at[pl.ds(0, 1)], dst_ref.at[pl.ds(start + r, 1)], zsem)))
        body = n - head
        for piece in _PAD_PIECES:
            at = pl.multiple_of(start + head + (body & ~(2 * piece - 1)), _SUBLANES)
            out.append(((body & piece) != 0, pltpu.make_async_copy(
                zero_s.at[pl.ds(0, piece)], dst_ref.at[pl.ds(at, piece)], zsem)))
        return out

    @pl.when(i == 0)
    def _zero_pads():
        zero_s[...] = jnp.zeros(zero_s.shape, zero_s.dtype)

        def issue(e, c):
            for cond, cp in pad_copies(e):
                @pl.when(cond)
                def _():
                    cp.start()
            return c

        def drain(e, c):
            for cond, cp in pad_copies(e):
                @pl.when(cond)
                def _():
                    cp.wait()
            return c

        lax.fori_loop(0, ne, issue, 0)
        lax.fori_loop(0, ne, drain, 0)

    def issue_rows(tl, c):
        for k in range(TOP_K):
            pltpu.make_async_copy(src_ref.at[pl.ds(tl, 1)], dst_ref.at[pl.ds(pos_ref[k, tl], 1)],
                                  sem).start()
        return c

    lax.fori_loop(0, tb, issue_rows, 0, unroll=4)
    for k in range(TOP_K):
        pltpu.make_async_copy(src_ref, dst_ref.at[pl.ds(0, tb)], sem).wait()


def _dispatch_call(pos, pad_start, pad_n, h2p, n_slots, tb=512):
    t, w = h2p.shape
    ne = pad_n.shape[0]
    return pl.pallas_call(
        functools.partial(_dispatch_kernel, tb=tb, ne=ne),
        grid=(t // tb,),
        in_specs=[pl.BlockSpec((TOP_K, tb), lambda i: (0, i), memory_space=pltpu.SMEM),
                  pl.BlockSpec(memory_space=pltpu.SMEM),
                  pl.BlockSpec(memory_space=pltpu.SMEM),
                  pl.BlockSpec((tb, w), lambda i: (i, 0))],
        out_specs=pl.BlockSpec(memory_space=pl.ANY),
        out_shape=jax.ShapeDtypeStruct((n_slots, w), U32),
        scratch_shapes=[pltpu.VMEM((_PAD_PIECES[0], w), U32),
                        pltpu.SemaphoreType.DMA(()),
                        pltpu.SemaphoreType.DMA(())],
        compiler_params=pltpu.CompilerParams(dimension_semantics=("arbitrary",),
                                             vmem_limit_bytes=VMEM_LIMIT, has_side_effects=True),
        name="dispatch",
    )(pos, pad_start, pad_n, h2p)


def _moe_kernel(we_ref, wt_ref, wn_ref, xs_ref, wg_ref, wu_ref, bg_ref, bu_ref, wo_ref, bo_ref,
                ys_ref, xu_s, acc_s, wgu_s, wo_s, yst_s, xsem, ysem, *, tmx, fc, nj):
    w = pl.program_id(0)
    j = pl.program_id(1)
    nt = wn_ref[w]
    t0 = wt_ref[w]
    half = xu_s.shape[-1]

    def hbm_rows(r):
        return pl.ds(pl.multiple_of((t0 + r) * tmx, tmx), tmx)

    def local_rows(r, n=1):
        return pl.ds(pl.multiple_of(r * tmx, tmx), n * tmx)

    def x_copy(r):
        return pltpu.make_async_copy(xs_ref.at[hbm_rows(r)], xu_s.at[local_rows(r)], xsem)

    def y_copy(r, slot):
        return pltpu.make_async_copy(yst_s.at[slot], ys_ref.at[hbm_rows(r)], ysem.at[slot])

    @pl.when(nt > 0)
    def _work():
        @pl.when(j == 0)
        def _load_rows():
            def start(r, c):
                x_copy(r).start()
                return c

            def finish(r, c):
                x_copy(r).wait()
                acc_s[local_rows(r), :] = jnp.zeros((tmx, acc_s.shape[1]), F32)
                return c

            lax.fori_loop(0, nt, start, 0)
            lax.fori_loop(0, nt, finish, 0)

        wgu_s[:, 0:fc] = wg_ref[...].astype(BF16)
        wgu_s[:, fc:2 * fc] = wu_ref[...].astype(BF16)
        wo_s[...] = wo_ref[...].astype(BF16)
        bias_gu = jnp.concatenate([bg_ref[...], bu_ref[...]], axis=-1)

        def rows_step(rows):
            xw = xu_s[rows, :]
            x_lo = pltpu.unpack_elementwise(xw, index=0, packed_dtype=BF16,
                                            unpacked_dtype=F32).astype(BF16)
            x_hi = pltpu.unpack_elementwise(xw, index=1, packed_dtype=BF16,
                                            unpacked_dtype=F32).astype(BF16)
            hh = (jnp.dot(x_lo, wgu_s[0:half, :], preferred_element_type=F32)
                  + jnp.dot(x_hi, wgu_s[half:2 * half, :], preferred_element_type=F32) + bias_gu)
            gg = jnp.minimum(hh[:, 0:fc], SWIGLU_LIMIT)
            uu = jnp.clip(hh[:, fc:2 * fc], -SWIGLU_LIMIT, SWIGLU_LIMIT)
            act = (uu + 1.0) * (gg * jax.nn.sigmoid(SWIGLU_ALPHA * gg))
            acc_s[rows, :] += jnp.dot(act.astype(BF16), wo_s[...], preferred_element_type=F32)

        def pair(pi, c):
            rows_step(local_rows(2 * pi, 2))
            return c

        lax.fori_loop(0, nt // 2, pair, 0)

        @pl.when(nt % 2 == 1)
        def _():
            rows_step(local_rows(nt - 1))

        @pl.when(j == nj - 1)
        def _store_rows():
            def emit(r, c):
                slot = r % 2

                @pl.when(r >= 2)
                def _():
                    y_copy(r - 2, slot).wait()

                y = acc_s[local_rows(r), :] + bo_ref[...]
                yst_s[slot] = pltpu.pack_elementwise([y[:, :half], y[:, half:]], packed_dtype=BF16)
                y_copy(r, slot).start()
                return c

            lax.fori_loop(0, nt, emit, 0)

            @pl.when(nt >= 2)
            def _():
                y_copy(nt - 2, nt % 2).wait()

            y_copy(nt - 1, (nt - 1) % 2).wait()


def _moe_call(work_e, work_t0, work_nt, xs, w_in, b_in, w_out, b_out,
              tmx=ROW_TILE, group=GROUP_TILES, fc=F_CHUNK):
    ne, d, de2 = w_in.shape
    de = de2 // 2
    nj = de // fc
    n_work = work_e.shape[0]
    n_slots, half = xs.shape

    def jj(w, j, wn):
        return jnp.where(wn[w] > 0, j, nj - 1)

    grid_spec = pltpu.PrefetchScalarGridSpec(
        num_scalar_prefetch=3,
        grid=(n_work, nj),
        in_specs=[
            pl.BlockSpec(memory_space=pl.ANY),
            pl.BlockSpec((None, d, fc), lambda w, j, we, wt, wn: (we[w], 0, jj(w, j, wn))),
            pl.BlockSpec((None, d, fc), lambda w, j, we, wt, wn: (we[w], 0, nj + jj(w, j, wn))),
            pl.BlockSpec((None, 1, fc), lambda w, j, we, wt, wn: (we[w], 0, jj(w, j, wn))),
            pl.BlockSpec((None, 1, fc), lambda w, j, we, wt, wn: (we[w], 0, nj + jj(w, j, wn))),
            pl.BlockSpec((None, fc, d), lambda w, j, we, wt, wn: (we[w], jj(w, j, wn), 0)),
            pl.BlockSpec((None, 1, d), lambda w, j, we, wt, wn: (we[w], 0, 0)),
        ],
        out_specs=pl.BlockSpec(memory_space=pl.ANY),
        scratch_shapes=[pltpu.VMEM((group * tmx, half), U32),
                        pltpu.VMEM((group * tmx, d), F32),
                        pltpu.VMEM((d, 2 * fc), BF16),
                        pltpu.VMEM((fc, d), BF16),
                        pltpu.VMEM((2, tmx, half), U32),
                        pltpu.SemaphoreType.DMA(()),
                        pltpu.SemaphoreType.DMA((2,))],
    )
    return pl.pallas_call(
        functools.partial(_moe_kernel, tmx=tmx, fc=fc, nj=nj),
        grid_spec=grid_spec,
        out_shape=jax.ShapeDtypeStruct((n_slots, half), U32),
        compiler_params=pltpu.CompilerParams(dimension_semantics=("arbitrary", "arbitrary"),
                                             vmem_limit_bytes=VMEM_LIMIT, has_side_effects=True),
        name="moe_experts",
    )(work_e, work_t0, work_nt, xs, w_in, w_in, b_in.reshape(ne, 1, de2), b_in.reshape(ne, 1, de2),
      w_out, b_out.reshape(ne, 1, d))


def _combine_kernel(pos_ref, x1_ref, gate_ref, ys_ref, o_ref, ybuf, sem, *, tb):
    half = ybuf.shape[-1]

    def issue(tl, c):
        for k in range(TOP_K):
            pltpu.make_async_copy(ys_ref.at[pl.ds(pos_ref[k, tl], 1)], ybuf.at[k, pl.ds(tl, 1)],
                                  sem).start()
        return c

    lax.fori_loop(0, tb, issue, 0, unroll=4)
    for k in range(TOP_K):
        pltpu.make_async_copy(ys_ref.at[pl.ds(0, tb)], ybuf.at[k], sem).wait()

    lo = x1_ref[:, 0:half]
    hi = x1_ref[:, half:2 * half]
    for k in range(TOP_K):
        g = gate_ref[:, k:k + 1]
        yw = ybuf[k]
        lo = lo + g * pltpu.unpack_elementwise(yw, index=0, packed_dtype=BF16, unpacked_dtype=F32)
        hi = hi + g * pltpu.unpack_elementwise(yw, index=1, packed_dtype=BF16, unpacked_dtype=F32)
    o_ref[:, 0:half] = lo
    o_ref[:, half:2 * half] = hi


def _combine_call(pos, x1, gate_tk, ys, tb=256):
    t, d = x1.shape
    half = ys.shape[1]
    return pl.pallas_call(
        functools.partial(_combine_kernel, tb=tb),
        grid=(t // tb,),
        in_specs=[pl.BlockSpec((TOP_K, tb), lambda i: (0, i), memory_space=pltpu.SMEM),
                  pl.BlockSpec((tb, d), lambda i: (i, 0)),
                  pl.BlockSpec((tb, TOP_K), lambda i: (i, 0)),
                  pl.BlockSpec(memory_space=pl.ANY)],
        out_specs=pl.BlockSpec((tb, d), lambda i: (i, 0)),
        out_shape=jax.ShapeDtypeStruct((t, d), F32),
        scratch_shapes=[pltpu.VMEM((TOP_K, tb, half), U32),
                        pltpu.SemaphoreType.DMA(())],
        compiler_params=_cparams(("arbitrary",)),
        name="combine",
    )(pos, x1, gate_tk, ys)


def _work_list(ntile, otile, n_work, group):
    ne = ntile.shape[0]
    items = (ntile + group - 1) // group
    ends = jnp.cumsum(items)
    total = ends[-1]
    w = jnp.arange(n_work, dtype=I32)
    valid = w < total
    e_w = jnp.clip(jnp.searchsorted(ends, w, side="right").astype(I32), 0, ne - 1)
    local = w - (ends - items)[e_w]
    t0_w = otile[e_w] + local * group
    nt_w = jnp.where(valid, jnp.clip(ntile[e_w] - local * group, 0, group), 0)
    e_last = e_w[jnp.maximum(total - 1, 0)]
    return (jnp.where(valid, e_w, e_last).astype(I32), jnp.where(valid, t0_w, 0).astype(I32),
            nt_w.astype(I32))


def _rotary_tables(seq, dk):
    inv_freq = ROPE_BASE ** (-jnp.arange(0, dk, 2, dtype=F32) / dk)
    ang = jnp.arange(seq, dtype=I32).astype(F32)[:, None] * inv_freq[None, :]
    cos = jnp.repeat(jnp.cos(ang), 2, axis=1)
    sin = jnp.stack([-jnp.sin(ang), jnp.sin(ang)], axis=-1).reshape(seq, dk)
    return cos, sin


def kernel(x, norm_mix_g, w_mix_in, ret_gn_g, q_norm_g, k_norm_g, lambda_q1, lambda_k1, lambda_q2, lambda_k2, diff_subln_g, rel_bias_table, w_mix_out, norm_ffn_g, w_router, b_router, w_exp_in, b_exp_in, w_exp_out, b_exp_out):
    batch, seq, d = x.shape
    t = batch * seq
    depth = norm_mix_g.shape[0]
    ne = w_router.shape[-1]
    n_tiles_max = (t * TOP_K) // ROW_TILE + ne
    n_work = ne + (n_tiles_max - ne) // GROUP_TILES
    cos, sin = _rotary_tables(seq, HEAD_W)
    log_gamma = jnp.log1p(-(2.0 ** (-5.0 - jnp.arange(RET_HEADS, dtype=F32))))

    x2 = x.reshape(t, d)
    for l in range(depth):
        h = _rmsnorm_call(x2, norm_mix_g[l])
        proj = _inproj_call(h, w_mix_in[l])
        y_ret = _ret_call(proj, cos, sin, log_gamma, ret_gn_g[l], batch, seq)
        lam_vecs = jnp.stack([lambda_q1[l], lambda_k1[l], lambda_q2[l], lambda_k2[l]])
        y_diff = _diff_call(proj, rel_bias_table, q_norm_g[l], k_norm_g[l], lam_vecs,
                            diff_subln_g[l], batch, seq)
        x1, h2p, logits_t = _outproj_call(y_ret, y_diff, x2, w_mix_out[l], norm_ffn_g[l],
                                          w_router[l], b_router[l])
        pos, gate, cnt, ntile, otile = _route_call(logits_t)
        cnt, ntile, otile = cnt[:, 0], ntile[:, 0], otile[:, 0]
        work_e, work_t0, work_nt = _work_list(ntile, otile, n_work, GROUP_TILES)
        xs = _dispatch_call(pos, otile * ROW_TILE + cnt, ntile * ROW_TILE - cnt, h2p,
                            n_tiles_max * ROW_TILE)
        ys = _moe_call(work_e, work_t0, work_nt, xs, w_exp_in[l], b_exp_in[l], w_exp_out[l],
                       b_exp_out[l])
        x2 = _combine_call(pos, x1, gate.T, ys)
    return x2.reshape(batch, seq, d)
```

```python
import functools
import math

import jax
import jax.numpy as jnp
from jax import lax
from jax.experimental import pallas as pl
from jax.experimental.pallas import tpu as pltpu

F32 = jnp.float32
BF16 = jnp.bfloat16
I32 = jnp.int32
U32 = jnp.uint32

EPS = 1e-6
CHUNK = 64
RET_HEADS = 8
DIFF_HEADS = 8
HEAD_W = 128
DIFF_DK = 64
ROPE_BASE = 10000.0
NUM_BUCKETS = 32
MAX_DISTANCE = 128
TOP_K = 4
SWIGLU_LIMIT = 7.0
SWIGLU_ALPHA = 1.702
LAM_INIT = 0.8 - 0.6 * math.exp(-0.3 * 0)

LANES = 128
NEG_BIG = -1e30
VMEM_LIMIT = 56 * 1024 * 1024

SEQ_BLK = 256
ROW_TILE = 256
GROUP_TILES = 8
F_CHUNK = 256


def _cparams(sem, vmem=VMEM_LIMIT):
    return pltpu.CompilerParams(dimension_semantics=sem, vmem_limit_bytes=vmem)


def _rmsnorm_kernel(x_ref, g_ref, o_ref):
    x = x_ref[...]
    ms = jnp.mean(x * x, axis=-1, keepdims=True)
    o_ref[...] = (x * lax.rsqrt(ms + EPS) * g_ref[...]).astype(o_ref.dtype)


def _rmsnorm_call(x2, g, tm=512):
    t, d = x2.shape
    return pl.pallas_call(
        _rmsnorm_kernel,
        grid=(t // tm,),
        in_specs=[pl.BlockSpec((tm, d), lambda i: (i, 0)),
                  pl.BlockSpec((1, d), lambda i: (0, 0))],
        out_specs=pl.BlockSpec((tm, d), lambda i: (i, 0)),
        out_shape=jax.ShapeDtypeStruct((t, d), BF16),
        compiler_params=_cparams(("arbitrary",)),
        name="rmsnorm_in",
    )(x2, g.reshape(1, d))


def _inproj_kernel(h_ref, w_ref, o_ref, wb_ref, *, rows_per_cast):
    @pl.when(pl.program_id(1) == 0)
    def _():
        d = w_ref.shape[0]
        for c in range(d // rows_per_cast):
            sl = slice(c * rows_per_cast, (c + 1) * rows_per_cast)
            wb_ref[sl, :] = w_ref[sl, :].astype(BF16)

    acc = jnp.dot(h_ref[...], wb_ref[...], preferred_element_type=F32)
    for j in range(o_ref.shape[0]):
        o_ref[j] = acc[:, j * LANES:(j + 1) * LANES].astype(o_ref.dtype)


def _inproj_call(h, w, tm=512, tn=1024):
    t, d = h.shape
    n = w.shape[1]
    return pl.pallas_call(
        functools.partial(_inproj_kernel, rows_per_cast=min(256, d)),
        grid=(n // tn, t // tm),
        in_specs=[pl.BlockSpec((tm, d), lambda j, i: (i, 0)),
                  pl.BlockSpec((d, tn), lambda j, i: (0, j))],
        out_specs=pl.BlockSpec((tn // LANES, tm, LANES), lambda j, i: (j, i, 0)),
        out_shape=jax.ShapeDtypeStruct((n // LANES, t, LANES), BF16),
        scratch_shapes=[pltpu.VMEM((d, tn), BF16)],
        compiler_params=_cparams(("arbitrary", "arbitrary")),
        name="in_proj",
    )(h, w)


def _ret_kernel(lg_ref, q_ref, k_ref, v_ref, g_ref, cos_ref, sin_ref, gn_ref, o_ref, *, blk, nblk):
    dk = q_ref.shape[-1]
    lg = lg_ref[pl.program_id(1)]
    row = lax.broadcasted_iota(I32, (blk, blk), 0)
    col = lax.broadcasted_iota(I32, (blk, blk), 1)
    dist = jnp.abs(row - col).astype(F32)
    visible = (col // CHUNK) <= (row // CHUNK)
    dmask = jnp.where(visible, jnp.exp(lg * dist), 0.0)
    rr = lax.broadcasted_iota(I32, (blk, dk), 0).astype(F32)
    qdec = jnp.exp(lg * (rr + 1.0))
    kdec = jnp.exp(lg * (blk - 1.0 - rr))
    bdec = jnp.exp(lg * jnp.full((1, HEAD_W), float(blk), F32))
    even = (lax.broadcasted_iota(I32, (blk, dk), 1) & 1) == 0
    scale = dk ** -0.5

    def body(i, state):
        rows = pl.ds(pl.multiple_of(i * blk, blk), blk)
        cos = cos_ref[rows, :]
        sin = sin_ref[rows, :]

        def rot(x):
            partner = jnp.where(even, pltpu.roll(x, dk - 1, 1), pltpu.roll(x, 1, 1))
            return x * cos + partner * sin

        qr = rot(q_ref[rows, :].astype(F32)) * scale
        kr = rot(k_ref[rows, :].astype(F32))
        v = v_ref[rows, :]
        s = lax.dot_general(qr.astype(BF16), kr.astype(BF16), (((1,), (1,)), ((), ())),
                            preferred_element_type=F32) * dmask
        out = jnp.dot(s.astype(BF16), v, preferred_element_type=F32)
        out = out + jnp.dot((qr * qdec).astype(BF16), state.astype(BF16),
                            preferred_element_type=F32)
        kv = lax.dot_general((kr * kdec).astype(BF16), v, (((0,), (0,)), ((), ())),
                             preferred_element_type=F32)
        state = state * bdec + kv
        ms = jnp.mean(out * out, axis=-1, keepdims=True)
        normed = out * lax.rsqrt(ms + EPS) * gn_ref[...]
        g = g_ref[rows, :].astype(F32)
        o_ref[rows, :] = (g * jax.nn.sigmoid(g) * normed).astype(o_ref.dtype)
        return state

    lax.fori_loop(0, nblk, body, jnp.zeros((dk, HEAD_W), F32), unroll=True)


def _ret_call(proj, cos, sin, log_gamma, gn_g, batch, seq, blk=SEQ_BLK):
    nh = RET_HEADS

    def head_spec(base):
        return pl.BlockSpec((None, seq, LANES), lambda b, h: (base + h, b, 0))

    return pl.pallas_call(
        functools.partial(_ret_kernel, blk=blk, nblk=seq // blk),
        grid=(batch, nh),
        in_specs=[pl.BlockSpec(memory_space=pltpu.SMEM),
                  head_spec(0), head_spec(nh), head_spec(2 * nh), head_spec(3 * nh),
                  pl.BlockSpec((seq, LANES), lambda b, h: (0, 0)),
                  pl.BlockSpec((seq, LANES), lambda b, h: (0, 0)),
                  pl.BlockSpec((None, 1, LANES), lambda b, h: (h, 0, 0))],
        out_specs=pl.BlockSpec((None, seq, LANES), lambda b, h: (h, b, 0)),
        out_shape=jax.ShapeDtypeStruct((nh, batch * seq, LANES), BF16),
        compiler_params=_cparams(("arbitrary", "arbitrary")),
        name="retention",
    )(log_gamma, proj, proj, proj, proj, cos, sin, gn_g.reshape(nh, 1, LANES))


def _diff_kernel(tbl_ref, q_ref, k_ref, v_ref, qg_ref, kg_ref, lam_ref, sg_ref, bidx_ref, o_ref,
                 qz_s, kn_s, bias_s, s_s, p_s, *, blk, nblk):
    h = pl.program_id(0)
    b = pl.program_id(1)
    far_bucket = NUM_BUCKETS // 2 - 1

    @pl.when(b == 0)
    def _build_bias():
        row = lax.broadcasted_iota(I32, (blk, blk), 0)
        col = lax.broadcasted_iota(I32, (blk, blk), 1)
        visible = (col // CHUNK) <= (row // CHUNK)
        for d in range(2):
            idx = bidx_ref[d]
            bias = jnp.zeros((blk, blk), F32)
            for bucket in range(NUM_BUCKETS):
                bias = jnp.where(idx == bucket, tbl_ref[bucket, h], bias)
            if d == 0:
                bias = jnp.where(visible, bias, NEG_BIG)
            bias_s[d, 0:blk, :] = bias
            bias_s[d, blk:2 * blk, :] = bias

    lo = lax.broadcasted_iota(I32, (blk, HEAD_W), 1) < DIFF_DK
    scale = DIFF_DK ** -0.5
    same_half = ((lax.broadcasted_iota(I32, (HEAD_W, HEAD_W), 0) // DIFF_DK)
                 == (lax.broadcasted_iota(I32, (HEAD_W, HEAD_W), 1) // DIFF_DK)).astype(BF16)

    def half_norm(x, g):
        ms = jnp.dot((x * x).astype(BF16), same_half, preferred_element_type=F32) * (1.0 / DIFF_DK)
        return x * lax.rsqrt(ms + EPS) * g

    def prep(i, carry):
        rows = pl.ds(pl.multiple_of(i * blk, blk), blk)
        qn = half_norm(q_ref[rows, :].astype(F32), qg_ref[...]) * scale
        kn = half_norm(k_ref[rows, :].astype(F32), kg_ref[...])
        base = pl.multiple_of(i * 2 * blk, 2 * blk)
        qz_s[pl.ds(base, blk), :] = jnp.where(lo, qn, 0.0).astype(BF16)
        qz_s[pl.ds(base + blk, blk), :] = jnp.where(lo, 0.0, qn).astype(BF16)
        kn_s[rows, :] = kn.astype(BF16)
        return carry

    lax.fori_loop(0, nblk, prep, 0)

    lam = (jnp.exp(jnp.sum(lam_ref[0:1, :] * lam_ref[1:2, :], axis=-1, keepdims=True))
           - jnp.exp(jnp.sum(lam_ref[2:3, :] * lam_ref[3:4, :], axis=-1, keepdims=True))
           + LAM_INIT)
    c_far = tbl_ref[far_bucket, h]

    def lane_fold(x, op):
        out = x[:, 0:LANES]
        for c in range(1, blk // LANES):
            out = op(out, x[:, c * LANES:(c + 1) * LANES])
        return out

    for i in range(nblk):
        qz = qz_s[i * 2 * blk:(i + 1) * 2 * blk, :]
        m_t = jnp.full((2 * blk, LANES), NEG_BIG, F32)
        for j in range(i + 1):
            keys = slice(j * blk, (j + 1) * blk)
            s = lax.dot_general(qz, kn_s[keys, :], (((1,), (1,)), ((), ())),
                                preferred_element_type=F32)
            s = s + (bias_s[0] if j == i else bias_s[1] if j == i - 1 else c_far)
            s_s[:, keys] = s
            m_t = jnp.maximum(m_t, lane_fold(s, jnp.maximum))
        m = jnp.max(m_t, axis=-1, keepdims=True)
        l_t = jnp.zeros((2 * blk, LANES), F32)
        for j in range(i + 1):
            keys = slice(j * blk, (j + 1) * blk)
            p = jnp.exp(s_s[:, keys] - m)
            l_t = l_t + lane_fold(p, jnp.add)
            p_s[:, keys] = p.astype(BF16)
        l = jnp.sum(l_t, axis=-1, keepdims=True)
        kend = (i + 1) * blk
        o = jnp.dot(p_s[:, 0:kend], v_ref[0:kend, :], preferred_element_type=F32) / l
        att = o[0:blk, :] - lam * o[blk:2 * blk, :]
        ms = jnp.mean(att * att, axis=-1, keepdims=True)
        y = att * lax.rsqrt(ms + EPS) * sg_ref[...] * (1.0 - LAM_INIT)
        o_ref[i * blk:(i + 1) * blk, :] = y.astype(o_ref.dtype)


def _t5_bucket(rel):
    nb = NUM_BUCKETS // 2
    max_exact = nb // 2
    base = jnp.where(rel > 0, nb, 0)
    n = jnp.abs(rel)
    large = max_exact + (jnp.log(jnp.maximum(n, 1).astype(jnp.float32) / max_exact)
                         / math.log(MAX_DISTANCE / max_exact) * (nb - max_exact)).astype(jnp.int32)
    large = jnp.minimum(large, nb - 1)
    return base + jnp.where(n < max_exact, n, large)


def _diff_call(proj, rel_table, qg, kg, lam_vecs, sg, batch, seq, blk=SEQ_BLK):
    nh = DIFF_HEADS
    first = 4 * RET_HEADS
    r = jnp.arange(blk, dtype=I32)
    rel0 = r[None, :] - r[:, None]
    bidx = jnp.stack([_t5_bucket(rel0), _t5_bucket(rel0 - blk)]).astype(I32)

    def head_spec(base):
        return pl.BlockSpec((None, seq, LANES), lambda h, b: (base + h, b, 0))

    def vec_spec():
        return pl.BlockSpec((1, LANES), lambda h, b: (0, 0))

    return pl.pallas_call(
        functools.partial(_diff_kernel, blk=blk, nblk=seq // blk),
        grid=(nh, batch),
        in_specs=[pl.BlockSpec(memory_space=pltpu.SMEM),
                  head_spec(first), head_spec(first + nh), head_spec(first + 2 * nh),
                  vec_spec(), vec_spec(),
                  pl.BlockSpec((4, DIFF_DK), lambda h, b: (0, 0)),
                  vec_spec(),
                  pl.BlockSpec((2, blk, blk), lambda h, b: (0, 0, 0))],
        out_specs=pl.BlockSpec((None, seq, LANES), lambda h, b: (h, b, 0)),
        out_shape=jax.ShapeDtypeStruct((nh, batch * seq, LANES), BF16),
        scratch_shapes=[pltpu.VMEM((2 * seq, LANES), BF16),
                        pltpu.VMEM((seq, LANES), BF16),
                        pltpu.VMEM((2, 2 * blk, blk), F32),
                        pltpu.VMEM((2 * blk, seq), F32),
                        pltpu.VMEM((2 * blk, seq), BF16)],
        compiler_params=_cparams(("arbitrary", "arbitrary")),
        name="diff_attn",
    )(rel_table, proj, proj, proj,
      jnp.tile(qg, 2).reshape(1, LANES), jnp.tile(kg, 2).reshape(1, LANES),
      lam_vecs, sg.reshape(1, LANES), bidx)


def _outproj_kernel(yr_ref, yd_ref, x_ref, w_hbm, g_ref, wr_ref, br_ref,
                    x1_ref, h2p_ref, lt_ref, wb_s, wstage_s, wr2_s, wsem,
                    *, rows_per_cast):
    ne = lt_ref.shape[0]
    nchunk = wb_s.shape[0] // rows_per_cast

    def w_copy(c):
        return pltpu.make_async_copy(w_hbm.at[pl.ds(c * rows_per_cast, rows_per_cast)],
                                     wstage_s.at[c % 2], wsem.at[c % 2])

    @pl.when(pl.program_id(0) == 0)
    def _stage_weights():
        w_copy(0).start()
        for c in range(nchunk):
            if c + 1 < nchunk:
                w_copy(c + 1).start()
            w_copy(c).wait()
            wb_s[c * rows_per_cast:(c + 1) * rows_per_cast, :] = wstage_s[c % 2].astype(BF16)
        wr = wr_ref[...]
        wr_hi = wr.astype(BF16)
        wr2_s[:, 0:LANES] = wr_hi
        wr2_s[:, LANES:2 * LANES] = (wr - wr_hi.astype(F32)).astype(BF16)

    y = jnp.concatenate([yr_ref[j] for j in range(yr_ref.shape[0])]
                        + [yd_ref[j] for j in range(yd_ref.shape[0])], axis=-1)
    x1 = x_ref[...] + jnp.dot(y, wb_s[...], preferred_element_type=F32)
    x1_ref[...] = x1
    ms = jnp.mean(x1 * x1, axis=-1, keepdims=True)
    h2 = x1 * lax.rsqrt(ms + EPS) * g_ref[...]
    h_hi = h2.astype(BF16)
    h_lo = (h2 - h_hi.astype(F32)).astype(BF16)
    parts = (jnp.dot(h_hi, wr2_s[...], preferred_element_type=F32)
             + jnp.dot(h_lo, wr2_s[...], preferred_element_type=F32))
    logits = parts[:, 0:LANES] + parts[:, LANES:2 * LANES]
    lt_ref[...] = logits.T[0:ne, :] + br_ref[...]
    half = h2.shape[1] // 2
    h2p_ref[...] = pltpu.pack_elementwise([h2[:, :half], h2[:, half:]], packed_dtype=BF16)


def _outproj_call(y_ret, y_diff, x2, w_out, g, w_router, b_router, tm=512):
    t, d = x2.shape
    ne = w_router.shape[1]
    nhr, nhd = y_ret.shape[0], y_diff.shape[0]
    rows_per_cast = min(256, d)
    wr_pad = jnp.pad(w_router, ((0, 0), (0, LANES - ne)))
    return pl.pallas_call(
        functools.partial(_outproj_kernel, rows_per_cast=rows_per_cast),
        grid=(t // tm,),
        in_specs=[pl.BlockSpec((nhr, tm, LANES), lambda i: (0, i, 0)),
                  pl.BlockSpec((nhd, tm, LANES), lambda i: (0, i, 0)),
                  pl.BlockSpec((tm, d), lambda i: (i, 0)),
                  pl.BlockSpec(memory_space=pl.ANY),
                  pl.BlockSpec((1, d), lambda i: (0, 0)),
                  pl.BlockSpec((d, LANES), lambda i: (0, 0)),
                  pl.BlockSpec((ne, 1), lambda i: (0, 0))],
        out_specs=[pl.BlockSpec((tm, d), lambda i: (i, 0)),
                   pl.BlockSpec((tm, d // 2), lambda i: (i, 0)),
                   pl.BlockSpec((ne, tm), lambda i: (0, i))],
        out_shape=[jax.ShapeDtypeStruct((t, d), F32),
                   jax.ShapeDtypeStruct((t, d // 2), U32),
                   jax.ShapeDtypeStruct((ne, t), F32)],
        scratch_shapes=[pltpu.VMEM((d, d), BF16),
                        pltpu.VMEM((2, rows_per_cast, d), F32),
                        pltpu.VMEM((d, 2 * LANES), BF16),
                        pltpu.SemaphoreType.DMA((2,))],
        compiler_params=_cparams(("arbitrary",)),
        name="out_proj_router",
    )(y_ret, y_diff, x2, w_out, g.reshape(1, d), wr_pad, b_router.reshape(ne, 1))


def _route_kernel(lt_ref, pos_ref, gate_ref, cnt_ref, nt_ref, ot_ref, idx_s, rank_s, *, tb, row_tile):
    ne, t = lt_ref.shape
    e_iota = lax.broadcasted_iota(I32, (ne, tb), 0)
    upper = (lax.broadcasted_iota(I32, (tb, tb), 0)
             < lax.broadcasted_iota(I32, (tb, tb), 1)).astype(BF16)

    def pass_a(i, running):
        cols = pl.ds(pl.multiple_of(i * tb, tb), tb)
        l = lt_ref[:, cols]
        tops, hots = [], []
        for k in range(TOP_K):
            m = jnp.max(l, axis=0, keepdims=True)
            idx = jnp.min(jnp.where(l == m, e_iota, ne), axis=0, keepdims=True)
            hot = e_iota == idx
            l = jnp.where(hot, -jnp.inf, l)
            idx_s[k:k + 1, cols] = idx
            tops.append(m)
            hots.append(hot)
        exps = [jnp.exp(m - tops[0]) for m in tops]
        denom = exps[0] + exps[1] + exps[2] + exps[3]
        for k in range(TOP_K):
            gate_ref[k:k + 1, cols] = exps[k] / denom
        hot_all = jnp.zeros((ne, tb), F32)
        for hot in hots:
            hot_all = hot_all + hot.astype(F32)
        before = running + jnp.dot(hot_all.astype(BF16), upper, preferred_element_type=F32)
        for k in range(TOP_K):
            rank_s[k:k + 1, cols] = jnp.sum(jnp.where(hots[k], before, 0.0), axis=0, keepdims=True)
        return running + jnp.sum(hot_all, axis=1, keepdims=True)

    cnt = lax.fori_loop(0, t // tb, pass_a, jnp.zeros((ne, 1), F32))
    ntile = jnp.floor((cnt + (row_tile - 1.0)) * (1.0 / row_tile))
    lower = (lax.broadcasted_iota(I32, (ne, ne), 1)
             < lax.broadcasted_iota(I32, (ne, ne), 0)).astype(BF16)
    otile = jnp.dot(lower, jnp.broadcast_to(ntile, (ne, LANES)).astype(BF16),
                    preferred_element_type=F32)
    cnt_ref[...] = jnp.broadcast_to(cnt, (ne, LANES)).astype(I32)
    nt_ref[...] = jnp.broadcast_to(ntile, (ne, LANES)).astype(I32)
    ot_ref[...] = otile.astype(I32)
    off_rows = otile[:, 0:1] * float(row_tile)

    def pass_b(i, carry):
        cols = pl.ds(pl.multiple_of(i * tb, tb), tb)
        for k in range(TOP_K):
            hot = e_iota == idx_s[k:k + 1, cols]
            off = jnp.sum(jnp.where(hot, off_rows, 0.0), axis=0, keepdims=True)
            pos_ref[k:k + 1, cols] = (rank_s[k:k + 1, cols] + off).astype(I32)
        return carry

    lax.fori_loop(0, t // tb, pass_b, 0)


def _route_call(logits_t, tb=256, row_tile=ROW_TILE):
    ne, t = logits_t.shape
    return pl.pallas_call(
        functools.partial(_route_kernel, tb=tb, row_tile=row_tile),
        out_shape=[jax.ShapeDtypeStruct((TOP_K, t), I32),
                   jax.ShapeDtypeStruct((TOP_K, t), F32),
                   jax.ShapeDtypeStruct((ne, LANES), I32),
                   jax.ShapeDtypeStruct((ne, LANES), I32),
                   jax.ShapeDtypeStruct((ne, LANES), I32)],
        scratch_shapes=[pltpu.VMEM((TOP_K, t), I32), pltpu.VMEM((TOP_K, t), F32)],
        compiler_params=_cparams(None),
        name="route",
    )(logits_t)


_PAD_PIECES = (128, 64, 32, 16, 8)
_SUBLANES = 8


def _dispatch_kernel(pos_ref, pad_start_ref, pad_n_ref, src_ref, dst_ref, zero_s, sem, zsem,
                     *, tb, ne):
    i = pl.program_id(0)

    def pad_copies(e):
        n = pad_n_ref[e]
        head = n & (_SUBLANES - 1)
        start = pad_start_ref[e]
        out = []
        for r in range(_SUBLANES - 1):
            out.append((r < head, pltpu.make_async_copy(
                zero_s.at[pl.ds(0, 1)], dst_ref.at[pl.ds(start + r, 1)], zsem)))
        body = n - head
        for piece in _PAD_PIECES:
            at = pl.multiple_of(start + head + (body & ~(2 * piece - 1)), _SUBLANES)
            out.append(((body & piece) != 0, pltpu.make_async_copy(
                zero_s.at[pl.ds(0, piece)], dst_ref.at[pl.ds(at, piece)], zsem)))
        return out

    @pl.when(i == 0)
    def _zero_pads():
        zero_s[...] = jnp.zeros(zero_s.shape, zero_s.dtype)

        def issue(e, c):
            for cond, cp in pad_copies(e):
                @pl.when(cond)
                def _():
                    cp.start()
            return c

        def drain(e, c):
            for cond, cp in pad_copies(e):
                @pl.when(cond)
                def _():
                    cp.wait()
            return c

        lax.fori_loop(0, ne, issue, 0)
        lax.fori_loop(0, ne, drain, 0)

    def issue_rows(tl, c):
        for k in range(TOP_K):
            pltpu.make_async_copy(src_ref.at[pl.ds(tl, 1)], dst_ref.at[pl.ds(pos_ref[k, tl], 1)],
                                  sem).start()
        return c

    lax.fori_loop(0, tb, issue_rows, 0, unroll=4)
    for k in range(TOP_K):
        pltpu.make_async_copy(src_ref, dst_ref.at[pl.ds(0, tb)], sem).wait()


def _dispatch_call(pos, pad_start, pad_n, h2p, n_slots, tb=512):
    t, w = h2p.shape
    ne = pad_n.shape[0]
    return pl.pallas_call(
        functools.partial(_dispatch_kernel, tb=tb, ne=ne),
        grid=(t // tb,),
        in_specs=[pl.BlockSpec((TOP_K, tb), lambda i: (0, i), memory_space=pltpu.SMEM),
                  pl.BlockSpec(memory_space=pltpu.SMEM),
                  pl.BlockSpec(memory_space=pltpu.SMEM),
                  pl.BlockSpec((tb, w), lambda i: (i, 0))],
        out_specs=pl.BlockSpec(memory_space=pl.ANY),
        out_shape=jax.ShapeDtypeStruct((n_slots, w), U32),
        scratch_shapes=[pltpu.VMEM((_PAD_PIECES[0], w), U32),
                        pltpu.SemaphoreType.DMA(()),
                        pltpu.SemaphoreType.DMA(())],
        compiler_params=pltpu.CompilerParams(dimension_semantics=("arbitrary",),
                                             vmem_limit_bytes=VMEM_LIMIT, has_side_effects=True),
        name="dispatch",
    )(pos, pad_start, pad_n, h2p)


def _moe_kernel(we_ref, wt_ref, wn_ref, xs_ref, win_ref, bin_ref, wout_ref, bo_ref,
                ys_ref, xu_s, acc_s, wg_st, wu_st, wo_st, wgu_s, wo_s, yst_s, xsem, wsem, ysem,
                *, tmx, fc, nj):
    w = pl.program_id(0)
    nw = pl.num_programs(0)
    nt = wn_ref[w]
    t0 = wt_ref[w]
    half = xu_s.shape[-1]
    de = nj * fc
    nxt = jnp.minimum(w + 1, nw - 1)
    has_next = jnp.logical_and(w + 1 < nw, wn_ref[nxt] > 0)
    xslot = w % 2

    def local_rows(r, n=1):
        return pl.ds(pl.multiple_of(r * tmx, tmx), n * tmx)

    def x_copy(item_t0, r, slot):
        src = pl.ds(pl.multiple_of((item_t0 + r) * tmx, tmx), tmx)
        return pltpu.make_async_copy(xs_ref.at[src], xu_s.at[slot, local_rows(r)], xsem.at[slot])

    def y_copy(r, slot):
        dst = pl.ds(pl.multiple_of((t0 + r) * tmx, tmx), tmx)
        return pltpu.make_async_copy(yst_s.at[slot], ys_ref.at[dst], ysem.at[slot])

    def w_copies(expert, j, slot):
        c0 = pl.multiple_of(j * fc, fc)
        return (pltpu.make_async_copy(win_ref.at[expert, :, pl.ds(c0, fc)], wg_st.at[slot], wsem.at[slot]),
                pltpu.make_async_copy(win_ref.at[expert, :, pl.ds(de + c0, fc)], wu_st.at[slot],
                                      wsem.at[slot]),
                pltpu.make_async_copy(wout_ref.at[expert, pl.ds(c0, fc), :], wo_st.at[slot],
                                      wsem.at[slot]))

    def start_rows(item_t0, item_nt, slot):
        def go(r, c):
            x_copy(item_t0, r, slot).start()
            return c
        lax.fori_loop(0, item_nt, go, 0)

    @pl.when(w == 0)
    def _prologue():
        for cp in w_copies(we_ref[0], 0, 0):
            cp.start()
        start_rows(t0, nt, 0)

    @pl.when(nt > 0)
    def _work():
        def chunk(j, carry):
            slot = j % 2

            @pl.when(j + 1 < nj)
            def _():
                for cp in w_copies(we_ref[w], j + 1, 1 - slot):
                    cp.start()

            @pl.when(jnp.logical_and(j + 1 == nj, has_next))
            def _():
                for cp in w_copies(we_ref[nxt], 0, 1 - slot):
                    cp.start()

            for cp in w_copies(we_ref[w], j, slot):
                cp.wait()
            wgu_s[:, 0:fc] = wg_st[slot].astype(BF16)
            wgu_s[:, fc:2 * fc] = wu_st[slot].astype(BF16)
            wo_s[...] = wo_st[slot].astype(BF16)
            bias_gu = jnp.concatenate([bin_ref[j], bin_ref[nj + j]], axis=-1)

            @pl.when(j == 0)
            def _rows_ready():
                def finish(r, c):
                    x_copy(t0, r, xslot).wait()
                    acc_s[local_rows(r), :] = jnp.zeros((tmx, acc_s.shape[1]), F32)
                    return c
                lax.fori_loop(0, nt, finish, 0)

                @pl.when(has_next)
                def _():
                    start_rows(wt_ref[nxt], wn_ref[nxt], 1 - xslot)

            _moe_passes(xu_s.at[xslot], acc_s, wgu_s, wo_s, bias_gu, nt, local_rows, fc, half)
            return carry

        lax.fori_loop(0, nj, chunk, 0)

        def y_wait(slot):
            y_copy(0, slot).wait()

        prev_nt = jnp.where(w > 0, wn_ref[jnp.maximum(w - 1, 0)], 0)

        @pl.when(prev_nt >= 1)
        def _():
            y_wait((prev_nt - 1) % 2)

        @pl.when(prev_nt >= 2)
        def _():
            y_wait(prev_nt % 2)

        def emit(r, c):
            slot = r % 2

            @pl.when(r >= 2)
            def _():
                y_wait(slot)

            y = acc_s[local_rows(r), :] + bo_ref[...]
            yst_s[slot] = pltpu.pack_elementwise([y[:, :half], y[:, half:]], packed_dtype=BF16)
            y_copy(r, slot).start()
            return c

        lax.fori_loop(0, nt, emit, 0)

        @pl.when(jnp.logical_not(has_next))
        def _drain():
            @pl.when(nt >= 2)
            def _():
                y_wait(nt % 2)

            y_wait((nt - 1) % 2)


def _moe_passes(xu_ref, acc_s, wgu_s, wo_s, bias_gu, nt, local_rows, fc, half):
    def rows_step(rows):
        xw = xu_ref[rows, :]
        x_lo = pltpu.unpack_elementwise(xw, index=0, packed_dtype=BF16,
                                        unpacked_dtype=F32).astype(BF16)
        x_hi = pltpu.unpack_elementwise(xw, index=1, packed_dtype=BF16,
                                        unpacked_dtype=F32).astype(BF16)
        hh = (jnp.dot(x_lo, wgu_s[0:half, :], preferred_element_type=F32)
              + jnp.dot(x_hi, wgu_s[half:2 * half, :], preferred_element_type=F32) + bias_gu)
        gg = jnp.minimum(hh[:, 0:fc], SWIGLU_LIMIT)
        uu = jnp.clip(hh[:, fc:2 * fc], -SWIGLU_LIMIT, SWIGLU_LIMIT)
        act = (uu + 1.0) * (gg * jax.nn.sigmoid(SWIGLU_ALPHA * gg))
        acc_s[rows, :] += jnp.dot(act.astype(BF16), wo_s[...], preferred_element_type=F32)

    def quad(qi, c):
        rows_step(local_rows(4 * qi, 4))
        return c

    lax.fori_loop(0, nt // 4, quad, 0)

    @pl.when((nt & 2) != 0)
    def _():
        rows_step(local_rows(nt & ~3, 2))

    @pl.when((nt & 1) != 0)
    def _():
        rows_step(local_rows(nt - 1))


def _moe_call(work_e, work_t0, work_nt, xs, w_in, b_in, w_out, b_out,
              tmx=ROW_TILE, group=GROUP_TILES, fc=F_CHUNK):
    ne, d, de2 = w_in.shape
    de = de2 // 2
    nj = de // fc
    assert nj % 2 == 0, "weight stage slots alternate per chunk and restart at 0 per work item"
    n_work = work_e.shape[0]
    n_slots, half = xs.shape

    grid_spec = pltpu.PrefetchScalarGridSpec(
        num_scalar_prefetch=3,
        grid=(n_work,),
        in_specs=[
            pl.BlockSpec(memory_space=pl.ANY),
            pl.BlockSpec(memory_space=pl.ANY),
            pl.BlockSpec((None, 2 * nj, 1, fc), lambda w, we, wt, wn: (we[w], 0, 0, 0)),
            pl.BlockSpec(memory_space=pl.ANY),
            pl.BlockSpec((None, 1, d), lambda w, we, wt, wn: (we[w], 0, 0)),
        ],
        out_specs=pl.BlockSpec(memory_space=pl.ANY),
        scratch_shapes=[pltpu.VMEM((2, group * tmx, half), U32),
                        pltpu.VMEM((group * tmx, d), F32),
                        pltpu.VMEM((2, d, fc), F32),
                        pltpu.VMEM((2, d, fc), F32),
                        pltpu.VMEM((2, fc, d), F32),
                        pltpu.VMEM((d, 2 * fc), BF16),
                        pltpu.VMEM((fc, d), BF16),
                        pltpu.VMEM((2, tmx, half), U32),
                        pltpu.SemaphoreType.DMA((2,)),
                        pltpu.SemaphoreType.DMA((2,)),
                        pltpu.SemaphoreType.DMA((2,))],
    )
    return pl.pallas_call(
        functools.partial(_moe_kernel, tmx=tmx, fc=fc, nj=nj),
        grid_spec=grid_spec,
        out_shape=jax.ShapeDtypeStruct((n_slots, half), U32),
        compiler_params=pltpu.CompilerParams(dimension_semantics=("arbitrary",),
                                             vmem_limit_bytes=VMEM_LIMIT, has_side_effects=True),
        name="moe_experts",
    )(work_e, work_t0, work_nt, xs, w_in, b_in.reshape(ne, 2 * nj, 1, fc), w_out,
      b_out.reshape(ne, 1, d))


def _combine_kernel(pos_ref, x1_ref, gate_ref, ys_ref, o_ref, ybuf, sem, *, tb):
    half = ybuf.shape[-1]

    def issue(tl, c):
        for k in range(TOP_K):
            pltpu.make_async_copy(ys_ref.at[pl.ds(pos_ref[k, tl], 1)], ybuf.at[k, pl.ds(tl, 1)],
                                  sem).start()
        return c

    lax.fori_loop(0, tb, issue, 0, unroll=4)
    for k in range(TOP_K):
        pltpu.make_async_copy(ys_ref.at[pl.ds(0, tb)], ybuf.at[k], sem).wait()

    lo = x1_ref[:, 0:half]
    hi = x1_ref[:, half:2 * half]
    for k in range(TOP_K):
        g = gate_ref[:, k:k + 1]
        yw = ybuf[k]
        lo = lo + g * pltpu.unpack_elementwise(yw, index=0, packed_dtype=BF16, unpacked_dtype=F32)
        hi = hi + g * pltpu.unpack_elementwise(yw, index=1, packed_dtype=BF16, unpacked_dtype=F32)
    o_ref[:, 0:half] = lo
    o_ref[:, half:2 * half] = hi


def _combine_call(pos, x1, gate_tk, ys, tb=256):
    t, d = x1.shape
    half = ys.shape[1]
    return pl.pallas_call(
        functools.partial(_combine_kernel, tb=tb),
        grid=(t // tb,),
        in_specs=[pl.BlockSpec((TOP_K, tb), lambda i: (0, i), memory_space=pltpu.SMEM),
                  pl.BlockSpec((tb, d), lambda i: (i, 0)),
                  pl.BlockSpec((tb, TOP_K), lambda i: (i, 0)),
                  pl.BlockSpec(memory_space=pl.ANY)],
        out_specs=pl.BlockSpec((tb, d), lambda i: (i, 0)),
        out_shape=jax.ShapeDtypeStruct((t, d), F32),
        scratch_shapes=[pltpu.VMEM((TOP_K, tb, half), U32),
                        pltpu.SemaphoreType.DMA(())],
        compiler_params=_cparams(("arbitrary",)),
        name="combine",
    )(pos, x1, gate_tk, ys)


def _work_list(ntile, otile, n_work, group):
    ne = ntile.shape[0]
    items = (ntile + group - 1) // group
    ends = jnp.cumsum(items)
    total = ends[-1]
    w = jnp.arange(n_work, dtype=I32)
    valid = w < total
    e_w = jnp.clip(jnp.searchsorted(ends, w, side="right").astype(I32), 0, ne - 1)
    local = w - (ends - items)[e_w]
    t0_w = otile[e_w] + local * group
    nt_w = jnp.where(valid, jnp.clip(ntile[e_w] - local * group, 0, group), 0)
    e_last = e_w[jnp.maximum(total - 1, 0)]
    return (jnp.where(valid, e_w, e_last).astype(I32), jnp.where(valid, t0_w, 0).astype(I32),
            nt_w.astype(I32))


def _rotary_tables(seq, dk):
    inv_freq = ROPE_BASE ** (-jnp.arange(0, dk, 2, dtype=F32) / dk)
    ang = jnp.arange(seq, dtype=I32).astype(F32)[:, None] * inv_freq[None, :]
    cos = jnp.repeat(jnp.cos(ang), 2, axis=1)
    sin = jnp.stack([-jnp.sin(ang), jnp.sin(ang)], axis=-1).reshape(seq, dk)
    return cos, sin


def kernel(x, norm_mix_g, w_mix_in, ret_gn_g, q_norm_g, k_norm_g, lambda_q1, lambda_k1, lambda_q2, lambda_k2, diff_subln_g, rel_bias_table, w_mix_out, norm_ffn_g, w_router, b_router, w_exp_in, b_exp_in, w_exp_out, b_exp_out):
    batch, seq, d = x.shape
    t = batch * seq
    depth = norm_mix_g.shape[0]
    ne = w_router.shape[-1]
    n_tiles_max = (t * TOP_K) // ROW_TILE + ne
    n_work = ne + (n_tiles_max - ne) // GROUP_TILES
    cos, sin = _rotary_tables(seq, HEAD_W)
    log_gamma = jnp.log1p(-(2.0 ** (-5.0 - jnp.arange(RET_HEADS, dtype=F32))))

    x2 = x.reshape(t, d)
    for l in range(depth):
        h = _rmsnorm_call(x2, norm_mix_g[l])
        proj = _inproj_call(h, w_mix_in[l])
        y_ret = _ret_call(proj, cos, sin, log_gamma, ret_gn_g[l], batch, seq)
        lam_vecs = jnp.stack([lambda_q1[l], lambda_k1[l], lambda_q2[l], lambda_k2[l]])
        y_diff = _diff_call(proj, rel_bias_table, q_norm_g[l], k_norm_g[l], lam_vecs,
                            diff_subln_g[l], batch, seq)
        x1, h2p, logits_t = _outproj_call(y_ret, y_diff, x2, w_mix_out[l], norm_ffn_g[l],
                                          w_router[l], b_router[l])
        pos, gate, cnt, ntile, otile = _route_call(logits_t)
        cnt, ntile, otile = cnt[:, 0], ntile[:, 0], otile[:, 0]
        work_e, work_t0, work_nt = _work_list(ntile, otile, n_work, GROUP_TILES)
        xs = _dispatch_call(pos, otile * ROW_TILE + cnt, ntile * ROW_TILE - cnt, h2p,
                            n_tiles_max * ROW_TILE)
        ys = _moe_call(work_e, work_t0, work_nt, xs, w_exp_in[l], b_exp_in[l], w_exp_out[l],
                       b_exp_out[l])
        x2 = _combine_call(pos, x1, gate.T, ys)
    return x2.reshape(batch, seq, d)
```

```python
import functools
import math

import jax
import jax.numpy as jnp
from jax import lax
from jax.experimental import pallas as pl
from jax.experimental.pallas import tpu as pltpu

F32 = jnp.float32
BF16 = jnp.bfloat16
I32 = jnp.int32
U32 = jnp.uint32

EPS = 1e-6
CHUNK = 64
RET_HEADS = 8
DIFF_HEADS = 8
HEAD_W = 128
DIFF_DK = 64
ROPE_BASE = 10000.0
NUM_BUCKETS = 32
MAX_DISTANCE = 128
TOP_K = 4
SWIGLU_LIMIT = 7.0
SWIGLU_ALPHA = 1.702
LAM_INIT = 0.8 - 0.6 * math.exp(-0.3 * 0)

LANES = 128
NEG_BIG = -1e30
VMEM_LIMIT = 56 * 1024 * 1024

SEQ_BLK = 256
ROW_TILE = 256
GROUP_TILES = 8
F_CHUNK = 256


def _cparams(sem, vmem=VMEM_LIMIT):
    return pltpu.CompilerParams(dimension_semantics=sem, vmem_limit_bytes=vmem)


def _rmsnorm_kernel(x_ref, g_ref, o_ref):
    x = x_ref[...]
    ms = jnp.mean(x * x, axis=-1, keepdims=True)
    o_ref[...] = (x * lax.rsqrt(ms + EPS) * g_ref[...]).astype(o_ref.dtype)


def _rmsnorm_call(x2, g, tm=512):
    t, d = x2.shape
    return pl.pallas_call(
        _rmsnorm_kernel,
        grid=(t // tm,),
        in_specs=[pl.BlockSpec((tm, d), lambda i: (i, 0)),
                  pl.BlockSpec((1, d), lambda i: (0, 0))],
        out_specs=pl.BlockSpec((tm, d), lambda i: (i, 0)),
        out_shape=jax.ShapeDtypeStruct((t, d), BF16),
        compiler_params=_cparams(("arbitrary",)),
        name="rmsnorm_in",
    )(x2, g.reshape(1, d))


def _inproj_kernel(h_ref, w_ref, o_ref, wb_ref, *, rows_per_cast):
    @pl.when(pl.program_id(1) == 0)
    def _():
        d = w_ref.shape[0]
        for c in range(d // rows_per_cast):
            sl = slice(c * rows_per_cast, (c + 1) * rows_per_cast)
            wb_ref[sl, :] = w_ref[sl, :].astype(BF16)

    acc = jnp.dot(h_ref[...], wb_ref[...], preferred_element_type=F32)
    for j in range(o_ref.shape[0]):
        o_ref[j] = acc[:, j * LANES:(j + 1) * LANES].astype(o_ref.dtype)


def _inproj_call(h, w, tm=1024, tn=1024):
    t, d = h.shape
    n = w.shape[1]
    return pl.pallas_call(
        functools.partial(_inproj_kernel, rows_per_cast=min(256, d)),
        grid=(n // tn, t // tm),
        in_specs=[pl.BlockSpec((tm, d), lambda j, i: (i, 0)),
                  pl.BlockSpec((d, tn), lambda j, i: (0, j))],
        out_specs=pl.BlockSpec((tn // LANES, tm, LANES), lambda j, i: (j, i, 0)),
        out_shape=jax.ShapeDtypeStruct((n // LANES, t, LANES), BF16),
        scratch_shapes=[pltpu.VMEM((d, tn), BF16)],
        compiler_params=_cparams(("arbitrary", "arbitrary")),
        name="in_proj",
    )(h, w)


def _ret_kernel(lg_ref, q_ref, k_ref, v_ref, g_ref, cos_ref, sin_ref, gn_ref, o_ref, *, blk, nblk):
    dk = q_ref.shape[-1]
    lg = lg_ref[pl.program_id(1)]
    row = lax.broadcasted_iota(I32, (blk, blk), 0)
    col = lax.broadcasted_iota(I32, (blk, blk), 1)
    dist = jnp.abs(row - col).astype(F32)
    visible = (col // CHUNK) <= (row // CHUNK)
    dmask = jnp.where(visible, jnp.exp(lg * dist), 0.0)
    rr = lax.broadcasted_iota(I32, (blk, dk), 0).astype(F32)
    qdec = jnp.exp(lg * (rr + 1.0))
    kdec = jnp.exp(lg * (blk - 1.0 - rr))
    bdec = jnp.exp(lg * jnp.full((1, HEAD_W), float(blk), F32))
    even = (lax.broadcasted_iota(I32, (blk, dk), 1) & 1) == 0
    scale = dk ** -0.5

    def body(i, state):
        rows = pl.ds(pl.multiple_of(i * blk, blk), blk)
        cos = cos_ref[rows, :]
        sin = sin_ref[rows, :]

        def rot(x):
            partner = jnp.where(even, pltpu.roll(x, dk - 1, 1), pltpu.roll(x, 1, 1))
            return x * cos + partner * sin

        qr = rot(q_ref[rows, :].astype(F32)) * scale
        kr = rot(k_ref[rows, :].astype(F32))
        v = v_ref[rows, :]
        s = lax.dot_general(qr.astype(BF16), kr.astype(BF16), (((1,), (1,)), ((), ())),
                            preferred_element_type=F32) * dmask
        out = jnp.dot(s.astype(BF16), v, preferred_element_type=F32)
        out = out + jnp.dot((qr * qdec).astype(BF16), state.astype(BF16),
                            preferred_element_type=F32)
        kv = lax.dot_general((kr * kdec).astype(BF16), v, (((0,), (0,)), ((), ())),
                             preferred_element_type=F32)
        state = state * bdec + kv
        ms = jnp.mean(out * out, axis=-1, keepdims=True)
        normed = out * lax.rsqrt(ms + EPS) * gn_ref[...]
        g = g_ref[rows, :].astype(F32)
        o_ref[rows, :] = (g * jax.nn.sigmoid(g) * normed).astype(o_ref.dtype)
        return state

    lax.fori_loop(0, nblk, body, jnp.zeros((dk, HEAD_W), F32), unroll=True)


def _ret_call(proj, cos, sin, log_gamma, gn_g, batch, seq, blk=SEQ_BLK):
    nh = RET_HEADS

    def head_spec(base):
        return pl.BlockSpec((None, seq, LANES), lambda b, h: (base + h, b, 0))

    return pl.pallas_call(
        functools.partial(_ret_kernel, blk=blk, nblk=seq // blk),
        grid=(batch, nh),
        in_specs=[pl.BlockSpec(memory_space=pltpu.SMEM),
                  head_spec(0), head_spec(nh), head_spec(2 * nh), head_spec(3 * nh),
                  pl.BlockSpec((seq, LANES), lambda b, h: (0, 0)),
                  pl.BlockSpec((seq, LANES), lambda b, h: (0, 0)),
                  pl.BlockSpec((None, 1, LANES), lambda b, h: (h, 0, 0))],
        out_specs=pl.BlockSpec((None, seq, LANES), lambda b, h: (h, b, 0)),
        out_shape=jax.ShapeDtypeStruct((nh, batch * seq, LANES), BF16),
        compiler_params=_cparams(("arbitrary", "arbitrary")),
        name="retention",
    )(log_gamma, proj, proj, proj, proj, cos, sin, gn_g.reshape(nh, 1, LANES))


def _diff_kernel(tbl_ref, q_ref, k_ref, v_ref, qg_ref, kg_ref, lam_ref, sg_ref, bidx_ref, o_ref,
                 qz_s, kn_s, bias_s, s_s, p_s, *, blk, nblk):
    h = pl.program_id(0)
    b = pl.program_id(1)
    far_bucket = NUM_BUCKETS // 2 - 1

    @pl.when(b == 0)
    def _build_bias():
        row = lax.broadcasted_iota(I32, (blk, blk), 0)
        col = lax.broadcasted_iota(I32, (blk, blk), 1)
        visible = (col // CHUNK) <= (row // CHUNK)
        for d in range(2):
            idx = bidx_ref[d]
            bias = jnp.zeros((blk, blk), F32)
            for bucket in range(NUM_BUCKETS):
                bias = jnp.where(idx == bucket, tbl_ref[bucket, h], bias)
            if d == 0:
                bias = jnp.where(visible, bias, NEG_BIG)
            bias_s[d, 0:blk, :] = bias
            bias_s[d, blk:2 * blk, :] = bias

    lo = lax.broadcasted_iota(I32, (blk, HEAD_W), 1) < DIFF_DK
    scale = DIFF_DK ** -0.5
    same_half = ((lax.broadcasted_iota(I32, (HEAD_W, HEAD_W), 0) // DIFF_DK)
                 == (lax.broadcasted_iota(I32, (HEAD_W, HEAD_W), 1) // DIFF_DK)).astype(BF16)

    def half_norm(x, g):
        ms = jnp.dot((x * x).astype(BF16), same_half, preferred_element_type=F32) * (1.0 / DIFF_DK)
        return x * lax.rsqrt(ms + EPS) * g

    def prep(i, carry):
        rows = pl.ds(pl.multiple_of(i * blk, blk), blk)
        qn = half_norm(q_ref[rows, :].astype(F32), qg_ref[...]) * scale
        kn = half_norm(k_ref[rows, :].astype(F32), kg_ref[...])
        base = pl.multiple_of(i * 2 * blk, 2 * blk)
        qz_s[pl.ds(base, blk), :] = jnp.where(lo, qn, 0.0).astype(BF16)
        qz_s[pl.ds(base + blk, blk), :] = jnp.where(lo, 0.0, qn).astype(BF16)
        kn_s[rows, :] = kn.astype(BF16)
        return carry

    lax.fori_loop(0, nblk, prep, 0)

    lam = (jnp.exp(jnp.sum(lam_ref[0:1, :] * lam_ref[1:2, :], axis=-1, keepdims=True))
           - jnp.exp(jnp.sum(lam_ref[2:3, :] * lam_ref[3:4, :], axis=-1, keepdims=True))
           + LAM_INIT)
    c_far = tbl_ref[far_bucket, h]

    def lane_fold(x, op):
        out = x[:, 0:LANES]
        for c in range(1, blk // LANES):
            out = op(out, x[:, c * LANES:(c + 1) * LANES])
        return out

    for i in range(nblk):
        qz = qz_s[i * 2 * blk:(i + 1) * 2 * blk, :]
        m_t = jnp.full((2 * blk, LANES), NEG_BIG, F32)
        for j in range(i + 1):
            keys = slice(j * blk, (j + 1) * blk)
            s = lax.dot_general(qz, kn_s[keys, :], (((1,), (1,)), ((), ())),
                                preferred_element_type=F32)
            s = s + (bias_s[0] if j == i else bias_s[1] if j == i - 1 else c_far)
            s_s[:, keys] = s
            m_t = jnp.maximum(m_t, lane_fold(s, jnp.maximum))
        m = jnp.max(m_t, axis=-1, keepdims=True)
        l_t = jnp.zeros((2 * blk, LANES), F32)
        for j in range(i + 1):
            keys = slice(j * blk, (j + 1) * blk)
            p = jnp.exp(s_s[:, keys] - m)
            l_t = l_t + lane_fold(p, jnp.add)
            p_s[:, keys] = p.astype(BF16)
        l = jnp.sum(l_t, axis=-1, keepdims=True)
        kend = (i + 1) * blk
        o = jnp.dot(p_s[:, 0:kend], v_ref[0:kend, :], preferred_element_type=F32) / l
        att = o[0:blk, :] - lam * o[blk:2 * blk, :]
        ms = jnp.mean(att * att, axis=-1, keepdims=True)
        y = att * lax.rsqrt(ms + EPS) * sg_ref[...] * (1.0 - LAM_INIT)
        o_ref[i * blk:(i + 1) * blk, :] = y.astype(o_ref.dtype)


def _t5_bucket(rel):
    nb = NUM_BUCKETS // 2
    max_exact = nb // 2
    base = jnp.where(rel > 0, nb, 0)
    n = jnp.abs(rel)
    large = max_exact + (jnp.log(jnp.maximum(n, 1).astype(jnp.float32) / max_exact)
                         / math.log(MAX_DISTANCE / max_exact) * (nb - max_exact)).astype(jnp.int32)
    large = jnp.minimum(large, nb - 1)
    return base + jnp.where(n < max_exact, n, large)


def _diff_call(proj, rel_table, qg, kg, lam_vecs, sg, batch, seq, blk=SEQ_BLK):
    nh = DIFF_HEADS
    first = 4 * RET_HEADS
    r = jnp.arange(blk, dtype=I32)
    rel0 = r[None, :] - r[:, None]
    bidx = jnp.stack([_t5_bucket(rel0), _t5_bucket(rel0 - blk)]).astype(I32) & (NUM_BUCKETS - 1)

    def head_spec(base):
        return pl.BlockSpec((None, seq, LANES), lambda h, b: (base + h, b, 0))

    def vec_spec():
        return pl.BlockSpec((1, LANES), lambda h, b: (0, 0))

    return pl.pallas_call(
        functools.partial(_diff_kernel, blk=blk, nblk=seq // blk),
        grid=(nh, batch),
        in_specs=[pl.BlockSpec(memory_space=pltpu.SMEM),
                  head_spec(first), head_spec(first + nh), head_spec(first + 2 * nh),
                  vec_spec(), vec_spec(),
                  pl.BlockSpec((4, DIFF_DK), lambda h, b: (0, 0)),
                  vec_spec(),
                  pl.BlockSpec((2, blk, blk), lambda h, b: (0, 0, 0))],
        out_specs=pl.BlockSpec((None, seq, LANES), lambda h, b: (h, b, 0)),
        out_shape=jax.ShapeDtypeStruct((nh, batch * seq, LANES), BF16),
        scratch_shapes=[pltpu.VMEM((2 * seq, LANES), BF16),
                        pltpu.VMEM((seq, LANES), BF16),
                        pltpu.VMEM((2, 2 * blk, blk), F32),
                        pltpu.VMEM((2 * blk, seq), F32),
                        pltpu.VMEM((2 * blk, seq), BF16)],
        compiler_params=_cparams(("arbitrary", "arbitrary")),
        name="diff_attn",
    )(rel_table, proj, proj, proj,
      jnp.tile(qg, 2).reshape(1, LANES), jnp.tile(kg, 2).reshape(1, LANES),
      lam_vecs, sg.reshape(1, LANES), bidx)


def _outproj_kernel(yr_ref, yd_ref, x_ref, w_hbm, g_ref, wr_ref, br_ref,
                    x1_ref, h2p_ref, lt_ref, wb_s, wstage_s, wr2_s, wsem,
                    *, rows_per_cast):
    ne = lt_ref.shape[0]
    nchunk = wb_s.shape[0] // rows_per_cast

    def w_copy(c):
        return pltpu.make_async_copy(w_hbm.at[pl.ds(c * rows_per_cast, rows_per_cast)],
                                     wstage_s.at[c % 2], wsem.at[c % 2])

    @pl.when(pl.program_id(0) == 0)
    def _stage_weights():
        w_copy(0).start()
        for c in range(nchunk):
            if c + 1 < nchunk:
                w_copy(c + 1).start()
            w_copy(c).wait()
            wb_s[c * rows_per_cast:(c + 1) * rows_per_cast, :] = wstage_s[c % 2].astype(BF16)
        wr = wr_ref[...]
        wr_hi = wr.astype(BF16)
        wr2_s[:, 0:LANES] = wr_hi
        wr2_s[:, LANES:2 * LANES] = (wr - wr_hi.astype(F32)).astype(BF16)

    y = jnp.concatenate([yr_ref[j] for j in range(yr_ref.shape[0])]
                        + [yd_ref[j] for j in range(yd_ref.shape[0])], axis=-1)
    x1 = x_ref[...] + jnp.dot(y, wb_s[...], preferred_element_type=F32)
    x1_ref[...] = x1
    ms = jnp.mean(x1 * x1, axis=-1, keepdims=True)
    h2 = x1 * lax.rsqrt(ms + EPS) * g_ref[...]
    h_hi = h2.astype(BF16)
    h_lo = (h2 - h_hi.astype(F32)).astype(BF16)
    parts = (jnp.dot(h_hi, wr2_s[...], preferred_element_type=F32)
             + jnp.dot(h_lo, wr2_s[...], preferred_element_type=F32))
    logits = parts[:, 0:LANES] + parts[:, LANES:2 * LANES]
    lt_ref[...] = logits.T[0:ne, :] + br_ref[...]
    half = h2.shape[1] // 2
    packed = pltpu.pack_elementwise([h2[:, :half], h2[:, half:]], packed_dtype=BF16)
    h2p_ref[...] = packed.reshape(h2p_ref.shape)


def _outproj_call(y_ret, y_diff, x2, w_out, g, w_router, b_router, tm=512):
    t, d = x2.shape
    ne = w_router.shape[1]
    nhr, nhd = y_ret.shape[0], y_diff.shape[0]
    rows_per_cast = min(256, d)
    wr_pad = jnp.pad(w_router, ((0, 0), (0, LANES - ne)))
    return pl.pallas_call(
        functools.partial(_outproj_kernel, rows_per_cast=rows_per_cast),
        grid=(t // tm,),
        in_specs=[pl.BlockSpec((nhr, tm, LANES), lambda i: (0, i, 0)),
                  pl.BlockSpec((nhd, tm, LANES), lambda i: (0, i, 0)),
                  pl.BlockSpec((tm, d), lambda i: (i, 0)),
                  pl.BlockSpec(memory_space=pl.ANY),
                  pl.BlockSpec((1, d), lambda i: (0, 0)),
                  pl.BlockSpec((d, LANES), lambda i: (0, 0)),
                  pl.BlockSpec((ne, 1), lambda i: (0, 0))],
        out_specs=[pl.BlockSpec((tm, d), lambda i: (i, 0)),
                   pl.BlockSpec((tm, d // 2 // LANES, LANES), lambda i: (i, 0, 0)),
                   pl.BlockSpec((ne, tm), lambda i: (0, i))],
        out_shape=[jax.ShapeDtypeStruct((t, d), F32),
                   jax.ShapeDtypeStruct((t, d // 2 // LANES, LANES), U32),
                   jax.ShapeDtypeStruct((ne, t), F32)],
        scratch_shapes=[pltpu.VMEM((d, d), BF16),
                        pltpu.VMEM((2, rows_per_cast, d), F32),
                        pltpu.VMEM((d, 2 * LANES), BF16),
                        pltpu.SemaphoreType.DMA((2,))],
        compiler_params=_cparams(("arbitrary",)),
        name="out_proj_router",
    )(y_ret, y_diff, x2, w_out, g.reshape(1, d), wr_pad, b_router.reshape(ne, 1))


def _route_kernel(lt_ref, pos_ref, gate_ref, cnt_ref, nt_ref, ot_ref, idx_s, rank_s, *, tb, row_tile):
    ne, t = lt_ref.shape
    e_iota = lax.broadcasted_iota(I32, (ne, tb), 0)
    upper = (lax.broadcasted_iota(I32, (tb, tb), 0)
             < lax.broadcasted_iota(I32, (tb, tb), 1)).astype(BF16)

    def pass_a(i, running):
        cols = pl.ds(pl.multiple_of(i * tb, tb), tb)
        l = lt_ref[:, cols]
        tops, hots = [], []
        for k in range(TOP_K):
            m = jnp.max(l, axis=0, keepdims=True)
            idx = jnp.min(jnp.where(l == m, e_iota, ne), axis=0, keepdims=True)
            hot = e_iota == idx
            l = jnp.where(hot, -jnp.inf, l)
            idx_s[k:k + 1, cols] = idx
            tops.append(m)
            hots.append(hot)
        exps = [jnp.exp(m - tops[0]) for m in tops]
        denom = exps[0] + exps[1] + exps[2] + exps[3]
        for k in range(TOP_K):
            gate_ref[k:k + 1, cols] = exps[k] / denom
        hot_all = jnp.zeros((ne, tb), F32)
        for hot in hots:
            hot_all = hot_all + hot.astype(F32)
        before = running + jnp.dot(hot_all.astype(BF16), upper, preferred_element_type=F32)
        for k in range(TOP_K):
            rank_s[k:k + 1, cols] = jnp.sum(jnp.where(hots[k], before, 0.0), axis=0, keepdims=True)
        return running + jnp.sum(hot_all, axis=1, keepdims=True)

    cnt = lax.fori_loop(0, t // tb, pass_a, jnp.zeros((ne, 1), F32))
    ntile = jnp.floor((cnt + (row_tile - 1.0)) * (1.0 / row_tile))
    lower = (lax.broadcasted_iota(I32, (ne, ne), 1)
             < lax.broadcasted_iota(I32, (ne, ne), 0)).astype(BF16)
    otile = jnp.dot(lower, jnp.broadcast_to(ntile, (ne, LANES)).astype(BF16),
                    preferred_element_type=F32)
    cnt_ref[...] = jnp.broadcast_to(cnt, (ne, LANES)).astype(I32)
    nt_ref[...] = jnp.broadcast_to(ntile, (ne, LANES)).astype(I32)
    ot_ref[...] = otile.astype(I32)
    off_rows = otile[:, 0:1] * float(row_tile)

    def pass_b(i, carry):
        cols = pl.ds(pl.multiple_of(i * tb, tb), tb)
        for k in range(TOP_K):
            hot = e_iota == idx_s[k:k + 1, cols]
            off = jnp.sum(jnp.where(hot, off_rows, 0.0), axis=0, keepdims=True)
            pos_ref[k:k + 1, cols] = (rank_s[k:k + 1, cols] + off).astype(I32)
        return carry

    lax.fori_loop(0, t // tb, pass_b, 0)


def _route_call(logits_t, tb=256, row_tile=ROW_TILE):
    ne, t = logits_t.shape
    return pl.pallas_call(
        functools.partial(_route_kernel, tb=tb, row_tile=row_tile),
        out_shape=[jax.ShapeDtypeStruct((TOP_K, t), I32),
                   jax.ShapeDtypeStruct((TOP_K, t), F32),
                   jax.ShapeDtypeStruct((ne, LANES), I32),
                   jax.ShapeDtypeStruct((ne, LANES), I32),
                   jax.ShapeDtypeStruct((ne, LANES), I32)],
        scratch_shapes=[pltpu.VMEM((TOP_K, t), I32), pltpu.VMEM((TOP_K, t), F32)],
        compiler_params=_cparams(None),
        name="route",
    )(logits_t)


_PAD_PIECES = (128, 64, 32, 16, 8, 4, 2, 1)


def _dispatch_kernel(pos_ref, pad_start_ref, pad_n_ref, src_ref, dst_ref, zero_s, sem, zsem,
                     *, tb, ne):
    i = pl.program_id(0)

    def pad_copies(e):
        n = pad_n_ref[e]
        start = pad_start_ref[e]
        out = []
        for piece in _PAD_PIECES:
            at = start + (n & ~(2 * piece - 1))
            out.append(((n & piece) != 0, pltpu.make_async_copy(
                zero_s.at[pl.ds(0, piece)], dst_ref.at[pl.ds(at, piece)], zsem)))
        return out

    @pl.when(i == 0)
    def _zero_pads():
        zero_s[...] = jnp.zeros(zero_s.shape, zero_s.dtype)

        def issue(e, c):
            for cond, cp in pad_copies(e):
                @pl.when(cond)
                def _():
                    cp.start()
            return c

        def drain(e, c):
            for cond, cp in pad_copies(e):
                @pl.when(cond)
                def _():
                    cp.wait()
            return c

        lax.fori_loop(0, ne, issue, 0)
        lax.fori_loop(0, ne, drain, 0)

    def issue_rows(tl, c):
        for k in range(TOP_K):
            pltpu.make_async_copy(src_ref.at[tl], dst_ref.at[pos_ref[k, tl]], sem).start(priority=k % 2)
        return c

    lax.fori_loop(0, tb, issue_rows, 0, unroll=4)
    for k in range(TOP_K):
        pltpu.make_async_copy(src_ref, dst_ref.at[pl.ds(0, tb)], sem).wait()


def _dispatch_call(pos, pad_start, pad_n, h2p, n_slots, tb=512):
    t = h2p.shape[0]
    row = h2p.shape[1:]
    ne = pad_n.shape[0]
    return pl.pallas_call(
        functools.partial(_dispatch_kernel, tb=tb, ne=ne),
        grid=(t // tb,),
        in_specs=[pl.BlockSpec((TOP_K, tb), lambda i: (0, i), memory_space=pltpu.SMEM),
                  pl.BlockSpec(memory_space=pltpu.SMEM),
                  pl.BlockSpec(memory_space=pltpu.SMEM),
                  pl.BlockSpec((tb, *row), lambda i: (i, 0, 0))],
        out_specs=pl.BlockSpec(memory_space=pl.ANY),
        out_shape=jax.ShapeDtypeStruct((n_slots, *row), U32),
        scratch_shapes=[pltpu.VMEM((_PAD_PIECES[0], *row), U32),
                        pltpu.SemaphoreType.DMA(()),
                        pltpu.SemaphoreType.DMA(())],
        compiler_params=pltpu.CompilerParams(dimension_semantics=("arbitrary",),
                                             vmem_limit_bytes=VMEM_LIMIT, has_side_effects=True),
        name="dispatch",
    )(pos, pad_start, pad_n, h2p)


def _moe_kernel(we_ref, wt_ref, wn_ref, xs_ref, win_ref, bin_ref, wout_ref, bo_ref,
                ys_ref, xu_s, acc_s, wg_st, wu_st, wo_st, yst_s, xsem, wsem, ysem,
                *, tmx, fc, nj):
    w = pl.program_id(0)
    nw = pl.num_programs(0)
    nt = wn_ref[w]
    t0 = wt_ref[w]
    half = xu_s.shape[-2] * xu_s.shape[-1]
    de = nj * fc
    nxt = jnp.minimum(w + 1, nw - 1)
    has_next = jnp.logical_and(w + 1 < nw, wn_ref[nxt] > 0)
    xslot = w % 2

    def local_rows(r, n=1):
        return pl.ds(pl.multiple_of(r * tmx, tmx), n * tmx)

    def x_copy(item_t0, r, slot):
        src = pl.ds(pl.multiple_of((item_t0 + r) * tmx, tmx), tmx)
        return pltpu.make_async_copy(xs_ref.at[src], xu_s.at[slot, local_rows(r)], xsem.at[slot])

    def y_copy(r, slot):
        dst = pl.ds(pl.multiple_of((t0 + r) * tmx, tmx), tmx)
        return pltpu.make_async_copy(yst_s.at[slot], ys_ref.at[dst], ysem.at[slot])

    def w_copies(expert, j, slot):
        c0 = pl.multiple_of(j * fc, fc)
        return (pltpu.make_async_copy(win_ref.at[expert, :, pl.ds(c0, fc)], wg_st.at[slot], wsem.at[slot]),
                pltpu.make_async_copy(win_ref.at[expert, :, pl.ds(de + c0, fc)], wu_st.at[slot],
                                      wsem.at[slot]),
                pltpu.make_async_copy(wout_ref.at[expert, pl.ds(c0, fc), :], wo_st.at[slot],
                                      wsem.at[slot]))

    def start_rows(item_t0, item_nt, slot):
        def go(r, c):
            x_copy(item_t0, r, slot).start()
            return c
        lax.fori_loop(0, item_nt, go, 0)

    @pl.when(w == 0)
    def _prologue():
        for cp in w_copies(we_ref[0], 0, 0):
            cp.start()
        start_rows(t0, nt, 0)

    @pl.when(nt > 0)
    def _work():
        def chunk(j, carry):
            slot = j % 2

            @pl.when(j + 1 < nj)
            def _():
                for cp in w_copies(we_ref[w], j + 1, 1 - slot):
                    cp.start()

            @pl.when(jnp.logical_and(j + 1 == nj, has_next))
            def _():
                for cp in w_copies(we_ref[nxt], 0, 1 - slot):
                    cp.start()

            for cp in w_copies(we_ref[w], j, slot):
                cp.wait()
            stage = (wg_st.at[slot], wu_st.at[slot], wo_st.at[slot])
            biases = (bin_ref[j], bin_ref[nj + j])

            @pl.when(j == 0)
            def _rows_ready():
                def finish(r, c):
                    x_copy(t0, r, xslot).wait()
                    acc_s[local_rows(r), :] = jnp.zeros((tmx, acc_s.shape[1]), F32)
                    return c
                lax.fori_loop(0, nt, finish, 0)

                @pl.when(has_next)
                def _():
                    start_rows(wt_ref[nxt], wn_ref[nxt], 1 - xslot)

            _moe_passes(xu_s.at[xslot], acc_s, stage, biases, nt, local_rows, half)
            return carry

        lax.fori_loop(0, nj, chunk, 0)

        def y_wait(slot):
            y_copy(0, slot).wait()

        prev_nt = jnp.where(w > 0, wn_ref[jnp.maximum(w - 1, 0)], 0)

        @pl.when(prev_nt >= 1)
        def _():
            y_wait((prev_nt - 1) % 2)

        @pl.when(prev_nt >= 2)
        def _():
            y_wait(prev_nt % 2)

        def emit(r, c):
            slot = r % 2

            @pl.when(r >= 2)
            def _():
                y_wait(slot)

            y = acc_s[local_rows(r), :] + bo_ref[...]
            yst_s[slot] = pltpu.pack_elementwise([y[:, :half], y[:, half:]],
                                                 packed_dtype=BF16).reshape(yst_s.shape[1:])
            y_copy(r, slot).start()
            return c

        lax.fori_loop(0, nt, emit, 0)

        @pl.when(jnp.logical_not(has_next))
        def _drain():
            @pl.when(nt >= 2)
            def _():
                y_wait(nt % 2)

            y_wait((nt - 1) % 2)


def _moe_passes(xu_ref, acc_s, stage, biases, nt, local_rows, half):
    wg_ref, wu_ref, wo_ref = stage
    bg, bu = biases

    def proj(x_lo, x_hi, w_ref, b):
        return (jnp.dot(x_lo, w_ref[0:half, :].astype(BF16), preferred_element_type=F32)
                + jnp.dot(x_hi, w_ref[half:2 * half, :].astype(BF16), preferred_element_type=F32) + b)

    def rows_step(rows):
        xw = xu_ref[rows]
        xw = xw.reshape(xw.shape[0], half)
        x_lo = pltpu.unpack_elementwise(xw, index=0, packed_dtype=BF16,
                                        unpacked_dtype=F32).astype(BF16)
        x_hi = pltpu.unpack_elementwise(xw, index=1, packed_dtype=BF16,
                                        unpacked_dtype=F32).astype(BF16)
        gg = jnp.minimum(proj(x_lo, x_hi, wg_ref, bg), SWIGLU_LIMIT)
        uu = jnp.clip(proj(x_lo, x_hi, wu_ref, bu), -SWIGLU_LIMIT, SWIGLU_LIMIT)
        act = (uu + 1.0) * (gg * jax.nn.sigmoid(SWIGLU_ALPHA * gg))
        acc_s[rows, :] += jnp.dot(act.astype(BF16), wo_ref[...].astype(BF16),
                                  preferred_element_type=F32)

    def quad(qi, c):
        rows_step(local_rows(4 * qi, 4))
        return c

    lax.fori_loop(0, nt // 4, quad, 0)

    @pl.when((nt & 2) != 0)
    def _():
        rows_step(local_rows(nt & ~3, 2))

    @pl.when((nt & 1) != 0)
    def _():
        rows_step(local_rows(nt - 1))


def _moe_call(work_e, work_t0, work_nt, xs, w_in, b_in, w_out, b_out,
              tmx=ROW_TILE, group=GROUP_TILES, fc=F_CHUNK):
    ne, d, de2 = w_in.shape
    de = de2 // 2
    nj = de // fc
    assert nj % 2 == 0, "weight stage slots alternate per chunk and restart at 0 per work item"
    n_work = work_e.shape[0]
    n_slots = xs.shape[0]
    row = xs.shape[1:]

    grid_spec = pltpu.PrefetchScalarGridSpec(
        num_scalar_prefetch=3,
        grid=(n_work,),
        in_specs=[
            pl.BlockSpec(memory_space=pl.ANY),
            pl.BlockSpec(memory_space=pl.ANY),
            pl.BlockSpec((None, 2 * nj, 1, fc), lambda w, we, wt, wn: (we[w], 0, 0, 0)),
            pl.BlockSpec(memory_space=pl.ANY),
            pl.BlockSpec((None, 1, d), lambda w, we, wt, wn: (we[w], 0, 0)),
        ],
        out_specs=pl.BlockSpec(memory_space=pl.ANY),
        scratch_shapes=[pltpu.VMEM((2, group * tmx, *row), U32),
                        pltpu.VMEM((group * tmx, d), F32),
                        pltpu.VMEM((2, d, fc), F32),
                        pltpu.VMEM((2, d, fc), F32),
                        pltpu.VMEM((2, fc, d), F32),
                        pltpu.VMEM((2, tmx, *row), U32),
                        pltpu.SemaphoreType.DMA((2,)),
                        pltpu.SemaphoreType.DMA((2,)),
                        pltpu.SemaphoreType.DMA((2,))],
    )
    return pl.pallas_call(
        functools.partial(_moe_kernel, tmx=tmx, fc=fc, nj=nj),
        grid_spec=grid_spec,
        out_shape=jax.ShapeDtypeStruct((n_slots, *row), U32),
        compiler_params=pltpu.CompilerParams(dimension_semantics=("arbitrary",),
                                             vmem_limit_bytes=VMEM_LIMIT, has_side_effects=True),
        name="moe_experts",
    )(work_e, work_t0, work_nt, xs, w_in, b_in.reshape(ne, 2 * nj, 1, fc), w_out,
      b_out.reshape(ne, 1, d))


def _combine_kernel(pos_ref, pos_next_ref, x1_ref, gate_ref, ys_ref, o_ref, ybuf, sem, *, tb):
    half = ybuf.shape[-2] * ybuf.shape[-1]
    i = pl.program_id(0)
    slot = i % 2

    def gather(p_ref, s):
        def issue(tl, c):
            for k in range(TOP_K):
                pltpu.make_async_copy(ys_ref.at[p_ref[k, tl]], ybuf.at[s, k, tl],
                                      sem.at[s]).start(priority=k % 2)
            return c
        lax.fori_loop(0, tb, issue, 0, unroll=4)

    @pl.when(i == 0)
    def _():
        gather(pos_ref, 0)

    @pl.when(i + 1 < pl.num_programs(0))
    def _():
        gather(pos_next_ref, 1 - slot)

    for k in range(TOP_K):
        pltpu.make_async_copy(ys_ref.at[pl.ds(0, tb)], ybuf.at[slot, k], sem.at[slot]).wait()

    lo = x1_ref[:, 0:half]
    hi = x1_ref[:, half:2 * half]
    for k in range(TOP_K):
        g = gate_ref[:, k:k + 1]
        yw = ybuf[slot, k].reshape(tb, half)
        lo = lo + g * pltpu.unpack_elementwise(yw, index=0, packed_dtype=BF16, unpacked_dtype=F32)
        hi = hi + g * pltpu.unpack_elementwise(yw, index=1, packed_dtype=BF16, unpacked_dtype=F32)
    o_ref[:, 0:half] = lo
    o_ref[:, half:2 * half] = hi


def _combine_call(pos, x1, gate_tk, ys, tb=256):
    t, d = x1.shape
    row = ys.shape[1:]
    last = t // tb - 1
    return pl.pallas_call(
        functools.partial(_combine_kernel, tb=tb),
        grid=(t // tb,),
        in_specs=[pl.BlockSpec((TOP_K, tb), lambda i: (0, i), memory_space=pltpu.SMEM),
                  pl.BlockSpec((TOP_K, tb), lambda i: (0, jnp.minimum(i + 1, last)),
                               memory_space=pltpu.SMEM),
                  pl.BlockSpec((tb, d), lambda i: (i, 0)),
                  pl.BlockSpec((tb, TOP_K), lambda i: (i, 0)),
                  pl.BlockSpec(memory_space=pl.ANY)],
        out_specs=pl.BlockSpec((tb, d), lambda i: (i, 0)),
        out_shape=jax.ShapeDtypeStruct((t, d), F32),
        scratch_shapes=[pltpu.VMEM((2, TOP_K, tb, *row), U32),
                        pltpu.SemaphoreType.DMA((2,))],
        compiler_params=_cparams(("arbitrary",)),
        name="combine",
    )(pos, pos, x1, gate_tk, ys)


def _work_list(ntile, otile, n_work, group):
    ne = ntile.shape[0]
    items = (ntile + group - 1) // group
    ends = jnp.cumsum(items)
    total = ends[-1]
    w = jnp.arange(n_work, dtype=I32)
    valid = w < total
    e_w = jnp.clip(jnp.sum((ends[None, :] <= w[:, None]).astype(I32), axis=1), 0, ne - 1)
    local = w - (ends - items)[e_w]
    t0_w = otile[e_w] + local * group
    nt_w = jnp.where(valid, jnp.clip(ntile[e_w] - local * group, 0, group), 0)
    e_last = e_w[jnp.maximum(total - 1, 0)]
    return (jnp.where(valid, e_w, e_last).astype(I32), jnp.where(valid, t0_w, 0).astype(I32),
            nt_w.astype(I32))


def _rotary_tables(seq, dk):
    inv_freq = ROPE_BASE ** (-jnp.arange(0, dk, 2, dtype=F32) / dk)
    ang = jnp.arange(seq, dtype=I32).astype(F32)[:, None] * inv_freq[None, :]
    cos = jnp.repeat(jnp.cos(ang), 2, axis=1)
    sin = jnp.stack([-jnp.sin(ang), jnp.sin(ang)], axis=-1).reshape(seq, dk)
    return cos, sin


def kernel(x, norm_mix_g, w_mix_in, ret_gn_g, q_norm_g, k_norm_g, lambda_q1, lambda_k1, lambda_q2, lambda_k2, diff_subln_g, rel_bias_table, w_mix_out, norm_ffn_g, w_router, b_router, w_exp_in, b_exp_in, w_exp_out, b_exp_out):
    batch, seq, d = x.shape
    t = batch * seq
    depth = norm_mix_g.shape[0]
    ne = w_router.shape[-1]
    n_tiles_max = (t * TOP_K) // ROW_TILE + ne
    n_work = ne + (n_tiles_max - ne) // GROUP_TILES
    cos, sin = _rotary_tables(seq, HEAD_W)
    log_gamma = jnp.log1p(-(2.0 ** (-5.0 - jnp.arange(RET_HEADS, dtype=F32))))

    x2 = x.reshape(t, d)
    for l in range(depth):
        h = _rmsnorm_call(x2, norm_mix_g[l])
        proj = _inproj_call(h, w_mix_in[l])
        y_ret = _ret_call(proj, cos, sin, log_gamma, ret_gn_g[l], batch, seq)
        lam_vecs = jnp.stack([lambda_q1[l], lambda_k1[l], lambda_q2[l], lambda_k2[l]])
        y_diff = _diff_call(proj, rel_bias_table, q_norm_g[l], k_norm_g[l], lam_vecs,
                            diff_subln_g[l], batch, seq)
        x1, h2p, logits_t = _outproj_call(y_ret, y_diff, x2, w_mix_out[l], norm_ffn_g[l],
                                          w_router[l], b_router[l])
        pos, gate, cnt, ntile, otile = _route_call(logits_t)
        cnt, ntile, otile = cnt[:, 0], ntile[:, 0], otile[:, 0]
        work_e, work_t0, work_nt = _work_list(ntile, otile, n_work, GROUP_TILES)
        xs = _dispatch_call(pos, otile * ROW_TILE + cnt, ntile * ROW_TILE - cnt, h2p,
                            n_tiles_max * ROW_TILE)
        ys = _moe_call(work_e, work_t0, work_nt, xs, w_exp_in[l], b_exp_in[l], w_exp_out[l],
                       b_exp_out[l])
        x2 = _combine_call(pos, x1, gate.T, ys)
    return x2.reshape(batch, seq, d)
```

```python
import functools
import math

import jax
import jax.numpy as jnp
from jax import lax
from jax.experimental import pallas as pl
from jax.experimental.pallas import tpu as pltpu

F32 = jnp.float32
BF16 = jnp.bfloat16
I32 = jnp.int32
U32 = jnp.uint32

EPS = 1e-6
CHUNK = 64
RET_HEADS = 8
DIFF_HEADS = 8
HEAD_W = 128
DIFF_DK = 64
ROPE_BASE = 10000.0
NUM_BUCKETS = 32
MAX_DISTANCE = 128
TOP_K = 4
SWIGLU_LIMIT = 7.0
SWIGLU_ALPHA = 1.702
LAM_INIT = 0.8 - 0.6 * math.exp(-0.3 * 0)

LANES = 128
NEG_BIG = -1e30
VMEM_LIMIT = 56 * 1024 * 1024

SEQ_BLK = 256
ROW_TILE = 128
GROUP_TILES = 16
PASS_TILES = 8
F_CHUNK = 256


def _cparams(sem, vmem=VMEM_LIMIT):
    return pltpu.CompilerParams(dimension_semantics=sem, vmem_limit_bytes=vmem)


def _rmsnorm_kernel(x_ref, g_ref, o_ref):
    x = x_ref[...]
    ms = jnp.mean(x * x, axis=-1, keepdims=True)
    o_ref[...] = (x * lax.rsqrt(ms + EPS) * g_ref[...]).astype(o_ref.dtype)


def _rmsnorm_call(x2, g, tm=512):
    t, d = x2.shape
    return pl.pallas_call(
        _rmsnorm_kernel,
        grid=(t // tm,),
        in_specs=[pl.BlockSpec((tm, d), lambda i: (i, 0)),
                  pl.BlockSpec((1, d), lambda i: (0, 0))],
        out_specs=pl.BlockSpec((tm, d), lambda i: (i, 0)),
        out_shape=jax.ShapeDtypeStruct((t, d), BF16),
        compiler_params=_cparams(("arbitrary",)),
        name="rmsnorm_in",
    )(x2, g.reshape(1, d))


def _inproj_kernel(h_ref, w_ref, o_ref, wb_ref, *, rows_per_cast):
    @pl.when(pl.program_id(1) == 0)
    def _():
        d = w_ref.shape[0]
        for c in range(d // rows_per_cast):
            sl = slice(c * rows_per_cast, (c + 1) * rows_per_cast)
            wb_ref[sl, :] = w_ref[sl, :].astype(BF16)

    acc = jnp.dot(h_ref[...], wb_ref[...], preferred_element_type=F32)
    for j in range(o_ref.shape[0]):
        o_ref[j] = acc[:, j * LANES:(j + 1) * LANES].astype(o_ref.dtype)


def _inproj_call(h, w, tm=1024, tn=1024):
    t, d = h.shape
    n = w.shape[1]
    return pl.pallas_call(
        functools.partial(_inproj_kernel, rows_per_cast=min(256, d)),
        grid=(n // tn, t // tm),
        in_specs=[pl.BlockSpec((tm, d), lambda j, i: (i, 0)),
                  pl.BlockSpec((d, tn), lambda j, i: (0, j))],
        out_specs=pl.BlockSpec((tn // LANES, tm, LANES), lambda j, i: (j, i, 0)),
        out_shape=jax.ShapeDtypeStruct((n // LANES, t, LANES), BF16),
        scratch_shapes=[pltpu.VMEM((d, tn), BF16)],
        compiler_params=_cparams(("arbitrary", "arbitrary")),
        name="in_proj",
    )(h, w)


def _ret_kernel(lg_ref, q_ref, k_ref, v_ref, g_ref, cos_ref, sin_ref, gn_ref, o_ref, *, blk, nblk):
    nhead, _, dk = q_ref.shape
    row = lax.broadcasted_iota(I32, (blk, blk), 0)
    col = lax.broadcasted_iota(I32, (blk, blk), 1)
    dist = jnp.abs(row - col).astype(F32)
    visible = (col // CHUNK) <= (row // CHUNK)
    rr = lax.broadcasted_iota(I32, (blk, dk), 0).astype(F32)
    even = (lax.broadcasted_iota(I32, (blk, dk), 1) & 1) == 0
    scale = dk ** -0.5
    decays = []
    for hh in range(nhead):
        lg = lg_ref[pl.program_id(1) * nhead + hh]
        decays.append((jnp.where(visible, jnp.exp(lg * dist), 0.0),
                       jnp.exp(lg * (rr + 1.0)),
                       jnp.exp(lg * (blk - 1.0 - rr)),
                       jnp.exp(lg * jnp.full((1, HEAD_W), float(blk), F32))))

    def body(i, states):
        rows = pl.ds(pl.multiple_of(i * blk, blk), blk)
        cos = cos_ref[rows, :]
        sin = sin_ref[rows, :]

        def rot(x):
            partner = jnp.where(even, pltpu.roll(x, dk - 1, 1), pltpu.roll(x, 1, 1))
            return x * cos + partner * sin

        new_states = []
        for hh in range(nhead):
            dmask, qdec, kdec, bdec = decays[hh]
            state = states[hh]
            qr = rot(q_ref[hh, rows, :].astype(F32)) * scale
            kr = rot(k_ref[hh, rows, :].astype(F32))
            v = v_ref[hh, rows, :]
            s = lax.dot_general(qr.astype(BF16), kr.astype(BF16), (((1,), (1,)), ((), ())),
                                preferred_element_type=F32) * dmask
            out = jnp.dot(s.astype(BF16), v, preferred_element_type=F32)
            out = out + jnp.dot((qr * qdec).astype(BF16), state.astype(BF16),
                                preferred_element_type=F32)
            kv = lax.dot_general((kr * kdec).astype(BF16), v, (((0,), (0,)), ((), ())),
                                 preferred_element_type=F32)
            new_states.append(state * bdec + kv)
            ms = jnp.mean(out * out, axis=-1, keepdims=True)
            normed = out * lax.rsqrt(ms + EPS) * gn_ref[hh]
            g = g_ref[hh, rows, :].astype(F32)
            o_ref[hh, rows, :] = (g * jax.nn.sigmoid(g) * normed).astype(o_ref.dtype)
        return tuple(new_states)

    lax.fori_loop(0, nblk, body, tuple(jnp.zeros((dk, HEAD_W), F32) for _ in range(nhead)),
                  unroll=True)


def _ret_call(proj, cos, sin, log_gamma, gn_g, batch, seq, blk=SEQ_BLK, heads_per_step=1):
    nh = RET_HEADS
    hp = heads_per_step

    def head_spec(base):
        return pl.BlockSpec((hp, seq, LANES), lambda b, h: (base // hp + h, b, 0))

    return pl.pallas_call(
        functools.partial(_ret_kernel, blk=blk, nblk=seq // blk),
        grid=(batch, nh // hp),
        in_specs=[pl.BlockSpec(memory_space=pltpu.SMEM),
                  head_spec(0), head_spec(nh), head_spec(2 * nh), head_spec(3 * nh),
                  pl.BlockSpec((seq, LANES), lambda b, h: (0, 0)),
                  pl.BlockSpec((seq, LANES), lambda b, h: (0, 0)),
                  pl.BlockSpec((hp, 1, LANES), lambda b, h: (h, 0, 0))],
        out_specs=pl.BlockSpec((hp, seq, LANES), lambda b, h: (h, b, 0)),
        out_shape=jax.ShapeDtypeStruct((nh, batch * seq, LANES), BF16),
        compiler_params=_cparams(("arbitrary", "arbitrary")),
        name="retention",
    )(log_gamma, proj, proj, proj, proj, cos, sin, gn_g.reshape(nh, 1, LANES))


def _diff_kernel(tbl_ref, q_ref, k_ref, v_ref, qg_ref, kg_ref, lam_ref, sg_ref, bidx_ref, o_ref,
                 qz_s, kn_s, bias_s, s_s, p_s, *, blk, nblk):
    h = pl.program_id(0)
    b = pl.program_id(1)
    far_bucket = NUM_BUCKETS // 2 - 1

    @pl.when(b == 0)
    def _build_bias():
        row = lax.broadcasted_iota(I32, (blk, blk), 0)
        col = lax.broadcasted_iota(I32, (blk, blk), 1)
        visible = (col // CHUNK) <= (row // CHUNK)
        for d in range(2):
            idx = bidx_ref[d]
            bias = jnp.zeros((blk, blk), F32)
            for bucket in range(NUM_BUCKETS):
                bias = jnp.where(idx == bucket, tbl_ref[bucket, h], bias)
            if d == 0:
                bias = jnp.where(visible, bias, NEG_BIG)
            bias_s[d, 0:blk, :] = bias
            bias_s[d, blk:2 * blk, :] = bias

    lo = lax.broadcasted_iota(I32, (blk, HEAD_W), 1) < DIFF_DK
    scale = DIFF_DK ** -0.5
    same_half = ((lax.broadcasted_iota(I32, (HEAD_W, HEAD_W), 0) // DIFF_DK)
                 == (lax.broadcasted_iota(I32, (HEAD_W, HEAD_W), 1) // DIFF_DK)).astype(BF16)

    def half_norm(x, g):
        ms = jnp.dot((x * x).astype(BF16), same_half, preferred_element_type=F32) * (1.0 / DIFF_DK)
        return x * lax.rsqrt(ms + EPS) * g

    def prep(i, carry):
        rows = pl.ds(pl.multiple_of(i * blk, blk), blk)
        qn = half_norm(q_ref[rows, :].astype(F32), qg_ref[...]) * scale
        kn = half_norm(k_ref[rows, :].astype(F32), kg_ref[...])
        base = pl.multiple_of(i * 2 * blk, 2 * blk)
        qz_s[pl.ds(base, blk), :] = jnp.where(lo, qn, 0.0).astype(BF16)
        qz_s[pl.ds(base + blk, blk), :] = jnp.where(lo, 0.0, qn).astype(BF16)
        kn_s[rows, :] = kn.astype(BF16)
        return carry

    lax.fori_loop(0, nblk, prep, 0)

    lam = (jnp.exp(jnp.sum(lam_ref[0:1, :] * lam_ref[1:2, :], axis=-1, keepdims=True))
           - jnp.exp(jnp.sum(lam_ref[2:3, :] * lam_ref[3:4, :], axis=-1, keepdims=True))
           + LAM_INIT)
    c_far = tbl_ref[far_bucket, h]

    def lane_fold(x, op):
        out = x[:, 0:LANES]
        for c in range(1, blk // LANES):
            out = op(out, x[:, c * LANES:(c + 1) * LANES])
        return out

    for i in range(nblk):
        qz = qz_s[i * 2 * blk:(i + 1) * 2 * blk, :]
        m_t = jnp.full((2 * blk, LANES), NEG_BIG, F32)
        for j in range(i + 1):
            keys = slice(j * blk, (j + 1) * blk)
            s = lax.dot_general(qz, kn_s[keys, :], (((1,), (1,)), ((), ())),
                                preferred_element_type=F32)
            s = s + (bias_s[0] if j == i else bias_s[1] if j == i - 1 else c_far)
            s_s[:, keys] = s
            m_t = jnp.maximum(m_t, lane_fold(s, jnp.maximum))
        m = jnp.max(m_t, axis=-1, keepdims=True)
        l_t = jnp.zeros((2 * blk, LANES), F32)
        for j in range(i + 1):
            keys = slice(j * blk, (j + 1) * blk)
            p = jnp.exp(s_s[:, keys] - m)
            l_t = l_t + lane_fold(p, jnp.add)
            p_s[:, keys] = p.astype(BF16)
        l = jnp.sum(l_t, axis=-1, keepdims=True)
        kend = (i + 1) * blk
        o = jnp.dot(p_s[:, 0:kend], v_ref[0:kend, :], preferred_element_type=F32) / l
        att = o[0:blk, :] - lam * o[blk:2 * blk, :]
        ms = jnp.mean(att * att, axis=-1, keepdims=True)
        y = att * lax.rsqrt(ms + EPS) * sg_ref[...] * (1.0 - LAM_INIT)
        o_ref[i * blk:(i + 1) * blk, :] = y.astype(o_ref.dtype)


def _t5_bucket(rel):
    nb = NUM_BUCKETS // 2
    max_exact = nb // 2
    base = jnp.where(rel > 0, nb, 0)
    n = jnp.abs(rel)
    large = max_exact + (jnp.log(jnp.maximum(n, 1).astype(jnp.float32) / max_exact)
                         / math.log(MAX_DISTANCE / max_exact) * (nb - max_exact)).astype(jnp.int32)
    large = jnp.minimum(large, nb - 1)
    return base + jnp.where(n < max_exact, n, large)


def _diff_call(proj, rel_table, qg, kg, lam_vecs, sg, batch, seq, blk=SEQ_BLK):
    nh = DIFF_HEADS
    first = 4 * RET_HEADS
    r = jnp.arange(blk, dtype=I32)
    rel0 = r[None, :] - r[:, None]
    bidx = jnp.stack([_t5_bucket(rel0), _t5_bucket(rel0 - blk)]).astype(I32) & (NUM_BUCKETS - 1)

    def head_spec(base):
        return pl.BlockSpec((None, seq, LANES), lambda h, b: (base + h, b, 0))

    def vec_spec():
        return pl.BlockSpec((1, LANES), lambda h, b: (0, 0))

    return pl.pallas_call(
        functools.partial(_diff_kernel, blk=blk, nblk=seq // blk),
        grid=(nh, batch),
        in_specs=[pl.BlockSpec(memory_space=pltpu.SMEM),
                  head_spec(first), head_spec(first + nh), head_spec(first + 2 * nh),
                  vec_spec(), vec_spec(),
                  pl.BlockSpec((4, DIFF_DK), lambda h, b: (0, 0)),
                  vec_spec(),
                  pl.BlockSpec((2, blk, blk), lambda h, b: (0, 0, 0))],
        out_specs=pl.BlockSpec((None, seq, LANES), lambda h, b: (h, b, 0)),
        out_shape=jax.ShapeDtypeStruct((nh, batch * seq, LANES), BF16),
        scratch_shapes=[pltpu.VMEM((2 * seq, LANES), BF16),
                        pltpu.VMEM((seq, LANES), BF16),
                        pltpu.VMEM((2, 2 * blk, blk), F32),
                        pltpu.VMEM((2 * blk, seq), F32),
                        pltpu.VMEM((2 * blk, seq), BF16)],
        compiler_params=_cparams(("arbitrary", "arbitrary")),
        name="diff_attn",
    )(rel_table, proj, proj, proj,
      jnp.tile(qg, 2).reshape(1, LANES), jnp.tile(kg, 2).reshape(1, LANES),
      lam_vecs, sg.reshape(1, LANES), bidx)


def _outproj_kernel(yr_ref, yd_ref, x_ref, w_hbm, g_ref, wr_ref, br_ref,
                    x1_ref, h2p_ref, lt_ref, wb_s, wstage_s, wr2_s, wsem,
                    *, rows_per_cast):
    ne = lt_ref.shape[0]
    nchunk = wb_s.shape[0] // rows_per_cast

    def w_copy(c):
        return pltpu.make_async_copy(w_hbm.at[pl.ds(c * rows_per_cast, rows_per_cast)],
                                     wstage_s.at[c % 2], wsem.at[c % 2])

    @pl.when(pl.program_id(0) == 0)
    def _stage_weights():
        w_copy(0).start()
        for c in range(nchunk):
            if c + 1 < nchunk:
                w_copy(c + 1).start()
            w_copy(c).wait()
            wb_s[c * rows_per_cast:(c + 1) * rows_per_cast, :] = wstage_s[c % 2].astype(BF16)
        wr = wr_ref[...]
        wr_hi = wr.astype(BF16)
        wr2_s[:, 0:LANES] = wr_hi
        wr2_s[:, LANES:2 * LANES] = (wr - wr_hi.astype(F32)).astype(BF16)

    y = jnp.concatenate([yr_ref[j] for j in range(yr_ref.shape[0])]
                        + [yd_ref[j] for j in range(yd_ref.shape[0])], axis=-1)
    x1 = x_ref[...] + jnp.dot(y, wb_s[...], preferred_element_type=F32)
    x1_ref[...] = x1
    ms = jnp.mean(x1 * x1, axis=-1, keepdims=True)
    h2 = x1 * lax.rsqrt(ms + EPS) * g_ref[...]
    h_hi = h2.astype(BF16)
    h_lo = (h2 - h_hi.astype(F32)).astype(BF16)
    parts = (jnp.dot(h_hi, wr2_s[...], preferred_element_type=F32)
             + jnp.dot(h_lo, wr2_s[...], preferred_element_type=F32))
    logits = parts[:, 0:LANES] + parts[:, LANES:2 * LANES]
    lt_ref[...] = logits.T[0:ne, :] + br_ref[...]
    half = h2.shape[1] // 2
    packed = pltpu.pack_elementwise([h2[:, :half], h2[:, half:]], packed_dtype=BF16)
    h2p_ref[...] = packed.reshape(h2p_ref.shape)


def _outproj_call(y_ret, y_diff, x2, w_out, g, w_router, b_router, tm=512):
    t, d = x2.shape
    ne = w_router.shape[1]
    nhr, nhd = y_ret.shape[0], y_diff.shape[0]
    rows_per_cast = min(256, d)
    wr_pad = jnp.pad(w_router, ((0, 0), (0, LANES - ne)))
    return pl.pallas_call(
        functools.partial(_outproj_kernel, rows_per_cast=rows_per_cast),
        grid=(t // tm,),
        in_specs=[pl.BlockSpec((nhr, tm, LANES), lambda i: (0, i, 0)),
                  pl.BlockSpec((nhd, tm, LANES), lambda i: (0, i, 0)),
                  pl.BlockSpec((tm, d), lambda i: (i, 0)),
                  pl.BlockSpec(memory_space=pl.ANY),
                  pl.BlockSpec((1, d), lambda i: (0, 0)),
                  pl.BlockSpec((d, LANES), lambda i: (0, 0)),
                  pl.BlockSpec((ne, 1), lambda i: (0, 0))],
        out_specs=[pl.BlockSpec((tm, d), lambda i: (i, 0)),
                   pl.BlockSpec((tm, d // 2 // LANES, LANES), lambda i: (i, 0, 0)),
                   pl.BlockSpec((ne, tm), lambda i: (0, i))],
        out_shape=[jax.ShapeDtypeStruct((t, d), F32),
                   jax.ShapeDtypeStruct((t, d // 2 // LANES, LANES), U32),
                   jax.ShapeDtypeStruct((ne, t), F32)],
        scratch_shapes=[pltpu.VMEM((d, d), BF16),
                        pltpu.VMEM((2, rows_per_cast, d), F32),
                        pltpu.VMEM((d, 2 * LANES), BF16),
                        pltpu.SemaphoreType.DMA((2,))],
        compiler_params=_cparams(("arbitrary",)),
        name="out_proj_router",
    )(y_ret, y_diff, x2, w_out, g.reshape(1, d), wr_pad, b_router.reshape(ne, 1))


def _route_kernel(lt_ref, pos_ref, gate_ref, cnt_ref, nt_ref, ot_ref, idx_s, rank_s, *, tb, row_tile):
    ne, t = lt_ref.shape
    e_iota = lax.broadcasted_iota(I32, (ne, tb), 0)
    upper = (lax.broadcasted_iota(I32, (tb, tb), 0)
             < lax.broadcasted_iota(I32, (tb, tb), 1)).astype(BF16)

    def pass_a(i, running):
        cols = pl.ds(pl.multiple_of(i * tb, tb), tb)
        l = lt_ref[:, cols]
        tops, hots = [], []
        for k in range(TOP_K):
            m = jnp.max(l, axis=0, keepdims=True)
            idx = jnp.min(jnp.where(l == m, e_iota, ne), axis=0, keepdims=True)
            hot = e_iota == idx
            l = jnp.where(hot, -jnp.inf, l)
            idx_s[k:k + 1, cols] = idx
            tops.append(m)
            hots.append(hot)
        exps = [jnp.exp(m - tops[0]) for m in tops]
        denom = exps[0] + exps[1] + exps[2] + exps[3]
        for k in range(TOP_K):
            gate_ref[k:k + 1, cols] = exps[k] / denom
        hot_all = jnp.zeros((ne, tb), F32)
        for hot in hots:
            hot_all = hot_all + hot.astype(F32)
        before = running + jnp.dot(hot_all.astype(BF16), upper, preferred_element_type=F32)
        for k in range(TOP_K):
            rank_s[k:k + 1, cols] = jnp.sum(jnp.where(hots[k], before, 0.0), axis=0, keepdims=True)
        return running + jnp.sum(hot_all, axis=1, keepdims=True)

    cnt = lax.fori_loop(0, t // tb, pass_a, jnp.zeros((ne, 1), F32))
    ntile = jnp.floor((cnt + (row_tile - 1.0)) * (1.0 / row_tile))
    lower = (lax.broadcasted_iota(I32, (ne, ne), 1)
             < lax.broadcasted_iota(I32, (ne, ne), 0)).astype(BF16)
    otile = jnp.dot(lower, jnp.broadcast_to(ntile, (ne, LANES)).astype(BF16),
                    preferred_element_type=F32)
    cnt_ref[...] = jnp.broadcast_to(cnt, (ne, LANES)).astype(I32)
    nt_ref[...] = jnp.broadcast_to(ntile, (ne, LANES)).astype(I32)
    ot_ref[...] = otile.astype(I32)
    off_rows = otile[:, 0:1] * float(row_tile)

    def pass_b(i, carry):
        cols = pl.ds(pl.multiple_of(i * tb, tb), tb)
        for k in range(TOP_K):
            hot = e_iota == idx_s[k:k + 1, cols]
            off = jnp.sum(jnp.where(hot, off_rows, 0.0), axis=0, keepdims=True)
            pos_ref[k:k + 1, cols] = (rank_s[k:k + 1, cols] + off).astype(I32)
        return carry

    lax.fori_loop(0, t // tb, pass_b, 0)


def _route_call(logits_t, tb=256, row_tile=ROW_TILE):
    ne, t = logits_t.shape
    return pl.pallas_call(
        functools.partial(_route_kernel, tb=tb, row_tile=row_tile),
        out_shape=[jax.ShapeDtypeStruct((TOP_K, t), I32),
                   jax.ShapeDtypeStruct((TOP_K, t), F32),
                   jax.ShapeDtypeStruct((ne, LANES), I32),
                   jax.ShapeDtypeStruct((ne, LANES), I32),
                   jax.ShapeDtypeStruct((ne, LANES), I32)],
        scratch_shapes=[pltpu.VMEM((TOP_K, t), I32), pltpu.VMEM((TOP_K, t), F32)],
        compiler_params=_cparams(None),
        name="route",
    )(logits_t)


_PAD_PIECES = tuple(ROW_TILE >> (k + 1) for k in range(ROW_TILE.bit_length() - 1))


def _dispatch_kernel(pos_ref, pad_start_ref, pad_n_ref, src_ref, dst_ref, zero_s, sem, zsem,
                     *, tb, ne):
    i = pl.program_id(0)

    def pad_copies(e):
        n = pad_n_ref[e]
        start = pad_start_ref[e]
        out = []
        for piece in _PAD_PIECES:
            at = start + (n & ~(2 * piece - 1))
            out.append(((n & piece) != 0, pltpu.make_async_copy(
                zero_s.at[pl.ds(0, piece)], dst_ref.at[pl.ds(at, piece)], zsem)))
        return out

    @pl.when(i == 0)
    def _zero_pads():
        zero_s[...] = jnp.zeros(zero_s.shape, zero_s.dtype)

        def issue(e, c):
            for cond, cp in pad_copies(e):
                @pl.when(cond)
                def _():
                    cp.start()
            return c

        def drain(e, c):
            for cond, cp in pad_copies(e):
                @pl.when(cond)
                def _():
                    cp.wait()
            return c

        lax.fori_loop(0, ne, issue, 0)
        lax.fori_loop(0, ne, drain, 0)

    def issue_rows(tl, c):
        for k in range(TOP_K):
            pltpu.make_async_copy(src_ref.at[tl], dst_ref.at[pos_ref[k, tl]], sem).start(priority=k % 2)
        return c

    lax.fori_loop(0, tb, issue_rows, 0, unroll=4)
    for k in range(TOP_K):
        pltpu.make_async_copy(src_ref, dst_ref.at[pl.ds(0, tb)], sem).wait()


def _dispatch_call(pos, pad_start, pad_n, h2p, n_slots, tb=512):
    t = h2p.shape[0]
    row = h2p.shape[1:]
    ne = pad_n.shape[0]
    return pl.pallas_call(
        functools.partial(_dispatch_kernel, tb=tb, ne=ne),
        grid=(t // tb,),
        in_specs=[pl.BlockSpec((TOP_K, tb), lambda i: (0, i), memory_space=pltpu.SMEM),
                  pl.BlockSpec(memory_space=pltpu.SMEM),
                  pl.BlockSpec(memory_space=pltpu.SMEM),
                  pl.BlockSpec((tb, *row), lambda i: (i, 0, 0))],
        out_specs=pl.BlockSpec(memory_space=pl.ANY),
        out_shape=jax.ShapeDtypeStruct((n_slots, *row), U32),
        scratch_shapes=[pltpu.VMEM((_PAD_PIECES[0], *row), U32),
                        pltpu.SemaphoreType.DMA(()),
                        pltpu.SemaphoreType.DMA(())],
        compiler_params=pltpu.CompilerParams(dimension_semantics=("arbitrary",),
                                             vmem_limit_bytes=VMEM_LIMIT, has_side_effects=True),
        name="dispatch",
    )(pos, pad_start, pad_n, h2p)


def _moe_kernel(we_ref, wt_ref, wn_ref, xs_ref, win_ref, bin_ref, wout_ref, bo_ref,
                ys_ref, xu_s, acc_s, wg_st, wu_st, wo_st, yst_s, xsem, wsem, ysem,
                *, tmx, fc, nj):
    w = pl.program_id(0)
    nw = pl.num_programs(0)
    nt = wn_ref[w]
    t0 = wt_ref[w]
    half = xu_s.shape[-2] * xu_s.shape[-1]
    de = nj * fc
    nxt = jnp.minimum(w + 1, nw - 1)
    has_next = jnp.logical_and(w + 1 < nw, wn_ref[nxt] > 0)
    xslot = w % 2

    def local_rows(r, n=1):
        return pl.ds(pl.multiple_of(r * tmx, tmx), n * tmx)

    def x_copy(item_t0, r, slot):
        src = pl.ds(pl.multiple_of((item_t0 + r) * tmx, tmx), tmx)
        return pltpu.make_async_copy(xs_ref.at[src], xu_s.at[slot, local_rows(r)], xsem.at[slot])

    def y_copy(r, slot):
        dst = pl.ds(pl.multiple_of((t0 + r) * tmx, tmx), tmx)
        return pltpu.make_async_copy(yst_s.at[slot], ys_ref.at[dst], ysem.at[slot])

    def w_copies(expert, j, slot):
        c0 = pl.multiple_of(j * fc, fc)
        return (pltpu.make_async_copy(win_ref.at[expert, :, pl.ds(c0, fc)], wg_st.at[slot], wsem.at[slot]),
                pltpu.make_async_copy(win_ref.at[expert, :, pl.ds(de + c0, fc)], wu_st.at[slot],
                                      wsem.at[slot]),
                pltpu.make_async_copy(wout_ref.at[expert, pl.ds(c0, fc), :], wo_st.at[slot],
                                      wsem.at[slot]))

    def start_rows(item_t0, item_nt, slot):
        def go(r, c):
            x_copy(item_t0, r, slot).start()
            return c
        lax.fori_loop(0, item_nt, go, 0)

    @pl.when(w == 0)
    def _prologue():
        for cp in w_copies(we_ref[0], 0, 0):
            cp.start()
        start_rows(t0, nt, 0)

    @pl.when(nt > 0)
    def _work():
        def chunk(j, carry):
            slot = j % 2

            @pl.when(j + 1 < nj)
            def _():
                for cp in w_copies(we_ref[w], j + 1, 1 - slot):
                    cp.start()

            @pl.when(jnp.logical_and(j + 1 == nj, has_next))
            def _():
                for cp in w_copies(we_ref[nxt], 0, 1 - slot):
                    cp.start()

            for cp in w_copies(we_ref[w], j, slot):
                cp.wait()
            stage = (wg_st.at[slot], wu_st.at[slot], wo_st.at[slot])
            biases = (bin_ref[j], bin_ref[nj + j])

            @pl.when(j == 0)
            def _rows_ready():
                def finish(r, c):
                    x_copy(t0, r, xslot).wait()
                    acc_s[local_rows(r), :] = jnp.zeros((tmx, acc_s.shape[1]), F32)
                    return c
                lax.fori_loop(0, nt, finish, 0)

                @pl.when(has_next)
                def _():
                    start_rows(wt_ref[nxt], wn_ref[nxt], 1 - xslot)

            _moe_passes(xu_s.at[xslot], acc_s, stage, biases, nt, local_rows, half)
            return carry

        lax.fori_loop(0, nj, chunk, 0)

        def y_wait(slot):
            y_copy(0, slot).wait()

        prev_nt = jnp.where(w > 0, wn_ref[jnp.maximum(w - 1, 0)], 0)

        @pl.when(prev_nt >= 1)
        def _():
            y_wait((prev_nt - 1) % 2)

        @pl.when(prev_nt >= 2)
        def _():
            y_wait(prev_nt % 2)

        def emit(r, c):
            slot = r % 2

            @pl.when(r >= 2)
            def _():
                y_wait(slot)

            y = acc_s[local_rows(r), :] + bo_ref[...]
            yst_s[slot] = pltpu.pack_elementwise([y[:, :half], y[:, half:]],
                                                 packed_dtype=BF16).reshape(yst_s.shape[1:])
            y_copy(r, slot).start()
            return c

        lax.fori_loop(0, nt, emit, 0)

        @pl.when(jnp.logical_not(has_next))
        def _drain():
            @pl.when(nt >= 2)
            def _():
                y_wait(nt % 2)

            y_wait((nt - 1) % 2)


def _moe_passes(xu_ref, acc_s, stage, biases, nt, local_rows, half):
    wg_ref, wu_ref, wo_ref = stage
    bg, bu = biases

    def proj(x_lo, x_hi, w_ref, b):
        return (jnp.dot(x_lo, w_ref[0:half, :].astype(BF16), preferred_element_type=F32)
                + jnp.dot(x_hi, w_ref[half:2 * half, :].astype(BF16), preferred_element_type=F32) + b)

    def rows_step(rows):
        xw = xu_ref[rows]
        xw = xw.reshape(xw.shape[0], half)
        x_lo = pltpu.unpack_elementwise(xw, index=0, packed_dtype=BF16,
                                        unpacked_dtype=F32).astype(BF16)
        x_hi = pltpu.unpack_elementwise(xw, index=1, packed_dtype=BF16,
                                        unpacked_dtype=F32).astype(BF16)
        gg = jnp.minimum(proj(x_lo, x_hi, wg_ref, bg), SWIGLU_LIMIT)
        uu = jnp.clip(proj(x_lo, x_hi, wu_ref, bu), -SWIGLU_LIMIT, SWIGLU_LIMIT)
        act = (uu + 1.0) * (gg * jax.nn.sigmoid(SWIGLU_ALPHA * gg))
        acc_s[rows, :] += jnp.dot(act.astype(BF16), wo_ref[...].astype(BF16),
                                  preferred_element_type=F32)

    def big(pi, c):
        rows_step(local_rows(PASS_TILES * pi, PASS_TILES))
        return c

    lax.fori_loop(0, nt // PASS_TILES, big, 0)
    piece = PASS_TILES // 2
    while piece >= 1:
        @pl.when((nt & piece) != 0)
        def _(piece=piece):
            rows_step(local_rows(nt & ~(2 * piece - 1), piece))
        piece //= 2


def _moe_call(work_e, work_t0, work_nt, xs, w_in, b_in, w_out, b_out,
              tmx=ROW_TILE, group=GROUP_TILES, fc=F_CHUNK):
    ne, d, de2 = w_in.shape
    de = de2 // 2
    nj = de // fc
    assert nj % 2 == 0, "weight stage slots alternate per chunk and restart at 0 per work item"
    n_work = work_e.shape[0]
    n_slots = xs.shape[0]
    row = xs.shape[1:]

    grid_spec = pltpu.PrefetchScalarGridSpec(
        num_scalar_prefetch=3,
        grid=(n_work,),
        in_specs=[
            pl.BlockSpec(memory_space=pl.ANY),
            pl.BlockSpec(memory_space=pl.ANY),
            pl.BlockSpec((None, 2 * nj, 1, fc), lambda w, we, wt, wn: (we[w], 0, 0, 0)),
            pl.BlockSpec(memory_space=pl.ANY),
            pl.BlockSpec((None, 1, d), lambda w, we, wt, wn: (we[w], 0, 0)),
        ],
        out_specs=pl.BlockSpec(memory_space=pl.ANY),
        scratch_shapes=[pltpu.VMEM((2, group * tmx, *row), U32),
                        pltpu.VMEM((group * tmx, d), F32),
                        pltpu.VMEM((2, d, fc), F32),
                        pltpu.VMEM((2, d, fc), F32),
                        pltpu.VMEM((2, fc, d), F32),
                        pltpu.VMEM((2, tmx, *row), U32),
                        pltpu.SemaphoreType.DMA((2,)),
                        pltpu.SemaphoreType.DMA((2,)),
                        pltpu.SemaphoreType.DMA((2,))],
    )
    return pl.pallas_call(
        functools.partial(_moe_kernel, tmx=tmx, fc=fc, nj=nj),
        grid_spec=grid_spec,
        out_shape=jax.ShapeDtypeStruct((n_slots, *row), U32),
        compiler_params=pltpu.CompilerParams(dimension_semantics=("arbitrary",),
                                             vmem_limit_bytes=VMEM_LIMIT, has_side_effects=True),
        name="moe_experts",
    )(work_e, work_t0, work_nt, xs, w_in, b_in.reshape(ne, 2 * nj, 1, fc), w_out,
      b_out.reshape(ne, 1, d))


def _combine_kernel(pos_ref, pos_next_ref, x1_ref, gate_ref, ys_ref, o_ref, ybuf, sem, *, tb):
    half = ybuf.shape[-2] * ybuf.shape[-1]
    i = pl.program_id(0)
    slot = i % 2

    def gather(p_ref, s):
        def issue(tl, c):
            for k in range(TOP_K):
                pltpu.make_async_copy(ys_ref.at[p_ref[k, tl]], ybuf.at[s, k, tl],
                                      sem.at[s]).start(priority=k % 2)
            return c
        lax.fori_loop(0, tb, issue, 0, unroll=4)

    @pl.when(i == 0)
    def _():
        gather(pos_ref, 0)

    @pl.when(i + 1 < pl.num_programs(0))
    def _():
        gather(pos_next_ref, 1 - slot)

    for k in range(TOP_K):
        pltpu.make_async_copy(ys_ref.at[pl.ds(0, tb)], ybuf.at[slot, k], sem.at[slot]).wait()

    lo = x1_ref[:, 0:half]
    hi = x1_ref[:, half:2 * half]
    for k in range(TOP_K):
        g = gate_ref[:, k:k + 1]
        yw = ybuf[slot, k].reshape(tb, half)
        lo = lo + g * pltpu.unpack_elementwise(yw, index=0, packed_dtype=BF16, unpacked_dtype=F32)
        hi = hi + g * pltpu.unpack_elementwise(yw, index=1, packed_dtype=BF16, unpacked_dtype=F32)
    o_ref[:, 0:half] = lo
    o_ref[:, half:2 * half] = hi


def _combine_call(pos, x1, gate_tk, ys, tb=256):
    t, d = x1.shape
    row = ys.shape[1:]
    last = t // tb - 1
    return pl.pallas_call(
        functools.partial(_combine_kernel, tb=tb),
        grid=(t // tb,),
        in_specs=[pl.BlockSpec((TOP_K, tb), lambda i: (0, i), memory_space=pltpu.SMEM),
                  pl.BlockSpec((TOP_K, tb), lambda i: (0, jnp.minimum(i + 1, last)),
                               memory_space=pltpu.SMEM),
                  pl.BlockSpec((tb, d), lambda i: (i, 0)),
                  pl.BlockSpec((tb, TOP_K), lambda i: (i, 0)),
                  pl.BlockSpec(memory_space=pl.ANY)],
        out_specs=pl.BlockSpec((tb, d), lambda i: (i, 0)),
        out_shape=jax.ShapeDtypeStruct((t, d), F32),
        scratch_shapes=[pltpu.VMEM((2, TOP_K, tb, *row), U32),
                        pltpu.SemaphoreType.DMA((2,))],
        compiler_params=_cparams(("arbitrary",)),
        name="combine",
    )(pos, pos, x1, gate_tk, ys)


def _work_list(ntile, otile, n_work, group):
    ne = ntile.shape[0]
    items = (ntile + group - 1) // group
    ends = jnp.cumsum(items)
    total = ends[-1]
    w = jnp.arange(n_work, dtype=I32)
    valid = w < total
    e_w = jnp.clip(jnp.sum((ends[None, :] <= w[:, None]).astype(I32), axis=1), 0, ne - 1)
    local = w - (ends - items)[e_w]
    t0_w = otile[e_w] + local * group
    nt_w = jnp.where(valid, jnp.clip(ntile[e_w] - local * group, 0, group), 0)
    e_last = e_w[jnp.maximum(total - 1, 0)]
    return (jnp.where(valid, e_w, e_last).astype(I32), jnp.where(valid, t0_w, 0).astype(I32),
            nt_w.astype(I32))


def _rotary_tables(seq, dk):
    inv_freq = ROPE_BASE ** (-jnp.arange(0, dk, 2, dtype=F32) / dk)
    ang = jnp.arange(seq, dtype=I32).astype(F32)[:, None] * inv_freq[None, :]
    cos = jnp.repeat(jnp.cos(ang), 2, axis=1)
    sin = jnp.stack([-jnp.sin(ang), jnp.sin(ang)], axis=-1).reshape(seq, dk)
    return cos, sin


def kernel(x, norm_mix_g, w_mix_in, ret_gn_g, q_norm_g, k_norm_g, lambda_q1, lambda_k1, lambda_q2, lambda_k2, diff_subln_g, rel_bias_table, w_mix_out, norm_ffn_g, w_router, b_router, w_exp_in, b_exp_in, w_exp_out, b_exp_out):
    batch, seq, d = x.shape
    t = batch * seq
    depth = norm_mix_g.shape[0]
    ne = w_router.shape[-1]
    n_tiles_max = (t * TOP_K) // ROW_TILE + ne
    n_work = ne + (n_tiles_max - ne) // GROUP_TILES
    cos, sin = _rotary_tables(seq, HEAD_W)
    log_gamma = jnp.log1p(-(2.0 ** (-5.0 - jnp.arange(RET_HEADS, dtype=F32))))

    x2 = x.reshape(t, d)
    for l in range(depth):
        h = _rmsnorm_call(x2, norm_mix_g[l])
        proj = _inproj_call(h, w_mix_in[l])
        y_ret = _ret_call(proj, cos, sin, log_gamma, ret_gn_g[l], batch, seq)
        lam_vecs = jnp.stack([lambda_q1[l], lambda_k1[l], lambda_q2[l], lambda_k2[l]])
        y_diff = _diff_call(proj, rel_bias_table, q_norm_g[l], k_norm_g[l], lam_vecs,
                            diff_subln_g[l], batch, seq)
        x1, h2p, logits_t = _outproj_call(y_ret, y_diff, x2, w_mix_out[l], norm_ffn_g[l],
                                          w_router[l], b_router[l])
        pos, gate, cnt, ntile, otile = _route_call(logits_t)
        cnt, ntile, otile = cnt[:, 0], ntile[:, 0], otile[:, 0]
        work_e, work_t0, work_nt = _work_list(ntile, otile, n_work, GROUP_TILES)
        xs = _dispatch_call(pos, otile * ROW_TILE + cnt, ntile * ROW_TILE - cnt, h2p,
                            n_tiles_max * ROW_TILE)
        ys = _moe_call(work_e, work_t0, work_nt, xs, w_exp_in[l], b_exp_in[l], w_exp_out[l],
                       b_exp_out[l])
        x2 = _combine_call(pos, x1, gate.T, ys)
    return x2.reshape(batch, seq, d)
```

```python
import functools
import math

import jax
import jax.numpy as jnp
from jax import lax
from jax.experimental import pallas as pl
from jax.experimental.pallas import tpu as pltpu

F32 = jnp.float32
BF16 = jnp.bfloat16
I32 = jnp.int32
U32 = jnp.uint32

EPS = 1e-6
CHUNK = 64
RET_HEADS = 8
DIFF_HEADS = 8
HEAD_W = 128
DIFF_DK = 64
ROPE_BASE = 10000.0
NUM_BUCKETS = 32
MAX_DISTANCE = 128
TOP_K = 4
SWIGLU_LIMIT = 7.0
SWIGLU_ALPHA = 1.702
LAM_INIT = 0.8 - 0.6 * math.exp(-0.3 * 0)

LANES = 128
NEG_BIG = -1e30
VMEM_LIMIT = 56 * 1024 * 1024

SEQ_BLK = 256
ROW_TILE = 256
GROUP_TILES = 8
PASS_TILES = 4
F_CHUNK = 256


def _cparams(sem, vmem=VMEM_LIMIT):
    return pltpu.CompilerParams(dimension_semantics=sem, vmem_limit_bytes=vmem)


def _rmsnorm_kernel(x_ref, g_ref, o_ref):
    x = x_ref[...]
    ms = jnp.mean(x * x, axis=-1, keepdims=True)
    o_ref[...] = (x * lax.rsqrt(ms + EPS) * g_ref[...]).astype(o_ref.dtype)


def _rmsnorm_call(x2, g, tm=512):
    t, d = x2.shape
    return pl.pallas_call(
        _rmsnorm_kernel,
        grid=(t // tm,),
        in_specs=[pl.BlockSpec((tm, d), lambda i: (i, 0)),
                  pl.BlockSpec((1, d), lambda i: (0, 0))],
        out_specs=pl.BlockSpec((tm, d), lambda i: (i, 0)),
        out_shape=jax.ShapeDtypeStruct((t, d), BF16),
        compiler_params=_cparams(("arbitrary",)),
        name="rmsnorm_in",
    )(x2, g.reshape(1, d))


def _inproj_kernel(h_ref, w_ref, o_ref, wb_ref, *, rows_per_cast):
    @pl.when(pl.program_id(1) == 0)
    def _():
        d = w_ref.shape[0]
        for c in range(d // rows_per_cast):
            sl = slice(c * rows_per_cast, (c + 1) * rows_per_cast)
            wb_ref[sl, :] = w_ref[sl, :].astype(BF16)

    acc = jnp.dot(h_ref[...], wb_ref[...], preferred_element_type=F32)
    for j in range(o_ref.shape[0]):
        o_ref[j] = acc[:, j * LANES:(j + 1) * LANES].astype(o_ref.dtype)


def _inproj_call(h, w, tm=1024, tn=1024):
    t, d = h.shape
    n = w.shape[1]
    return pl.pallas_call(
        functools.partial(_inproj_kernel, rows_per_cast=min(256, d)),
        grid=(n // tn, t // tm),
        in_specs=[pl.BlockSpec((tm, d), lambda j, i: (i, 0)),
                  pl.BlockSpec((d, tn), lambda j, i: (0, j))],
        out_specs=pl.BlockSpec((tn // LANES, tm, LANES), lambda j, i: (j, i, 0)),
        out_shape=jax.ShapeDtypeStruct((n // LANES, t, LANES), BF16),
        scratch_shapes=[pltpu.VMEM((d, tn), BF16)],
        compiler_params=_cparams(("arbitrary", "arbitrary")),
        name="in_proj",
    )(h, w)


def _ret_kernel(lg_ref, q_ref, k_ref, v_ref, g_ref, cos_ref, sin_ref, gn_ref, o_ref, *, blk, nblk):
    dk = q_ref.shape[-1]
    lg = lg_ref[pl.program_id(1)]
    row = lax.broadcasted_iota(I32, (blk, blk), 0)
    col = lax.broadcasted_iota(I32, (blk, blk), 1)
    dist = jnp.abs(row - col).astype(F32)
    visible = (col // CHUNK) <= (row // CHUNK)
    dmask = jnp.where(visible, jnp.exp(lg * dist), 0.0)
    rr = lax.broadcasted_iota(I32, (blk, dk), 0).astype(F32)
    qdec = jnp.exp(lg * (rr + 1.0))
    kdec = jnp.exp(lg * (blk - 1.0 - rr))
    bdec = jnp.exp(lg * jnp.full((1, HEAD_W), float(blk), F32))
    even = (lax.broadcasted_iota(I32, (blk, dk), 1) & 1) == 0
    scale = dk ** -0.5

    def body(i, state):
        rows = pl.ds(pl.multiple_of(i * blk, blk), blk)
        cos = cos_ref[rows, :]
        sin = sin_ref[rows, :]

        def rot(x):
            partner = jnp.where(even, pltpu.roll(x, dk - 1, 1), pltpu.roll(x, 1, 1))
            return x * cos + partner * sin

        qr = rot(q_ref[rows, :].astype(F32)) * scale
        kr = rot(k_ref[rows, :].astype(F32))
        v = v_ref[rows, :]
        s = lax.dot_general(qr.astype(BF16), kr.astype(BF16), (((1,), (1,)), ((), ())),
                            preferred_element_type=F32) * dmask
        out = jnp.dot(s.astype(BF16), v, preferred_element_type=F32)
        out = out + jnp.dot((qr * qdec).astype(BF16), state.astype(BF16),
                            preferred_element_type=F32)
        kv = lax.dot_general((kr * kdec).astype(BF16), v, (((0,), (0,)), ((), ())),
                             preferred_element_type=F32)
        state = state * bdec + kv
        ms = jnp.mean(out * out, axis=-1, keepdims=True)
        normed = out * lax.rsqrt(ms + EPS) * gn_ref[...]
        g = g_ref[rows, :].astype(F32)
        o_ref[rows, :] = (g * jax.nn.sigmoid(g) * normed).astype(o_ref.dtype)
        return state

    lax.fori_loop(0, nblk, body, jnp.zeros((dk, HEAD_W), F32), unroll=True)


def _ret_call(proj, cos, sin, log_gamma, gn_g, batch, seq, blk=SEQ_BLK):
    nh = RET_HEADS

    def head_spec(base):
        return pl.BlockSpec((None, seq, LANES), lambda b, h: (base + h, b, 0))

    return pl.pallas_call(
        functools.partial(_ret_kernel, blk=blk, nblk=seq // blk),
        grid=(batch, nh),
        in_specs=[pl.BlockSpec(memory_space=pltpu.SMEM),
                  head_spec(0), head_spec(nh), head_spec(2 * nh), head_spec(3 * nh),
                  pl.BlockSpec((seq, LANES), lambda b, h: (0, 0)),
                  pl.BlockSpec((seq, LANES), lambda b, h: (0, 0)),
                  pl.BlockSpec((None, 1, LANES), lambda b, h: (h, 0, 0))],
        out_specs=pl.BlockSpec((None, seq, LANES), lambda b, h: (h, b, 0)),
        out_shape=jax.ShapeDtypeStruct((nh, batch * seq, LANES), BF16),
        compiler_params=_cparams(("arbitrary", "arbitrary")),
        name="retention",
    )(log_gamma, proj, proj, proj, proj, cos, sin, gn_g.reshape(nh, 1, LANES))


def _diff_kernel(tbl_ref, q_ref, k_ref, v_ref, qg_ref, kg_ref, lam_ref, sg_ref, bidx_ref, o_ref,
                 qz_s, kn_s, bias_s, s_s, p_s, *, blk, nblk):
    h = pl.program_id(0)
    b = pl.program_id(1)
    far_bucket = NUM_BUCKETS // 2 - 1

    @pl.when(b == 0)
    def _build_bias():
        row = lax.broadcasted_iota(I32, (blk, blk), 0)
        col = lax.broadcasted_iota(I32, (blk, blk), 1)
        visible = (col // CHUNK) <= (row // CHUNK)
        for d in range(2):
            idx = bidx_ref[d]
            bias = jnp.zeros((blk, blk), F32)
            for bucket in range(NUM_BUCKETS):
                bias = jnp.where(idx == bucket, tbl_ref[bucket, h], bias)
            if d == 0:
                bias = jnp.where(visible, bias, NEG_BIG)
            bias_s[d, 0:blk, :] = bias
            bias_s[d, blk:2 * blk, :] = bias

    lo = lax.broadcasted_iota(I32, (blk, HEAD_W), 1) < DIFF_DK
    scale = DIFF_DK ** -0.5
    same_half = ((lax.broadcasted_iota(I32, (HEAD_W, HEAD_W), 0) // DIFF_DK)
                 == (lax.broadcasted_iota(I32, (HEAD_W, HEAD_W), 1) // DIFF_DK)).astype(BF16)

    def half_norm(x, g):
        ms = jnp.dot((x * x).astype(BF16), same_half, preferred_element_type=F32) * (1.0 / DIFF_DK)
        return x * lax.rsqrt(ms + EPS) * g

    def prep(i, carry):
        rows = pl.ds(pl.multiple_of(i * blk, blk), blk)
        qn = half_norm(q_ref[rows, :].astype(F32), qg_ref[...]) * scale
        kn = half_norm(k_ref[rows, :].astype(F32), kg_ref[...])
        base = pl.multiple_of(i * 2 * blk, 2 * blk)
        qz_s[pl.ds(base, blk), :] = jnp.where(lo, qn, 0.0).astype(BF16)
        qz_s[pl.ds(base + blk, blk), :] = jnp.where(lo, 0.0, qn).astype(BF16)
        kn_s[rows, :] = kn.astype(BF16)
        return carry

    lax.fori_loop(0, nblk, prep, 0)

    lam = (jnp.exp(jnp.sum(lam_ref[0:1, :] * lam_ref[1:2, :], axis=-1, keepdims=True))
           - jnp.exp(jnp.sum(lam_ref[2:3, :] * lam_ref[3:4, :], axis=-1, keepdims=True))
           + LAM_INIT)
    c_far = tbl_ref[far_bucket, h]

    def lane_fold(x, op):
        out = x[:, 0:LANES]
        for c in range(1, blk // LANES):
            out = op(out, x[:, c * LANES:(c + 1) * LANES])
        return out

    for i in range(nblk):
        qz = qz_s[i * 2 * blk:(i + 1) * 2 * blk, :]
        m_t = jnp.full((2 * blk, LANES), NEG_BIG, F32)
        for j in range(i + 1):
            keys = slice(j * blk, (j + 1) * blk)
            s = lax.dot_general(qz, kn_s[keys, :], (((1,), (1,)), ((), ())),
                                preferred_element_type=F32)
            s = s + (bias_s[0] if j == i else bias_s[1] if j == i - 1 else c_far)
            s_s[:, keys] = s
            m_t = jnp.maximum(m_t, lane_fold(s, jnp.maximum))
        m = jnp.max(m_t, axis=-1, keepdims=True)
        l_t = jnp.zeros((2 * blk, LANES), F32)
        for j in range(i + 1):
            keys = slice(j * blk, (j + 1) * blk)
            p = jnp.exp(s_s[:, keys] - m)
            l_t = l_t + lane_fold(p, jnp.add)
            p_s[:, keys] = p.astype(BF16)
        l = jnp.sum(l_t, axis=-1, keepdims=True)
        kend = (i + 1) * blk
        o = jnp.dot(p_s[:, 0:kend], v_ref[0:kend, :], preferred_element_type=F32) / l
        att = o[0:blk, :] - lam * o[blk:2 * blk, :]
        ms = jnp.mean(att * att, axis=-1, keepdims=True)
        y = att * lax.rsqrt(ms + EPS) * sg_ref[...] * (1.0 - LAM_INIT)
        o_ref[i * blk:(i + 1) * blk, :] = y.astype(o_ref.dtype)


def _t5_bucket(rel):
    nb = NUM_BUCKETS // 2
    max_exact = nb // 2
    base = jnp.where(rel > 0, nb, 0)
    n = jnp.abs(rel)
    large = max_exact + (jnp.log(jnp.maximum(n, 1).astype(jnp.float32) / max_exact)
                         / math.log(MAX_DISTANCE / max_exact) * (nb - max_exact)).astype(jnp.int32)
    large = jnp.minimum(large, nb - 1)
    return base + jnp.where(n < max_exact, n, large)


def _diff_call(proj, rel_table, qg, kg, lam_vecs, sg, batch, seq, blk=SEQ_BLK):
    nh = DIFF_HEADS
    first = 4 * RET_HEADS
    r = jnp.arange(blk, dtype=I32)
    rel0 = r[None, :] - r[:, None]
    bidx = jnp.stack([_t5_bucket(rel0), _t5_bucket(rel0 - blk)]).astype(I32) & (NUM_BUCKETS - 1)

    def head_spec(base):
        return pl.BlockSpec((None, seq, LANES), lambda h, b: (base + h, b, 0))

    def vec_spec():
        return pl.BlockSpec((1, LANES), lambda h, b: (0, 0))

    return pl.pallas_call(
        functools.partial(_diff_kernel, blk=blk, nblk=seq // blk),
        grid=(nh, batch),
        in_specs=[pl.BlockSpec(memory_space=pltpu.SMEM),
                  head_spec(first), head_spec(first + nh), head_spec(first + 2 * nh),
                  vec_spec(), vec_spec(),
                  pl.BlockSpec((4, DIFF_DK), lambda h, b: (0, 0)),
                  vec_spec(),
                  pl.BlockSpec((2, blk, blk), lambda h, b: (0, 0, 0))],
        out_specs=pl.BlockSpec((None, seq, LANES), lambda h, b: (h, b, 0)),
        out_shape=jax.ShapeDtypeStruct((nh, batch * seq, LANES), BF16),
        scratch_shapes=[pltpu.VMEM((2 * seq, LANES), BF16),
                        pltpu.VMEM((seq, LANES), BF16),
                        pltpu.VMEM((2, 2 * blk, blk), F32),
                        pltpu.VMEM((2 * blk, seq), F32),
                        pltpu.VMEM((2 * blk, seq), BF16)],
        compiler_params=_cparams(("arbitrary", "arbitrary")),
        name="diff_attn",
    )(rel_table, proj, proj, proj,
      jnp.tile(qg, 2).reshape(1, LANES), jnp.tile(kg, 2).reshape(1, LANES),
      lam_vecs, sg.reshape(1, LANES), bidx)


def _outproj_kernel(yr_ref, yd_ref, x_ref, w_hbm, g_ref, wr_ref, br_ref,
                    x1_ref, h2p_ref, lt_ref, wb_s, wstage_s, wr2_s, wsem,
                    *, rows_per_cast):
    ne = lt_ref.shape[0]
    nchunk = wb_s.shape[0] // rows_per_cast

    def w_copy(c):
        return pltpu.make_async_copy(w_hbm.at[pl.ds(c * rows_per_cast, rows_per_cast)],
                                     wstage_s.at[c % 2], wsem.at[c % 2])

    @pl.when(pl.program_id(0) == 0)
    def _stage_weights():
        w_copy(0).start()
        for c in range(nchunk):
            if c + 1 < nchunk:
                w_copy(c + 1).start()
            w_copy(c).wait()
            wb_s[c * rows_per_cast:(c + 1) * rows_per_cast, :] = wstage_s[c % 2].astype(BF16)
        wr = wr_ref[...]
        wr_hi = wr.astype(BF16)
        wr2_s[:, 0:LANES] = wr_hi
        wr2_s[:, LANES:2 * LANES] = (wr - wr_hi.astype(F32)).astype(BF16)

    y = jnp.concatenate([yr_ref[j] for j in range(yr_ref.shape[0])]
                        + [yd_ref[j] for j in range(yd_ref.shape[0])], axis=-1)
    x1 = x_ref[...] + jnp.dot(y, wb_s[...], preferred_element_type=F32)
    x1_ref[...] = x1
    ms = jnp.mean(x1 * x1, axis=-1, keepdims=True)
    h2 = x1 * lax.rsqrt(ms + EPS) * g_ref[...]
    h_hi = h2.astype(BF16)
    h_lo = (h2 - h_hi.astype(F32)).astype(BF16)
    parts = (jnp.dot(h_hi, wr2_s[...], preferred_element_type=F32)
             + jnp.dot(h_lo, wr2_s[...], preferred_element_type=F32))
    logits = parts[:, 0:LANES] + parts[:, LANES:2 * LANES]
    lt_ref[...] = logits.T[0:ne, :] + br_ref[...]
    half = h2.shape[1] // 2
    packed = pltpu.pack_elementwise([h2[:, :half], h2[:, half:]], packed_dtype=BF16)
    h2p_ref[...] = packed.reshape(h2p_ref.shape)


def _outproj_call(y_ret, y_diff, x2, w_out, g, w_router, b_router, tm=512):
    t, d = x2.shape
    ne = w_router.shape[1]
    nhr, nhd = y_ret.shape[0], y_diff.shape[0]
    rows_per_cast = min(256, d)
    wr_pad = jnp.pad(w_router, ((0, 0), (0, LANES - ne)))
    return pl.pallas_call(
        functools.partial(_outproj_kernel, rows_per_cast=rows_per_cast),
        grid=(t // tm,),
        in_specs=[pl.BlockSpec((nhr, tm, LANES), lambda i: (0, i, 0)),
                  pl.BlockSpec((nhd, tm, LANES), lambda i: (0, i, 0)),
                  pl.BlockSpec((tm, d), lambda i: (i, 0)),
                  pl.BlockSpec(memory_space=pl.ANY),
                  pl.BlockSpec((1, d), lambda i: (0, 0)),
                  pl.BlockSpec((d, LANES), lambda i: (0, 0)),
                  pl.BlockSpec((ne, 1), lambda i: (0, 0))],
        out_specs=[pl.BlockSpec((tm, d), lambda i: (i, 0)),
                   pl.BlockSpec((tm, d // 2 // LANES, LANES), lambda i: (i, 0, 0)),
                   pl.BlockSpec((ne, tm), lambda i: (0, i))],
        out_shape=[jax.ShapeDtypeStruct((t, d), F32),
                   jax.ShapeDtypeStruct((t, d // 2 // LANES, LANES), U32),
                   jax.ShapeDtypeStruct((ne, t), F32)],
        scratch_shapes=[pltpu.VMEM((d, d), BF16),
                        pltpu.VMEM((2, rows_per_cast, d), F32),
                        pltpu.VMEM((d, 2 * LANES), BF16),
                        pltpu.SemaphoreType.DMA((2,))],
        compiler_params=_cparams(("arbitrary",)),
        name="out_proj_router",
    )(y_ret, y_diff, x2, w_out, g.reshape(1, d), wr_pad, b_router.reshape(ne, 1))


def _route_kernel(lt_ref, pos_ref, gate_ref, cnt_ref, nt_ref, ot_ref, idx_s, rank_s, *, tb, row_tile):
    ne, t = lt_ref.shape
    e_iota = lax.broadcasted_iota(I32, (ne, tb), 0)
    upper = (lax.broadcasted_iota(I32, (tb, tb), 0)
             < lax.broadcasted_iota(I32, (tb, tb), 1)).astype(BF16)

    def pass_a(i, running):
        cols = pl.ds(pl.multiple_of(i * tb, tb), tb)
        l = lt_ref[:, cols]
        tops, hots = [], []
        for k in range(TOP_K):
            m = jnp.max(l, axis=0, keepdims=True)
            idx = jnp.min(jnp.where(l == m, e_iota, ne), axis=0, keepdims=True)
            hot = e_iota == idx
            l = jnp.where(hot, -jnp.inf, l)
            idx_s[k:k + 1, cols] = idx
            tops.append(m)
            hots.append(hot)
        exps = [jnp.exp(m - tops[0]) for m in tops]
        denom = exps[0] + exps[1] + exps[2] + exps[3]
        for k in range(TOP_K):
            gate_ref[k:k + 1, cols] = exps[k] / denom
        hot_all = jnp.zeros((ne, tb), F32)
        for hot in hots:
            hot_all = hot_all + hot.astype(F32)
        before = running + jnp.dot(hot_all.astype(BF16), upper, preferred_element_type=F32)
        for k in range(TOP_K):
            rank_s[k:k + 1, cols] = jnp.sum(jnp.where(hots[k], before, 0.0), axis=0, keepdims=True)
        return running + jnp.sum(hot_all, axis=1, keepdims=True)

    cnt = lax.fori_loop(0, t // tb, pass_a, jnp.zeros((ne, 1), F32))
    ntile = jnp.floor((cnt + (row_tile - 1.0)) * (1.0 / row_tile))
    lower = (lax.broadcasted_iota(I32, (ne, ne), 1)
             < lax.broadcasted_iota(I32, (ne, ne), 0)).astype(BF16)
    otile = jnp.dot(lower, jnp.broadcast_to(ntile, (ne, LANES)).astype(BF16),
                    preferred_element_type=F32)
    cnt_ref[...] = jnp.broadcast_to(cnt, (ne, LANES)).astype(I32)
    nt_ref[...] = jnp.broadcast_to(ntile, (ne, LANES)).astype(I32)
    ot_ref[...] = otile.astype(I32)
    off_rows = otile[:, 0:1] * float(row_tile)

    def pass_b(i, carry):
        cols = pl.ds(pl.multiple_of(i * tb, tb), tb)
        for k in range(TOP_K):
            hot = e_iota == idx_s[k:k + 1, cols]
            off = jnp.sum(jnp.where(hot, off_rows, 0.0), axis=0, keepdims=True)
            pos_ref[k:k + 1, cols] = (rank_s[k:k + 1, cols] + off).astype(I32)
        return carry

    lax.fori_loop(0, t // tb, pass_b, 0)


def _route_call(logits_t, tb=256, row_tile=ROW_TILE):
    ne, t = logits_t.shape
    return pl.pallas_call(
        functools.partial(_route_kernel, tb=tb, row_tile=row_tile),
        out_shape=[jax.ShapeDtypeStruct((TOP_K, t), I32),
                   jax.ShapeDtypeStruct((TOP_K, t), F32),
                   jax.ShapeDtypeStruct((ne, LANES), I32),
                   jax.ShapeDtypeStruct((ne, LANES), I32),
                   jax.ShapeDtypeStruct((ne, LANES), I32)],
        scratch_shapes=[pltpu.VMEM((TOP_K, t), I32), pltpu.VMEM((TOP_K, t), F32)],
        compiler_params=_cparams(None),
        name="route",
    )(logits_t)


_PAD_PIECES = tuple(ROW_TILE >> (k + 1) for k in range(ROW_TILE.bit_length() - 1))


def _dispatch_kernel(pos_ref, pad_start_ref, pad_n_ref, src_ref, dst_ref, zero_s, sem, zsem,
                     *, tb, ne):
    i = pl.program_id(0)

    def pad_copies(e):
        n = pad_n_ref[e]
        start = pad_start_ref[e]
        out = []
        for piece in _PAD_PIECES:
            at = start + (n & ~(2 * piece - 1))
            out.append(((n & piece) != 0, pltpu.make_async_copy(
                zero_s.at[pl.ds(0, piece)], dst_ref.at[pl.ds(at, piece)], zsem)))
        return out

    @pl.when(i == 0)
    def _zero_pads():
        zero_s[...] = jnp.zeros(zero_s.shape, zero_s.dtype)

        def issue(e, c):
            for cond, cp in pad_copies(e):
                @pl.when(cond)
                def _():
                    cp.start()
            return c

        def drain(e, c):
            for cond, cp in pad_copies(e):
                @pl.when(cond)
                def _():
                    cp.wait()
            return c

        lax.fori_loop(0, ne, issue, 0)
        lax.fori_loop(0, ne, drain, 0)

    def issue_rows(tl, c):
        for k in range(TOP_K):
            pltpu.make_async_copy(src_ref.at[tl], dst_ref.at[pos_ref[k, tl]], sem).start(priority=k % 2)
        return c

    lax.fori_loop(0, tb, issue_rows, 0, unroll=4)
    for k in range(TOP_K):
        pltpu.make_async_copy(src_ref, dst_ref.at[pl.ds(0, tb)], sem).wait()


def _dispatch_call(pos, pad_start, pad_n, h2p, n_slots, tb=512):
    t = h2p.shape[0]
    row = h2p.shape[1:]
    ne = pad_n.shape[0]
    return pl.pallas_call(
        functools.partial(_dispatch_kernel, tb=tb, ne=ne),
        grid=(t // tb,),
        in_specs=[pl.BlockSpec((TOP_K, tb), lambda i: (0, i), memory_space=pltpu.SMEM),
                  pl.BlockSpec(memory_space=pltpu.SMEM),
                  pl.BlockSpec(memory_space=pltpu.SMEM),
                  pl.BlockSpec((tb, *row), lambda i: (i, 0, 0))],
        out_specs=pl.BlockSpec(memory_space=pl.ANY),
        out_shape=jax.ShapeDtypeStruct((n_slots, *row), U32),
        scratch_shapes=[pltpu.VMEM((_PAD_PIECES[0], *row), U32),
                        pltpu.SemaphoreType.DMA(()),
                        pltpu.SemaphoreType.DMA(())],
        compiler_params=pltpu.CompilerParams(dimension_semantics=("arbitrary",),
                                             vmem_limit_bytes=VMEM_LIMIT, has_side_effects=True),
        name="dispatch",
    )(pos, pad_start, pad_n, h2p)


def _moe_kernel(we_ref, wt_ref, wn_ref, xs_ref, win_ref, bin_ref, wout_ref, bo_ref,
                ys_ref, xu_s, acc_s, wg_st, wu_st, wo_st, yst_s, xsem, wsem, ysem,
                *, tmx, fc, nj):
    w = pl.program_id(0)
    nw = pl.num_programs(0)
    nt = wn_ref[w]
    t0 = wt_ref[w]
    half = xu_s.shape[-2] * xu_s.shape[-1]
    de = nj * fc
    nxt = jnp.minimum(w + 1, nw - 1)
    has_next = jnp.logical_and(w + 1 < nw, wn_ref[nxt] > 0)
    xslot = w % 2

    def local_rows(r, n=1):
        return pl.ds(pl.multiple_of(r * tmx, tmx), n * tmx)

    def x_copy(item_t0, r, slot):
        src = pl.ds(pl.multiple_of((item_t0 + r) * tmx, tmx), tmx)
        return pltpu.make_async_copy(xs_ref.at[src], xu_s.at[slot, local_rows(r)], xsem.at[slot])

    def y_copy(r, slot):
        dst = pl.ds(pl.multiple_of((t0 + r) * tmx, tmx), tmx)
        return pltpu.make_async_copy(yst_s.at[slot], ys_ref.at[dst], ysem.at[slot])

    def w_copies(expert, j, slot):
        c0 = pl.multiple_of(j * fc, fc)
        return (pltpu.make_async_copy(win_ref.at[expert, :, pl.ds(c0, fc)], wg_st.at[slot], wsem.at[slot]),
                pltpu.make_async_copy(win_ref.at[expert, :, pl.ds(de + c0, fc)], wu_st.at[slot],
                                      wsem.at[slot]),
                pltpu.make_async_copy(wout_ref.at[expert, pl.ds(c0, fc), :], wo_st.at[slot],
                                      wsem.at[slot]))

    def start_weights(expert, j, slot):
        for n, cp in enumerate(w_copies(expert, j, slot)):
            cp.start(priority=n % 2)

    def start_rows(item_t0, item_nt, slot):
        def go(r, c):
            x_copy(item_t0, r, slot).start()
            return c
        lax.fori_loop(0, item_nt, go, 0)

    @pl.when(w == 0)
    def _prologue():
        start_weights(we_ref[0], 0, 0)
        start_rows(t0, nt, 0)

    @pl.when(nt > 0)
    def _work():
        def chunk(j, carry):
            slot = j % 2

            @pl.when(j + 1 < nj)
            def _():
                start_weights(we_ref[w], j + 1, 1 - slot)

            @pl.when(jnp.logical_and(j + 1 == nj, has_next))
            def _():
                start_weights(we_ref[nxt], 0, 1 - slot)

            for cp in w_copies(we_ref[w], j, slot):
                cp.wait()
            stage = (wg_st.at[slot], wu_st.at[slot], wo_st.at[slot])
            biases = (bin_ref[j], bin_ref[nj + j])

            @pl.when(j == 0)
            def _rows_ready():
                def finish(r, c):
                    x_copy(t0, r, xslot).wait()
                    acc_s[local_rows(r), :] = jnp.zeros((tmx, acc_s.shape[1]), F32)
                    return c
                lax.fori_loop(0, nt, finish, 0)

                @pl.when(has_next)
                def _():
                    start_rows(wt_ref[nxt], wn_ref[nxt], 1 - xslot)

            _moe_passes(xu_s.at[xslot], acc_s, stage, biases, nt, local_rows, half)
            return carry

        lax.fori_loop(0, nj, chunk, 0)

        def y_wait(slot):
            y_copy(0, slot).wait()

        prev_nt = jnp.where(w > 0, wn_ref[jnp.maximum(w - 1, 0)], 0)

        @pl.when(prev_nt >= 1)
        def _():
            y_wait((prev_nt - 1) % 2)

        @pl.when(prev_nt >= 2)
        def _():
            y_wait(prev_nt % 2)

        def emit(r, c):
            slot = r % 2

            @pl.when(r >= 2)
            def _():
                y_wait(slot)

            y = acc_s[local_rows(r), :] + bo_ref[...]
            yst_s[slot] = pltpu.pack_elementwise([y[:, :half], y[:, half:]],
                                                 packed_dtype=BF16).reshape(yst_s.shape[1:])
            y_copy(r, slot).start()
            return c

        lax.fori_loop(0, nt, emit, 0)

        @pl.when(jnp.logical_not(has_next))
        def _drain():
            @pl.when(nt >= 2)
            def _():
                y_wait(nt % 2)

            y_wait((nt - 1) % 2)


def _moe_passes(xu_ref, acc_s, stage, biases, nt, local_rows, half):
    wg_ref, wu_ref, wo_ref = stage
    bg, bu = biases

    def proj(x_lo, x_hi, w_ref, b):
        return (jnp.dot(x_lo, w_ref[0:half, :].astype(BF16), preferred_element_type=F32)
                + jnp.dot(x_hi, w_ref[half:2 * half, :].astype(BF16), preferred_element_type=F32) + b)

    def rows_step(rows):
        xw = xu_ref[rows]
        xw = xw.reshape(xw.shape[0], half)
        x_lo = pltpu.unpack_elementwise(xw, index=0, packed_dtype=BF16,
                                        unpacked_dtype=F32).astype(BF16)
        x_hi = pltpu.unpack_elementwise(xw, index=1, packed_dtype=BF16,
                                        unpacked_dtype=F32).astype(BF16)
        gg = jnp.minimum(proj(x_lo, x_hi, wg_ref, bg), SWIGLU_LIMIT)
        uu = jnp.clip(proj(x_lo, x_hi, wu_ref, bu), -SWIGLU_LIMIT, SWIGLU_LIMIT)
        act = (uu + 1.0) * (gg * jax.nn.sigmoid(SWIGLU_ALPHA * gg))
        acc_s[rows, :] += jnp.dot(act.astype(BF16), wo_ref[...].astype(BF16),
                                  preferred_element_type=F32)

    def big(pi, c):
        rows_step(local_rows(PASS_TILES * pi, PASS_TILES))
        return c

    lax.fori_loop(0, nt // PASS_TILES, big, 0)
    piece = PASS_TILES // 2
    while piece >= 1:
        @pl.when((nt & piece) != 0)
        def _(piece=piece):
            rows_step(local_rows(nt & ~(2 * piece - 1), piece))
        piece //= 2


def _moe_call(work_e, work_t0, work_nt, xs, w_in, b_in, w_out, b_out,
              tmx=ROW_TILE, group=GROUP_TILES, fc=F_CHUNK):
    ne, d, de2 = w_in.shape
    de = de2 // 2
    nj = de // fc
    assert nj % 2 == 0, "weight stage slots alternate per chunk and restart at 0 per work item"
    n_work = work_e.shape[0]
    n_slots = xs.shape[0]
    row = xs.shape[1:]

    grid_spec = pltpu.PrefetchScalarGridSpec(
        num_scalar_prefetch=3,
        grid=(n_work,),
        in_specs=[
            pl.BlockSpec(memory_space=pl.ANY),
            pl.BlockSpec(memory_space=pl.ANY),
            pl.BlockSpec((None, 2 * nj, 1, fc), lambda w, we, wt, wn: (we[w], 0, 0, 0)),
            pl.BlockSpec(memory_space=pl.ANY),
            pl.BlockSpec((None, 1, d), lambda w, we, wt, wn: (we[w], 0, 0)),
        ],
        out_specs=pl.BlockSpec(memory_space=pl.ANY),
        scratch_shapes=[pltpu.VMEM((2, group * tmx, *row), U32),
                        pltpu.VMEM((group * tmx, d), F32),
                        pltpu.VMEM((2, d, fc), F32),
                        pltpu.VMEM((2, d, fc), F32),
                        pltpu.VMEM((2, fc, d), F32),
                        pltpu.VMEM((2, tmx, *row), U32),
                        pltpu.SemaphoreType.DMA((2,)),
                        pltpu.SemaphoreType.DMA((2,)),
                        pltpu.SemaphoreType.DMA((2,))],
    )
    return pl.pallas_call(
        functools.partial(_moe_kernel, tmx=tmx, fc=fc, nj=nj),
        grid_spec=grid_spec,
        out_shape=jax.ShapeDtypeStruct((n_slots, *row), U32),
        compiler_params=pltpu.CompilerParams(dimension_semantics=("arbitrary",),
                                             vmem_limit_bytes=VMEM_LIMIT, has_side_effects=True),
        name="moe_experts",
    )(work_e, work_t0, work_nt, xs, w_in, b_in.reshape(ne, 2 * nj, 1, fc), w_out,
      b_out.reshape(ne, 1, d))


def _combine_kernel(pos_ref, pos_next_ref, x1_ref, gate_ref, ys_ref, o_ref, ybuf, sem, *, tb):
    half = ybuf.shape[-2] * ybuf.shape[-1]
    i = pl.program_id(0)
    slot = i % 2

    def gather(p_ref, s):
        def issue(tl, c):
            for k in range(TOP_K):
                pltpu.make_async_copy(ys_ref.at[p_ref[k, tl]], ybuf.at[s, k, tl],
                                      sem.at[s]).start(priority=k % 2)
            return c
        lax.fori_loop(0, tb, issue, 0, unroll=4)

    @pl.when(i == 0)
    def _():
        gather(pos_ref, 0)

    @pl.when(i + 1 < pl.num_programs(0))
    def _():
        gather(pos_next_ref, 1 - slot)

    for k in range(TOP_K):
        pltpu.make_async_copy(ys_ref.at[pl.ds(0, tb)], ybuf.at[slot, k], sem.at[slot]).wait()

    lo = x1_ref[:, 0:half]
    hi = x1_ref[:, half:2 * half]
    for k in range(TOP_K):
        g = gate_ref[:, k:k + 1]
        yw = ybuf[slot, k].reshape(tb, half)
        lo = lo + g * pltpu.unpack_elementwise(yw, index=0, packed_dtype=BF16, unpacked_dtype=F32)
        hi = hi + g * pltpu.unpack_elementwise(yw, index=1, packed_dtype=BF16, unpacked_dtype=F32)
    o_ref[:, 0:half] = lo
    o_ref[:, half:2 * half] = hi


def _combine_call(pos, x1, gate_tk, ys, tb=256):
    t, d = x1.shape
    row = ys.shape[1:]
    last = t // tb - 1
    return pl.pallas_call(
        functools.partial(_combine_kernel, tb=tb),
        grid=(t // tb,),
        in_specs=[pl.BlockSpec((TOP_K, tb), lambda i: (0, i), memory_space=pltpu.SMEM),
                  pl.BlockSpec((TOP_K, tb), lambda i: (0, jnp.minimum(i + 1, last)),
                               memory_space=pltpu.SMEM),
                  pl.BlockSpec((tb, d), lambda i: (i, 0)),
                  pl.BlockSpec((tb, TOP_K), lambda i: (i, 0)),
                  pl.BlockSpec(memory_space=pl.ANY)],
        out_specs=pl.BlockSpec((tb, d), lambda i: (i, 0)),
        out_shape=jax.ShapeDtypeStruct((t, d), F32),
        scratch_shapes=[pltpu.VMEM((2, TOP_K, tb, *row), U32),
                        pltpu.SemaphoreType.DMA((2,))],
        compiler_params=_cparams(("arbitrary",)),
        name="combine",
    )(pos, pos, x1, gate_tk, ys)


def _work_list(ntile, otile, n_work, group):
    ne = ntile.shape[0]
    items = (ntile + group - 1) // group
    ends = jnp.cumsum(items)
    total = ends[-1]
    w = jnp.arange(n_work, dtype=I32)
    valid = w < total
    e_w = jnp.clip(jnp.sum((ends[None, :] <= w[:, None]).astype(I32), axis=1), 0, ne - 1)
    local = w - (ends - items)[e_w]
    t0_w = otile[e_w] + local * group
    nt_w = jnp.where(valid, jnp.clip(ntile[e_w] - local * group, 0, group), 0)
    e_last = e_w[jnp.maximum(total - 1, 0)]
    return (jnp.where(valid, e_w, e_last).astype(I32), jnp.where(valid, t0_w, 0).astype(I32),
            nt_w.astype(I32))


def _rotary_tables(seq, dk):
    inv_freq = ROPE_BASE ** (-jnp.arange(0, dk, 2, dtype=F32) / dk)
    ang = jnp.arange(seq, dtype=I32).astype(F32)[:, None] * inv_freq[None, :]
    cos = jnp.repeat(jnp.cos(ang), 2, axis=1)
    sin = jnp.stack([-jnp.sin(ang), jnp.sin(ang)], axis=-1).reshape(seq, dk)
    return cos, sin


def kernel(x, norm_mix_g, w_mix_in, ret_gn_g, q_norm_g, k_norm_g, lambda_q1, lambda_k1, lambda_q2, lambda_k2, diff_subln_g, rel_bias_table, w_mix_out, norm_ffn_g, w_router, b_router, w_exp_in, b_exp_in, w_exp_out, b_exp_out):
    batch, seq, d = x.shape
    t = batch * seq
    depth = norm_mix_g.shape[0]
    ne = w_router.shape[-1]
    n_tiles_max = (t * TOP_K) // ROW_TILE + ne
    n_work = ne + (n_tiles_max - ne) // GROUP_TILES
    cos, sin = _rotary_tables(seq, HEAD_W)
    log_gamma = jnp.log1p(-(2.0 ** (-5.0 - jnp.arange(RET_HEADS, dtype=F32))))

    x2 = x.reshape(t, d)
    for l in range(depth):
        h = _rmsnorm_call(x2, norm_mix_g[l])
        proj = _inproj_call(h, w_mix_in[l])
        y_ret = _ret_call(proj, cos, sin, log_gamma, ret_gn_g[l], batch, seq)
        lam_vecs = jnp.stack([lambda_q1[l], lambda_k1[l], lambda_q2[l], lambda_k2[l]])
        y_diff = _diff_call(proj, rel_bias_table, q_norm_g[l], k_norm_g[l], lam_vecs,
                            diff_subln_g[l], batch, seq)
        x1, h2p, logits_t = _outproj_call(y_ret, y_diff, x2, w_mix_out[l], norm_ffn_g[l],
                                          w_router[l], b_router[l])
        pos, gate, cnt, ntile, otile = _route_call(logits_t)
        cnt, ntile, otile = cnt[:, 0], ntile[:, 0], otile[:, 0]
        work_e, work_t0, work_nt = _work_list(ntile, otile, n_work, GROUP_TILES)
        xs = _dispatch_call(pos, otile * ROW_TILE + cnt, ntile * ROW_TILE - cnt, h2p,
                            n_tiles_max * ROW_TILE)
        ys = _moe_call(work_e, work_t0, work_nt, xs, w_exp_in[l], b_exp_in[l], w_exp_out[l],
                       b_exp_out[l])
        x2 = _combine_call(pos, x1, gate.T, ys)
    return x2.reshape(batch, seq, d)
```

```python
import functools
import math

import jax
import jax.numpy as jnp
from jax import lax
from jax.experimental import pallas as pl
from jax.experimental.pallas import tpu as pltpu

F32 = jnp.float32
BF16 = jnp.bfloat16
I32 = jnp.int32
U32 = jnp.uint32

EPS = 1e-6
CHUNK = 64
RET_HEADS = 8
DIFF_HEADS = 8
HEAD_W = 128
DIFF_DK = 64
ROPE_BASE = 10000.0
NUM_BUCKETS = 32
MAX_DISTANCE = 128
TOP_K = 4
SWIGLU_LIMIT = 7.0
SWIGLU_ALPHA = 1.702
LAM_INIT = 0.8 - 0.6 * math.exp(-0.3 * 0)

LANES = 128
NEG_BIG = -1e30
VMEM_LIMIT = 56 * 1024 * 1024

SEQ_BLK = 256
ROW_TILE = 256
GROUP_TILES = 8
PASS_TILES = 4
F_CHUNK = 256


def _cparams(sem, vmem=VMEM_LIMIT):
    return pltpu.CompilerParams(dimension_semantics=sem, vmem_limit_bytes=vmem)


def _inproj_kernel(x_ref, g_ref, w_ref, o_ref, wb_ref, *, rows_per_cast):
    @pl.when(pl.program_id(1) == 0)
    def _():
        d = w_ref.shape[0]
        for c in range(d // rows_per_cast):
            sl = slice(c * rows_per_cast, (c + 1) * rows_per_cast)
            wb_ref[sl, :] = w_ref[sl, :].astype(BF16)

    x = x_ref[...]
    ms = jnp.mean(x * x, axis=-1, keepdims=True)
    h = (x * lax.rsqrt(ms + EPS) * g_ref[...]).astype(BF16)
    acc = jnp.dot(h, wb_ref[...], preferred_element_type=F32)
    for j in range(o_ref.shape[0]):
        o_ref[j] = acc[:, j * LANES:(j + 1) * LANES].astype(o_ref.dtype)


def _inproj_call(x2, g, w, tm=1024, tn=1024):
    t, d = x2.shape
    n = w.shape[1]
    return pl.pallas_call(
        functools.partial(_inproj_kernel, rows_per_cast=min(256, d)),
        grid=(n // tn, t // tm),
        in_specs=[pl.BlockSpec((tm, d), lambda j, i: (i, 0)),
                  pl.BlockSpec((1, d), lambda j, i: (0, 0)),
                  pl.BlockSpec((d, tn), lambda j, i: (0, j))],
        out_specs=pl.BlockSpec((tn // LANES, tm, LANES), lambda j, i: (j, i, 0)),
        out_shape=jax.ShapeDtypeStruct((n // LANES, t, LANES), BF16),
        scratch_shapes=[pltpu.VMEM((d, tn), BF16)],
        compiler_params=_cparams(("arbitrary", "arbitrary")),
        name="in_proj",
    )(x2, g.reshape(1, d), w)


def _ret_kernel(lg_ref, q_ref, k_ref, v_ref, g_ref, cos_ref, sin_ref, gn_ref, o_ref, *, blk, nblk):
    dk = q_ref.shape[-1]
    lg = lg_ref[pl.program_id(1)]
    row = lax.broadcasted_iota(I32, (blk, blk), 0)
    col = lax.broadcasted_iota(I32, (blk, blk), 1)
    dist = jnp.abs(row - col).astype(F32)
    visible = (col // CHUNK) <= (row // CHUNK)
    dmask = jnp.where(visible, jnp.exp(lg * dist), 0.0)
    rr = lax.broadcasted_iota(I32, (blk, dk), 0).astype(F32)
    qdec = jnp.exp(lg * (rr + 1.0))
    kdec = jnp.exp(lg * (blk - 1.0 - rr))
    bdec = jnp.exp(lg * jnp.full((1, HEAD_W), float(blk), F32))
    even = (lax.broadcasted_iota(I32, (blk, dk), 1) & 1) == 0
    scale = dk ** -0.5

    def body(i, state):
        rows = pl.ds(pl.multiple_of(i * blk, blk), blk)
        cos = cos_ref[rows, :]
        sin = sin_ref[rows, :]

        def rot(x):
            partner = jnp.where(even, pltpu.roll(x, dk - 1, 1), pltpu.roll(x, 1, 1))
            return x * cos + partner * sin

        qr = rot(q_ref[rows, :].astype(F32)) * scale
        kr = rot(k_ref[rows, :].astype(F32))
        v = v_ref[rows, :]
        s = lax.dot_general(qr.astype(BF16), kr.astype(BF16), (((1,), (1,)), ((), ())),
                            preferred_element_type=F32) * dmask
        out = jnp.dot(s.astype(BF16), v, preferred_element_type=F32)
        out = out + jnp.dot((qr * qdec).astype(BF16), state.astype(BF16),
                            preferred_element_type=F32)
        kv = lax.dot_general((kr * kdec).astype(BF16), v, (((0,), (0,)), ((), ())),
                             preferred_element_type=F32)
        state = state * bdec + kv
        ms = jnp.mean(out * out, axis=-1, keepdims=True)
        normed = out * lax.rsqrt(ms + EPS) * gn_ref[...]
        g = g_ref[rows, :].astype(F32)
        o_ref[rows, :] = (g * jax.nn.sigmoid(g) * normed).astype(o_ref.dtype)
        return state

    lax.fori_loop(0, nblk, body, jnp.zeros((dk, HEAD_W), F32), unroll=True)


def _ret_call(proj, cos, sin, log_gamma, gn_g, batch, seq, blk=SEQ_BLK):
    nh = RET_HEADS

    def head_spec(base):
        return pl.BlockSpec((None, seq, LANES), lambda b, h: (base + h, b, 0))

    return pl.pallas_call(
        functools.partial(_ret_kernel, blk=blk, nblk=seq // blk),
        grid=(batch, nh),
        in_specs=[pl.BlockSpec(memory_space=pltpu.SMEM),
                  head_spec(0), head_spec(nh), head_spec(2 * nh), head_spec(3 * nh),
                  pl.BlockSpec((seq, LANES), lambda b, h: (0, 0)),
                  pl.BlockSpec((seq, LANES), lambda b, h: (0, 0)),
                  pl.BlockSpec((None, 1, LANES), lambda b, h: (h, 0, 0))],
        out_specs=pl.BlockSpec((None, seq, LANES), lambda b, h: (h, b, 0)),
        out_shape=jax.ShapeDtypeStruct((nh, batch * seq, LANES), BF16),
        compiler_params=_cparams(("arbitrary", "arbitrary")),
        name="retention",
    )(log_gamma, proj, proj, proj, proj, cos, sin, gn_g.reshape(nh, 1, LANES))


def _diff_kernel(tbl_ref, q_ref, k_ref, v_ref, qg_ref, kg_ref, lam_ref, sg_ref, bidx_ref, o_ref,
                 qz_s, kn_s, bias_s, s_s, p_s, *, blk, nblk):
    h = pl.program_id(0)
    b = pl.program_id(1)
    far_bucket = NUM_BUCKETS // 2 - 1

    @pl.when(b == 0)
    def _build_bias():
        row = lax.broadcasted_iota(I32, (blk, blk), 0)
        col = lax.broadcasted_iota(I32, (blk, blk), 1)
        visible = (col // CHUNK) <= (row // CHUNK)
        for d in range(2):
            idx = bidx_ref[d]
            bias = jnp.zeros((blk, blk), F32)
            for bucket in range(NUM_BUCKETS):
                bias = jnp.where(idx == bucket, tbl_ref[bucket, h], bias)
            if d == 0:
                bias = jnp.where(visible, bias, NEG_BIG)
            bias_s[d, 0:blk, :] = bias
            bias_s[d, blk:2 * blk, :] = bias

    lo = lax.broadcasted_iota(I32, (blk, HEAD_W), 1) < DIFF_DK
    scale = DIFF_DK ** -0.5
    same_half = ((lax.broadcasted_iota(I32, (HEAD_W, HEAD_W), 0) // DIFF_DK)
                 == (lax.broadcasted_iota(I32, (HEAD_W, HEAD_W), 1) // DIFF_DK)).astype(BF16)

    def half_norm(x, g):
        ms = jnp.dot((x * x).astype(BF16), same_half, preferred_element_type=F32) * (1.0 / DIFF_DK)
        return x * lax.rsqrt(ms + EPS) * g

    def prep(i, carry):
        rows = pl.ds(pl.multiple_of(i * blk, blk), blk)
        qn = half_norm(q_ref[rows, :].astype(F32), qg_ref[...]) * scale
        kn = half_norm(k_ref[rows, :].astype(F32), kg_ref[...])
        base = pl.multiple_of(i * 2 * blk, 2 * blk)
        qz_s[pl.ds(base, blk), :] = jnp.where(lo, qn, 0.0).astype(BF16)
        qz_s[pl.ds(base + blk, blk), :] = jnp.where(lo, 0.0, qn).astype(BF16)
        kn_s[rows, :] = kn.astype(BF16)
        return carry

    lax.fori_loop(0, nblk, prep, 0)

    lam = (jnp.exp(jnp.sum(lam_ref[0:1, :] * lam_ref[1:2, :], axis=-1, keepdims=True))
           - jnp.exp(jnp.sum(lam_ref[2:3, :] * lam_ref[3:4, :], axis=-1, keepdims=True))
           + LAM_INIT)
    c_far = tbl_ref[far_bucket, h]

    def lane_fold(x, op):
        out = x[:, 0:LANES]
        for c in range(1, blk // LANES):
            out = op(out, x[:, c * LANES:(c + 1) * LANES])
        return out

    for i in range(nblk):
        qz = qz_s[i * 2 * blk:(i + 1) * 2 * blk, :]
        m_t = jnp.full((2 * blk, LANES), NEG_BIG, F32)
        for j in range(i + 1):
            keys = slice(j * blk, (j + 1) * blk)
            s = lax.dot_general(qz, kn_s[keys, :], (((1,), (1,)), ((), ())),
                                preferred_element_type=F32)
            s = s + (bias_s[0] if j == i else bias_s[1] if j == i - 1 else c_far)
            s_s[:, keys] = s
            m_t = jnp.maximum(m_t, lane_fold(s, jnp.maximum))
        m = jnp.max(m_t, axis=-1, keepdims=True)
        l_t = jnp.zeros((2 * blk, LANES), F32)
        for j in range(i + 1):
            keys = slice(j * blk, (j + 1) * blk)
            p = jnp.exp(s_s[:, keys] - m)
            l_t = l_t + lane_fold(p, jnp.add)
            p_s[:, keys] = p.astype(BF16)
        l = jnp.sum(l_t, axis=-1, keepdims=True)
        kend = (i + 1) * blk
        o = jnp.dot(p_s[:, 0:kend], v_ref[0:kend, :], preferred_element_type=F32) / l
        att = o[0:blk, :] - lam * o[blk:2 * blk, :]
        ms = jnp.mean(att * att, axis=-1, keepdims=True)
        y = att * lax.rsqrt(ms + EPS) * sg_ref[...] * (1.0 - LAM_INIT)
        o_ref[i * blk:(i + 1) * blk, :] = y.astype(o_ref.dtype)


def _t5_bucket(rel):
    nb = NUM_BUCKETS // 2
    max_exact = nb // 2
    base = jnp.where(rel > 0, nb, 0)
    n = jnp.abs(rel)
    large = max_exact + (jnp.log(jnp.maximum(n, 1).astype(jnp.float32) / max_exact)
                         / math.log(MAX_DISTANCE / max_exact) * (nb - max_exact)).astype(jnp.int32)
    large = jnp.minimum(large, nb - 1)
    return base + jnp.where(n < max_exact, n, large)


def _diff_call(proj, rel_table, qg, kg, lam_vecs, sg, batch, seq, blk=SEQ_BLK):
    nh = DIFF_HEADS
    first = 4 * RET_HEADS
    r = jnp.arange(blk, dtype=I32)
    rel0 = r[None, :] - r[:, None]
    bidx = jnp.stack([_t5_bucket(rel0), _t5_bucket(rel0 - blk)]).astype(I32) & (NUM_BUCKETS - 1)

    def head_spec(base):
        return pl.BlockSpec((None, seq, LANES), lambda h, b: (base + h, b, 0))

    def vec_spec():
        return pl.BlockSpec((1, LANES), lambda h, b: (0, 0))

    return pl.pallas_call(
        functools.partial(_diff_kernel, blk=blk, nblk=seq // blk),
        grid=(nh, batch),
        in_specs=[pl.BlockSpec(memory_space=pltpu.SMEM),
                  head_spec(first), head_spec(first + nh), head_spec(first + 2 * nh),
                  vec_spec(), vec_spec(),
                  pl.BlockSpec((4, DIFF_DK), lambda h, b: (0, 0)),
                  vec_spec(),
                  pl.BlockSpec((2, blk, blk), lambda h, b: (0, 0, 0))],
        out_specs=pl.BlockSpec((None, seq, LANES), lambda h, b: (h, b, 0)),
        out_shape=jax.ShapeDtypeStruct((nh, batch * seq, LANES), BF16),
        scratch_shapes=[pltpu.VMEM((2 * seq, LANES), BF16),
                        pltpu.VMEM((seq, LANES), BF16),
                        pltpu.VMEM((2, 2 * blk, blk), F32),
                        pltpu.VMEM((2 * blk, seq), F32),
                        pltpu.VMEM((2 * blk, seq), BF16)],
        compiler_params=_cparams(("arbitrary", "arbitrary")),
        name="diff_attn",
    )(rel_table, proj, proj, proj,
      jnp.tile(qg, 2).reshape(1, LANES), jnp.tile(kg, 2).reshape(1, LANES),
      lam_vecs, sg.reshape(1, LANES), bidx)


def _outproj_kernel(yr_ref, yd_ref, x_ref, w_hbm, g_ref, wr_ref, br_ref,
                    x1_ref, h2p_ref, lt_ref, wb_s, wstage_s, wr2_s, wsem,
                    *, rows_per_cast):
    ne = lt_ref.shape[0]
    nchunk = wb_s.shape[0] // rows_per_cast

    def w_copy(c):
        return pltpu.make_async_copy(w_hbm.at[pl.ds(c * rows_per_cast, rows_per_cast)],
                                     wstage_s.at[c % 2], wsem.at[c % 2])

    @pl.when(pl.program_id(0) == 0)
    def _stage_weights():
        w_copy(0).start()
        for c in range(nchunk):
            if c + 1 < nchunk:
                w_copy(c + 1).start()
            w_copy(c).wait()
            wb_s[c * rows_per_cast:(c + 1) * rows_per_cast, :] = wstage_s[c % 2].astype(BF16)
        wr = wr_ref[...]
        wr_hi = wr.astype(BF16)
        wr2_s[:, 0:LANES] = wr_hi
        wr2_s[:, LANES:2 * LANES] = (wr - wr_hi.astype(F32)).astype(BF16)

    y = jnp.concatenate([yr_ref[j] for j in range(yr_ref.shape[0])]
                        + [yd_ref[j] for j in range(yd_ref.shape[0])], axis=-1)
    x1 = x_ref[...] + jnp.dot(y, wb_s[...], preferred_element_type=F32)
    x1_ref[...] = x1
    ms = jnp.mean(x1 * x1, axis=-1, keepdims=True)
    h2 = x1 * lax.rsqrt(ms + EPS) * g_ref[...]
    h_hi = h2.astype(BF16)
    h_lo = (h2 - h_hi.astype(F32)).astype(BF16)
    parts = (jnp.dot(h_hi, wr2_s[...], preferred_element_type=F32)
             + jnp.dot(h_lo, wr2_s[...], preferred_element_type=F32))
    logits = parts[:, 0:LANES] + parts[:, LANES:2 * LANES]
    lt_ref[...] = logits.T[0:ne, :] + br_ref[...]
    half = h2.shape[1] // 2
    packed = pltpu.pack_elementwise([h2[:, :half], h2[:, half:]], packed_dtype=BF16)
    h2p_ref[...] = packed.reshape(h2p_ref.shape)


def _outproj_call(y_ret, y_diff, x2, w_out, g, w_router, b_router, tm=512):
    t, d = x2.shape
    ne = w_router.shape[1]
    nhr, nhd = y_ret.shape[0], y_diff.shape[0]
    rows_per_cast = min(256, d)
    wr_pad = jnp.pad(w_router, ((0, 0), (0, LANES - ne)))
    return pl.pallas_call(
        functools.partial(_outproj_kernel, rows_per_cast=rows_per_cast),
        grid=(t // tm,),
        in_specs=[pl.BlockSpec((nhr, tm, LANES), lambda i: (0, i, 0)),
                  pl.BlockSpec((nhd, tm, LANES), lambda i: (0, i, 0)),
                  pl.BlockSpec((tm, d), lambda i: (i, 0)),
                  pl.BlockSpec(memory_space=pl.ANY),
                  pl.BlockSpec((1, d), lambda i: (0, 0)),
                  pl.BlockSpec((d, LANES), lambda i: (0, 0)),
                  pl.BlockSpec((ne, 1), lambda i: (0, 0))],
        out_specs=[pl.BlockSpec((tm, d), lambda i: (i, 0)),
                   pl.BlockSpec((tm, d // 2 // LANES, LANES), lambda i: (i, 0, 0)),
                   pl.BlockSpec((ne, tm), lambda i: (0, i))],
        out_shape=[jax.ShapeDtypeStruct((t, d), F32),
                   jax.ShapeDtypeStruct((t, d // 2 // LANES, LANES), U32),
                   jax.ShapeDtypeStruct((ne, t), F32)],
        scratch_shapes=[pltpu.VMEM((d, d), BF16),
                        pltpu.VMEM((2, rows_per_cast, d), F32),
                        pltpu.VMEM((d, 2 * LANES), BF16),
                        pltpu.SemaphoreType.DMA((2,))],
        compiler_params=_cparams(("arbitrary",)),
        name="out_proj_router",
    )(y_ret, y_diff, x2, w_out, g.reshape(1, d), wr_pad, b_router.reshape(ne, 1))


def _route_kernel(lt_ref, pos_ref, gate_ref, cnt_ref, nt_ref, ot_ref, idx_s, rank_s, *, tb, row_tile):
    ne, t = lt_ref.shape
    e_iota = lax.broadcasted_iota(I32, (ne, tb), 0)
    upper = (lax.broadcasted_iota(I32, (tb, tb), 0)
             < lax.broadcasted_iota(I32, (tb, tb), 1)).astype(BF16)

    def pass_a(i, running):
        cols = pl.ds(pl.multiple_of(i * tb, tb), tb)
        l = lt_ref[:, cols]
        tops, hots = [], []
        for k in range(TOP_K):
            m = jnp.max(l, axis=0, keepdims=True)
            idx = jnp.min(jnp.where(l == m, e_iota, ne), axis=0, keepdims=True)
            hot = e_iota == idx
            l = jnp.where(hot, -jnp.inf, l)
            idx_s[k:k + 1, cols] = idx
            tops.append(m)
            hots.append(hot)
        exps = [jnp.exp(m - tops[0]) for m in tops]
        denom = exps[0] + exps[1] + exps[2] + exps[3]
        for k in range(TOP_K):
            gate_ref[k:k + 1, cols] = exps[k] / denom
        hot_all = jnp.zeros((ne, tb), F32)
        for hot in hots:
            hot_all = hot_all + hot.astype(F32)
        before = running + jnp.dot(hot_all.astype(BF16), upper, preferred_element_type=F32)
        for k in range(TOP_K):
            rank_s[k:k + 1, cols] = jnp.sum(jnp.where(hots[k], before, 0.0), axis=0, keepdims=True)
        return running + jnp.sum(hot_all, axis=1, keepdims=True)

    cnt = lax.fori_loop(0, t // tb, pass_a, jnp.zeros((ne, 1), F32))
    ntile = jnp.floor((cnt + (row_tile - 1.0)) * (1.0 / row_tile))
    lower = (lax.broadcasted_iota(I32, (ne, ne), 1)
             < lax.broadcasted_iota(I32, (ne, ne), 0)).astype(BF16)
    otile = jnp.dot(lower, jnp.broadcast_to(ntile, (ne, LANES)).astype(BF16),
                    preferred_element_type=F32)
    cnt_ref[...] = jnp.broadcast_to(cnt, (ne, LANES)).astype(I32)
    nt_ref[...] = jnp.broadcast_to(ntile, (ne, LANES)).astype(I32)
    ot_ref[...] = otile.astype(I32)
    off_rows = otile[:, 0:1] * float(row_tile)

    def pass_b(i, carry):
        cols = pl.ds(pl.multiple_of(i * tb, tb), tb)
        for k in range(TOP_K):
            hot = e_iota == idx_s[k:k + 1, cols]
            off = jnp.sum(jnp.where(hot, off_rows, 0.0), axis=0, keepdims=True)
            pos_ref[k:k + 1, cols] = (rank_s[k:k + 1, cols] + off).astype(I32)
        return carry

    lax.fori_loop(0, t // tb, pass_b, 0)


def _route_call(logits_t, tb=256, row_tile=ROW_TILE):
    ne, t = logits_t.shape
    return pl.pallas_call(
        functools.partial(_route_kernel, tb=tb, row_tile=row_tile),
        out_shape=[jax.ShapeDtypeStruct((TOP_K, t), I32),
                   jax.ShapeDtypeStruct((TOP_K, t), F32),
                   jax.ShapeDtypeStruct((ne, LANES), I32),
                   jax.ShapeDtypeStruct((ne, LANES), I32),
                   jax.ShapeDtypeStruct((ne, LANES), I32)],
        scratch_shapes=[pltpu.VMEM((TOP_K, t), I32), pltpu.VMEM((TOP_K, t), F32)],
        compiler_params=_cparams(None),
        name="route",
    )(logits_t)


_PAD_PIECES = tuple(ROW_TILE >> (k + 1) for k in range(ROW_TILE.bit_length() - 1))


def _dispatch_kernel(pos_ref, pad_start_ref, pad_n_ref, src_ref, dst_ref, stage_s, zero_s,
                     lsem, ssem, zsem, *, tb, ne):
    i = pl.program_id(0)
    nsteps = pl.num_programs(0)
    slot = i % 2

    def load(step, s):
        return pltpu.make_async_copy(src_ref.at[pl.ds(step * tb, tb)], stage_s.at[s], lsem.at[s])

    def wait_rows(s):
        for k in range(TOP_K):
            pltpu.make_async_copy(stage_s.at[s], dst_ref.at[pl.ds(0, tb)], ssem.at[s]).wait()

    def pad_copies(e):
        n = pad_n_ref[e]
        start = pad_start_ref[e]
        out = []
        for piece in _PAD_PIECES:
            at = start + (n & ~(2 * piece - 1))
            out.append(((n & piece) != 0, pltpu.make_async_copy(
                zero_s.at[pl.ds(0, piece)], dst_ref.at[pl.ds(at, piece)], zsem)))
        return out

    @pl.when(i == 0)
    def _zero_pads():
        zero_s[...] = jnp.zeros(zero_s.shape, zero_s.dtype)

        def issue(e, c):
            for cond, cp in pad_copies(e):
                @pl.when(cond)
                def _():
                    cp.start()
            return c

        def drain(e, c):
            for cond, cp in pad_copies(e):
                @pl.when(cond)
                def _():
                    cp.wait()
            return c

        lax.fori_loop(0, ne, issue, 0)
        lax.fori_loop(0, ne, drain, 0)
        load(0, 0).start()

    @pl.when(i > 0)
    def _():
        wait_rows(1 - slot)

    @pl.when(i + 1 < nsteps)
    def _():
        load(i + 1, 1 - slot).start()

    load(i, slot).wait()

    def issue_rows(tl, c):
        for k in range(TOP_K):
            pltpu.make_async_copy(stage_s.at[slot, tl], dst_ref.at[pos_ref[k, tl]],
                                  ssem.at[slot]).start(priority=k % 2)
        return c

    lax.fori_loop(0, tb, issue_rows, 0, unroll=4)

    @pl.when(i == nsteps - 1)
    def _():
        wait_rows(slot)


def _dispatch_call(pos, pad_start, pad_n, h2p, n_slots, tb=512):
    t = h2p.shape[0]
    row = h2p.shape[1:]
    ne = pad_n.shape[0]
    return pl.pallas_call(
        functools.partial(_dispatch_kernel, tb=tb, ne=ne),
        grid=(t // tb,),
        in_specs=[pl.BlockSpec((TOP_K, tb), lambda i: (0, i), memory_space=pltpu.SMEM),
                  pl.BlockSpec(memory_space=pltpu.SMEM),
                  pl.BlockSpec(memory_space=pltpu.SMEM),
                  pl.BlockSpec(memory_space=pl.ANY)],
        out_specs=pl.BlockSpec(memory_space=pl.ANY),
        out_shape=jax.ShapeDtypeStruct((n_slots, *row), U32),
        scratch_shapes=[pltpu.VMEM((2, tb, *row), U32),
                        pltpu.VMEM((_PAD_PIECES[0], *row), U32),
                        pltpu.SemaphoreType.DMA((2,)),
                        pltpu.SemaphoreType.DMA((2,)),
                        pltpu.SemaphoreType.DMA(())],
        compiler_params=pltpu.CompilerParams(dimension_semantics=("arbitrary",),
                                             vmem_limit_bytes=VMEM_LIMIT, has_side_effects=True),
        name="dispatch",
    )(pos, pad_start, pad_n, h2p)


def _moe_kernel(we_ref, wt_ref, wn_ref, xs_ref, win_ref, bin_ref, wout_ref, bo_ref,
                ys_ref, xu_s, acc_s, wg_st, wu_st, wo_st, yst_s, xsem, wsem, ysem,
                *, tmx, fc, nj):
    w = pl.program_id(0)
    nw = pl.num_programs(0)
    nt = wn_ref[w]
    t0 = wt_ref[w]
    half = xu_s.shape[-2] * xu_s.shape[-1]
    de = nj * fc
    nxt = jnp.minimum(w + 1, nw - 1)
    has_next = jnp.logical_and(w + 1 < nw, wn_ref[nxt] > 0)
    xslot = w % 2

    def local_rows(r, n=1):
        return pl.ds(pl.multiple_of(r * tmx, tmx), n * tmx)

    def x_copy(item_t0, r, slot):
        src = pl.ds(pl.multiple_of((item_t0 + r) * tmx, tmx), tmx)
        return pltpu.make_async_copy(xs_ref.at[src], xu_s.at[slot, local_rows(r)], xsem.at[slot])

    def y_copy(r, slot):
        dst = pl.ds(pl.multiple_of((t0 + r) * tmx, tmx), tmx)
        return pltpu.make_async_copy(yst_s.at[slot], ys_ref.at[dst], ysem.at[slot])

    def w_copies(expert, j, slot):
        c0 = pl.multiple_of(j * fc, fc)
        return (pltpu.make_async_copy(win_ref.at[expert, :, pl.ds(c0, fc)], wg_st.at[slot], wsem.at[slot]),
                pltpu.make_async_copy(win_ref.at[expert, :, pl.ds(de + c0, fc)], wu_st.at[slot],
                                      wsem.at[slot]),
                pltpu.make_async_copy(wout_ref.at[expert, pl.ds(c0, fc), :], wo_st.at[slot],
                                      wsem.at[slot]))

    def start_weights(expert, j, slot):
        for cp in w_copies(expert, j, slot):
            cp.start()

    def start_rows(item_t0, item_nt, slot):
        def go(r, c):
            x_copy(item_t0, r, slot).start()
            return c
        lax.fori_loop(0, item_nt, go, 0)

    @pl.when(w == 0)
    def _prologue():
        start_weights(we_ref[0], 0, 0)
        start_rows(t0, nt, 0)

    @pl.when(nt > 0)
    def _work():
        def chunk(j, carry):
            slot = j % 2

            @pl.when(j + 1 < nj)
            def _():
                start_weights(we_ref[w], j + 1, 1 - slot)

            @pl.when(jnp.logical_and(j + 1 == nj, has_next))
            def _():
                start_weights(we_ref[nxt], 0, 1 - slot)

            for cp in w_copies(we_ref[w], j, slot):
                cp.wait()
            stage = (wg_st.at[slot], wu_st.at[slot], wo_st.at[slot])
            biases = (bin_ref[j], bin_ref[nj + j])

            @pl.when(j == 0)
            def _rows_ready():
                def finish(r, c):
                    x_copy(t0, r, xslot).wait()
                    acc_s[local_rows(r), :] = jnp.zeros((tmx, acc_s.shape[1]), F32)
                    return c
                lax.fori_loop(0, nt, finish, 0)

                @pl.when(has_next)
                def _():
                    start_rows(wt_ref[nxt], wn_ref[nxt], 1 - xslot)

            _moe_passes(xu_s.at[xslot], acc_s, stage, biases, nt, local_rows, half)
            return carry

        lax.fori_loop(0, nj, chunk, 0)

        def y_wait(slot):
            y_copy(0, slot).wait()

        prev_nt = jnp.where(w > 0, wn_ref[jnp.maximum(w - 1, 0)], 0)

        @pl.when(prev_nt >= 1)
        def _():
            y_wait((prev_nt - 1) % 2)

        @pl.when(prev_nt >= 2)
        def _():
            y_wait(prev_nt % 2)

        def emit(r, c):
            slot = r % 2

            @pl.when(r >= 2)
            def _():
                y_wait(slot)

            y = acc_s[local_rows(r), :] + bo_ref[...]
            yst_s[slot] = pltpu.pack_elementwise([y[:, :half], y[:, half:]],
                                                 packed_dtype=BF16).reshape(yst_s.shape[1:])
            y_copy(r, slot).start()
            return c

        lax.fori_loop(0, nt, emit, 0)

        @pl.when(jnp.logical_not(has_next))
        def _drain():
            @pl.when(nt >= 2)
            def _():
                y_wait(nt % 2)

            y_wait((nt - 1) % 2)


def _moe_passes(xu_ref, acc_s, stage, biases, nt, local_rows, half):
    wg_ref, wu_ref, wo_ref = stage
    bg, bu = biases

    def proj(x_lo, x_hi, w_ref, b):
        return (jnp.dot(x_lo, w_ref[0:half, :].astype(BF16), preferred_element_type=F32)
                + jnp.dot(x_hi, w_ref[half:2 * half, :].astype(BF16), preferred_element_type=F32) + b)

    def rows_step(rows):
        xw = xu_ref[rows]
        xw = xw.reshape(xw.shape[0], half)
        x_lo = pltpu.unpack_elementwise(xw, index=0, packed_dtype=BF16,
                                        unpacked_dtype=F32).astype(BF16)
        x_hi = pltpu.unpack_elementwise(xw, index=1, packed_dtype=BF16,
                                        unpacked_dtype=F32).astype(BF16)
        gg = jnp.minimum(proj(x_lo, x_hi, wg_ref, bg), SWIGLU_LIMIT)
        uu = jnp.clip(proj(x_lo, x_hi, wu_ref, bu), -SWIGLU_LIMIT, SWIGLU_LIMIT)
        act = (uu + 1.0) * (gg * jax.nn.sigmoid(SWIGLU_ALPHA * gg))
        acc_s[rows, :] += jnp.dot(act.astype(BF16), wo_ref[...].astype(BF16),
                                  preferred_element_type=F32)

    def big(pi, c):
        rows_step(local_rows(PASS_TILES * pi, PASS_TILES))
        return c

    lax.fori_loop(0, nt // PASS_TILES, big, 0)
    piece = PASS_TILES // 2
    while piece >= 1:
        @pl.when((nt & piece) != 0)
        def _(piece=piece):
            rows_step(local_rows(nt & ~(2 * piece - 1), piece))
        piece //= 2


def _moe_call(work_e, work_t0, work_nt, xs, w_in, b_in, w_out, b_out,
              tmx=ROW_TILE, group=GROUP_TILES, fc=F_CHUNK):
    ne, d, de2 = w_in.shape
    de = de2 // 2
    nj = de // fc
    assert nj % 2 == 0, "weight stage slots alternate per chunk and restart at 0 per work item"
    n_work = work_e.shape[0]
    n_slots = xs.shape[0]
    row = xs.shape[1:]

    grid_spec = pltpu.PrefetchScalarGridSpec(
        num_scalar_prefetch=3,
        grid=(n_work,),
        in_specs=[
            pl.BlockSpec(memory_space=pl.ANY),
            pl.BlockSpec(memory_space=pl.ANY),
            pl.BlockSpec((None, 2 * nj, 1, fc), lambda w, we, wt, wn: (we[w], 0, 0, 0)),
            pl.BlockSpec(memory_space=pl.ANY),
            pl.BlockSpec((None, 1, d), lambda w, we, wt, wn: (we[w], 0, 0)),
        ],
        out_specs=pl.BlockSpec(memory_space=pl.ANY),
        scratch_shapes=[pltpu.VMEM((2, group * tmx, *row), U32),
                        pltpu.VMEM((group * tmx, d), F32),
                        pltpu.VMEM((2, d, fc), F32),
                        pltpu.VMEM((2, d, fc), F32),
                        pltpu.VMEM((2, fc, d), F32),
                        pltpu.VMEM((2, tmx, *row), U32),
                        pltpu.SemaphoreType.DMA((2,)),
                        pltpu.SemaphoreType.DMA((2,)),
                        pltpu.SemaphoreType.DMA((2,))],
    )
    return pl.pallas_call(
        functools.partial(_moe_kernel, tmx=tmx, fc=fc, nj=nj),
        grid_spec=grid_spec,
        out_shape=jax.ShapeDtypeStruct((n_slots, *row), U32),
        compiler_params=pltpu.CompilerParams(dimension_semantics=("arbitrary",),
                                             vmem_limit_bytes=VMEM_LIMIT, has_side_effects=True),
        name="moe_experts",
    )(work_e, work_t0, work_nt, xs, w_in, b_in.reshape(ne, 2 * nj, 1, fc), w_out,
      b_out.reshape(ne, 1, d))


def _combine_kernel(pos_ref, pos_next_ref, x1_ref, gate_ref, ys_ref, o_ref, ybuf, sem, *, tb):
    half = ybuf.shape[-2] * ybuf.shape[-1]
    i = pl.program_id(0)
    slot = i % 2

    def gather(p_ref, s):
        def issue(tl, c):
            for k in range(TOP_K):
                pltpu.make_async_copy(ys_ref.at[p_ref[k, tl]], ybuf.at[s, k, tl],
                                      sem.at[s]).start(priority=k % 2)
            return c
        lax.fori_loop(0, tb, issue, 0, unroll=4)

    @pl.when(i == 0)
    def _():
        gather(pos_ref, 0)

    @pl.when(i + 1 < pl.num_programs(0))
    def _():
        gather(pos_next_ref, 1 - slot)

    for k in range(TOP_K):
        pltpu.make_async_copy(ys_ref.at[pl.ds(0, tb)], ybuf.at[slot, k], sem.at[slot]).wait()

    lo = x1_ref[:, 0:half]
    hi = x1_ref[:, half:2 * half]
    for k in range(TOP_K):
        g = gate_ref[:, k:k + 1]
        yw = ybuf[slot, k].reshape(tb, half)
        lo = lo + g * pltpu.unpack_elementwise(yw, index=0, packed_dtype=BF16, unpacked_dtype=F32)
        hi = hi + g * pltpu.unpack_elementwise(yw, index=1, packed_dtype=BF16, unpacked_dtype=F32)
    o_ref[:, 0:half] = lo
    o_ref[:, half:2 * half] = hi


def _combine_call(pos, x1, gate_tk, ys, tb=256):
    t, d = x1.shape
    row = ys.shape[1:]
    last = t // tb - 1
    return pl.pallas_call(
        functools.partial(_combine_kernel, tb=tb),
        grid=(t // tb,),
        in_specs=[pl.BlockSpec((TOP_K, tb), lambda i: (0, i), memory_space=pltpu.SMEM),
                  pl.BlockSpec((TOP_K, tb), lambda i: (0, jnp.minimum(i + 1, last)),
                               memory_space=pltpu.SMEM),
                  pl.BlockSpec((tb, d), lambda i: (i, 0)),
                  pl.BlockSpec((tb, TOP_K), lambda i: (i, 0)),
                  pl.BlockSpec(memory_space=pl.ANY)],
        out_specs=pl.BlockSpec((tb, d), lambda i: (i, 0)),
        out_shape=jax.ShapeDtypeStruct((t, d), F32),
        scratch_shapes=[pltpu.VMEM((2, TOP_K, tb, *row), U32),
                        pltpu.SemaphoreType.DMA((2,))],
        compiler_params=_cparams(("arbitrary",)),
        name="combine",
    )(pos, pos, x1, gate_tk, ys)


def _work_list(ntile, otile, n_work, group):
    ne = ntile.shape[0]
    items = (ntile + group - 1) // group
    ends = jnp.cumsum(items)
    total = ends[-1]
    w = jnp.arange(n_work, dtype=I32)
    valid = w < total
    e_w = jnp.clip(jnp.sum((ends[None, :] <= w[:, None]).astype(I32), axis=1), 0, ne - 1)
    local = w - (ends - items)[e_w]
    t0_w = otile[e_w] + local * group
    nt_w = jnp.where(valid, jnp.clip(ntile[e_w] - local * group, 0, group), 0)
    e_last = e_w[jnp.maximum(total - 1, 0)]
    return (jnp.where(valid, e_w, e_last).astype(I32), jnp.where(valid, t0_w, 0).astype(I32),
            nt_w.astype(I32))


def _rotary_tables(seq, dk):
    inv_freq = ROPE_BASE ** (-jnp.arange(0, dk, 2, dtype=F32) / dk)
    ang = jnp.arange(seq, dtype=I32).astype(F32)[:, None] * inv_freq[None, :]
    cos = jnp.repeat(jnp.cos(ang), 2, axis=1)
    sin = jnp.stack([-jnp.sin(ang), jnp.sin(ang)], axis=-1).reshape(seq, dk)
    return cos, sin


def kernel(x, norm_mix_g, w_mix_in, ret_gn_g, q_norm_g, k_norm_g, lambda_q1, lambda_k1, lambda_q2, lambda_k2, diff_subln_g, rel_bias_table, w_mix_out, norm_ffn_g, w_router, b_router, w_exp_in, b_exp_in, w_exp_out, b_exp_out):
    batch, seq, d = x.shape
    t = batch * seq
    depth = norm_mix_g.shape[0]
    ne = w_router.shape[-1]
    n_tiles_max = (t * TOP_K) // ROW_TILE + ne
    n_work = ne + (n_tiles_max - ne) // GROUP_TILES
    cos, sin = _rotary_tables(seq, HEAD_W)
    log_gamma = jnp.log1p(-(2.0 ** (-5.0 - jnp.arange(RET_HEADS, dtype=F32))))

    x2 = x.reshape(t, d)
    for l in range(depth):
        proj = _inproj_call(x2, norm_mix_g[l], w_mix_in[l])
        y_ret = _ret_call(proj, cos, sin, log_gamma, ret_gn_g[l], batch, seq)
        lam_vecs = jnp.stack([lambda_q1[l], lambda_k1[l], lambda_q2[l], lambda_k2[l]])
        y_diff = _diff_call(proj, rel_bias_table, q_norm_g[l], k_norm_g[l], lam_vecs,
                            diff_subln_g[l], batch, seq)
        x1, h2p, logits_t = _outproj_call(y_ret, y_diff, x2, w_mix_out[l], norm_ffn_g[l],
                                          w_router[l], b_router[l])
        pos, gate, cnt, ntile, otile = _route_call(logits_t)
        cnt, ntile, otile = cnt[:, 0], ntile[:, 0], otile[:, 0]
        work_e, work_t0, work_nt = _work_list(ntile, otile, n_work, GROUP_TILES)
        xs = _dispatch_call(pos, otile * ROW_TILE + cnt, ntile * ROW_TILE - cnt, h2p,
                            n_tiles_max * ROW_TILE)
        ys = _moe_call(work_e, work_t0, work_nt, xs, w_exp_in[l], b_exp_in[l], w_exp_out[l],
                       b_exp_out[l])
        x2 = _combine_call(pos, x1, gate.T, ys)
    return x2.reshape(batch, seq, d)
```

```python
import functools
import math

import jax
import jax.numpy as jnp
from jax import lax
from jax.experimental import pallas as pl
from jax.experimental.pallas import tpu as pltpu

F32 = jnp.float32
BF16 = jnp.bfloat16
I32 = jnp.int32
U32 = jnp.uint32

EPS = 1e-6
CHUNK = 64
RET_HEADS = 8
DIFF_HEADS = 8
HEAD_W = 128
DIFF_DK = 64
ROPE_BASE = 10000.0
NUM_BUCKETS = 32
MAX_DISTANCE = 128
TOP_K = 4
SWIGLU_LIMIT = 7.0
SWIGLU_ALPHA = 1.702
LAM_INIT = 0.8 - 0.6 * math.exp(-0.3 * 0)

LANES = 128
NEG_BIG = -1e30
VMEM_LIMIT = 56 * 1024 * 1024

SEQ_BLK = 256
ROW_TILE = 256
GROUP_TILES = 8
PASS_TILES = 4
F_CHUNK = 256
CAST_ROWS = 256


def _cparams(sem, vmem=VMEM_LIMIT):
    return pltpu.CompilerParams(dimension_semantics=sem, vmem_limit_bytes=vmem)


def _inproj_kernel(x_ref, g_ref, w_ref, o_ref, wb_ref, *, rows_per_cast, sub_blocks=2):
    @pl.when(pl.program_id(1) == 0)
    def _():
        d = w_ref.shape[0]
        for c in range(d // rows_per_cast):
            sl = slice(c * rows_per_cast, (c + 1) * rows_per_cast)
            wb_ref[sl, :] = w_ref[sl, :].astype(BF16)

    sub = x_ref.shape[0] // sub_blocks
    for sb in range(sub_blocks):
        rs = slice(sb * sub, (sb + 1) * sub)
        x = x_ref[rs, :]
        ms = jnp.mean(x * x, axis=-1, keepdims=True)
        h = (x * lax.rsqrt(ms + EPS) * g_ref[...]).astype(BF16)
        acc = jnp.dot(h, wb_ref[...], preferred_element_type=F32)
        for j in range(o_ref.shape[0]):
            o_ref[j, rs, :] = acc[:, j * LANES:(j + 1) * LANES].astype(o_ref.dtype)


def _inproj_call(x2, g, w, tm=1024, tn=1024):
    t, d = x2.shape
    n = w.shape[1]
    return pl.pallas_call(
        functools.partial(_inproj_kernel, rows_per_cast=min(CAST_ROWS, d)),
        grid=(n // tn, t // tm),
        in_specs=[pl.BlockSpec((tm, d), lambda j, i: (i, 0)),
                  pl.BlockSpec((1, d), lambda j, i: (0, 0)),
                  pl.BlockSpec((d, tn), lambda j, i: (0, j))],
        out_specs=pl.BlockSpec((tn // LANES, tm, LANES), lambda j, i: (j, i, 0)),
        out_shape=jax.ShapeDtypeStruct((n // LANES, t, LANES), BF16),
        scratch_shapes=[pltpu.VMEM((d, tn), BF16)],
        compiler_params=_cparams(("arbitrary", "arbitrary")),
        name="in_proj",
    )(x2, g.reshape(1, d), w)


def _ret_kernel(lg_ref, q_ref, k_ref, v_ref, g_ref, cos_ref, sin_ref, gn_ref, o_ref, *, blk, nblk):
    dk = q_ref.shape[-1]
    lg = lg_ref[pl.program_id(1)]
    row = lax.broadcasted_iota(I32, (blk, blk), 0)
    col = lax.broadcasted_iota(I32, (blk, blk), 1)
    dist = jnp.abs(row - col).astype(F32)
    visible = (col // CHUNK) <= (row // CHUNK)
    dmask = jnp.where(visible, jnp.exp(lg * dist), 0.0)
    rr = lax.broadcasted_iota(I32, (blk, dk), 0).astype(F32)
    qdec = jnp.exp(lg * (rr + 1.0))
    kdec = jnp.exp(lg * (blk - 1.0 - rr))
    bdec = jnp.exp(lg * jnp.full((1, HEAD_W), float(blk), F32))
    even = (lax.broadcasted_iota(I32, (blk, dk), 1) & 1) == 0
    scale = dk ** -0.5

    def body(i, state):
        rows = pl.ds(pl.multiple_of(i * blk, blk), blk)
        cos = cos_ref[rows, :]
        sin = sin_ref[rows, :]

        def rot(x):
            partner = jnp.where(even, pltpu.roll(x, dk - 1, 1), pltpu.roll(x, 1, 1))
            return x * cos + partner * sin

        qr = rot(q_ref[rows, :].astype(F32)) * scale
        kr = rot(k_ref[rows, :].astype(F32))
        v = v_ref[rows, :]
        s = lax.dot_general(qr.astype(BF16), kr.astype(BF16), (((1,), (1,)), ((), ())),
                            preferred_element_type=F32) * dmask
        out = jnp.dot(s.astype(BF16), v, preferred_element_type=F32)
        out = out + jnp.dot((qr * qdec).astype(BF16), state.astype(BF16),
                            preferred_element_type=F32)
        kv = lax.dot_general((kr * kdec).astype(BF16), v, (((0,), (0,)), ((), ())),
                             preferred_element_type=F32)
        state = state * bdec + kv
        ms = jnp.mean(out * out, axis=-1, keepdims=True)
        normed = out * lax.rsqrt(ms + EPS) * gn_ref[...]
        g = g_ref[rows, :].astype(F32)
        o_ref[rows, :] = (g * jax.nn.sigmoid(g) * normed).astype(o_ref.dtype)
        return state

    lax.fori_loop(0, nblk, body, jnp.zeros((dk, HEAD_W), F32), unroll=True)


def _ret_call(proj, cos, sin, log_gamma, gn_g, batch, seq, blk=SEQ_BLK):
    nh = RET_HEADS

    def head_spec(base):
        return pl.BlockSpec((None, seq, LANES), lambda b, h: (base + h, b, 0))

    return pl.pallas_call(
        functools.partial(_ret_kernel, blk=blk, nblk=seq // blk),
        grid=(batch, nh),
        in_specs=[pl.BlockSpec(memory_space=pltpu.SMEM),
                  head_spec(0), head_spec(nh), head_spec(2 * nh), head_spec(3 * nh),
                  pl.BlockSpec((seq, LANES), lambda b, h: (0, 0)),
                  pl.BlockSpec((seq, LANES), lambda b, h: (0, 0)),
                  pl.BlockSpec((None, 1, LANES), lambda b, h: (h, 0, 0))],
        out_specs=pl.BlockSpec((None, seq, LANES), lambda b, h: (h, b, 0)),
        out_shape=jax.ShapeDtypeStruct((nh, batch * seq, LANES), BF16),
        compiler_params=_cparams(("arbitrary", "arbitrary")),
        name="retention",
    )(log_gamma, proj, proj, proj, proj, cos, sin, gn_g.reshape(nh, 1, LANES))


def _diff_kernel(tbl_ref, q_ref, k_ref, v_ref, qg_ref, kg_ref, lam_ref, sg_ref, bidx_ref, o_ref,
                 qz_s, kn_s, bias_s, s_s, p_s, *, blk, nblk):
    h = pl.program_id(0)
    b = pl.program_id(1)
    far_bucket = NUM_BUCKETS // 2 - 1

    @pl.when(b == 0)
    def _build_bias():
        row = lax.broadcasted_iota(I32, (blk, blk), 0)
        col = lax.broadcasted_iota(I32, (blk, blk), 1)
        visible = (col // CHUNK) <= (row // CHUNK)
        for d in range(2):
            idx = bidx_ref[d]
            bias = jnp.zeros((blk, blk), F32)
            for bucket in range(NUM_BUCKETS):
                bias = jnp.where(idx == bucket, tbl_ref[bucket, h], bias)
            if d == 0:
                bias = jnp.where(visible, bias, NEG_BIG)
            bias_s[d, 0:blk, :] = bias
            bias_s[d, blk:2 * blk, :] = bias

    lo = lax.broadcasted_iota(I32, (blk, HEAD_W), 1) < DIFF_DK
    scale = DIFF_DK ** -0.5
    same_half = ((lax.broadcasted_iota(I32, (HEAD_W, HEAD_W), 0) // DIFF_DK)
                 == (lax.broadcasted_iota(I32, (HEAD_W, HEAD_W), 1) // DIFF_DK)).astype(BF16)

    def half_norm(x, g):
        ms = jnp.dot((x * x).astype(BF16), same_half, preferred_element_type=F32) * (1.0 / DIFF_DK)
        return x * lax.rsqrt(ms + EPS) * g

    def prep(i, carry):
        rows = pl.ds(pl.multiple_of(i * blk, blk), blk)
        qn = half_norm(q_ref[rows, :].astype(F32), qg_ref[...]) * scale
        kn = half_norm(k_ref[rows, :].astype(F32), kg_ref[...])
        base = pl.multiple_of(i * 2 * blk, 2 * blk)
        qz_s[pl.ds(base, blk), :] = jnp.where(lo, qn, 0.0).astype(BF16)
        qz_s[pl.ds(base + blk, blk), :] = jnp.where(lo, 0.0, qn).astype(BF16)
        kn_s[rows, :] = kn.astype(BF16)
        return carry

    lax.fori_loop(0, nblk, prep, 0)

    lam = (jnp.exp(jnp.sum(lam_ref[0:1, :] * lam_ref[1:2, :], axis=-1, keepdims=True))
           - jnp.exp(jnp.sum(lam_ref[2:3, :] * lam_ref[3:4, :], axis=-1, keepdims=True))
           + LAM_INIT)
    c_far = tbl_ref[far_bucket, h]

    def lane_fold(x, op):
        out = x[:, 0:LANES]
        for c in range(1, blk // LANES):
            out = op(out, x[:, c * LANES:(c + 1) * LANES])
        return out

    for i in range(nblk):
        qz = qz_s[i * 2 * blk:(i + 1) * 2 * blk, :]
        m_t = jnp.full((2 * blk, LANES), NEG_BIG, F32)
        for j in range(i + 1):
            keys = slice(j * blk, (j + 1) * blk)
            s = lax.dot_general(qz, kn_s[keys, :], (((1,), (1,)), ((), ())),
                                preferred_element_type=F32)
            s = s + (bias_s[0] if j == i else bias_s[1] if j == i - 1 else c_far)
            s_s[:, keys] = s
            m_t = jnp.maximum(m_t, lane_fold(s, jnp.maximum))
        m = jnp.max(m_t, axis=-1, keepdims=True)
        l_t = jnp.zeros((2 * blk, LANES), F32)
        for j in range(i + 1):
            keys = slice(j * blk, (j + 1) * blk)
            p = jnp.exp(s_s[:, keys] - m)
            l_t = l_t + lane_fold(p, jnp.add)
            p_s[:, keys] = p.astype(BF16)
        l = jnp.sum(l_t, axis=-1, keepdims=True)
        kend = (i + 1) * blk
        o = jnp.dot(p_s[:, 0:kend], v_ref[0:kend, :], preferred_element_type=F32) / l
        att = o[0:blk, :] - lam * o[blk:2 * blk, :]
        ms = jnp.mean(att * att, axis=-1, keepdims=True)
        y = att * lax.rsqrt(ms + EPS) * sg_ref[...] * (1.0 - LAM_INIT)
        o_ref[i * blk:(i + 1) * blk, :] = y.astype(o_ref.dtype)


def _t5_bucket(rel):
    nb = NUM_BUCKETS // 2
    max_exact = nb // 2
    base = jnp.where(rel > 0, nb, 0)
    n = jnp.abs(rel)
    large = max_exact + (jnp.log(jnp.maximum(n, 1).astype(jnp.float32) / max_exact)
                         / math.log(MAX_DISTANCE / max_exact) * (nb - max_exact)).astype(jnp.int32)
    large = jnp.minimum(large, nb - 1)
    return base + jnp.where(n < max_exact, n, large)


def _diff_call(proj, rel_table, qg, kg, lam_vecs, sg, batch, seq, blk=SEQ_BLK):
    nh = DIFF_HEADS
    first = 4 * RET_HEADS
    r = jnp.arange(blk, dtype=I32)
    rel0 = r[None, :] - r[:, None]
    bidx = jnp.stack([_t5_bucket(rel0), _t5_bucket(rel0 - blk)]).astype(I32) & (NUM_BUCKETS - 1)

    def head_spec(base):
        return pl.BlockSpec((None, seq, LANES), lambda h, b: (base + h, b, 0))

    def vec_spec():
        return pl.BlockSpec((1, LANES), lambda h, b: (0, 0))

    return pl.pallas_call(
        functools.partial(_diff_kernel, blk=blk, nblk=seq // blk),
        grid=(nh, batch),
        in_specs=[pl.BlockSpec(memory_space=pltpu.SMEM),
                  head_spec(first), head_spec(first + nh), head_spec(first + 2 * nh),
                  vec_spec(), vec_spec(),
                  pl.BlockSpec((4, DIFF_DK), lambda h, b: (0, 0)),
                  vec_spec(),
                  pl.BlockSpec((2, blk, blk), lambda h, b: (0, 0, 0))],
        out_specs=pl.BlockSpec((None, seq, LANES), lambda h, b: (h, b, 0)),
        out_shape=jax.ShapeDtypeStruct((nh, batch * seq, LANES), BF16),
        scratch_shapes=[pltpu.VMEM((2 * seq, LANES), BF16),
                        pltpu.VMEM((seq, LANES), BF16),
                        pltpu.VMEM((2, 2 * blk, blk), F32),
                        pltpu.VMEM((2 * blk, seq), F32),
                        pltpu.VMEM((2 * blk, seq), BF16)],
        compiler_params=_cparams(("arbitrary", "arbitrary")),
        name="diff_attn",
    )(rel_table, proj, proj, proj,
      jnp.tile(qg, 2).reshape(1, LANES), jnp.tile(kg, 2).reshape(1, LANES),
      lam_vecs, sg.reshape(1, LANES), bidx)


def _outproj_kernel(yr_ref, yd_ref, x_ref, w_hbm, g_ref, wr_ref, br_ref,
                    x1_ref, h2p_ref, lt_ref, wb_s, wstage_s, wr2_s, wsem,
                    *, rows_per_cast, sub_blocks=2):
    ne = lt_ref.shape[0]
    nchunk = wb_s.shape[0] // rows_per_cast

    def w_copy(c):
        return pltpu.make_async_copy(w_hbm.at[pl.ds(c * rows_per_cast, rows_per_cast)],
                                     wstage_s.at[c % 2], wsem.at[c % 2])

    @pl.when(pl.program_id(0) == 0)
    def _stage_weights():
        w_copy(0).start()
        for c in range(nchunk):
            if c + 1 < nchunk:
                w_copy(c + 1).start()
            w_copy(c).wait()
            wb_s[c * rows_per_cast:(c + 1) * rows_per_cast, :] = wstage_s[c % 2].astype(BF16)
        wr = wr_ref[...]
        wr_hi = wr.astype(BF16)
        wr2_s[:, 0:LANES] = wr_hi
        wr2_s[:, LANES:2 * LANES] = (wr - wr_hi.astype(F32)).astype(BF16)

    tm = x_ref.shape[0]
    sub = tm // sub_blocks
    for sb in range(sub_blocks):
        rs = slice(sb * sub, (sb + 1) * sub)
        y = jnp.concatenate([yr_ref[j, rs, :] for j in range(yr_ref.shape[0])]
                            + [yd_ref[j, rs, :] for j in range(yd_ref.shape[0])], axis=-1)
        x1 = x_ref[rs, :] + jnp.dot(y, wb_s[...], preferred_element_type=F32)
        x1_ref[rs, :] = x1
        ms = jnp.mean(x1 * x1, axis=-1, keepdims=True)
        h2 = x1 * lax.rsqrt(ms + EPS) * g_ref[...]
        h_hi = h2.astype(BF16)
        h_lo = (h2 - h_hi.astype(F32)).astype(BF16)
        parts = (jnp.dot(h_hi, wr2_s[...], preferred_element_type=F32)
                 + jnp.dot(h_lo, wr2_s[...], preferred_element_type=F32))
        logits = parts[:, 0:LANES] + parts[:, LANES:2 * LANES]
        lt_ref[:, rs] = logits.T[0:ne, :] + br_ref[...]
        half = h2.shape[1] // 2
        packed = pltpu.pack_elementwise([h2[:, :half], h2[:, half:]], packed_dtype=BF16)
        h2p_ref[rs] = packed.reshape((sub,) + h2p_ref.shape[1:])


def _outproj_call(y_ret, y_diff, x2, w_out, g, w_router, b_router, tm=512):
    t, d = x2.shape
    ne = w_router.shape[1]
    nhr, nhd = y_ret.shape[0], y_diff.shape[0]
    rows_per_cast = min(CAST_ROWS, d)
    wr_pad = jnp.pad(w_router, ((0, 0), (0, LANES - ne)))
    return pl.pallas_call(
        functools.partial(_outproj_kernel, rows_per_cast=rows_per_cast),
        grid=(t // tm,),
        in_specs=[pl.BlockSpec((nhr, tm, LANES), lambda i: (0, i, 0)),
                  pl.BlockSpec((nhd, tm, LANES), lambda i: (0, i, 0)),
                  pl.BlockSpec((tm, d), lambda i: (i, 0)),
                  pl.BlockSpec(memory_space=pl.ANY),
                  pl.BlockSpec((1, d), lambda i: (0, 0)),
                  pl.BlockSpec((d, LANES), lambda i: (0, 0)),
                  pl.BlockSpec((ne, 1), lambda i: (0, 0))],
        out_specs=[pl.BlockSpec((tm, d), lambda i: (i, 0)),
                   pl.BlockSpec((tm, d // 2 // LANES, LANES), lambda i: (i, 0, 0)),
                   pl.BlockSpec((ne, tm), lambda i: (0, i))],
        out_shape=[jax.ShapeDtypeStruct((t, d), F32),
                   jax.ShapeDtypeStruct((t, d // 2 // LANES, LANES), U32),
                   jax.ShapeDtypeStruct((ne, t), F32)],
        scratch_shapes=[pltpu.VMEM((d, d), BF16),
                        pltpu.VMEM((2, rows_per_cast, d), F32),
                        pltpu.VMEM((d, 2 * LANES), BF16),
                        pltpu.SemaphoreType.DMA((2,))],
        compiler_params=_cparams(("arbitrary",)),
        name="out_proj_router",
    )(y_ret, y_diff, x2, w_out, g.reshape(1, d), wr_pad, b_router.reshape(ne, 1))


def _route_kernel(lt_ref, pos_ref, gate_ref, cnt_ref, nt_ref, ot_ref, idx_s, rank_s, *, tb, row_tile):
    ne, t = lt_ref.shape
    e_iota = lax.broadcasted_iota(I32, (ne, tb), 0)
    upper = (lax.broadcasted_iota(I32, (tb, tb), 0)
             < lax.broadcasted_iota(I32, (tb, tb), 1)).astype(BF16)

    def pass_a(i, running):
        cols = pl.ds(pl.multiple_of(i * tb, tb), tb)
        l = lt_ref[:, cols]
        tops, hots = [], []
        for k in range(TOP_K):
            m = jnp.max(l, axis=0, keepdims=True)
            idx = jnp.min(jnp.where(l == m, e_iota, ne), axis=0, keepdims=True)
            hot = e_iota == idx
            l = jnp.where(hot, -jnp.inf, l)
            idx_s[k:k + 1, cols] = idx
            tops.append(m)
            hots.append(hot)
        exps = [jnp.exp(m - tops[0]) for m in tops]
        denom = exps[0] + exps[1] + exps[2] + exps[3]
        for k in range(TOP_K):
            gate_ref[k:k + 1, cols] = exps[k] / denom
        hot_all = jnp.zeros((ne, tb), F32)
        for hot in hots:
            hot_all = hot_all + hot.astype(F32)
        before = running + jnp.dot(hot_all.astype(BF16), upper, preferred_element_type=F32)
        for k in range(TOP_K):
            rank_s[k:k + 1, cols] = jnp.sum(jnp.where(hots[k], before, 0.0), axis=0, keepdims=True)
        return running + jnp.sum(hot_all, axis=1, keepdims=True)

    cnt = lax.fori_loop(0, t // tb, pass_a, jnp.zeros((ne, 1), F32))
    ntile = jnp.floor((cnt + (row_tile - 1.0)) * (1.0 / row_tile))
    lower = (lax.broadcasted_iota(I32, (ne, ne), 1)
             < lax.broadcasted_iota(I32, (ne, ne), 0)).astype(BF16)
    otile = jnp.dot(lower, jnp.broadcast_to(ntile, (ne, LANES)).astype(BF16),
                    preferred_element_type=F32)
    cnt_ref[...] = jnp.broadcast_to(cnt, (ne, LANES)).astype(I32)
    nt_ref[...] = jnp.broadcast_to(ntile, (ne, LANES)).astype(I32)
    ot_ref[...] = otile.astype(I32)
    off_rows = otile[:, 0:1] * float(row_tile)

    def pass_b(i, carry):
        cols = pl.ds(pl.multiple_of(i * tb, tb), tb)
        for k in range(TOP_K):
            hot = e_iota == idx_s[k:k + 1, cols]
            off = jnp.sum(jnp.where(hot, off_rows, 0.0), axis=0, keepdims=True)
            pos_ref[k:k + 1, cols] = (rank_s[k:k + 1, cols] + off).astype(I32)
        return carry

    lax.fori_loop(0, t // tb, pass_b, 0)


def _route_call(logits_t, tb=256, row_tile=ROW_TILE):
    ne, t = logits_t.shape
    return pl.pallas_call(
        functools.partial(_route_kernel, tb=tb, row_tile=row_tile),
        out_shape=[jax.ShapeDtypeStruct((TOP_K, t), I32),
                   jax.ShapeDtypeStruct((TOP_K, t), F32),
                   jax.ShapeDtypeStruct((ne, LANES), I32),
                   jax.ShapeDtypeStruct((ne, LANES), I32),
                   jax.ShapeDtypeStruct((ne, LANES), I32)],
        scratch_shapes=[pltpu.VMEM((TOP_K, t), I32), pltpu.VMEM((TOP_K, t), F32)],
        compiler_params=_cparams(None),
        name="route",
    )(logits_t)


_PAD_PIECES = tuple(ROW_TILE >> (k + 1) for k in range(ROW_TILE.bit_length() - 1))


def _dispatch_kernel(pos_ref, pad_start_ref, pad_n_ref, src_ref, dst_ref, stage_s, zero_s,
                     lsem, ssem, zsem, *, tb, ne):
    i = pl.program_id(0)
    nsteps = pl.num_programs(0)
    slot = i % 2

    def load(step, s):
        return pltpu.make_async_copy(src_ref.at[pl.ds(step * tb, tb)], stage_s.at[s], lsem.at[s])

    def wait_rows(s):
        for k in range(TOP_K):
            pltpu.make_async_copy(stage_s.at[s], dst_ref.at[pl.ds(0, tb)], ssem.at[s]).wait()

    def pad_copies(e):
        n = pad_n_ref[e]
        start = pad_start_ref[e]
        out = []
        for piece in _PAD_PIECES:
            at = start + (n & ~(2 * piece - 1))
            out.append(((n & piece) != 0, pltpu.make_async_copy(
                zero_s.at[pl.ds(0, piece)], dst_ref.at[pl.ds(at, piece)], zsem)))
        return out

    @pl.when(i == 0)
    def _zero_pads():
        zero_s[...] = jnp.zeros(zero_s.shape, zero_s.dtype)

        def issue(e, c):
            for cond, cp in pad_copies(e):
                @pl.when(cond)
                def _():
                    cp.start()
            return c

        def drain(e, c):
            for cond, cp in pad_copies(e):
                @pl.when(cond)
                def _():
                    cp.wait()
            return c

        lax.fori_loop(0, ne, issue, 0)
        lax.fori_loop(0, ne, drain, 0)
        load(0, 0).start()

    @pl.when(i > 0)
    def _():
        wait_rows(1 - slot)

    @pl.when(i + 1 < nsteps)
    def _():
        load(i + 1, 1 - slot).start()

    load(i, slot).wait()

    def issue_rows(tl, c):
        for k in range(TOP_K):
            pltpu.make_async_copy(stage_s.at[slot, tl], dst_ref.at[pos_ref[k, tl]],
                                  ssem.at[slot]).start(priority=k % 2)
        return c

    lax.fori_loop(0, tb, issue_rows, 0, unroll=4)

    @pl.when(i == nsteps - 1)
    def _():
        wait_rows(slot)


def _dispatch_call(pos, pad_start, pad_n, h2p, n_slots, tb=512):
    t = h2p.shape[0]
    row = h2p.shape[1:]
    ne = pad_n.shape[0]
    return pl.pallas_call(
        functools.partial(_dispatch_kernel, tb=tb, ne=ne),
        grid=(t // tb,),
        in_specs=[pl.BlockSpec((TOP_K, tb), lambda i: (0, i), memory_space=pltpu.SMEM),
                  pl.BlockSpec(memory_space=pltpu.SMEM),
                  pl.BlockSpec(memory_space=pltpu.SMEM),
                  pl.BlockSpec(memory_space=pl.ANY)],
        out_specs=pl.BlockSpec(memory_space=pl.ANY),
        out_shape=jax.ShapeDtypeStruct((n_slots, *row), U32),
        scratch_shapes=[pltpu.VMEM((2, tb, *row), U32),
                        pltpu.VMEM((_PAD_PIECES[0], *row), U32),
                        pltpu.SemaphoreType.DMA((2,)),
                        pltpu.SemaphoreType.DMA((2,)),
                        pltpu.SemaphoreType.DMA(())],
        compiler_params=pltpu.CompilerParams(dimension_semantics=("arbitrary",),
                                             vmem_limit_bytes=VMEM_LIMIT, has_side_effects=True),
        name="dispatch",
    )(pos, pad_start, pad_n, h2p)


def _moe_kernel(we_ref, wt_ref, wn_ref, xs_ref, win_ref, bin_ref, wout_ref, bo_ref,
                ys_ref, xu_s, acc_s, wg_st, wu_st, wo_st, yst_s, xsem, wsem, ysem,
                *, tmx, fc, nj):
    w = pl.program_id(0)
    nw = pl.num_programs(0)
    nt = wn_ref[w]
    t0 = wt_ref[w]
    half = xu_s.shape[-2] * xu_s.shape[-1]
    de = nj * fc
    nxt = jnp.minimum(w + 1, nw - 1)
    has_next = jnp.logical_and(w + 1 < nw, wn_ref[nxt] > 0)
    xslot = w % 2

    def local_rows(r, n=1):
        return pl.ds(pl.multiple_of(r * tmx, tmx), n * tmx)

    def x_copy(item_t0, r, slot):
        src = pl.ds(pl.multiple_of((item_t0 + r) * tmx, tmx), tmx)
        return pltpu.make_async_copy(xs_ref.at[src], xu_s.at[slot, local_rows(r)], xsem.at[slot])

    def y_copy(r, slot):
        dst = pl.ds(pl.multiple_of((t0 + r) * tmx, tmx), tmx)
        return pltpu.make_async_copy(yst_s.at[slot], ys_ref.at[dst], ysem.at[slot])

    def w_copies(expert, j, slot):
        c0 = pl.multiple_of(j * fc, fc)
        return (pltpu.make_async_copy(win_ref.at[expert, :, pl.ds(c0, fc)], wg_st.at[slot], wsem.at[slot]),
                pltpu.make_async_copy(win_ref.at[expert, :, pl.ds(de + c0, fc)], wu_st.at[slot],
                                      wsem.at[slot]),
                pltpu.make_async_copy(wout_ref.at[expert, pl.ds(c0, fc), :], wo_st.at[slot],
                                      wsem.at[slot]))

    def start_weights(expert, j, slot):
        for cp in w_copies(expert, j, slot):
            cp.start()

    def start_rows(item_t0, item_nt, slot):
        def go(r, c):
            x_copy(item_t0, r, slot).start()
            return c
        lax.fori_loop(0, item_nt, go, 0)

    @pl.when(w == 0)
    def _prologue():
        start_weights(we_ref[0], 0, 0)
        start_rows(t0, nt, 0)

    @pl.when(nt > 0)
    def _work():
        def chunk(j, carry):
            slot = j % 2

            @pl.when(j + 1 < nj)
            def _():
                start_weights(we_ref[w], j + 1, 1 - slot)

            @pl.when(jnp.logical_and(j + 1 == nj, has_next))
            def _():
                start_weights(we_ref[nxt], 0, 1 - slot)

            for cp in w_copies(we_ref[w], j, slot):
                cp.wait()
            stage = (wg_st.at[slot], wu_st.at[slot], wo_st.at[slot])
            biases = (bin_ref[j], bin_ref[nj + j])

            @pl.when(j == 0)
            def _rows_ready():
                def finish(r, c):
                    x_copy(t0, r, xslot).wait()
                    acc_s[local_rows(r), :] = jnp.zeros((tmx, acc_s.shape[1]), F32)
                    return c
                lax.fori_loop(0, nt, finish, 0)

                @pl.when(has_next)
                def _():
                    start_rows(wt_ref[nxt], wn_ref[nxt], 1 - xslot)

            _moe_passes(xu_s.at[xslot], acc_s, stage, biases, nt, local_rows, half)
            return carry

        lax.fori_loop(0, nj, chunk, 0)

        def y_wait(slot):
            y_copy(0, slot).wait()

        prev_nt = jnp.where(w > 0, wn_ref[jnp.maximum(w - 1, 0)], 0)

        @pl.when(prev_nt >= 1)
        def _():
            y_wait((prev_nt - 1) % 2)

        @pl.when(prev_nt >= 2)
        def _():
            y_wait(prev_nt % 2)

        def emit(r, c):
            slot = r % 2

            @pl.when(r >= 2)
            def _():
                y_wait(slot)

            y = acc_s[local_rows(r), :] + bo_ref[...]
            yst_s[slot] = pltpu.pack_elementwise([y[:, :half], y[:, half:]],
                                                 packed_dtype=BF16).reshape(yst_s.shape[1:])
            y_copy(r, slot).start()
            return c

        lax.fori_loop(0, nt, emit, 0)

        @pl.when(jnp.logical_not(has_next))
        def _drain():
            @pl.when(nt >= 2)
            def _():
                y_wait(nt % 2)

            y_wait((nt - 1) % 2)


def _moe_passes(xu_ref, acc_s, stage, biases, nt, local_rows, half):
    wg_ref, wu_ref, wo_ref = stage
    bg, bu = biases

    def proj(x_lo, x_hi, w_ref, b):
        return (jnp.dot(x_lo, w_ref[0:half, :].astype(BF16), preferred_element_type=F32)
                + jnp.dot(x_hi, w_ref[half:2 * half, :].astype(BF16), preferred_element_type=F32) + b)

    def rows_step(rows):
        xw = xu_ref[rows]
        xw = xw.reshape(xw.shape[0], half)
        x_lo = pltpu.unpack_elementwise(xw, index=0, packed_dtype=BF16,
                                        unpacked_dtype=F32).astype(BF16)
        x_hi = pltpu.unpack_elementwise(xw, index=1, packed_dtype=BF16,
                                        unpacked_dtype=F32).astype(BF16)
        gg = jnp.minimum(proj(x_lo, x_hi, wg_ref, bg), SWIGLU_LIMIT)
        uu = jnp.clip(proj(x_lo, x_hi, wu_ref, bu), -SWIGLU_LIMIT, SWIGLU_LIMIT)
        act = (uu + 1.0) * (gg * jax.nn.sigmoid(SWIGLU_ALPHA * gg))
        acc_s[rows, :] += jnp.dot(act.astype(BF16), wo_ref[...].astype(BF16),
                                  preferred_element_type=F32)

    def big(pi, c):
        rows_step(local_rows(PASS_TILES * pi, PASS_TILES))
        return c

    lax.fori_loop(0, nt // PASS_TILES, big, 0)
    piece = PASS_TILES // 2
    while piece >= 1:
        @pl.when((nt & piece) != 0)
        def _(piece=piece):
            rows_step(local_rows(nt & ~(2 * piece - 1), piece))
        piece //= 2


def _moe_call(work_e, work_t0, work_nt, xs, w_in, b_in, w_out, b_out,
              tmx=ROW_TILE, group=GROUP_TILES, fc=F_CHUNK):
    ne, d, de2 = w_in.shape
    de = de2 // 2
    nj = de // fc
    assert nj % 2 == 0, "weight stage slots alternate per chunk and restart at 0 per work item"
    n_work = work_e.shape[0]
    n_slots = xs.shape[0]
    row = xs.shape[1:]

    grid_spec = pltpu.PrefetchScalarGridSpec(
        num_scalar_prefetch=3,
        grid=(n_work,),
        in_specs=[
            pl.BlockSpec(memory_space=pl.ANY),
            pl.BlockSpec(memory_space=pl.ANY),
            pl.BlockSpec((None, 2 * nj, 1, fc), lambda w, we, wt, wn: (we[w], 0, 0, 0)),
            pl.BlockSpec(memory_space=pl.ANY),
            pl.BlockSpec((None, 1, d), lambda w, we, wt, wn: (we[w], 0, 0)),
        ],
        out_specs=pl.BlockSpec(memory_space=pl.ANY),
        scratch_shapes=[pltpu.VMEM((2, group * tmx, *row), U32),
                        pltpu.VMEM((group * tmx, d), F32),
                        pltpu.VMEM((2, d, fc), F32),
                        pltpu.VMEM((2, d, fc), F32),
                        pltpu.VMEM((2, fc, d), F32),
                        pltpu.VMEM((2, tmx, *row), U32),
                        pltpu.SemaphoreType.DMA((2,)),
                        pltpu.SemaphoreType.DMA((2,)),
                        pltpu.SemaphoreType.DMA((2,))],
    )
    return pl.pallas_call(
        functools.partial(_moe_kernel, tmx=tmx, fc=fc, nj=nj),
        grid_spec=grid_spec,
        out_shape=jax.ShapeDtypeStruct((n_slots, *row), U32),
        compiler_params=pltpu.CompilerParams(dimension_semantics=("arbitrary",),
                                             vmem_limit_bytes=VMEM_LIMIT, has_side_effects=True),
        name="moe_experts",
    )(work_e, work_t0, work_nt, xs, w_in, b_in.reshape(ne, 2 * nj, 1, fc), w_out,
      b_out.reshape(ne, 1, d))


def _combine_kernel(pos_ref, pos_next_ref, x1_ref, gate_ref, ys_ref, o_ref, ybuf, sem, *, tb):
    half = ybuf.shape[-2] * ybuf.shape[-1]
    i = pl.program_id(0)
    slot = i % 2

    def gather(p_ref, s):
        def issue(tl, c):
            for k in range(TOP_K):
                pltpu.make_async_copy(ys_ref.at[p_ref[k, tl]], ybuf.at[s, k, tl],
                                      sem.at[s]).start(priority=k % 2)
            return c
        lax.fori_loop(0, tb, issue, 0, unroll=4)

    @pl.when(i == 0)
    def _():
        gather(pos_ref, 0)

    @pl.when(i + 1 < pl.num_programs(0))
    def _():
        gather(pos_next_ref, 1 - slot)

    for k in range(TOP_K):
        pltpu.make_async_copy(ys_ref.at[pl.ds(0, tb)], ybuf.at[slot, k], sem.at[slot]).wait()

    lo = x1_ref[:, 0:half]
    hi = x1_ref[:, half:2 * half]
    for k in range(TOP_K):
        g = gate_ref[:, k:k + 1]
        yw = ybuf[slot, k].reshape(tb, half)
        lo = lo + g * pltpu.unpack_elementwise(yw, index=0, packed_dtype=BF16, unpacked_dtype=F32)
        hi = hi + g * pltpu.unpack_elementwise(yw, index=1, packed_dtype=BF16, unpacked_dtype=F32)
    o_ref[:, 0:half] = lo
    o_ref[:, half:2 * half] = hi


def _combine_call(pos, x1, gate_tk, ys, tb=256):
    t, d = x1.shape
    row = ys.shape[1:]
    last = t // tb - 1
    return pl.pallas_call(
        functools.partial(_combine_kernel, tb=tb),
        grid=(t // tb,),
        in_specs=[pl.BlockSpec((TOP_K, tb), lambda i: (0, i), memory_space=pltpu.SMEM),
                  pl.BlockSpec((TOP_K, tb), lambda i: (0, jnp.minimum(i + 1, last)),
                               memory_space=pltpu.SMEM),
                  pl.BlockSpec((tb, d), lambda i: (i, 0)),
                  pl.BlockSpec((tb, TOP_K), lambda i: (i, 0)),
                  pl.BlockSpec(memory_space=pl.ANY)],
        out_specs=pl.BlockSpec((tb, d), lambda i: (i, 0)),
        out_shape=jax.ShapeDtypeStruct((t, d), F32),
        scratch_shapes=[pltpu.VMEM((2, TOP_K, tb, *row), U32),
                        pltpu.SemaphoreType.DMA((2,))],
        compiler_params=_cparams(("arbitrary",)),
        name="combine",
    )(pos, pos, x1, gate_tk, ys)


def _work_list(ntile, otile, n_work, group):
    ne = ntile.shape[0]
    items = (ntile + group - 1) // group
    ends = jnp.cumsum(items)
    total = ends[-1]
    w = jnp.arange(n_work, dtype=I32)
    valid = w < total
    e_w = jnp.clip(jnp.sum((ends[None, :] <= w[:, None]).astype(I32), axis=1), 0, ne - 1)
    local = w - (ends - items)[e_w]
    t0_w = otile[e_w] + local * group
    nt_w = jnp.where(valid, jnp.clip(ntile[e_w] - local * group, 0, group), 0)
    e_last = e_w[jnp.maximum(total - 1, 0)]
    return (jnp.where(valid, e_w, e_last).astype(I32), jnp.where(valid, t0_w, 0).astype(I32),
            nt_w.astype(I32))


def _rotary_tables(seq, dk):
    inv_freq = ROPE_BASE ** (-jnp.arange(0, dk, 2, dtype=F32) / dk)
    ang = jnp.arange(seq, dtype=I32).astype(F32)[:, None] * inv_freq[None, :]
    cos = jnp.repeat(jnp.cos(ang), 2, axis=1)
    sin = jnp.stack([-jnp.sin(ang), jnp.sin(ang)], axis=-1).reshape(seq, dk)
    return cos, sin


def kernel(x, norm_mix_g, w_mix_in, ret_gn_g, q_norm_g, k_norm_g, lambda_q1, lambda_k1, lambda_q2, lambda_k2, diff_subln_g, rel_bias_table, w_mix_out, norm_ffn_g, w_router, b_router, w_exp_in, b_exp_in, w_exp_out, b_exp_out):
    batch, seq, d = x.shape
    t = batch * seq
    depth = norm_mix_g.shape[0]
    ne = w_router.shape[-1]
    n_tiles_max = (t * TOP_K) // ROW_TILE + ne
    n_work = ne + (n_tiles_max - ne) // GROUP_TILES
    cos, sin = _rotary_tables(seq, HEAD_W)
    log_gamma = jnp.log1p(-(2.0 ** (-5.0 - jnp.arange(RET_HEADS, dtype=F32))))

    x2 = x.reshape(t, d)
    for l in range(depth):
        proj = _inproj_call(x2, norm_mix_g[l], w_mix_in[l])
        y_ret = _ret_call(proj, cos, sin, log_gamma, ret_gn_g[l], batch, seq)
        lam_vecs = jnp.stack([lambda_q1[l], lambda_k1[l], lambda_q2[l], lambda_k2[l]])
        y_diff = _diff_call(proj, rel_bias_table, q_norm_g[l], k_norm_g[l], lam_vecs,
                            diff_subln_g[l], batch, seq)
        x1, h2p, logits_t = _outproj_call(y_ret, y_diff, x2, w_mix_out[l], norm_ffn_g[l],
                                          w_router[l], b_router[l])
        pos, gate, cnt, ntile, otile = _route_call(logits_t)
        cnt, ntile, otile = cnt[:, 0], ntile[:, 0], otile[:, 0]
        work_e, work_t0, work_nt = _work_list(ntile, otile, n_work, GROUP_TILES)
        xs = _dispatch_call(pos, otile * ROW_TILE + cnt, ntile * ROW_TILE - cnt, h2p,
                            n_tiles_max * ROW_TILE)
        ys = _moe_call(work_e, work_t0, work_nt, xs, w_exp_in[l], b_exp_in[l], w_exp_out[l],
                       b_exp_out[l])
        x2 = _combine_call(pos, x1, gate.T, ys)
    return x2.reshape(batch, seq, d)
```

```python
import functools
import math

import jax
import jax.numpy as jnp
from jax import lax
from jax.experimental import pallas as pl
from jax.experimental.pallas import tpu as pltpu

F32 = jnp.float32
BF16 = jnp.bfloat16
I32 = jnp.int32
U32 = jnp.uint32

EPS = 1e-6
CHUNK = 64
RET_HEADS = 8
DIFF_HEADS = 8
HEAD_W = 128
DIFF_DK = 64
ROPE_BASE = 10000.0
NUM_BUCKETS = 32
MAX_DISTANCE = 128
TOP_K = 4
SWIGLU_LIMIT = 7.0
SWIGLU_ALPHA = 1.702
LAM_INIT = 0.8 - 0.6 * math.exp(-0.3 * 0)

LANES = 128
NEG_BIG = -1e30
VMEM_LIMIT = 60 * 1024 * 1024

SEQ_BLK = 256
ROW_TILE = 256
GROUP_TILES = 6
PASS_TILES = 4
F_CHUNK = 512
CAST_ROWS = 256


def _cparams(sem, vmem=VMEM_LIMIT):
    return pltpu.CompilerParams(dimension_semantics=sem, vmem_limit_bytes=vmem)


def _inproj_kernel(x_ref, g_ref, w_ref, o_ref, wb_ref, *, rows_per_cast, sub_blocks=2):
    @pl.when(pl.program_id(1) == 0)
    def _():
        d = w_ref.shape[0]
        for c in range(d // rows_per_cast):
            sl = slice(c * rows_per_cast, (c + 1) * rows_per_cast)
            wb_ref[sl, :] = w_ref[sl, :].astype(BF16)

    sub = x_ref.shape[0] // sub_blocks
    for sb in range(sub_blocks):
        rs = slice(sb * sub, (sb + 1) * sub)
        x = x_ref[rs, :]
        ms = jnp.mean(x * x, axis=-1, keepdims=True)
        h = (x * lax.rsqrt(ms + EPS) * g_ref[...]).astype(BF16)
        acc = jnp.dot(h, wb_ref[...], preferred_element_type=F32)
        for j in range(o_ref.shape[0]):
            o_ref[j, rs, :] = acc[:, j * LANES:(j + 1) * LANES].astype(o_ref.dtype)


def _inproj_call(x2, g, w, tm=1024, tn=1024):
    t, d = x2.shape
    n = w.shape[1]
    return pl.pallas_call(
        functools.partial(_inproj_kernel, rows_per_cast=min(CAST_ROWS, d)),
        grid=(n // tn, t // tm),
        in_specs=[pl.BlockSpec((tm, d), lambda j, i: (i, 0)),
                  pl.BlockSpec((1, d), lambda j, i: (0, 0)),
                  pl.BlockSpec((d, tn), lambda j, i: (0, j))],
        out_specs=pl.BlockSpec((tn // LANES, tm, LANES), lambda j, i: (j, i, 0)),
        out_shape=jax.ShapeDtypeStruct((n // LANES, t, LANES), BF16),
        scratch_shapes=[pltpu.VMEM((d, tn), BF16)],
        compiler_params=_cparams(("arbitrary", "arbitrary")),
        name="in_proj",
    )(x2, g.reshape(1, d), w)


def _ret_kernel(lg_ref, q_ref, k_ref, v_ref, g_ref, cos_ref, sin_ref, gn_ref, o_ref, *, blk, nblk):
    dk = q_ref.shape[-1]
    lg = lg_ref[pl.program_id(1)]
    row = lax.broadcasted_iota(I32, (blk, blk), 0)
    col = lax.broadcasted_iota(I32, (blk, blk), 1)
    dist = jnp.abs(row - col).astype(F32)
    visible = (col // CHUNK) <= (row // CHUNK)
    dmask = jnp.where(visible, jnp.exp(lg * dist), 0.0)
    rr = lax.broadcasted_iota(I32, (blk, dk), 0).astype(F32)
    qdec = jnp.exp(lg * (rr + 1.0))
    kdec = jnp.exp(lg * (blk - 1.0 - rr))
    bdec = jnp.exp(lg * jnp.full((1, HEAD_W), float(blk), F32))
    even = (lax.broadcasted_iota(I32, (blk, dk), 1) & 1) == 0
    scale = dk ** -0.5

    def body(i, state):
        rows = pl.ds(pl.multiple_of(i * blk, blk), blk)
        cos = cos_ref[rows, :]
        sin = sin_ref[rows, :]

        def rot(x):
            partner = jnp.where(even, pltpu.roll(x, dk - 1, 1), pltpu.roll(x, 1, 1))
            return x * cos + partner * sin

        qr = rot(q_ref[rows, :].astype(F32)) * scale
        kr = rot(k_ref[rows, :].astype(F32))
        v = v_ref[rows, :]
        s = lax.dot_general(qr.astype(BF16), kr.astype(BF16), (((1,), (1,)), ((), ())),
                            preferred_element_type=F32) * dmask
        out = jnp.dot(s.astype(BF16), v, preferred_element_type=F32)
        out = out + jnp.dot((qr * qdec).astype(BF16), state.astype(BF16),
                            preferred_element_type=F32)
        kv = lax.dot_general((kr * kdec).astype(BF16), v, (((0,), (0,)), ((), ())),
                             preferred_element_type=F32)
        state = state * bdec + kv
        ms = jnp.mean(out * out, axis=-1, keepdims=True)
        normed = out * lax.rsqrt(ms + EPS) * gn_ref[...]
        g = g_ref[rows, :].astype(F32)
        o_ref[rows, :] = (g * jax.nn.sigmoid(g) * normed).astype(o_ref.dtype)
        return state

    lax.fori_loop(0, nblk, body, jnp.zeros((dk, HEAD_W), F32), unroll=True)


def _ret_call(proj, cos, sin, log_gamma, gn_g, batch, seq, blk=SEQ_BLK):
    nh = RET_HEADS

    def head_spec(base):
        return pl.BlockSpec((None, seq, LANES), lambda b, h: (base + h, b, 0))

    return pl.pallas_call(
        functools.partial(_ret_kernel, blk=blk, nblk=seq // blk),
        grid=(batch, nh),
        in_specs=[pl.BlockSpec(memory_space=pltpu.SMEM),
                  head_spec(0), head_spec(nh), head_spec(2 * nh), head_spec(3 * nh),
                  pl.BlockSpec((seq, LANES), lambda b, h: (0, 0)),
                  pl.BlockSpec((seq, LANES), lambda b, h: (0, 0)),
                  pl.BlockSpec((None, 1, LANES), lambda b, h: (h, 0, 0))],
        out_specs=pl.BlockSpec((None, seq, LANES), lambda b, h: (h, b, 0)),
        out_shape=jax.ShapeDtypeStruct((nh, batch * seq, LANES), BF16),
        compiler_params=_cparams(("arbitrary", "arbitrary")),
        name="retention",
    )(log_gamma, proj, proj, proj, proj, cos, sin, gn_g.reshape(nh, 1, LANES))


def _diff_kernel(tbl_ref, q_ref, k_ref, v_ref, qg_ref, kg_ref, lam_ref, sg_ref, bidx_ref, o_ref,
                 qz_s, kn_s, bias_s, s_s, p_s, *, blk, nblk):
    h = pl.program_id(0)
    b = pl.program_id(1)
    far_bucket = NUM_BUCKETS // 2 - 1

    @pl.when(b == 0)
    def _build_bias():
        row = lax.broadcasted_iota(I32, (blk, blk), 0)
        col = lax.broadcasted_iota(I32, (blk, blk), 1)
        visible = (col // CHUNK) <= (row // CHUNK)
        for d in range(2):
            idx = bidx_ref[d]
            bias = jnp.zeros((blk, blk), F32)
            for bucket in range(NUM_BUCKETS):
                bias = jnp.where(idx == bucket, tbl_ref[bucket, h], bias)
            if d == 0:
                bias = jnp.where(visible, bias, NEG_BIG)
            bias_s[d, 0:blk, :] = bias
            bias_s[d, blk:2 * blk, :] = bias

    lo = lax.broadcasted_iota(I32, (blk, HEAD_W), 1) < DIFF_DK
    scale = DIFF_DK ** -0.5
    same_half = ((lax.broadcasted_iota(I32, (HEAD_W, HEAD_W), 0) // DIFF_DK)
                 == (lax.broadcasted_iota(I32, (HEAD_W, HEAD_W), 1) // DIFF_DK)).astype(BF16)

    def half_norm(x, g):
        ms = jnp.dot((x * x).astype(BF16), same_half, preferred_element_type=F32) * (1.0 / DIFF_DK)
        return x * lax.rsqrt(ms + EPS) * g

    def prep(i, carry):
        rows = pl.ds(pl.multiple_of(i * blk, blk), blk)
        qn = half_norm(q_ref[rows, :].astype(F32), qg_ref[...]) * scale
        kn = half_norm(k_ref[rows, :].astype(F32), kg_ref[...])
        base = pl.multiple_of(i * 2 * blk, 2 * blk)
        qz_s[pl.ds(base, blk), :] = jnp.where(lo, qn, 0.0).astype(BF16)
        qz_s[pl.ds(base + blk, blk), :] = jnp.where(lo, 0.0, qn).astype(BF16)
        kn_s[rows, :] = kn.astype(BF16)
        return carry

    lax.fori_loop(0, nblk, prep, 0)

    lam = (jnp.exp(jnp.sum(lam_ref[0:1, :] * lam_ref[1:2, :], axis=-1, keepdims=True))
           - jnp.exp(jnp.sum(lam_ref[2:3, :] * lam_ref[3:4, :], axis=-1, keepdims=True))
           + LAM_INIT)
    c_far = tbl_ref[far_bucket, h]

    def lane_fold(x, op):
        out = x[:, 0:LANES]
        for c in range(1, blk // LANES):
            out = op(out, x[:, c * LANES:(c + 1) * LANES])
        return out

    for i in range(nblk):
        qz = qz_s[i * 2 * blk:(i + 1) * 2 * blk, :]
        m_t = jnp.full((2 * blk, LANES), NEG_BIG, F32)
        for j in range(i + 1):
            keys = slice(j * blk, (j + 1) * blk)
            s = lax.dot_general(qz, kn_s[keys, :], (((1,), (1,)), ((), ())),
                                preferred_element_type=F32)
            s = s + (bias_s[0] if j == i else bias_s[1] if j == i - 1 else c_far)
            s_s[:, keys] = s
            m_t = jnp.maximum(m_t, lane_fold(s, jnp.maximum))
        m = jnp.max(m_t, axis=-1, keepdims=True)
        l_t = jnp.zeros((2 * blk, LANES), F32)
        for j in range(i + 1):
            keys = slice(j * blk, (j + 1) * blk)
            p = jnp.exp(s_s[:, keys] - m)
            l_t = l_t + lane_fold(p, jnp.add)
            p_s[:, keys] = p.astype(BF16)
        l = jnp.sum(l_t, axis=-1, keepdims=True)
        kend = (i + 1) * blk
        o = jnp.dot(p_s[:, 0:kend], v_ref[0:kend, :], preferred_element_type=F32) / l
        att = o[0:blk, :] - lam * o[blk:2 * blk, :]
        ms = jnp.mean(att * att, axis=-1, keepdims=True)
        y = att * lax.rsqrt(ms + EPS) * sg_ref[...] * (1.0 - LAM_INIT)
        o_ref[i * blk:(i + 1) * blk, :] = y.astype(o_ref.dtype)


def _t5_bucket(rel):
    nb = NUM_BUCKETS // 2
    max_exact = nb // 2
    base = jnp.where(rel > 0, nb, 0)
    n = jnp.abs(rel)
    large = max_exact + (jnp.log(jnp.maximum(n, 1).astype(jnp.float32) / max_exact)
                         / math.log(MAX_DISTANCE / max_exact) * (nb - max_exact)).astype(jnp.int32)
    large = jnp.minimum(large, nb - 1)
    return base + jnp.where(n < max_exact, n, large)


def _diff_call(proj, rel_table, qg, kg, lam_vecs, sg, batch, seq, blk=SEQ_BLK):
    nh = DIFF_HEADS
    first = 4 * RET_HEADS
    r = jnp.arange(blk, dtype=I32)
    rel0 = r[None, :] - r[:, None]
    bidx = jnp.stack([_t5_bucket(rel0), _t5_bucket(rel0 - blk)]).astype(I32) & (NUM_BUCKETS - 1)

    def head_spec(base):
        return pl.BlockSpec((None, seq, LANES), lambda h, b: (base + h, b, 0))

    def vec_spec():
        return pl.BlockSpec((1, LANES), lambda h, b: (0, 0))

    return pl.pallas_call(
        functools.partial(_diff_kernel, blk=blk, nblk=seq // blk),
        grid=(nh, batch),
        in_specs=[pl.BlockSpec(memory_space=pltpu.SMEM),
                  head_spec(first), head_spec(first + nh), head_spec(first + 2 * nh),
                  vec_spec(), vec_spec(),
                  pl.BlockSpec((4, DIFF_DK), lambda h, b: (0, 0)),
                  vec_spec(),
                  pl.BlockSpec((2, blk, blk), lambda h, b: (0, 0, 0))],
        out_specs=pl.BlockSpec((None, seq, LANES), lambda h, b: (h, b, 0)),
        out_shape=jax.ShapeDtypeStruct((nh, batch * seq, LANES), BF16),
        scratch_shapes=[pltpu.VMEM((2 * seq, LANES), BF16),
                        pltpu.VMEM((seq, LANES), BF16),
                        pltpu.VMEM((2, 2 * blk, blk), F32),
                        pltpu.VMEM((2 * blk, seq), F32),
                        pltpu.VMEM((2 * blk, seq), BF16)],
        compiler_params=_cparams(("arbitrary", "arbitrary")),
        name="diff_attn",
    )(rel_table, proj, proj, proj,
      jnp.tile(qg, 2).reshape(1, LANES), jnp.tile(kg, 2).reshape(1, LANES),
      lam_vecs, sg.reshape(1, LANES), bidx)


def _outproj_kernel(yr_ref, yd_ref, x_ref, w_hbm, g_ref, wr_ref, br_ref,
                    x1_ref, h2p_ref, lt_ref, wb_s, wstage_s, wr2_s, wsem,
                    *, rows_per_cast, sub_blocks=2):
    ne = lt_ref.shape[0]
    nchunk = wb_s.shape[0] // rows_per_cast

    def w_copy(c):
        return pltpu.make_async_copy(w_hbm.at[pl.ds(c * rows_per_cast, rows_per_cast)],
                                     wstage_s.at[c % 2], wsem.at[c % 2])

    @pl.when(pl.program_id(0) == 0)
    def _stage_weights():
        w_copy(0).start()
        for c in range(nchunk):
            if c + 1 < nchunk:
                w_copy(c + 1).start()
            w_copy(c).wait()
            wb_s[c * rows_per_cast:(c + 1) * rows_per_cast, :] = wstage_s[c % 2].astype(BF16)
        wr = wr_ref[...]
        wr_hi = wr.astype(BF16)
        wr2_s[:, 0:LANES] = wr_hi
        wr2_s[:, LANES:2 * LANES] = (wr - wr_hi.astype(F32)).astype(BF16)

    tm = x_ref.shape[0]
    sub = tm // sub_blocks
    for sb in range(sub_blocks):
        rs = slice(sb * sub, (sb + 1) * sub)
        y = jnp.concatenate([yr_ref[j, rs, :] for j in range(yr_ref.shape[0])]
                            + [yd_ref[j, rs, :] for j in range(yd_ref.shape[0])], axis=-1)
        x1 = x_ref[rs, :] + jnp.dot(y, wb_s[...], preferred_element_type=F32)
        x1_ref[rs, :] = x1
        ms = jnp.mean(x1 * x1, axis=-1, keepdims=True)
        h2 = x1 * lax.rsqrt(ms + EPS) * g_ref[...]
        h_hi = h2.astype(BF16)
        h_lo = (h2 - h_hi.astype(F32)).astype(BF16)
        parts = (jnp.dot(h_hi, wr2_s[...], preferred_element_type=F32)
                 + jnp.dot(h_lo, wr2_s[...], preferred_element_type=F32))
        logits = parts[:, 0:LANES] + parts[:, LANES:2 * LANES]
        lt_ref[:, rs] = logits.T[0:ne, :] + br_ref[...]
        half = h2.shape[1] // 2
        packed = pltpu.pack_elementwise([h2[:, :half], h2[:, half:]], packed_dtype=BF16)
        h2p_ref[rs] = packed.reshape((sub,) + h2p_ref.shape[1:])


def _outproj_call(y_ret, y_diff, x2, w_out, g, w_router, b_router, tm=512):
    t, d = x2.shape
    ne = w_router.shape[1]
    nhr, nhd = y_ret.shape[0], y_diff.shape[0]
    rows_per_cast = min(CAST_ROWS, d)
    wr_pad = jnp.pad(w_router, ((0, 0), (0, LANES - ne)))
    return pl.pallas_call(
        functools.partial(_outproj_kernel, rows_per_cast=rows_per_cast),
        grid=(t // tm,),
        in_specs=[pl.BlockSpec((nhr, tm, LANES), lambda i: (0, i, 0)),
                  pl.BlockSpec((nhd, tm, LANES), lambda i: (0, i, 0)),
                  pl.BlockSpec((tm, d), lambda i: (i, 0)),
                  pl.BlockSpec(memory_space=pl.ANY),
                  pl.BlockSpec((1, d), lambda i: (0, 0)),
                  pl.BlockSpec((d, LANES), lambda i: (0, 0)),
                  pl.BlockSpec((ne, 1), lambda i: (0, 0))],
        out_specs=[pl.BlockSpec((tm, d), lambda i: (i, 0)),
                   pl.BlockSpec((tm, d // 2 // LANES, LANES), lambda i: (i, 0, 0)),
                   pl.BlockSpec((ne, tm), lambda i: (0, i))],
        out_shape=[jax.ShapeDtypeStruct((t, d), F32),
                   jax.ShapeDtypeStruct((t, d // 2 // LANES, LANES), U32),
                   jax.ShapeDtypeStruct((ne, t), F32)],
        scratch_shapes=[pltpu.VMEM((d, d), BF16),
                        pltpu.VMEM((2, rows_per_cast, d), F32),
                        pltpu.VMEM((d, 2 * LANES), BF16),
                        pltpu.SemaphoreType.DMA((2,))],
        compiler_params=_cparams(("arbitrary",)),
        name="out_proj_router",
    )(y_ret, y_diff, x2, w_out, g.reshape(1, d), wr_pad, b_router.reshape(ne, 1))


def _route_kernel(lt_ref, pos_ref, gate_ref, cnt_ref, nt_ref, ot_ref, idx_s, rank_s, *, tb, row_tile):
    ne, t = lt_ref.shape
    e_iota = lax.broadcasted_iota(I32, (ne, tb), 0)
    upper = (lax.broadcasted_iota(I32, (tb, tb), 0)
             < lax.broadcasted_iota(I32, (tb, tb), 1)).astype(BF16)

    def pass_a(i, running):
        cols = pl.ds(pl.multiple_of(i * tb, tb), tb)
        l = lt_ref[:, cols]
        tops, hots = [], []
        for k in range(TOP_K):
            m = jnp.max(l, axis=0, keepdims=True)
            idx = jnp.min(jnp.where(l == m, e_iota, ne), axis=0, keepdims=True)
            hot = e_iota == idx
            l = jnp.where(hot, -jnp.inf, l)
            idx_s[k:k + 1, cols] = idx
            tops.append(m)
            hots.append(hot)
        exps = [jnp.exp(m - tops[0]) for m in tops]
        denom = exps[0] + exps[1] + exps[2] + exps[3]
        for k in range(TOP_K):
            gate_ref[k:k + 1, cols] = exps[k] / denom
        hot_all = jnp.zeros((ne, tb), F32)
        for hot in hots:
            hot_all = hot_all + hot.astype(F32)
        before = running + jnp.dot(hot_all.astype(BF16), upper, preferred_element_type=F32)
        for k in range(TOP_K):
            rank_s[k:k + 1, cols] = jnp.sum(jnp.where(hots[k], before, 0.0), axis=0, keepdims=True)
        return running + jnp.sum(hot_all, axis=1, keepdims=True)

    cnt = lax.fori_loop(0, t // tb, pass_a, jnp.zeros((ne, 1), F32))
    ntile = jnp.floor((cnt + (row_tile - 1.0)) * (1.0 / row_tile))
    lower = (lax.broadcasted_iota(I32, (ne, ne), 1)
             < lax.broadcasted_iota(I32, (ne, ne), 0)).astype(BF16)
    otile = jnp.dot(lower, jnp.broadcast_to(ntile, (ne, LANES)).astype(BF16),
                    preferred_element_type=F32)
    cnt_ref[...] = jnp.broadcast_to(cnt, (ne, LANES)).astype(I32)
    nt_ref[...] = jnp.broadcast_to(ntile, (ne, LANES)).astype(I32)
    ot_ref[...] = otile.astype(I32)
    off_rows = otile[:, 0:1] * float(row_tile)

    def pass_b(i, carry):
        cols = pl.ds(pl.multiple_of(i * tb, tb), tb)
        for k in range(TOP_K):
            hot = e_iota == idx_s[k:k + 1, cols]
            off = jnp.sum(jnp.where(hot, off_rows, 0.0), axis=0, keepdims=True)
            pos_ref[k:k + 1, cols] = (rank_s[k:k + 1, cols] + off).astype(I32)
        return carry

    lax.fori_loop(0, t // tb, pass_b, 0)


def _route_call(logits_t, tb=256, row_tile=ROW_TILE):
    ne, t = logits_t.shape
    return pl.pallas_call(
        functools.partial(_route_kernel, tb=tb, row_tile=row_tile),
        out_shape=[jax.ShapeDtypeStruct((TOP_K, t), I32),
                   jax.ShapeDtypeStruct((TOP_K, t), F32),
                   jax.ShapeDtypeStruct((ne, LANES), I32),
                   jax.ShapeDtypeStruct((ne, LANES), I32),
                   jax.ShapeDtypeStruct((ne, LANES), I32)],
        scratch_shapes=[pltpu.VMEM((TOP_K, t), I32), pltpu.VMEM((TOP_K, t), F32)],
        compiler_params=_cparams(None),
        name="route",
    )(logits_t)


_PAD_PIECES = tuple(ROW_TILE >> (k + 1) for k in range(ROW_TILE.bit_length() - 1))


def _dispatch_kernel(pos_ref, pad_start_ref, pad_n_ref, src_ref, dst_ref, stage_s, zero_s,
                     lsem, ssem, zsem, *, tb, ne):
    i = pl.program_id(0)
    nsteps = pl.num_programs(0)
    slot = i % 2

    def load(step, s):
        return pltpu.make_async_copy(src_ref.at[pl.ds(step * tb, tb)], stage_s.at[s], lsem.at[s])

    def wait_rows(s):
        for k in range(TOP_K):
            pltpu.make_async_copy(stage_s.at[s], dst_ref.at[pl.ds(0, tb)], ssem.at[s]).wait()

    def pad_copies(e):
        n = pad_n_ref[e]
        start = pad_start_ref[e]
        out = []
        for piece in _PAD_PIECES:
            at = start + (n & ~(2 * piece - 1))
            out.append(((n & piece) != 0, pltpu.make_async_copy(
                zero_s.at[pl.ds(0, piece)], dst_ref.at[pl.ds(at, piece)], zsem)))
        return out

    @pl.when(i == 0)
    def _zero_pads():
        zero_s[...] = jnp.zeros(zero_s.shape, zero_s.dtype)

        def issue(e, c):
            for cond, cp in pad_copies(e):
                @pl.when(cond)
                def _():
                    cp.start()
            return c

        def drain(e, c):
            for cond, cp in pad_copies(e):
                @pl.when(cond)
                def _():
                    cp.wait()
            return c

        lax.fori_loop(0, ne, issue, 0)
        lax.fori_loop(0, ne, drain, 0)
        load(0, 0).start()

    @pl.when(i > 0)
    def _():
        wait_rows(1 - slot)

    @pl.when(i + 1 < nsteps)
    def _():
        load(i + 1, 1 - slot).start()

    load(i, slot).wait()

    def issue_rows(tl, c):
        for k in range(TOP_K):
            pltpu.make_async_copy(stage_s.at[slot, tl], dst_ref.at[pos_ref[k, tl]],
                                  ssem.at[slot]).start(priority=k % 2)
        return c

    lax.fori_loop(0, tb, issue_rows, 0, unroll=4)

    @pl.when(i == nsteps - 1)
    def _():
        wait_rows(slot)


def _dispatch_call(pos, pad_start, pad_n, h2p, n_slots, tb=512):
    t = h2p.shape[0]
    row = h2p.shape[1:]
    ne = pad_n.shape[0]
    return pl.pallas_call(
        functools.partial(_dispatch_kernel, tb=tb, ne=ne),
        grid=(t // tb,),
        in_specs=[pl.BlockSpec((TOP_K, tb), lambda i: (0, i), memory_space=pltpu.SMEM),
                  pl.BlockSpec(memory_space=pltpu.SMEM),
                  pl.BlockSpec(memory_space=pltpu.SMEM),
                  pl.BlockSpec(memory_space=pl.ANY)],
        out_specs=pl.BlockSpec(memory_space=pl.ANY),
        out_shape=jax.ShapeDtypeStruct((n_slots, *row), U32),
        scratch_shapes=[pltpu.VMEM((2, tb, *row), U32),
                        pltpu.VMEM((_PAD_PIECES[0], *row), U32),
                        pltpu.SemaphoreType.DMA((2,)),
                        pltpu.SemaphoreType.DMA((2,)),
                        pltpu.SemaphoreType.DMA(())],
        compiler_params=pltpu.CompilerParams(dimension_semantics=("arbitrary",),
                                             vmem_limit_bytes=VMEM_LIMIT, has_side_effects=True),
        name="dispatch",
    )(pos, pad_start, pad_n, h2p)


def _moe_kernel(we_ref, wt_ref, wn_ref, xs_ref, win_ref, bin_ref, wout_ref, bo_ref,
                ys_ref, xu_s, acc_s, wg_st, wu_st, wo_st, yst_s, xsem, wsem, ysem,
                *, tmx, fc, nj):
    w = pl.program_id(0)
    nw = pl.num_programs(0)
    nt = wn_ref[w]
    t0 = wt_ref[w]
    half = xu_s.shape[-2] * xu_s.shape[-1]
    de = nj * fc
    nxt = jnp.minimum(w + 1, nw - 1)
    has_next = jnp.logical_and(w + 1 < nw, wn_ref[nxt] > 0)
    xslot = w % 2

    def local_rows(r, n=1):
        return pl.ds(pl.multiple_of(r * tmx, tmx), n * tmx)

    def x_copy(item_t0, r, slot):
        src = pl.ds(pl.multiple_of((item_t0 + r) * tmx, tmx), tmx)
        return pltpu.make_async_copy(xs_ref.at[src], xu_s.at[slot, local_rows(r)], xsem.at[slot])

    def y_copy(r, slot):
        dst = pl.ds(pl.multiple_of((t0 + r) * tmx, tmx), tmx)
        return pltpu.make_async_copy(yst_s.at[slot], ys_ref.at[dst], ysem.at[slot])

    def w_copies(expert, j, slot):
        c0 = pl.multiple_of(j * fc, fc)
        return (pltpu.make_async_copy(win_ref.at[expert, :, pl.ds(c0, fc)], wg_st.at[slot], wsem.at[slot]),
                pltpu.make_async_copy(win_ref.at[expert, :, pl.ds(de + c0, fc)], wu_st.at[slot],
                                      wsem.at[slot]),
                pltpu.make_async_copy(wout_ref.at[expert, pl.ds(c0, fc), :], wo_st.at[slot],
                                      wsem.at[slot]))

    def start_weights(expert, j, slot):
        for cp in w_copies(expert, j, slot):
            cp.start()

    def start_rows(item_t0, item_nt, slot):
        def go(r, c):
            x_copy(item_t0, r, slot).start()
            return c
        lax.fori_loop(0, item_nt, go, 0)

    @pl.when(w == 0)
    def _prologue():
        start_weights(we_ref[0], 0, 0)
        start_rows(t0, nt, 0)

    @pl.when(nt > 0)
    def _work():
        def chunk(j, carry):
            slot = j % 2

            @pl.when(j + 1 < nj)
            def _():
                start_weights(we_ref[w], j + 1, 1 - slot)

            @pl.when(jnp.logical_and(j + 1 == nj, has_next))
            def _():
                start_weights(we_ref[nxt], 0, 1 - slot)

            for cp in w_copies(we_ref[w], j, slot):
                cp.wait()
            stage = (wg_st.at[slot], wu_st.at[slot], wo_st.at[slot])
            biases = (bin_ref[j], bin_ref[nj + j])

            @pl.when(j == 0)
            def _rows_ready():
                def finish(r, c):
                    x_copy(t0, r, xslot).wait()
                    acc_s[local_rows(r), :] = jnp.zeros((tmx, acc_s.shape[1]), F32)
                    return c
                lax.fori_loop(0, nt, finish, 0)

                @pl.when(has_next)
                def _():
                    start_rows(wt_ref[nxt], wn_ref[nxt], 1 - xslot)

            _moe_passes(xu_s.at[xslot], acc_s, stage, biases, nt, local_rows, half)
            return carry

        lax.fori_loop(0, nj, chunk, 0)

        def y_wait(slot):
            y_copy(0, slot).wait()

        prev_nt = jnp.where(w > 0, wn_ref[jnp.maximum(w - 1, 0)], 0)

        @pl.when(prev_nt >= 1)
        def _():
            y_wait((prev_nt - 1) % 2)

        @pl.when(prev_nt >= 2)
        def _():
            y_wait(prev_nt % 2)

        def emit(r, c):
            slot = r % 2

            @pl.when(r >= 2)
            def _():
                y_wait(slot)

            y = acc_s[local_rows(r), :] + bo_ref[...]
            yst_s[slot] = pltpu.pack_elementwise([y[:, :half], y[:, half:]],
                                                 packed_dtype=BF16).reshape(yst_s.shape[1:])
            y_copy(r, slot).start()
            return c

        lax.fori_loop(0, nt, emit, 0)

        @pl.when(jnp.logical_not(has_next))
        def _drain():
            @pl.when(nt >= 2)
            def _():
                y_wait(nt % 2)

            y_wait((nt - 1) % 2)


def _moe_passes(xu_ref, acc_s, stage, biases, nt, local_rows, half):
    wg_ref, wu_ref, wo_ref = stage
    bg, bu = biases

    def proj(x_lo, x_hi, w_ref, b):
        return (jnp.dot(x_lo, w_ref[0:half, :].astype(BF16), preferred_element_type=F32)
                + jnp.dot(x_hi, w_ref[half:2 * half, :].astype(BF16), preferred_element_type=F32) + b)

    def rows_step(rows):
        xw = xu_ref[rows]
        xw = xw.reshape(xw.shape[0], half)
        x_lo = pltpu.unpack_elementwise(xw, index=0, packed_dtype=BF16,
                                        unpacked_dtype=F32).astype(BF16)
        x_hi = pltpu.unpack_elementwise(xw, index=1, packed_dtype=BF16,
                                        unpacked_dtype=F32).astype(BF16)
        gg = jnp.minimum(proj(x_lo, x_hi, wg_ref, bg), SWIGLU_LIMIT)
        uu = jnp.clip(proj(x_lo, x_hi, wu_ref, bu), -SWIGLU_LIMIT, SWIGLU_LIMIT)
        act = (uu + 1.0) * (gg * jax.nn.sigmoid(SWIGLU_ALPHA * gg))
        acc_s[rows, :] += jnp.dot(act.astype(BF16), wo_ref[...].astype(BF16),
                                  preferred_element_type=F32)

    def big(pi, c):
        rows_step(local_rows(PASS_TILES * pi, PASS_TILES))
        return c

    lax.fori_loop(0, nt // PASS_TILES, big, 0)
    piece = PASS_TILES // 2
    while piece >= 1:
        @pl.when((nt & piece) != 0)
        def _(piece=piece):
            rows_step(local_rows(nt & ~(2 * piece - 1), piece))
        piece //= 2


def _moe_call(work_e, work_t0, work_nt, xs, w_in, b_in, w_out, b_out,
              tmx=ROW_TILE, group=GROUP_TILES, fc=F_CHUNK):
    ne, d, de2 = w_in.shape
    de = de2 // 2
    nj = de // fc
    assert nj % 2 == 0, "weight stage slots alternate per chunk and restart at 0 per work item"
    n_work = work_e.shape[0]
    n_slots = xs.shape[0]
    row = xs.shape[1:]

    grid_spec = pltpu.PrefetchScalarGridSpec(
        num_scalar_prefetch=3,
        grid=(n_work,),
        in_specs=[
            pl.BlockSpec(memory_space=pl.ANY),
            pl.BlockSpec(memory_space=pl.ANY),
            pl.BlockSpec((None, 2 * nj, 1, fc), lambda w, we, wt, wn: (we[w], 0, 0, 0)),
            pl.BlockSpec(memory_space=pl.ANY),
            pl.BlockSpec((None, 1, d), lambda w, we, wt, wn: (we[w], 0, 0)),
        ],
        out_specs=pl.BlockSpec(memory_space=pl.ANY),
        scratch_shapes=[pltpu.VMEM((2, group * tmx, *row), U32),
                        pltpu.VMEM((group * tmx, d), F32),
                        pltpu.VMEM((2, d, fc), F32),
                        pltpu.VMEM((2, d, fc), F32),
                        pltpu.VMEM((2, fc, d), F32),
                        pltpu.VMEM((2, tmx, *row), U32),
                        pltpu.SemaphoreType.DMA((2,)),
                        pltpu.SemaphoreType.DMA((2,)),
                        pltpu.SemaphoreType.DMA((2,))],
    )
    return pl.pallas_call(
        functools.partial(_moe_kernel, tmx=tmx, fc=fc, nj=nj),
        grid_spec=grid_spec,
        out_shape=jax.ShapeDtypeStruct((n_slots, *row), U32),
        compiler_params=pltpu.CompilerParams(dimension_semantics=("arbitrary",),
                                             vmem_limit_bytes=VMEM_LIMIT, has_side_effects=True),
        name="moe_experts",
    )(work_e, work_t0, work_nt, xs, w_in, b_in.reshape(ne, 2 * nj, 1, fc), w_out,
      b_out.reshape(ne, 1, d))


def _combine_kernel(pos_ref, pos_next_ref, x1_ref, gate_ref, ys_ref, o_ref, ybuf, sem, *, tb):
    half = ybuf.shape[-2] * ybuf.shape[-1]
    i = pl.program_id(0)
    slot = i % 2

    def gather(p_ref, s):
        def issue(tl, c):
            for k in range(TOP_K):
                pltpu.make_async_copy(ys_ref.at[p_ref[k, tl]], ybuf.at[s, k, tl],
                                      sem.at[s]).start(priority=k % 2)
            return c
        lax.fori_loop(0, tb, issue, 0, unroll=4)

    @pl.when(i == 0)
    def _():
        gather(pos_ref, 0)

    @pl.when(i + 1 < pl.num_programs(0))
    def _():
        gather(pos_next_ref, 1 - slot)

    for k in range(TOP_K):
        pltpu.make_async_copy(ys_ref.at[pl.ds(0, tb)], ybuf.at[slot, k], sem.at[slot]).wait()

    lo = x1_ref[:, 0:half]
    hi = x1_ref[:, half:2 * half]
    for k in range(TOP_K):
        g = gate_ref[:, k:k + 1]
        yw = ybuf[slot, k].reshape(tb, half)
        lo = lo + g * pltpu.unpack_elementwise(yw, index=0, packed_dtype=BF16, unpacked_dtype=F32)
        hi = hi + g * pltpu.unpack_elementwise(yw, index=1, packed_dtype=BF16, unpacked_dtype=F32)
    o_ref[:, 0:half] = lo
    o_ref[:, half:2 * half] = hi


def _combine_call(pos, x1, gate_tk, ys, tb=256):
    t, d = x1.shape
    row = ys.shape[1:]
    last = t // tb - 1
    return pl.pallas_call(
        functools.partial(_combine_kernel, tb=tb),
        grid=(t // tb,),
        in_specs=[pl.BlockSpec((TOP_K, tb), lambda i: (0, i), memory_space=pltpu.SMEM),
                  pl.BlockSpec((TOP_K, tb), lambda i: (0, jnp.minimum(i + 1, last)),
                               memory_space=pltpu.SMEM),
                  pl.BlockSpec((tb, d), lambda i: (i, 0)),
                  pl.BlockSpec((tb, TOP_K), lambda i: (i, 0)),
                  pl.BlockSpec(memory_space=pl.ANY)],
        out_specs=pl.BlockSpec((tb, d), lambda i: (i, 0)),
        out_shape=jax.ShapeDtypeStruct((t, d), F32),
        scratch_shapes=[pltpu.VMEM((2, TOP_K, tb, *row), U32),
                        pltpu.SemaphoreType.DMA((2,))],
        compiler_params=_cparams(("arbitrary",)),
        name="combine",
    )(pos, pos, x1, gate_tk, ys)


def _work_list(ntile, otile, n_work, group):
    ne = ntile.shape[0]
    items = (ntile + group - 1) // group
    ends = jnp.cumsum(items)
    total = ends[-1]
    w = jnp.arange(n_work, dtype=I32)
    valid = w < total
    e_w = jnp.clip(jnp.sum((ends[None, :] <= w[:, None]).astype(I32), axis=1), 0, ne - 1)
    local = w - (ends - items)[e_w]
    t0_w = otile[e_w] + local * group
    nt_w = jnp.where(valid, jnp.clip(ntile[e_w] - local * group, 0, group), 0)
    e_last = e_w[jnp.maximum(total - 1, 0)]
    return (jnp.where(valid, e_w, e_last).astype(I32), jnp.where(valid, t0_w, 0).astype(I32),
            nt_w.astype(I32))


def _rotary_tables(seq, dk):
    inv_freq = ROPE_BASE ** (-jnp.arange(0, dk, 2, dtype=F32) / dk)
    ang = jnp.arange(seq, dtype=I32).astype(F32)[:, None] * inv_freq[None, :]
    cos = jnp.repeat(jnp.cos(ang), 2, axis=1)
    sin = jnp.stack([-jnp.sin(ang), jnp.sin(ang)], axis=-1).reshape(seq, dk)
    return cos, sin


def kernel(x, norm_mix_g, w_mix_in, ret_gn_g, q_norm_g, k_norm_g, lambda_q1, lambda_k1, lambda_q2, lambda_k2, diff_subln_g, rel_bias_table, w_mix_out, norm_ffn_g, w_router, b_router, w_exp_in, b_exp_in, w_exp_out, b_exp_out):
    batch, seq, d = x.shape
    t = batch * seq
    depth = norm_mix_g.shape[0]
    ne = w_router.shape[-1]
    n_tiles_max = (t * TOP_K) // ROW_TILE + ne
    n_work = ne + (n_tiles_max - ne) // GROUP_TILES
    cos, sin = _rotary_tables(seq, HEAD_W)
    log_gamma = jnp.log1p(-(2.0 ** (-5.0 - jnp.arange(RET_HEADS, dtype=F32))))

    x2 = x.reshape(t, d)
    for l in range(depth):
        proj = _inproj_call(x2, norm_mix_g[l], w_mix_in[l])
        y_ret = _ret_call(proj, cos, sin, log_gamma, ret_gn_g[l], batch, seq)
        lam_vecs = jnp.stack([lambda_q1[l], lambda_k1[l], lambda_q2[l], lambda_k2[l]])
        y_diff = _diff_call(proj, rel_bias_table, q_norm_g[l], k_norm_g[l], lam_vecs,
                            diff_subln_g[l], batch, seq)
        x1, h2p, logits_t = _outproj_call(y_ret, y_diff, x2, w_mix_out[l], norm_ffn_g[l],
                                          w_router[l], b_router[l])
        pos, gate, cnt, ntile, otile = _route_call(logits_t)
        cnt, ntile, otile = cnt[:, 0], ntile[:, 0], otile[:, 0]
        work_e, work_t0, work_nt = _work_list(ntile, otile, n_work, GROUP_TILES)
        xs = _dispatch_call(pos, otile * ROW_TILE + cnt, ntile * ROW_TILE - cnt, h2p,
                            n_tiles_max * ROW_TILE)
        ys = _moe_call(work_e, work_t0, work_nt, xs, w_exp_in[l], b_exp_in[l], w_exp_out[l],
                       b_exp_out[l])
        x2 = _combine_call(pos, x1, gate.T, ys)
    return x2.reshape(batch, seq, d)
```

```python
import functools
import math

import jax
import jax.numpy as jnp
from jax import lax
from jax.experimental import pallas as pl
from jax.experimental.pallas import tpu as pltpu

F32 = jnp.float32
BF16 = jnp.bfloat16
I32 = jnp.int32
U32 = jnp.uint32

EPS = 1e-6
CHUNK = 64
RET_HEADS = 8
DIFF_HEADS = 8
HEAD_W = 128
DIFF_DK = 64
ROPE_BASE = 10000.0
NUM_BUCKETS = 32
MAX_DISTANCE = 128
TOP_K = 4
SWIGLU_LIMIT = 7.0
SWIGLU_ALPHA = 1.702
LAM_INIT = 0.8 - 0.6 * math.exp(-0.3 * 0)

LANES = 128
NEG_BIG = -1e30
VMEM_LIMIT = 60 * 1024 * 1024

SEQ_BLK = 256
ROW_TILE = 256
GROUP_TILES = 6
PASS_TILES = 4
F_CHUNK = 512
CAST_ROWS = 256


def _cparams(sem, vmem=VMEM_LIMIT):
    return pltpu.CompilerParams(dimension_semantics=sem, vmem_limit_bytes=vmem)


def _inproj_kernel(x_ref, g_ref, w_ref, o_ref, wb_ref, *, rows_per_cast, sub_blocks=2):
    @pl.when(pl.program_id(1) == 0)
    def _():
        d = w_ref.shape[0]
        for c in range(d // rows_per_cast):
            sl = slice(c * rows_per_cast, (c + 1) * rows_per_cast)
            wb_ref[sl, :] = w_ref[sl, :].astype(BF16)

    sub = x_ref.shape[0] // sub_blocks
    for sb in range(sub_blocks):
        rs = slice(sb * sub, (sb + 1) * sub)
        x = x_ref[rs, :]
        ms = jnp.mean(x * x, axis=-1, keepdims=True)
        h = (x * lax.rsqrt(ms + EPS) * g_ref[...]).astype(BF16)
        acc = jnp.dot(h, wb_ref[...], preferred_element_type=F32)
        for j in range(o_ref.shape[0]):
            o_ref[j, rs, :] = acc[:, j * LANES:(j + 1) * LANES].astype(o_ref.dtype)


def _inproj_call(x2, g, w, tm=1024, tn=1024):
    t, d = x2.shape
    n = w.shape[1]
    return pl.pallas_call(
        functools.partial(_inproj_kernel, rows_per_cast=min(CAST_ROWS, d)),
        grid=(n // tn, t // tm),
        in_specs=[pl.BlockSpec((tm, d), lambda j, i: (i, 0)),
                  pl.BlockSpec((1, d), lambda j, i: (0, 0)),
                  pl.BlockSpec((d, tn), lambda j, i: (0, j))],
        out_specs=pl.BlockSpec((tn // LANES, tm, LANES), lambda j, i: (j, i, 0)),
        out_shape=jax.ShapeDtypeStruct((n // LANES, t, LANES), BF16),
        scratch_shapes=[pltpu.VMEM((d, tn), BF16)],
        compiler_params=_cparams(("arbitrary", "arbitrary")),
        name="in_proj",
    )(x2, g.reshape(1, d), w)


def _ret_kernel(lg_ref, q_ref, k_ref, v_ref, g_ref, cos_ref, sin_ref, gn_ref, o_ref, *, blk, nblk):
    dk = q_ref.shape[-1]
    lg = lg_ref[pl.program_id(1)]
    row = lax.broadcasted_iota(I32, (blk, blk), 0)
    col = lax.broadcasted_iota(I32, (blk, blk), 1)
    dist = jnp.abs(row - col).astype(F32)
    visible = (col // CHUNK) <= (row // CHUNK)
    dmask = jnp.where(visible, jnp.exp(lg * dist), 0.0)
    rr = lax.broadcasted_iota(I32, (blk, dk), 0).astype(F32)
    qdec = jnp.exp(lg * (rr + 1.0))
    kdec = jnp.exp(lg * (blk - 1.0 - rr))
    bdec = jnp.exp(lg * jnp.full((1, HEAD_W), float(blk), F32))
    even = (lax.broadcasted_iota(I32, (blk, dk), 1) & 1) == 0
    scale = dk ** -0.5

    def body(i, state):
        rows = pl.ds(pl.multiple_of(i * blk, blk), blk)
        cos = cos_ref[rows, :]
        sin = sin_ref[rows, :]

        def rot(x):
            partner = jnp.where(even, pltpu.roll(x, dk - 1, 1), pltpu.roll(x, 1, 1))
            return x * cos + partner * sin

        qr = rot(q_ref[rows, :].astype(F32)) * scale
        kr = rot(k_ref[rows, :].astype(F32))
        v = v_ref[rows, :]
        s = lax.dot_general(qr.astype(BF16), kr.astype(BF16), (((1,), (1,)), ((), ())),
                            preferred_element_type=F32) * dmask
        out = jnp.dot(s.astype(BF16), v, preferred_element_type=F32)
        out = out + jnp.dot((qr * qdec).astype(BF16), state.astype(BF16),
                            preferred_element_type=F32)
        kv = lax.dot_general((kr * kdec).astype(BF16), v, (((0,), (0,)), ((), ())),
                             preferred_element_type=F32)
        state = state * bdec + kv
        ms = jnp.mean(out * out, axis=-1, keepdims=True)
        normed = out * lax.rsqrt(ms + EPS) * gn_ref[...]
        g = g_ref[rows, :].astype(F32)
        o_ref[rows, :] = (g * jax.nn.sigmoid(g) * normed).astype(o_ref.dtype)
        return state

    lax.fori_loop(0, nblk, body, jnp.zeros((dk, HEAD_W), F32), unroll=True)


def _ret_call(proj, cos, sin, log_gamma, gn_g, batch, seq, blk=SEQ_BLK):
    nh = RET_HEADS

    def head_spec(base):
        return pl.BlockSpec((None, seq, LANES), lambda b, h: (base + h, b, 0))

    return pl.pallas_call(
        functools.partial(_ret_kernel, blk=blk, nblk=seq // blk),
        grid=(batch, nh),
        in_specs=[pl.BlockSpec(memory_space=pltpu.SMEM),
                  head_spec(0), head_spec(nh), head_spec(2 * nh), head_spec(3 * nh),
                  pl.BlockSpec((seq, LANES), lambda b, h: (0, 0)),
                  pl.BlockSpec((seq, LANES), lambda b, h: (0, 0)),
                  pl.BlockSpec((None, 1, LANES), lambda b, h: (h, 0, 0))],
        out_specs=pl.BlockSpec((None, seq, LANES), lambda b, h: (h, b, 0)),
        out_shape=jax.ShapeDtypeStruct((nh, batch * seq, LANES), BF16),
        compiler_params=_cparams(("arbitrary", "arbitrary")),
        name="retention",
    )(log_gamma, proj, proj, proj, proj, cos, sin, gn_g.reshape(nh, 1, LANES))


def _diff_kernel(tbl_ref, q_ref, k_ref, v_ref, qg_ref, kg_ref, lam_ref, sg_ref, bidx_ref, o_ref,
                 qz_s, kn_s, bias_s, s_s, p_s, *, blk, nblk):
    h = pl.program_id(0)
    b = pl.program_id(1)
    far_bucket = NUM_BUCKETS // 2 - 1

    @pl.when(b == 0)
    def _build_bias():
        row = lax.broadcasted_iota(I32, (blk, blk), 0)
        col = lax.broadcasted_iota(I32, (blk, blk), 1)
        visible = (col // CHUNK) <= (row // CHUNK)
        for d in range(2):
            idx = bidx_ref[d]
            bias = jnp.zeros((blk, blk), F32)
            for bucket in range(NUM_BUCKETS):
                bias = jnp.where(idx == bucket, tbl_ref[bucket, h], bias)
            if d == 0:
                bias = jnp.where(visible, bias, NEG_BIG)
            bias_s[d, 0:blk, :] = bias
            bias_s[d, blk:2 * blk, :] = bias

    lo = lax.broadcasted_iota(I32, (blk, HEAD_W), 1) < DIFF_DK
    scale = DIFF_DK ** -0.5
    same_half = ((lax.broadcasted_iota(I32, (HEAD_W, HEAD_W), 0) // DIFF_DK)
                 == (lax.broadcasted_iota(I32, (HEAD_W, HEAD_W), 1) // DIFF_DK)).astype(BF16)

    def half_norm(x, g):
        ms = jnp.dot((x * x).astype(BF16), same_half, preferred_element_type=F32) * (1.0 / DIFF_DK)
        return x * lax.rsqrt(ms + EPS) * g

    def prep(i, carry):
        rows = pl.ds(pl.multiple_of(i * blk, blk), blk)
        qn = half_norm(q_ref[rows, :].astype(F32), qg_ref[...]) * scale
        kn = half_norm(k_ref[rows, :].astype(F32), kg_ref[...])
        base = pl.multiple_of(i * 2 * blk, 2 * blk)
        qz_s[pl.ds(base, blk), :] = jnp.where(lo, qn, 0.0).astype(BF16)
        qz_s[pl.ds(base + blk, blk), :] = jnp.where(lo, 0.0, qn).astype(BF16)
        kn_s[rows, :] = kn.astype(BF16)
        return carry

    lax.fori_loop(0, nblk, prep, 0)

    lam = (jnp.exp(jnp.sum(lam_ref[0:1, :] * lam_ref[1:2, :], axis=-1, keepdims=True))
           - jnp.exp(jnp.sum(lam_ref[2:3, :] * lam_ref[3:4, :], axis=-1, keepdims=True))
           + LAM_INIT)
    c_far = tbl_ref[far_bucket, h]

    def lane_fold(x, op):
        out = x[:, 0:LANES]
        for c in range(1, blk // LANES):
            out = op(out, x[:, c * LANES:(c + 1) * LANES])
        return out

    for i in range(nblk):
        qz = qz_s[i * 2 * blk:(i + 1) * 2 * blk, :]
        m_t = jnp.full((2 * blk, LANES), NEG_BIG, F32)
        for j in range(i + 1):
            keys = slice(j * blk, (j + 1) * blk)
            s = lax.dot_general(qz, kn_s[keys, :], (((1,), (1,)), ((), ())),
                                preferred_element_type=F32)
            s = s + (bias_s[0] if j == i else bias_s[1] if j == i - 1 else c_far)
            s_s[:, keys] = s
            m_t = jnp.maximum(m_t, lane_fold(s, jnp.maximum))
        m = jnp.max(m_t, axis=-1, keepdims=True)
        l_t = jnp.zeros((2 * blk, LANES), F32)
        for j in range(i + 1):
            keys = slice(j * blk, (j + 1) * blk)
            p = jnp.exp(s_s[:, keys] - m)
            l_t = l_t + lane_fold(p, jnp.add)
            p_s[:, keys] = p.astype(BF16)
        l = jnp.sum(l_t, axis=-1, keepdims=True)
        kend = (i + 1) * blk
        o = jnp.dot(p_s[:, 0:kend], v_ref[0:kend, :], preferred_element_type=F32) / l
        att = o[0:blk, :] - lam * o[blk:2 * blk, :]
        ms = jnp.mean(att * att, axis=-1, keepdims=True)
        y = att * lax.rsqrt(ms + EPS) * sg_ref[...] * (1.0 - LAM_INIT)
        o_ref[i * blk:(i + 1) * blk, :] = y.astype(o_ref.dtype)


def _t5_bucket(rel):
    nb = NUM_BUCKETS // 2
    max_exact = nb // 2
    base = jnp.where(rel > 0, nb, 0)
    n = jnp.abs(rel)
    large = max_exact + (jnp.log(jnp.maximum(n, 1).astype(jnp.float32) / max_exact)
                         / math.log(MAX_DISTANCE / max_exact) * (nb - max_exact)).astype(jnp.int32)
    large = jnp.minimum(large, nb - 1)
    return base + jnp.where(n < max_exact, n, large)


def _diff_call(proj, rel_table, qg, kg, lam_vecs, sg, batch, seq, blk=SEQ_BLK):
    nh = DIFF_HEADS
    first = 4 * RET_HEADS
    r = jnp.arange(blk, dtype=I32)
    rel0 = r[None, :] - r[:, None]
    bidx = jnp.stack([_t5_bucket(rel0), _t5_bucket(rel0 - blk)]).astype(I32) & (NUM_BUCKETS - 1)

    def head_spec(base):
        return pl.BlockSpec((None, seq, LANES), lambda h, b: (base + h, b, 0))

    def vec_spec():
        return pl.BlockSpec((1, LANES), lambda h, b: (0, 0))

    return pl.pallas_call(
        functools.partial(_diff_kernel, blk=blk, nblk=seq // blk),
        grid=(nh, batch),
        in_specs=[pl.BlockSpec(memory_space=pltpu.SMEM),
                  head_spec(first), head_spec(first + nh), head_spec(first + 2 * nh),
                  vec_spec(), vec_spec(),
                  pl.BlockSpec((4, DIFF_DK), lambda h, b: (0, 0)),
                  vec_spec(),
                  pl.BlockSpec((2, blk, blk), lambda h, b: (0, 0, 0))],
        out_specs=pl.BlockSpec((None, seq, LANES), lambda h, b: (h, b, 0)),
        out_shape=jax.ShapeDtypeStruct((nh, batch * seq, LANES), BF16),
        scratch_shapes=[pltpu.VMEM((2 * seq, LANES), BF16),
                        pltpu.VMEM((seq, LANES), BF16),
                        pltpu.VMEM((2, 2 * blk, blk), F32),
                        pltpu.VMEM((2 * blk, seq), F32),
                        pltpu.VMEM((2 * blk, seq), BF16)],
        compiler_params=_cparams(("arbitrary", "arbitrary")),
        name="diff_attn",
    )(rel_table, proj, proj, proj,
      jnp.tile(qg, 2).reshape(1, LANES), jnp.tile(kg, 2).reshape(1, LANES),
      lam_vecs, sg.reshape(1, LANES), bidx)


def _outproj_kernel(yr_ref, yd_ref, x_ref, w_hbm, g_ref, wr_ref, br_ref,
                    x1_ref, h2p_ref, lt_ref, wb_s, wstage_s, wr2_s, wsem,
                    *, rows_per_cast, sub_blocks=2):
    ne = lt_ref.shape[0]
    nchunk = wb_s.shape[0] // rows_per_cast

    def w_copy(c):
        return pltpu.make_async_copy(w_hbm.at[pl.ds(c * rows_per_cast, rows_per_cast)],
                                     wstage_s.at[c % 2], wsem.at[c % 2])

    @pl.when(pl.program_id(0) == 0)
    def _stage_weights():
        w_copy(0).start()
        for c in range(nchunk):
            if c + 1 < nchunk:
                w_copy(c + 1).start()
            w_copy(c).wait()
            wb_s[c * rows_per_cast:(c + 1) * rows_per_cast, :] = wstage_s[c % 2].astype(BF16)
        wr = wr_ref[...]
        wr_hi = wr.astype(BF16)
        wr2_s[:, 0:LANES] = wr_hi
        wr2_s[:, LANES:2 * LANES] = (wr - wr_hi.astype(F32)).astype(BF16)

    tm = x_ref.shape[0]
    sub = tm // sub_blocks
    for sb in range(sub_blocks):
        rs = slice(sb * sub, (sb + 1) * sub)
        y = jnp.concatenate([yr_ref[j, rs, :] for j in range(yr_ref.shape[0])]
                            + [yd_ref[j, rs, :] for j in range(yd_ref.shape[0])], axis=-1)
        x1 = x_ref[rs, :] + jnp.dot(y, wb_s[...], preferred_element_type=F32)
        x1_ref[rs, :] = x1
        ms = jnp.mean(x1 * x1, axis=-1, keepdims=True)
        h2 = x1 * lax.rsqrt(ms + EPS) * g_ref[...]
        h_hi = h2.astype(BF16)
        h_lo = (h2 - h_hi.astype(F32)).astype(BF16)
        parts = (jnp.dot(h_hi, wr2_s[...], preferred_element_type=F32)
                 + jnp.dot(h_lo, wr2_s[...], preferred_element_type=F32))
        logits = parts[:, 0:LANES] + parts[:, LANES:2 * LANES]
        lt_ref[:, rs] = logits.T[0:ne, :] + br_ref[...]
        half = h2.shape[1] // 2
        packed = pltpu.pack_elementwise([h2[:, :half], h2[:, half:]], packed_dtype=BF16)
        h2p_ref[rs] = packed.reshape((sub,) + h2p_ref.shape[1:])


def _outproj_call(y_ret, y_diff, x2, w_out, g, w_router, b_router, tm=512):
    t, d = x2.shape
    ne = w_router.shape[1]
    nhr, nhd = y_ret.shape[0], y_diff.shape[0]
    rows_per_cast = min(CAST_ROWS, d)
    wr_pad = jnp.pad(w_router, ((0, 0), (0, LANES - ne)))
    return pl.pallas_call(
        functools.partial(_outproj_kernel, rows_per_cast=rows_per_cast),
        grid=(t // tm,),
        in_specs=[pl.BlockSpec((nhr, tm, LANES), lambda i: (0, i, 0)),
                  pl.BlockSpec((nhd, tm, LANES), lambda i: (0, i, 0)),
                  pl.BlockSpec((tm, d), lambda i: (i, 0)),
                  pl.BlockSpec(memory_space=pl.ANY),
                  pl.BlockSpec((1, d), lambda i: (0, 0)),
                  pl.BlockSpec((d, LANES), lambda i: (0, 0)),
                  pl.BlockSpec((ne, 1), lambda i: (0, 0))],
        out_specs=[pl.BlockSpec((tm, d), lambda i: (i, 0)),
                   pl.BlockSpec((tm, d // 2 // LANES, LANES), lambda i: (i, 0, 0)),
                   pl.BlockSpec((ne, tm), lambda i: (0, i))],
        out_shape=[jax.ShapeDtypeStruct((t, d), F32),
                   jax.ShapeDtypeStruct((t, d // 2 // LANES, LANES), U32),
                   jax.ShapeDtypeStruct((ne, t), F32)],
        scratch_shapes=[pltpu.VMEM((d, d), BF16),
                        pltpu.VMEM((2, rows_per_cast, d), F32),
                        pltpu.VMEM((d, 2 * LANES), BF16),
                        pltpu.SemaphoreType.DMA((2,))],
        compiler_params=_cparams(("arbitrary",)),
        name="out_proj_router",
    )(y_ret, y_diff, x2, w_out, g.reshape(1, d), wr_pad, b_router.reshape(ne, 1))


def _route_kernel(lt_ref, pos_ref, gate_ref, cnt_ref, nt_ref, ot_ref, idx_s, rank_s, *, tb, row_tile):
    ne, t = lt_ref.shape
    e_iota = lax.broadcasted_iota(I32, (ne, tb), 0)
    upper = (lax.broadcasted_iota(I32, (tb, tb), 0)
             < lax.broadcasted_iota(I32, (tb, tb), 1)).astype(BF16)

    def pass_a(i, running):
        cols = pl.ds(pl.multiple_of(i * tb, tb), tb)
        l = lt_ref[:, cols]
        tops, hots = [], []
        for k in range(TOP_K):
            m = jnp.max(l, axis=0, keepdims=True)
            idx = jnp.min(jnp.where(l == m, e_iota, ne), axis=0, keepdims=True)
            hot = e_iota == idx
            l = jnp.where(hot, -jnp.inf, l)
            idx_s[k:k + 1, cols] = idx
            tops.append(m)
            hots.append(hot)
        exps = [jnp.exp(m - tops[0]) for m in tops]
        denom = exps[0] + exps[1] + exps[2] + exps[3]
        for k in range(TOP_K):
            gate_ref[k:k + 1, cols] = exps[k] / denom
        hot_all = jnp.zeros((ne, tb), F32)
        for hot in hots:
            hot_all = hot_all + hot.astype(F32)
        before = running + jnp.dot(hot_all.astype(BF16), upper, preferred_element_type=F32)
        for k in range(TOP_K):
            rank_s[k:k + 1, cols] = jnp.sum(jnp.where(hots[k], before, 0.0), axis=0, keepdims=True)
        return running + jnp.sum(hot_all, axis=1, keepdims=True)

    cnt = lax.fori_loop(0, t // tb, pass_a, jnp.zeros((ne, 1), F32))
    ntile = jnp.floor((cnt + (row_tile - 1.0)) * (1.0 / row_tile))
    lower = (lax.broadcasted_iota(I32, (ne, ne), 1)
             < lax.broadcasted_iota(I32, (ne, ne), 0)).astype(BF16)
    otile = jnp.dot(lower, jnp.broadcast_to(ntile, (ne, LANES)).astype(BF16),
                    preferred_element_type=F32)
    cnt_ref[...] = jnp.broadcast_to(cnt, (ne, LANES)).astype(I32)
    nt_ref[...] = jnp.broadcast_to(ntile, (ne, LANES)).astype(I32)
    ot_ref[...] = otile.astype(I32)
    off_rows = otile[:, 0:1] * float(row_tile)

    def pass_b(i, carry):
        cols = pl.ds(pl.multiple_of(i * tb, tb), tb)
        for k in range(TOP_K):
            hot = e_iota == idx_s[k:k + 1, cols]
            off = jnp.sum(jnp.where(hot, off_rows, 0.0), axis=0, keepdims=True)
            pos_ref[k:k + 1, cols] = (rank_s[k:k + 1, cols] + off).astype(I32)
        return carry

    lax.fori_loop(0, t // tb, pass_b, 0)


def _route_call(logits_t, tb=256, row_tile=ROW_TILE):
    ne, t = logits_t.shape
    return pl.pallas_call(
        functools.partial(_route_kernel, tb=tb, row_tile=row_tile),
        out_shape=[jax.ShapeDtypeStruct((TOP_K, t), I32),
                   jax.ShapeDtypeStruct((TOP_K, t), F32),
                   jax.ShapeDtypeStruct((ne, LANES), I32),
                   jax.ShapeDtypeStruct((ne, LANES), I32),
                   jax.ShapeDtypeStruct((ne, LANES), I32)],
        scratch_shapes=[pltpu.VMEM((TOP_K, t), I32), pltpu.VMEM((TOP_K, t), F32)],
        compiler_params=_cparams(None),
        name="route",
    )(logits_t)


_PAD_PIECES = tuple(ROW_TILE >> (k + 1) for k in range(ROW_TILE.bit_length() - 1))


def _dispatch_kernel(pos_ref, pad_start_ref, pad_n_ref, src_ref, dst_ref, stage_s, zero_s,
                     lsem, ssem, zsem, *, tb, ne):
    i = pl.program_id(0)
    nsteps = pl.num_programs(0)
    slot = i % 2

    def load(step, s):
        return pltpu.make_async_copy(src_ref.at[pl.ds(step * tb, tb)], stage_s.at[s], lsem.at[s])

    def wait_rows(s):
        for k in range(TOP_K):
            pltpu.make_async_copy(stage_s.at[s], dst_ref.at[pl.ds(0, tb)], ssem.at[s]).wait()

    def pad_copies(e):
        n = pad_n_ref[e]
        start = pad_start_ref[e]
        out = []
        for piece in _PAD_PIECES:
            at = start + (n & ~(2 * piece - 1))
            out.append(((n & piece) != 0, pltpu.make_async_copy(
                zero_s.at[pl.ds(0, piece)], dst_ref.at[pl.ds(at, piece)], zsem)))
        return out

    @pl.when(i == 0)
    def _zero_pads():
        zero_s[...] = jnp.zeros(zero_s.shape, zero_s.dtype)

        def issue(e, c):
            for cond, cp in pad_copies(e):
                @pl.when(cond)
                def _():
                    cp.start()
            return c

        def drain(e, c):
            for cond, cp in pad_copies(e):
                @pl.when(cond)
                def _():
                    cp.wait()
            return c

        lax.fori_loop(0, ne, issue, 0)
        lax.fori_loop(0, ne, drain, 0)
        load(0, 0).start()

    @pl.when(i > 0)
    def _():
        wait_rows(1 - slot)

    @pl.when(i + 1 < nsteps)
    def _():
        load(i + 1, 1 - slot).start()

    load(i, slot).wait()

    def issue_rows(tl, c):
        for k in range(TOP_K):
            pltpu.make_async_copy(stage_s.at[slot, tl], dst_ref.at[pos_ref[k, tl]],
                                  ssem.at[slot]).start(priority=k % 2)
        return c

    lax.fori_loop(0, tb, issue_rows, 0, unroll=4)

    @pl.when(i == nsteps - 1)
    def _():
        wait_rows(slot)


def _dispatch_call(pos, pad_start, pad_n, h2p, n_slots, tb=512):
    t = h2p.shape[0]
    row = h2p.shape[1:]
    ne = pad_n.shape[0]
    return pl.pallas_call(
        functools.partial(_dispatch_kernel, tb=tb, ne=ne),
        grid=(t // tb,),
        in_specs=[pl.BlockSpec((TOP_K, tb), lambda i: (0, i), memory_space=pltpu.SMEM),
                  pl.BlockSpec(memory_space=pltpu.SMEM),
                  pl.BlockSpec(memory_space=pltpu.SMEM),
                  pl.BlockSpec(memory_space=pl.ANY)],
        out_specs=pl.BlockSpec(memory_space=pl.ANY),
        out_shape=jax.ShapeDtypeStruct((n_slots, *row), U32),
        scratch_shapes=[pltpu.VMEM((2, tb, *row), U32),
                        pltpu.VMEM((_PAD_PIECES[0], *row), U32),
                        pltpu.SemaphoreType.DMA((2,)),
                        pltpu.SemaphoreType.DMA((2,)),
                        pltpu.SemaphoreType.DMA(())],
        compiler_params=pltpu.CompilerParams(dimension_semantics=("arbitrary",),
                                             vmem_limit_bytes=VMEM_LIMIT, has_side_effects=True),
        name="dispatch",
    )(pos, pad_start, pad_n, h2p)


def _moe_kernel(we_ref, wt_ref, wn_ref, xs_ref, win_ref, bin_ref, wout_ref, bo_ref,
                ys_ref, xu_s, acc_s, wg_st, wu_st, wo_st, yst_s, xsem, wsem, ysem,
                *, tmx, fc, nj):
    w = pl.program_id(0)
    nw = pl.num_programs(0)
    nt = wn_ref[w]
    t0 = wt_ref[w]
    half = xu_s.shape[-2] * xu_s.shape[-1]
    de = nj * fc
    nxt = jnp.minimum(w + 1, nw - 1)
    has_next = jnp.logical_and(w + 1 < nw, wn_ref[nxt] > 0)
    xslot = w % 2

    def local_rows(r, n=1):
        return pl.ds(pl.multiple_of(r * tmx, tmx), n * tmx)

    def x_copy(item_t0, r, slot):
        src = pl.ds(pl.multiple_of((item_t0 + r) * tmx, tmx), tmx)
        return pltpu.make_async_copy(xs_ref.at[src], xu_s.at[slot, local_rows(r)], xsem.at[slot])

    def y_copy(r, slot):
        dst = pl.ds(pl.multiple_of((t0 + r) * tmx, tmx), tmx)
        return pltpu.make_async_copy(yst_s.at[slot], ys_ref.at[dst], ysem.at[slot])

    def w_copies(expert, j, slot):
        c0 = pl.multiple_of(j * fc, fc)
        return (pltpu.make_async_copy(win_ref.at[expert, :, pl.ds(c0, fc)], wg_st.at[slot], wsem.at[slot]),
                pltpu.make_async_copy(win_ref.at[expert, :, pl.ds(de + c0, fc)], wu_st.at[slot],
                                      wsem.at[slot]),
                pltpu.make_async_copy(wout_ref.at[expert, pl.ds(c0, fc), :], wo_st.at[slot],
                                      wsem.at[slot]))

    def start_weights(expert, j, slot):
        for cp in w_copies(expert, j, slot):
            cp.start()

    def start_rows(item_t0, item_nt, slot):
        def go(r, c):
            x_copy(item_t0, r, slot).start()
            return c
        lax.fori_loop(0, item_nt, go, 0)

    @pl.when(w == 0)
    def _prologue():
        start_weights(we_ref[0], 0, 0)
        start_rows(t0, nt, 0)

    @pl.when(nt > 0)
    def _work():
        def chunk(j, carry):
            slot = j % 2

            @pl.when(j + 1 < nj)
            def _():
                start_weights(we_ref[w], j + 1, 1 - slot)

            @pl.when(jnp.logical_and(j + 1 == nj, has_next))
            def _():
                start_weights(we_ref[nxt], 0, 1 - slot)

            for cp in w_copies(we_ref[w], j, slot):
                cp.wait()
            stage = (wg_st.at[slot], wu_st.at[slot], wo_st.at[slot])
            biases = (bin_ref[j], bin_ref[nj + j])

            @pl.when(j == 0)
            def _rows_ready():
                def finish(r, c):
                    x_copy(t0, r, xslot).wait()
                    return c
                lax.fori_loop(0, nt, finish, 0)

                @pl.when(has_next)
                def _():
                    start_rows(wt_ref[nxt], wn_ref[nxt], 1 - xslot)

            @pl.when(j == 0)
            def _():
                _moe_passes(xu_s.at[xslot], acc_s, stage, biases, nt, local_rows, half, True)

            @pl.when(j > 0)
            def _():
                _moe_passes(xu_s.at[xslot], acc_s, stage, biases, nt, local_rows, half, False)

            return carry

        lax.fori_loop(0, nj, chunk, 0)

        def y_wait(slot):
            y_copy(0, slot).wait()

        prev_nt = jnp.where(w > 0, wn_ref[jnp.maximum(w - 1, 0)], 0)

        @pl.when(prev_nt >= 1)
        def _():
            y_wait((prev_nt - 1) % 2)

        @pl.when(prev_nt >= 2)
        def _():
            y_wait(prev_nt % 2)

        def emit(r, c):
            slot = r % 2

            @pl.when(r >= 2)
            def _():
                y_wait(slot)

            y = acc_s[local_rows(r), :] + bo_ref[...]
            yst_s[slot] = pltpu.pack_elementwise([y[:, :half], y[:, half:]],
                                                 packed_dtype=BF16).reshape(yst_s.shape[1:])
            y_copy(r, slot).start()
            return c

        lax.fori_loop(0, nt, emit, 0)

        @pl.when(jnp.logical_not(has_next))
        def _drain():
            @pl.when(nt >= 2)
            def _():
                y_wait(nt % 2)

            y_wait((nt - 1) % 2)


def _moe_passes(xu_ref, acc_s, stage, biases, nt, local_rows, half, assign):
    wg_ref, wu_ref, wo_ref = stage
    bg, bu = biases

    def proj(x_lo, x_hi, w_ref, b):
        return (jnp.dot(x_lo, w_ref[0:half, :].astype(BF16), preferred_element_type=F32)
                + jnp.dot(x_hi, w_ref[half:2 * half, :].astype(BF16), preferred_element_type=F32) + b)

    def rows_step(rows):
        xw = xu_ref[rows]
        xw = xw.reshape(xw.shape[0], half)
        x_lo = pltpu.unpack_elementwise(xw, index=0, packed_dtype=BF16,
                                        unpacked_dtype=F32).astype(BF16)
        x_hi = pltpu.unpack_elementwise(xw, index=1, packed_dtype=BF16,
                                        unpacked_dtype=F32).astype(BF16)
        gg = jnp.minimum(proj(x_lo, x_hi, wg_ref, bg), SWIGLU_LIMIT)
        uu = jnp.clip(proj(x_lo, x_hi, wu_ref, bu), -SWIGLU_LIMIT, SWIGLU_LIMIT)
        act = (uu + 1.0) * (gg * jax.nn.sigmoid(SWIGLU_ALPHA * gg))
        y = jnp.dot(act.astype(BF16), wo_ref[...].astype(BF16), preferred_element_type=F32)
        if assign:
            acc_s[rows, :] = y
        else:
            acc_s[rows, :] += y

    def big(pi, c):
        rows_step(local_rows(PASS_TILES * pi, PASS_TILES))
        return c

    lax.fori_loop(0, nt // PASS_TILES, big, 0)
    piece = PASS_TILES // 2
    while piece >= 1:
        @pl.when((nt & piece) != 0)
        def _(piece=piece):
            rows_step(local_rows(nt & ~(2 * piece - 1), piece))
        piece //= 2


def _moe_call(work_e, work_t0, work_nt, xs, w_in, b_in, w_out, b_out,
              tmx=ROW_TILE, group=GROUP_TILES, fc=F_CHUNK):
    ne, d, de2 = w_in.shape
    de = de2 // 2
    nj = de // fc
    assert nj % 2 == 0, "weight stage slots alternate per chunk and restart at 0 per work item"
    n_work = work_e.shape[0]
    n_slots = xs.shape[0]
    row = xs.shape[1:]

    grid_spec = pltpu.PrefetchScalarGridSpec(
        num_scalar_prefetch=3,
        grid=(n_work,),
        in_specs=[
            pl.BlockSpec(memory_space=pl.ANY),
            pl.BlockSpec(memory_space=pl.ANY),
            pl.BlockSpec((None, 2 * nj, 1, fc), lambda w, we, wt, wn: (we[w], 0, 0, 0)),
            pl.BlockSpec(memory_space=pl.ANY),
            pl.BlockSpec((None, 1, d), lambda w, we, wt, wn: (we[w], 0, 0)),
        ],
        out_specs=pl.BlockSpec(memory_space=pl.ANY),
        scratch_shapes=[pltpu.VMEM((2, group * tmx, *row), U32),
                        pltpu.VMEM((group * tmx, d), F32),
                        pltpu.VMEM((2, d, fc), F32),
                        pltpu.VMEM((2, d, fc), F32),
                        pltpu.VMEM((2, fc, d), F32),
                        pltpu.VMEM((2, tmx, *row), U32),
                        pltpu.SemaphoreType.DMA((2,)),
                        pltpu.SemaphoreType.DMA((2,)),
                        pltpu.SemaphoreType.DMA((2,))],
    )
    return pl.pallas_call(
        functools.partial(_moe_kernel, tmx=tmx, fc=fc, nj=nj),
        grid_spec=grid_spec,
        out_shape=jax.ShapeDtypeStruct((n_slots, *row), U32),
        compiler_params=pltpu.CompilerParams(dimension_semantics=("arbitrary",),
                                             vmem_limit_bytes=VMEM_LIMIT, has_side_effects=True),
        name="moe_experts",
    )(work_e, work_t0, work_nt, xs, w_in, b_in.reshape(ne, 2 * nj, 1, fc), w_out,
      b_out.reshape(ne, 1, d))


def _combine_kernel(pos_ref, pos_next_ref, x1_ref, gate_ref, ys_ref, o_ref, ybuf, sem, *, tb):
    half = ybuf.shape[-2] * ybuf.shape[-1]
    i = pl.program_id(0)
    slot = i % 2

    def gather(p_ref, s):
        def issue(tl, c):
            for k in range(TOP_K):
                pltpu.make_async_copy(ys_ref.at[p_ref[k, tl]], ybuf.at[s, k, tl],
                                      sem.at[s]).start(priority=k % 2)
            return c
        lax.fori_loop(0, tb, issue, 0, unroll=4)

    @pl.when(i == 0)
    def _():
        gather(pos_ref, 0)

    @pl.when(i + 1 < pl.num_programs(0))
    def _():
        gather(pos_next_ref, 1 - slot)

    for k in range(TOP_K):
        pltpu.make_async_copy(ys_ref.at[pl.ds(0, tb)], ybuf.at[slot, k], sem.at[slot]).wait()

    lo = x1_ref[:, 0:half]
    hi = x1_ref[:, half:2 * half]
    for k in range(TOP_K):
        g = gate_ref[:, k:k + 1]
        yw = ybuf[slot, k].reshape(tb, half)
        lo = lo + g * pltpu.unpack_elementwise(yw, index=0, packed_dtype=BF16, unpacked_dtype=F32)
        hi = hi + g * pltpu.unpack_elementwise(yw, index=1, packed_dtype=BF16, unpacked_dtype=F32)
    o_ref[:, 0:half] = lo
    o_ref[:, half:2 * half] = hi


def _combine_call(pos, x1, gate_tk, ys, tb=256):
    t, d = x1.shape
    row = ys.shape[1:]
    last = t // tb - 1
    return pl.pallas_call(
        functools.partial(_combine_kernel, tb=tb),
        grid=(t // tb,),
        in_specs=[pl.BlockSpec((TOP_K, tb), lambda i: (0, i), memory_space=pltpu.SMEM),
                  pl.BlockSpec((TOP_K, tb), lambda i: (0, jnp.minimum(i + 1, last)),
                               memory_space=pltpu.SMEM),
                  pl.BlockSpec((tb, d), lambda i: (i, 0)),
                  pl.BlockSpec((tb, TOP_K), lambda i: (i, 0)),
                  pl.BlockSpec(memory_space=pl.ANY)],
        out_specs=pl.BlockSpec((tb, d), lambda i: (i, 0)),
        out_shape=jax.ShapeDtypeStruct((t, d), F32),
        scratch_shapes=[pltpu.VMEM((2, TOP_K, tb, *row), U32),
                        pltpu.SemaphoreType.DMA((2,))],
        compiler_params=_cparams(("arbitrary",)),
        name="combine",
    )(pos, pos, x1, gate_tk, ys)


def _work_list(ntile, otile, n_work, group):
    ne = ntile.shape[0]
    items = (ntile + group - 1) // group
    ends = jnp.cumsum(items)
    total = ends[-1]
    w = jnp.arange(n_work, dtype=I32)
    valid = w < total
    e_w = jnp.clip(jnp.sum((ends[None, :] <= w[:, None]).astype(I32), axis=1), 0, ne - 1)
    local = w - (ends - items)[e_w]
    t0_w = otile[e_w] + local * group
    nt_w = jnp.where(valid, jnp.clip(ntile[e_w] - local * group, 0, group), 0)
    e_last = e_w[jnp.maximum(total - 1, 0)]
    return (jnp.where(valid, e_w, e_last).astype(I32), jnp.where(valid, t0_w, 0).astype(I32),
            nt_w.astype(I32))


def _rotary_tables(seq, dk):
    inv_freq = ROPE_BASE ** (-jnp.arange(0, dk, 2, dtype=F32) / dk)
    ang = jnp.arange(seq, dtype=I32).astype(F32)[:, None] * inv_freq[None, :]
    cos = jnp.repeat(jnp.cos(ang), 2, axis=1)
    sin = jnp.stack([-jnp.sin(ang), jnp.sin(ang)], axis=-1).reshape(seq, dk)
    return cos, sin


def kernel(x, norm_mix_g, w_mix_in, ret_gn_g, q_norm_g, k_norm_g, lambda_q1, lambda_k1, lambda_q2, lambda_k2, diff_subln_g, rel_bias_table, w_mix_out, norm_ffn_g, w_router, b_router, w_exp_in, b_exp_in, w_exp_out, b_exp_out):
    batch, seq, d = x.shape
    t = batch * seq
    depth = norm_mix_g.shape[0]
    ne = w_router.shape[-1]
    n_tiles_max = (t * TOP_K) // ROW_TILE + ne
    n_work = ne + (n_tiles_max - ne) // GROUP_TILES
    cos, sin = _rotary_tables(seq, HEAD_W)
    log_gamma = jnp.log1p(-(2.0 ** (-5.0 - jnp.arange(RET_HEADS, dtype=F32))))

    x2 = x.reshape(t, d)
    for l in range(depth):
        proj = _inproj_call(x2, norm_mix_g[l], w_mix_in[l])
        y_ret = _ret_call(proj, cos, sin, log_gamma, ret_gn_g[l], batch, seq)
        lam_vecs = jnp.stack([lambda_q1[l], lambda_k1[l], lambda_q2[l], lambda_k2[l]])
        y_diff = _diff_call(proj, rel_bias_table, q_norm_g[l], k_norm_g[l], lam_vecs,
                            diff_subln_g[l], batch, seq)
        x1, h2p, logits_t = _outproj_call(y_ret, y_diff, x2, w_mix_out[l], norm_ffn_g[l],
                                          w_router[l], b_router[l])
        pos, gate, cnt, ntile, otile = _route_call(logits_t)
        cnt, ntile, otile = cnt[:, 0], ntile[:, 0], otile[:, 0]
        work_e, work_t0, work_nt = _work_list(ntile, otile, n_work, GROUP_TILES)
        xs = _dispatch_call(pos, otile * ROW_TILE + cnt, ntile * ROW_TILE - cnt, h2p,
                            n_tiles_max * ROW_TILE)
        ys = _moe_call(work_e, work_t0, work_nt, xs, w_exp_in[l], b_exp_in[l], w_exp_out[l],
                       b_exp_out[l])
        x2 = _combine_call(pos, x1, gate.T, ys)
    return x2.reshape(batch, seq, d)
```

```python
import functools
import math

import jax
import jax.numpy as jnp
from jax import lax
from jax.experimental import pallas as pl
from jax.experimental.pallas import tpu as pltpu

F32 = jnp.float32
BF16 = jnp.bfloat16
I32 = jnp.int32
U32 = jnp.uint32

EPS = 1e-6
CHUNK = 64
RET_HEADS = 8
DIFF_HEADS = 8
HEAD_W = 128
DIFF_DK = 64
ROPE_BASE = 10000.0
NUM_BUCKETS = 32
MAX_DISTANCE = 128
TOP_K = 4
SWIGLU_LIMIT = 7.0
SWIGLU_ALPHA = 1.702
LAM_INIT = 0.8 - 0.6 * math.exp(-0.3 * 0)

LANES = 128
NEG_BIG = -1e30
VMEM_LIMIT = 60 * 1024 * 1024

SEQ_BLK = 256
ROW_TILE = 256
GROUP_TILES = 6
PASS_TILES = 4
F_CHUNK = 512
CAST_ROWS = 256


def _cparams(sem, vmem=VMEM_LIMIT):
    return pltpu.CompilerParams(dimension_semantics=sem, vmem_limit_bytes=vmem)


def _inproj_kernel(x_ref, g_ref, w_ref, o_ref, wb_ref, *, rows_per_cast, sub_blocks=2):
    @pl.when(pl.program_id(1) == 0)
    def _():
        d = w_ref.shape[0]
        for c in range(d // rows_per_cast):
            sl = slice(c * rows_per_cast, (c + 1) * rows_per_cast)
            wb_ref[sl, :] = w_ref[sl, :].astype(BF16)

    sub = x_ref.shape[0] // sub_blocks
    for sb in range(sub_blocks):
        rs = slice(sb * sub, (sb + 1) * sub)
        x = x_ref[rs, :]
        ms = jnp.mean(x * x, axis=-1, keepdims=True)
        h = (x * lax.rsqrt(ms + EPS) * g_ref[...]).astype(BF16)
        acc = jnp.dot(h, wb_ref[...], preferred_element_type=F32)
        for j in range(o_ref.shape[0]):
            o_ref[j, rs, :] = acc[:, j * LANES:(j + 1) * LANES].astype(o_ref.dtype)


def _inproj_call(x2, g, w, tm=1024, tn=1024):
    t, d = x2.shape
    n = w.shape[1]
    return pl.pallas_call(
        functools.partial(_inproj_kernel, rows_per_cast=min(CAST_ROWS, d)),
        grid=(n // tn, t // tm),
        in_specs=[pl.BlockSpec((tm, d), lambda j, i: (i, 0)),
                  pl.BlockSpec((1, d), lambda j, i: (0, 0)),
                  pl.BlockSpec((d, tn), lambda j, i: (0, j))],
        out_specs=pl.BlockSpec((tn // LANES, tm, LANES), lambda j, i: (j, i, 0)),
        out_shape=jax.ShapeDtypeStruct((n // LANES, t, LANES), BF16),
        scratch_shapes=[pltpu.VMEM((d, tn), BF16)],
        compiler_params=_cparams(("arbitrary", "arbitrary")),
        name="in_proj",
    )(x2, g.reshape(1, d), w)


def _ret_kernel(lg_ref, q_ref, k_ref, v_ref, g_ref, cos_ref, sin_ref, gn_ref, o_ref, *, blk, nblk):
    dk = q_ref.shape[-1]
    lg = lg_ref[pl.program_id(1)]
    row = lax.broadcasted_iota(I32, (blk, blk), 0)
    col = lax.broadcasted_iota(I32, (blk, blk), 1)
    dist = jnp.abs(row - col).astype(F32)
    visible = (col // CHUNK) <= (row // CHUNK)
    dmask = jnp.where(visible, jnp.exp(lg * dist), 0.0)
    rr = lax.broadcasted_iota(I32, (blk, dk), 0).astype(F32)
    qdec = jnp.exp(lg * (rr + 1.0))
    kdec = jnp.exp(lg * (blk - 1.0 - rr))
    bdec = jnp.exp(lg * jnp.full((1, HEAD_W), float(blk), F32))
    even = (lax.broadcasted_iota(I32, (blk, dk), 1) & 1) == 0
    scale = dk ** -0.5

    def body(i, state):
        rows = pl.ds(pl.multiple_of(i * blk, blk), blk)
        cos = cos_ref[rows, :]
        sin = sin_ref[rows, :]

        def rot(x):
            partner = jnp.where(even, pltpu.roll(x, dk - 1, 1), pltpu.roll(x, 1, 1))
            return x * cos + partner * sin

        qr = rot(q_ref[rows, :].astype(F32)) * scale
        kr = rot(k_ref[rows, :].astype(F32))
        v = v_ref[rows, :]
        s = lax.dot_general(qr.astype(BF16), kr.astype(BF16), (((1,), (1,)), ((), ())),
                            preferred_element_type=F32) * dmask
        out = jnp.dot(s.astype(BF16), v, preferred_element_type=F32)
        out = out + jnp.dot((qr * qdec).astype(BF16), state.astype(BF16),
                            preferred_element_type=F32)
        kv = lax.dot_general((kr * kdec).astype(BF16), v, (((0,), (0,)), ((), ())),
                             preferred_element_type=F32)
        state = state * bdec + kv
        ms = jnp.mean(out * out, axis=-1, keepdims=True)
        normed = out * lax.rsqrt(ms + EPS) * gn_ref[...]
        g = g_ref[rows, :].astype(F32)
        o_ref[rows, :] = (g * jax.nn.sigmoid(g) * normed).astype(o_ref.dtype)
        return state

    lax.fori_loop(0, nblk, body, jnp.zeros((dk, HEAD_W), F32), unroll=True)


def _ret_call(proj, cos, sin, log_gamma, gn_g, batch, seq, blk=SEQ_BLK):
    nh = RET_HEADS

    def head_spec(base):
        return pl.BlockSpec((None, seq, LANES), lambda b, h: (base + h, b, 0))

    return pl.pallas_call(
        functools.partial(_ret_kernel, blk=blk, nblk=seq // blk),
        grid=(batch, nh),
        in_specs=[pl.BlockSpec(memory_space=pltpu.SMEM),
                  head_spec(0), head_spec(nh), head_spec(2 * nh), head_spec(3 * nh),
                  pl.BlockSpec((seq, LANES), lambda b, h: (0, 0)),
                  pl.BlockSpec((seq, LANES), lambda b, h: (0, 0)),
                  pl.BlockSpec((None, 1, LANES), lambda b, h: (h, 0, 0))],
        out_specs=pl.BlockSpec((None, seq, LANES), lambda b, h: (h, b, 0)),
        out_shape=jax.ShapeDtypeStruct((nh, batch * seq, LANES), BF16),
        compiler_params=_cparams(("arbitrary", "arbitrary")),
        name="retention",
    )(log_gamma, proj, proj, proj, proj, cos, sin, gn_g.reshape(nh, 1, LANES))


def _diff_kernel(tbl_ref, q_ref, k_ref, v_ref, qg_ref, kg_ref, lam_ref, sg_ref, bidx_ref, o_ref,
                 qz_s, kn_s, bias_s, s_s, p_s, *, blk, nblk):
    h = pl.program_id(0)
    b = pl.program_id(1)
    far_bucket = NUM_BUCKETS // 2 - 1

    @pl.when(b == 0)
    def _build_bias():
        row = lax.broadcasted_iota(I32, (blk, blk), 0)
        col = lax.broadcasted_iota(I32, (blk, blk), 1)
        visible = (col // CHUNK) <= (row // CHUNK)
        for d in range(2):
            idx = bidx_ref[d]
            bias = jnp.zeros((blk, blk), F32)
            for bucket in range(NUM_BUCKETS):
                bias = jnp.where(idx == bucket, tbl_ref[bucket, h], bias)
            if d == 0:
                bias = jnp.where(visible, bias, NEG_BIG)
            bias_s[d, 0:blk, :] = bias
            bias_s[d, blk:2 * blk, :] = bias

    lo = lax.broadcasted_iota(I32, (blk, HEAD_W), 1) < DIFF_DK
    scale = DIFF_DK ** -0.5
    same_half = ((lax.broadcasted_iota(I32, (HEAD_W, HEAD_W), 0) // DIFF_DK)
                 == (lax.broadcasted_iota(I32, (HEAD_W, HEAD_W), 1) // DIFF_DK)).astype(BF16)

    def half_norm(x, g):
        ms = jnp.dot((x * x).astype(BF16), same_half, preferred_element_type=F32) * (1.0 / DIFF_DK)
        return x * lax.rsqrt(ms + EPS) * g

    def prep(i, carry):
        rows = pl.ds(pl.multiple_of(i * blk, blk), blk)
        qn = half_norm(q_ref[rows, :].astype(F32), qg_ref[...]) * scale
        kn = half_norm(k_ref[rows, :].astype(F32), kg_ref[...])
        base = pl.multiple_of(i * 2 * blk, 2 * blk)
        qz_s[pl.ds(base, blk), :] = jnp.where(lo, qn, 0.0).astype(BF16)
        qz_s[pl.ds(base + blk, blk), :] = jnp.where(lo, 0.0, qn).astype(BF16)
        kn_s[rows, :] = kn.astype(BF16)
        return carry

    lax.fori_loop(0, nblk, prep, 0)

    lam = (jnp.exp(jnp.sum(lam_ref[0:1, :] * lam_ref[1:2, :], axis=-1, keepdims=True))
           - jnp.exp(jnp.sum(lam_ref[2:3, :] * lam_ref[3:4, :], axis=-1, keepdims=True))
           + LAM_INIT)
    c_far = tbl_ref[far_bucket, h]

    def lane_fold(x, op):
        out = x[:, 0:LANES]
        for c in range(1, blk // LANES):
            out = op(out, x[:, c * LANES:(c + 1) * LANES])
        return out

    for i in range(nblk):
        qz = qz_s[i * 2 * blk:(i + 1) * 2 * blk, :]
        m_t = jnp.full((2 * blk, LANES), NEG_BIG, F32)
        for j in range(i + 1):
            keys = slice(j * blk, (j + 1) * blk)
            s = lax.dot_general(qz, kn_s[keys, :], (((1,), (1,)), ((), ())),
                                preferred_element_type=F32)
            s = s + (bias_s[0] if j == i else bias_s[1] if j == i - 1 else c_far)
            s_s[:, keys] = s
            m_t = jnp.maximum(m_t, lane_fold(s, jnp.maximum))
        m = jnp.max(m_t, axis=-1, keepdims=True)
        l_t = jnp.zeros((2 * blk, LANES), F32)
        for j in range(i + 1):
            keys = slice(j * blk, (j + 1) * blk)
            p = jnp.exp(s_s[:, keys] - m)
            l_t = l_t + lane_fold(p, jnp.add)
            p_s[:, keys] = p.astype(BF16)
        l = jnp.sum(l_t, axis=-1, keepdims=True)
        kend = (i + 1) * blk
        o = jnp.dot(p_s[:, 0:kend], v_ref[0:kend, :], preferred_element_type=F32) / l
        att = o[0:blk, :] - lam * o[blk:2 * blk, :]
        ms = jnp.mean(att * att, axis=-1, keepdims=True)
        y = att * lax.rsqrt(ms + EPS) * sg_ref[...] * (1.0 - LAM_INIT)
        o_ref[i * blk:(i + 1) * blk, :] = y.astype(o_ref.dtype)


def _t5_bucket(rel):
    nb = NUM_BUCKETS // 2
    max_exact = nb // 2
    base = jnp.where(rel > 0, nb, 0)
    n = jnp.abs(rel)
    large = max_exact + (jnp.log(jnp.maximum(n, 1).astype(jnp.float32) / max_exact)
                         / math.log(MAX_DISTANCE / max_exact) * (nb - max_exact)).astype(jnp.int32)
    large = jnp.minimum(large, nb - 1)
    return base + jnp.where(n < max_exact, n, large)


def _diff_call(proj, rel_table, qg, kg, lam_vecs, sg, batch, seq, blk=SEQ_BLK):
    nh = DIFF_HEADS
    first = 4 * RET_HEADS
    r = jnp.arange(blk, dtype=I32)
    rel0 = r[None, :] - r[:, None]
    bidx = jnp.stack([_t5_bucket(rel0), _t5_bucket(rel0 - blk)]).astype(I32) & (NUM_BUCKETS - 1)

    def head_spec(base):
        return pl.BlockSpec((None, seq, LANES), lambda h, b: (base + h, b, 0))

    def vec_spec():
        return pl.BlockSpec((1, LANES), lambda h, b: (0, 0))

    return pl.pallas_call(
        functools.partial(_diff_kernel, blk=blk, nblk=seq // blk),
        grid=(nh, batch),
        in_specs=[pl.BlockSpec(memory_space=pltpu.SMEM),
                  head_spec(first), head_spec(first + nh), head_spec(first + 2 * nh),
                  vec_spec(), vec_spec(),
                  pl.BlockSpec((4, DIFF_DK), lambda h, b: (0, 0)),
                  vec_spec(),
                  pl.BlockSpec((2, blk, blk), lambda h, b: (0, 0, 0))],
        out_specs=pl.BlockSpec((None, seq, LANES), lambda h, b: (h, b, 0)),
        out_shape=jax.ShapeDtypeStruct((nh, batch * seq, LANES), BF16),
        scratch_shapes=[pltpu.VMEM((2 * seq, LANES), BF16),
                        pltpu.VMEM((seq, LANES), BF16),
                        pltpu.VMEM((2, 2 * blk, blk), F32),
                        pltpu.VMEM((2 * blk, seq), F32),
                        pltpu.VMEM((2 * blk, seq), BF16)],
        compiler_params=_cparams(("arbitrary", "arbitrary")),
        name="diff_attn",
    )(rel_table, proj, proj, proj,
      jnp.tile(qg, 2).reshape(1, LANES), jnp.tile(kg, 2).reshape(1, LANES),
      lam_vecs, sg.reshape(1, LANES), bidx)


def _outproj_kernel(yr_ref, yd_ref, x_ref, w_hbm, g_ref, wr_ref, br_ref,
                    x1_ref, h2p_ref, lt_ref, wb_s, wstage_s, wr2_s, wsem,
                    *, rows_per_cast, sub_blocks=2):
    ne = lt_ref.shape[0]
    nchunk = wb_s.shape[0] // rows_per_cast

    def w_copy(c):
        return pltpu.make_async_copy(w_hbm.at[pl.ds(c * rows_per_cast, rows_per_cast)],
                                     wstage_s.at[c % 2], wsem.at[c % 2])

    @pl.when(pl.program_id(0) == 0)
    def _stage_weights():
        w_copy(0).start()
        for c in range(nchunk):
            if c + 1 < nchunk:
                w_copy(c + 1).start()
            w_copy(c).wait()
            wb_s[c * rows_per_cast:(c + 1) * rows_per_cast, :] = wstage_s[c % 2].astype(BF16)
        wr = wr_ref[...]
        wr_hi = wr.astype(BF16)
        wr2_s[:, 0:LANES] = wr_hi
        wr2_s[:, LANES:2 * LANES] = (wr - wr_hi.astype(F32)).astype(BF16)

    tm = x_ref.shape[0]
    sub = tm // sub_blocks
    for sb in range(sub_blocks):
        rs = slice(sb * sub, (sb + 1) * sub)
        y = jnp.concatenate([yr_ref[j, rs, :] for j in range(yr_ref.shape[0])]
                            + [yd_ref[j, rs, :] for j in range(yd_ref.shape[0])], axis=-1)
        x1 = x_ref[rs, :] + jnp.dot(y, wb_s[...], preferred_element_type=F32)
        x1_ref[rs, :] = x1
        ms = jnp.mean(x1 * x1, axis=-1, keepdims=True)
        h2 = x1 * lax.rsqrt(ms + EPS) * g_ref[...]
        h_hi = h2.astype(BF16)
        h_lo = (h2 - h_hi.astype(F32)).astype(BF16)
        parts = (jnp.dot(h_hi, wr2_s[...], preferred_element_type=F32)
                 + jnp.dot(h_lo, wr2_s[...], preferred_element_type=F32))
        logits = parts[:, 0:LANES] + parts[:, LANES:2 * LANES]
        lt_ref[:, rs] = logits.T[0:ne, :] + br_ref[...]
        half = h2.shape[1] // 2
        packed = pltpu.pack_elementwise([h2[:, :half], h2[:, half:]], packed_dtype=BF16)
        h2p_ref[rs] = packed.reshape((sub,) + h2p_ref.shape[1:])


def _outproj_call(y_ret, y_diff, x2, w_out, g, w_router, b_router, tm=512):
    t, d = x2.shape
    ne = w_router.shape[1]
    nhr, nhd = y_ret.shape[0], y_diff.shape[0]
    rows_per_cast = min(CAST_ROWS, d)
    wr_pad = jnp.pad(w_router, ((0, 0), (0, LANES - ne)))
    return pl.pallas_call(
        functools.partial(_outproj_kernel, rows_per_cast=rows_per_cast),
        grid=(t // tm,),
        in_specs=[pl.BlockSpec((nhr, tm, LANES), lambda i: (0, i, 0)),
                  pl.BlockSpec((nhd, tm, LANES), lambda i: (0, i, 0)),
                  pl.BlockSpec((tm, d), lambda i: (i, 0)),
                  pl.BlockSpec(memory_space=pl.ANY),
                  pl.BlockSpec((1, d), lambda i: (0, 0)),
                  pl.BlockSpec((d, LANES), lambda i: (0, 0)),
                  pl.BlockSpec((ne, 1), lambda i: (0, 0))],
        out_specs=[pl.BlockSpec((tm, d), lambda i: (i, 0)),
                   pl.BlockSpec((tm, d // 2 // LANES, LANES), lambda i: (i, 0, 0)),
                   pl.BlockSpec((ne, tm), lambda i: (0, i))],
        out_shape=[jax.ShapeDtypeStruct((t, d), F32),
                   jax.ShapeDtypeStruct((t, d // 2 // LANES, LANES), U32),
                   jax.ShapeDtypeStruct((ne, t), F32)],
        scratch_shapes=[pltpu.VMEM((d, d), BF16),
                        pltpu.VMEM((2, rows_per_cast, d), F32),
                        pltpu.VMEM((d, 2 * LANES), BF16),
                        pltpu.SemaphoreType.DMA((2,))],
        compiler_params=_cparams(("arbitrary",)),
        name="out_proj_router",
    )(y_ret, y_diff, x2, w_out, g.reshape(1, d), wr_pad, b_router.reshape(ne, 1))


def _route_kernel(lt_ref, pos_ref, gate_ref, cnt_ref, nt_ref, ot_ref, idx_s, rank_s, *, tb, row_tile):
    ne, t = lt_ref.shape
    e_iota = lax.broadcasted_iota(I32, (ne, tb), 0)
    upper = (lax.broadcasted_iota(I32, (tb, tb), 0)
             < lax.broadcasted_iota(I32, (tb, tb), 1)).astype(BF16)

    def pass_a(i, running):
        cols = pl.ds(pl.multiple_of(i * tb, tb), tb)
        l = lt_ref[:, cols]
        tops, hots = [], []
        for k in range(TOP_K):
            m = jnp.max(l, axis=0, keepdims=True)
            idx = jnp.min(jnp.where(l == m, e_iota, ne), axis=0, keepdims=True)
            hot = e_iota == idx
            l = jnp.where(hot, -jnp.inf, l)
            idx_s[k:k + 1, cols] = idx
            tops.append(m)
            hots.append(hot)
        exps = [jnp.exp(m - tops[0]) for m in tops]
        denom = exps[0] + exps[1] + exps[2] + exps[3]
        for k in range(TOP_K):
            gate_ref[k:k + 1, cols] = exps[k] / denom
        hot_all = jnp.zeros((ne, tb), F32)
        for hot in hots:
            hot_all = hot_all + hot.astype(F32)
        before = running + jnp.dot(hot_all.astype(BF16), upper, preferred_element_type=F32)
        for k in range(TOP_K):
            rank_s[k:k + 1, cols] = jnp.sum(jnp.where(hots[k], before, 0.0), axis=0, keepdims=True)
        return running + jnp.sum(hot_all, axis=1, keepdims=True)

    cnt = lax.fori_loop(0, t // tb, pass_a, jnp.zeros((ne, 1), F32))
    ntile = jnp.floor((cnt + (row_tile - 1.0)) * (1.0 / row_tile))
    lower = (lax.broadcasted_iota(I32, (ne, ne), 1)
             < lax.broadcasted_iota(I32, (ne, ne), 0)).astype(BF16)
    otile = jnp.dot(lower, jnp.broadcast_to(ntile, (ne, LANES)).astype(BF16),
                    preferred_element_type=F32)
    cnt_ref[...] = jnp.broadcast_to(cnt, (ne, LANES)).astype(I32)
    nt_ref[...] = jnp.broadcast_to(ntile, (ne, LANES)).astype(I32)
    ot_ref[...] = otile.astype(I32)
    off_rows = otile[:, 0:1] * float(row_tile)

    def pass_b(i, carry):
        cols = pl.ds(pl.multiple_of(i * tb, tb), tb)
        for k in range(TOP_K):
            hot = e_iota == idx_s[k:k + 1, cols]
            off = jnp.sum(jnp.where(hot, off_rows, 0.0), axis=0, keepdims=True)
            pos_ref[k:k + 1, cols] = (rank_s[k:k + 1, cols] + off).astype(I32)
        return carry

    lax.fori_loop(0, t // tb, pass_b, 0)


def _route_call(logits_t, tb=256, row_tile=ROW_TILE):
    ne, t = logits_t.shape
    return pl.pallas_call(
        functools.partial(_route_kernel, tb=tb, row_tile=row_tile),
        out_shape=[jax.ShapeDtypeStruct((TOP_K, t), I32),
                   jax.ShapeDtypeStruct((TOP_K, t), F32),
                   jax.ShapeDtypeStruct((ne, LANES), I32),
                   jax.ShapeDtypeStruct((ne, LANES), I32),
                   jax.ShapeDtypeStruct((ne, LANES), I32)],
        scratch_shapes=[pltpu.VMEM((TOP_K, t), I32), pltpu.VMEM((TOP_K, t), F32)],
        compiler_params=_cparams(None),
        name="route",
    )(logits_t)


_PAD_PIECES = tuple(ROW_TILE >> (k + 1) for k in range(ROW_TILE.bit_length() - 1))


def _dispatch_kernel(pos_ref, pad_start_ref, pad_n_ref, src_ref, dst_ref, stage_s, zero_s,
                     lsem, ssem, zsem, *, tb, ne):
    i = pl.program_id(0)
    nsteps = pl.num_programs(0)
    slot = i % 2

    def load(step, s):
        return pltpu.make_async_copy(src_ref.at[pl.ds(step * tb, tb)], stage_s.at[s], lsem.at[s])

    def wait_rows(s):
        for k in range(TOP_K):
            pltpu.make_async_copy(stage_s.at[s], dst_ref.at[pl.ds(0, tb)], ssem.at[s]).wait()

    def pad_copies(e):
        n = pad_n_ref[e]
        start = pad_start_ref[e]
        out = []
        for piece in _PAD_PIECES:
            at = start + (n & ~(2 * piece - 1))
            out.append(((n & piece) != 0, pltpu.make_async_copy(
                zero_s.at[pl.ds(0, piece)], dst_ref.at[pl.ds(at, piece)], zsem)))
        return out

    @pl.when(i == 0)
    def _zero_pads():
        zero_s[...] = jnp.zeros(zero_s.shape, zero_s.dtype)

        def issue(e, c):
            for cond, cp in pad_copies(e):
                @pl.when(cond)
                def _():
                    cp.start()
            return c

        def drain(e, c):
            for cond, cp in pad_copies(e):
                @pl.when(cond)
                def _():
                    cp.wait()
            return c

        lax.fori_loop(0, ne, issue, 0)
        lax.fori_loop(0, ne, drain, 0)
        load(0, 0).start()

    @pl.when(i > 0)
    def _():
        wait_rows(1 - slot)

    @pl.when(i + 1 < nsteps)
    def _():
        load(i + 1, 1 - slot).start()

    load(i, slot).wait()

    def issue_rows(tl, c):
        for k in range(TOP_K):
            pltpu.make_async_copy(stage_s.at[slot, tl], dst_ref.at[pos_ref[k, tl]],
                                  ssem.at[slot]).start(priority=k % 2)
        return c

    lax.fori_loop(0, tb, issue_rows, 0, unroll=4)

    @pl.when(i == nsteps - 1)
    def _():
        wait_rows(slot)


def _dispatch_call(pos, pad_start, pad_n, h2p, n_slots, tb=512):
    t = h2p.shape[0]
    row = h2p.shape[1:]
    ne = pad_n.shape[0]
    return pl.pallas_call(
        functools.partial(_dispatch_kernel, tb=tb, ne=ne),
        grid=(t // tb,),
        in_specs=[pl.BlockSpec((TOP_K, tb), lambda i: (0, i), memory_space=pltpu.SMEM),
                  pl.BlockSpec(memory_space=pltpu.SMEM),
                  pl.BlockSpec(memory_space=pltpu.SMEM),
                  pl.BlockSpec(memory_space=pl.ANY)],
        out_specs=pl.BlockSpec(memory_space=pl.ANY),
        out_shape=jax.ShapeDtypeStruct((n_slots, *row), U32),
        scratch_shapes=[pltpu.VMEM((2, tb, *row), U32),
                        pltpu.VMEM((_PAD_PIECES[0], *row), U32),
                        pltpu.SemaphoreType.DMA((2,)),
                        pltpu.SemaphoreType.DMA((2,)),
                        pltpu.SemaphoreType.DMA(())],
        compiler_params=pltpu.CompilerParams(dimension_semantics=("arbitrary",),
                                             vmem_limit_bytes=VMEM_LIMIT, has_side_effects=True),
        name="dispatch",
    )(pos, pad_start, pad_n, h2p)


def _moe_kernel(we_ref, wt_ref, wn_ref, xs_ref, win_ref, bin_ref, wout_ref, bo_ref,
                ys_ref, xu_s, xb_s, acc_s, wg_st, wu_st, wo_st, yst_s, xsem, wsem, ysem,
                *, tmx, fc, nj):
    w = pl.program_id(0)
    nw = pl.num_programs(0)
    nt = wn_ref[w]
    t0 = wt_ref[w]
    half = xu_s.shape[-2] * xu_s.shape[-1]
    de = nj * fc
    nxt = jnp.minimum(w + 1, nw - 1)
    has_next = jnp.logical_and(w + 1 < nw, wn_ref[nxt] > 0)

    def local_rows(r, n=1):
        return pl.ds(pl.multiple_of(r * tmx, tmx), n * tmx)

    def x_copy(item_t0, r):
        src = pl.ds(pl.multiple_of((item_t0 + r) * tmx, tmx), tmx)
        return pltpu.make_async_copy(xs_ref.at[src], xu_s.at[local_rows(r)], xsem)

    def y_copy(r, slot):
        dst = pl.ds(pl.multiple_of((t0 + r) * tmx, tmx), tmx)
        return pltpu.make_async_copy(yst_s.at[slot], ys_ref.at[dst], ysem.at[slot])

    def w_copies(expert, j, slot):
        c0 = pl.multiple_of(j * fc, fc)
        return (pltpu.make_async_copy(win_ref.at[expert, :, pl.ds(c0, fc)], wg_st.at[slot], wsem.at[slot]),
                pltpu.make_async_copy(win_ref.at[expert, :, pl.ds(de + c0, fc)], wu_st.at[slot],
                                      wsem.at[slot]),
                pltpu.make_async_copy(wout_ref.at[expert, pl.ds(c0, fc), :], wo_st.at[slot],
                                      wsem.at[slot]))

    def start_weights(expert, j, slot):
        for cp in w_copies(expert, j, slot):
            cp.start()

    def start_rows(item_t0, item_nt):
        def go(r, c):
            x_copy(item_t0, r).start()
            return c
        lax.fori_loop(0, item_nt, go, 0)

    @pl.when(w == 0)
    def _prologue():
        start_weights(we_ref[0], 0, 0)
        start_rows(t0, nt)

    @pl.when(nt > 0)
    def _work():
        def chunk(j, carry):
            slot = j % 2

            @pl.when(j + 1 < nj)
            def _():
                start_weights(we_ref[w], j + 1, 1 - slot)

            @pl.when(jnp.logical_and(j + 1 == nj, has_next))
            def _():
                start_weights(we_ref[nxt], 0, 1 - slot)

            for cp in w_copies(we_ref[w], j, slot):
                cp.wait()
            stage = (wg_st.at[slot], wu_st.at[slot], wo_st.at[slot])
            biases = (bin_ref[j], bin_ref[nj + j])

            @pl.when(j == 0)
            def _rows_ready():
                def finish(r, c):
                    x_copy(t0, r).wait()
                    return c
                lax.fori_loop(0, nt, finish, 0)

            @pl.when(jnp.logical_and(j == 1, has_next))
            def _():
                start_rows(wt_ref[nxt], wn_ref[nxt])

            @pl.when(j == 0)
            def _():
                _moe_passes(xu_s, xb_s, acc_s, stage, biases, nt, local_rows, half, True)

            @pl.when(j > 0)
            def _():
                _moe_passes(xu_s, xb_s, acc_s, stage, biases, nt, local_rows, half, False)

            return carry

        lax.fori_loop(0, nj, chunk, 0)

        def y_wait(slot):
            y_copy(0, slot).wait()

        prev_nt = jnp.where(w > 0, wn_ref[jnp.maximum(w - 1, 0)], 0)

        @pl.when(prev_nt >= 1)
        def _():
            y_wait((prev_nt - 1) % 2)

        @pl.when(prev_nt >= 2)
        def _():
            y_wait(prev_nt % 2)

        def emit(r, c):
            slot = r % 2

            @pl.when(r >= 2)
            def _():
                y_wait(slot)

            y = acc_s[local_rows(r), :] + bo_ref[...]
            yst_s[slot] = pltpu.pack_elementwise([y[:, :half], y[:, half:]],
                                                 packed_dtype=BF16).reshape(yst_s.shape[1:])
            y_copy(r, slot).start()
            return c

        lax.fori_loop(0, nt, emit, 0)

        @pl.when(jnp.logical_not(has_next))
        def _drain():
            @pl.when(nt >= 2)
            def _():
                y_wait(nt % 2)

            y_wait((nt - 1) % 2)


def _moe_passes(xu_ref, xb_ref, acc_s, stage, biases, nt, local_rows, half, first):
    wg_ref, wu_ref, wo_ref = stage
    bg, bu = biases

    def proj(x_lo, x_hi, w_ref, b):
        return (jnp.dot(x_lo, w_ref[0:half, :].astype(BF16), preferred_element_type=F32)
                + jnp.dot(x_hi, w_ref[half:2 * half, :].astype(BF16), preferred_element_type=F32) + b)

    def rows_step(rows):
        if first:
            xw = xu_ref[rows]
            xw = xw.reshape(xw.shape[0], half)
            x_lo = pltpu.unpack_elementwise(xw, index=0, packed_dtype=BF16,
                                            unpacked_dtype=F32).astype(BF16)
            x_hi = pltpu.unpack_elementwise(xw, index=1, packed_dtype=BF16,
                                            unpacked_dtype=F32).astype(BF16)
            xb_ref[rows, 0:half] = x_lo
            xb_ref[rows, half:2 * half] = x_hi
        else:
            x_lo = xb_ref[rows, 0:half]
            x_hi = xb_ref[rows, half:2 * half]
        gg = jnp.minimum(proj(x_lo, x_hi, wg_ref, bg), SWIGLU_LIMIT)
        uu = jnp.clip(proj(x_lo, x_hi, wu_ref, bu), -SWIGLU_LIMIT, SWIGLU_LIMIT)
        act = (uu + 1.0) * (gg * jax.nn.sigmoid(SWIGLU_ALPHA * gg))
        y = jnp.dot(act.astype(BF16), wo_ref[...].astype(BF16), preferred_element_type=F32)
        if first:
            acc_s[rows, :] = y
        else:
            acc_s[rows, :] += y

    def big(pi, c):
        rows_step(local_rows(PASS_TILES * pi, PASS_TILES))
        return c

    lax.fori_loop(0, nt // PASS_TILES, big, 0)
    piece = PASS_TILES // 2
    while piece >= 1:
        @pl.when((nt & piece) != 0)
        def _(piece=piece):
            rows_step(local_rows(nt & ~(2 * piece - 1), piece))
        piece //= 2


def _moe_call(work_e, work_t0, work_nt, xs, w_in, b_in, w_out, b_out,
              tmx=ROW_TILE, group=GROUP_TILES, fc=F_CHUNK):
    ne, d, de2 = w_in.shape
    de = de2 // 2
    nj = de // fc
    assert nj % 2 == 0, "weight stage slots alternate per chunk and restart at 0 per work item"
    n_work = work_e.shape[0]
    n_slots = xs.shape[0]
    row = xs.shape[1:]

    grid_spec = pltpu.PrefetchScalarGridSpec(
        num_scalar_prefetch=3,
        grid=(n_work,),
        in_specs=[
            pl.BlockSpec(memory_space=pl.ANY),
            pl.BlockSpec(memory_space=pl.ANY),
            pl.BlockSpec((None, 2 * nj, 1, fc), lambda w, we, wt, wn: (we[w], 0, 0, 0)),
            pl.BlockSpec(memory_space=pl.ANY),
            pl.BlockSpec((None, 1, d), lambda w, we, wt, wn: (we[w], 0, 0)),
        ],
        out_specs=pl.BlockSpec(memory_space=pl.ANY),
        scratch_shapes=[pltpu.VMEM((group * tmx, *row), U32),
                        pltpu.VMEM((group * tmx, d), BF16),
                        pltpu.VMEM((group * tmx, d), F32),
                        pltpu.VMEM((2, d, fc), F32),
                        pltpu.VMEM((2, d, fc), F32),
                        pltpu.VMEM((2, fc, d), F32),
                        pltpu.VMEM((2, tmx, *row), U32),
                        pltpu.SemaphoreType.DMA(()),
                        pltpu.SemaphoreType.DMA((2,)),
                        pltpu.SemaphoreType.DMA((2,))],
    )
    return pl.pallas_call(
        functools.partial(_moe_kernel, tmx=tmx, fc=fc, nj=nj),
        grid_spec=grid_spec,
        out_shape=jax.ShapeDtypeStruct((n_slots, *row), U32),
        compiler_params=pltpu.CompilerParams(dimension_semantics=("arbitrary",),
                                             vmem_limit_bytes=VMEM_LIMIT, has_side_effects=True),
        name="moe_experts",
    )(work_e, work_t0, work_nt, xs, w_in, b_in.reshape(ne, 2 * nj, 1, fc), w_out,
      b_out.reshape(ne, 1, d))


def _combine_kernel(pos_ref, pos_next_ref, x1_ref, gate_ref, ys_ref, o_ref, ybuf, sem, *, tb):
    half = ybuf.shape[-2] * ybuf.shape[-1]
    i = pl.program_id(0)
    slot = i % 2

    def gather(p_ref, s):
        def issue(tl, c):
            for k in range(TOP_K):
                pltpu.make_async_copy(ys_ref.at[p_ref[k, tl]], ybuf.at[s, k, tl],
                                      sem.at[s]).start(priority=k % 2)
            return c
        lax.fori_loop(0, tb, issue, 0, unroll=4)

    @pl.when(i == 0)
    def _():
        gather(pos_ref, 0)

    @pl.when(i + 1 < pl.num_programs(0))
    def _():
        gather(pos_next_ref, 1 - slot)

    for k in range(TOP_K):
        pltpu.make_async_copy(ys_ref.at[pl.ds(0, tb)], ybuf.at[slot, k], sem.at[slot]).wait()

    lo = x1_ref[:, 0:half]
    hi = x1_ref[:, half:2 * half]
    for k in range(TOP_K):
        g = gate_ref[:, k:k + 1]
        yw = ybuf[slot, k].reshape(tb, half)
        lo = lo + g * pltpu.unpack_elementwise(yw, index=0, packed_dtype=BF16, unpacked_dtype=F32)
        hi = hi + g * pltpu.unpack_elementwise(yw, index=1, packed_dtype=BF16, unpacked_dtype=F32)
    o_ref[:, 0:half] = lo
    o_ref[:, half:2 * half] = hi


def _combine_call(pos, x1, gate_tk, ys, tb=256):
    t, d = x1.shape
    row = ys.shape[1:]
    last = t // tb - 1
    return pl.pallas_call(
        functools.partial(_combine_kernel, tb=tb),
        grid=(t // tb,),
        in_specs=[pl.BlockSpec((TOP_K, tb), lambda i: (0, i), memory_space=pltpu.SMEM),
                  pl.BlockSpec((TOP_K, tb), lambda i: (0, jnp.minimum(i + 1, last)),
                               memory_space=pltpu.SMEM),
                  pl.BlockSpec((tb, d), lambda i: (i, 0)),
                  pl.BlockSpec((tb, TOP_K), lambda i: (i, 0)),
                  pl.BlockSpec(memory_space=pl.ANY)],
        out_specs=pl.BlockSpec((tb, d), lambda i: (i, 0)),
        out_shape=jax.ShapeDtypeStruct((t, d), F32),
        scratch_shapes=[pltpu.VMEM((2, TOP_K, tb, *row), U32),
                        pltpu.SemaphoreType.DMA((2,))],
        compiler_params=_cparams(("arbitrary",)),
        name="combine",
    )(pos, pos, x1, gate_tk, ys)


def _work_list(ntile, otile, n_work, group):
    ne = ntile.shape[0]
    items = (ntile + group - 1) // group
    ends = jnp.cumsum(items)
    total = ends[-1]
    w = jnp.arange(n_work, dtype=I32)
    valid = w < total
    e_w = jnp.clip(jnp.sum((ends[None, :] <= w[:, None]).astype(I32), axis=1), 0, ne - 1)
    local = w - (ends - items)[e_w]
    t0_w = otile[e_w] + local * group
    nt_w = jnp.where(valid, jnp.clip(ntile[e_w] - local * group, 0, group), 0)
    e_last = e_w[jnp.maximum(total - 1, 0)]
    return (jnp.where(valid, e_w, e_last).astype(I32), jnp.where(valid, t0_w, 0).astype(I32),
            nt_w.astype(I32))


def _rotary_tables(seq, dk):
    inv_freq = ROPE_BASE ** (-jnp.arange(0, dk, 2, dtype=F32) / dk)
    ang = jnp.arange(seq, dtype=I32).astype(F32)[:, None] * inv_freq[None, :]
    cos = jnp.repeat(jnp.cos(ang), 2, axis=1)
    sin = jnp.stack([-jnp.sin(ang), jnp.sin(ang)], axis=-1).reshape(seq, dk)
    return cos, sin


def kernel(x, norm_mix_g, w_mix_in, ret_gn_g, q_norm_g, k_norm_g, lambda_q1, lambda_k1, lambda_q2, lambda_k2, diff_subln_g, rel_bias_table, w_mix_out, norm_ffn_g, w_router, b_router, w_exp_in, b_exp_in, w_exp_out, b_exp_out):
    batch, seq, d = x.shape
    t = batch * seq
    depth = norm_mix_g.shape[0]
    ne = w_router.shape[-1]
    n_tiles_max = (t * TOP_K) // ROW_TILE + ne
    n_work = ne + (n_tiles_max - ne) // GROUP_TILES
    cos, sin = _rotary_tables(seq, HEAD_W)
    log_gamma = jnp.log1p(-(2.0 ** (-5.0 - jnp.arange(RET_HEADS, dtype=F32))))

    x2 = x.reshape(t, d)
    for l in range(depth):
        proj = _inproj_call(x2, norm_mix_g[l], w_mix_in[l])
        y_ret = _ret_call(proj, cos, sin, log_gamma, ret_gn_g[l], batch, seq)
        lam_vecs = jnp.stack([lambda_q1[l], lambda_k1[l], lambda_q2[l], lambda_k2[l]])
        y_diff = _diff_call(proj, rel_bias_table, q_norm_g[l], k_norm_g[l], lam_vecs,
                            diff_subln_g[l], batch, seq)
        x1, h2p, logits_t = _outproj_call(y_ret, y_diff, x2, w_mix_out[l], norm_ffn_g[l],
                                          w_router[l], b_router[l])
        pos, gate, cnt, ntile, otile = _route_call(logits_t)
        cnt, ntile, otile = cnt[:, 0], ntile[:, 0], otile[:, 0]
        work_e, work_t0, work_nt = _work_list(ntile, otile, n_work, GROUP_TILES)
        xs = _dispatch_call(pos, otile * ROW_TILE + cnt, ntile * ROW_TILE - cnt, h2p,
                            n_tiles_max * ROW_TILE)
        ys = _moe_call(work_e, work_t0, work_nt, xs, w_exp_in[l], b_exp_in[l], w_exp_out[l],
                       b_exp_out[l])
        x2 = _combine_call(pos, x1, gate.T, ys)
    return x2.reshape(batch, seq, d)
```

```python
import functools
import math

import jax
import jax.numpy as jnp
from jax import lax
from jax.experimental import pallas as pl
from jax.experimental.pallas import tpu as pltpu

F32 = jnp.float32
BF16 = jnp.bfloat16
I32 = jnp.int32
U32 = jnp.uint32

EPS = 1e-6
CHUNK = 64
RET_HEADS = 8
DIFF_HEADS = 8
HEAD_W = 128
DIFF_DK = 64
ROPE_BASE = 10000.0
NUM_BUCKETS = 32
MAX_DISTANCE = 128
TOP_K = 4
SWIGLU_LIMIT = 7.0
SWIGLU_ALPHA = 1.702
LAM_INIT = 0.8 - 0.6 * math.exp(-0.3 * 0)

LANES = 128
NEG_BIG = -1e30
VMEM_LIMIT = 60 * 1024 * 1024

SEQ_BLK = 256
RET_BLK = 512
ROW_TILE = 256
GROUP_TILES = 6
PASS_TILES = 4
F_CHUNK = 512
CAST_ROWS = 256


def _cparams(sem, vmem=VMEM_LIMIT):
    return pltpu.CompilerParams(dimension_semantics=sem, vmem_limit_bytes=vmem)


def _inproj_kernel(x_ref, g_ref, w_ref, o_ref, wb_ref, *, rows_per_cast, sub_blocks=2):
    @pl.when(pl.program_id(1) == 0)
    def _():
        d = w_ref.shape[0]
        for c in range(d // rows_per_cast):
            sl = slice(c * rows_per_cast, (c + 1) * rows_per_cast)
            wb_ref[sl, :] = w_ref[sl, :].astype(BF16)

    sub = x_ref.shape[0] // sub_blocks
    for sb in range(sub_blocks):
        rs = slice(sb * sub, (sb + 1) * sub)
        x = x_ref[rs, :]
        ms = jnp.mean(x * x, axis=-1, keepdims=True)
        h = (x * lax.rsqrt(ms + EPS) * g_ref[...]).astype(BF16)
        acc = jnp.dot(h, wb_ref[...], preferred_element_type=F32)
        for j in range(o_ref.shape[0]):
            o_ref[j, rs, :] = acc[:, j * LANES:(j + 1) * LANES].astype(o_ref.dtype)


def _inproj_call(x2, g, w, tm=1024, tn=1024):
    t, d = x2.shape
    n = w.shape[1]
    return pl.pallas_call(
        functools.partial(_inproj_kernel, rows_per_cast=min(CAST_ROWS, d)),
        grid=(n // tn, t // tm),
        in_specs=[pl.BlockSpec((tm, d), lambda j, i: (i, 0)),
                  pl.BlockSpec((1, d), lambda j, i: (0, 0)),
                  pl.BlockSpec((d, tn), lambda j, i: (0, j))],
        out_specs=pl.BlockSpec((tn // LANES, tm, LANES), lambda j, i: (j, i, 0)),
        out_shape=jax.ShapeDtypeStruct((n // LANES, t, LANES), BF16),
        scratch_shapes=[pltpu.VMEM((d, tn), BF16)],
        compiler_params=_cparams(("arbitrary", "arbitrary")),
        name="in_proj",
    )(x2, g.reshape(1, d), w)


def _ret_kernel(lg_ref, q_ref, k_ref, v_ref, g_ref, cos_ref, sin_ref, gn_ref, o_ref, *, blk, nblk):
    dk = q_ref.shape[-1]
    lg = lg_ref[pl.program_id(1)]
    row = lax.broadcasted_iota(I32, (blk, blk), 0)
    col = lax.broadcasted_iota(I32, (blk, blk), 1)
    dist = jnp.abs(row - col).astype(F32)
    visible = (col // CHUNK) <= (row // CHUNK)
    dmask = jnp.where(visible, jnp.exp(lg * dist), 0.0)
    rr = lax.broadcasted_iota(I32, (blk, dk), 0).astype(F32)
    qdec = jnp.exp(lg * (rr + 1.0))
    kdec = jnp.exp(lg * (blk - 1.0 - rr))
    bdec = jnp.exp(lg * jnp.full((1, HEAD_W), float(blk), F32))
    even = (lax.broadcasted_iota(I32, (blk, dk), 1) & 1) == 0
    scale = dk ** -0.5

    def body(i, state):
        rows = pl.ds(pl.multiple_of(i * blk, blk), blk)
        cos = cos_ref[rows, :]
        sin = sin_ref[rows, :]

        def rot(x):
            partner = jnp.where(even, pltpu.roll(x, dk - 1, 1), pltpu.roll(x, 1, 1))
            return x * cos + partner * sin

        qr = rot(q_ref[rows, :].astype(F32)) * scale
        kr = rot(k_ref[rows, :].astype(F32))
        v = v_ref[rows, :]
        s = lax.dot_general(qr.astype(BF16), kr.astype(BF16), (((1,), (1,)), ((), ())),
                            preferred_element_type=F32) * dmask
        out = jnp.dot(s.astype(BF16), v, preferred_element_type=F32)
        out = out + jnp.dot((qr * qdec).astype(BF16), state.astype(BF16),
                            preferred_element_type=F32)
        kv = lax.dot_general((kr * kdec).astype(BF16), v, (((0,), (0,)), ((), ())),
                             preferred_element_type=F32)
        state = state * bdec + kv
        ms = jnp.mean(out * out, axis=-1, keepdims=True)
        normed = out * lax.rsqrt(ms + EPS) * gn_ref[...]
        g = g_ref[rows, :].astype(F32)
        o_ref[rows, :] = (g * jax.nn.sigmoid(g) * normed).astype(o_ref.dtype)
        return state

    lax.fori_loop(0, nblk, body, jnp.zeros((dk, HEAD_W), F32), unroll=True)


def _ret_call(proj, cos, sin, log_gamma, gn_g, batch, seq, blk=RET_BLK):
    nh = RET_HEADS

    def head_spec(base):
        return pl.BlockSpec((None, seq, LANES), lambda b, h: (base + h, b, 0))

    return pl.pallas_call(
        functools.partial(_ret_kernel, blk=blk, nblk=seq // blk),
        grid=(batch, nh),
        in_specs=[pl.BlockSpec(memory_space=pltpu.SMEM),
                  head_spec(0), head_spec(nh), head_spec(2 * nh), head_spec(3 * nh),
                  pl.BlockSpec((seq, LANES), lambda b, h: (0, 0)),
                  pl.BlockSpec((seq, LANES), lambda b, h: (0, 0)),
                  pl.BlockSpec((None, 1, LANES), lambda b, h: (h, 0, 0))],
        out_specs=pl.BlockSpec((None, seq, LANES), lambda b, h: (h, b, 0)),
        out_shape=jax.ShapeDtypeStruct((nh, batch * seq, LANES), BF16),
        compiler_params=_cparams(("arbitrary", "arbitrary")),
        name="retention",
    )(log_gamma, proj, proj, proj, proj, cos, sin, gn_g.reshape(nh, 1, LANES))


def _diff_kernel(tbl_ref, q_ref, k_ref, v_ref, qg_ref, kg_ref, lam_ref, sg_ref, bidx_ref, o_ref,
                 qz_s, kn_s, bias_s, s_s, p_s, *, blk, nblk):
    h = pl.program_id(0)
    b = pl.program_id(1)
    far_bucket = NUM_BUCKETS // 2 - 1

    @pl.when(b == 0)
    def _build_bias():
        row = lax.broadcasted_iota(I32, (blk, blk), 0)
        col = lax.broadcasted_iota(I32, (blk, blk), 1)
        visible = (col // CHUNK) <= (row // CHUNK)
        for d in range(2):
            idx = bidx_ref[d]
            bias = jnp.zeros((blk, blk), F32)
            for bucket in range(NUM_BUCKETS):
                bias = jnp.where(idx == bucket, tbl_ref[bucket, h], bias)
            if d == 0:
                bias = jnp.where(visible, bias, NEG_BIG)
            bias_s[d, 0:blk, :] = bias
            bias_s[d, blk:2 * blk, :] = bias

    lo = lax.broadcasted_iota(I32, (blk, HEAD_W), 1) < DIFF_DK
    scale = DIFF_DK ** -0.5
    same_half = ((lax.broadcasted_iota(I32, (HEAD_W, HEAD_W), 0) // DIFF_DK)
                 == (lax.broadcasted_iota(I32, (HEAD_W, HEAD_W), 1) // DIFF_DK)).astype(BF16)

    def half_norm(x, g):
        ms = jnp.dot((x * x).astype(BF16), same_half, preferred_element_type=F32) * (1.0 / DIFF_DK)
        return x * lax.rsqrt(ms + EPS) * g

    def prep(i, carry):
        rows = pl.ds(pl.multiple_of(i * blk, blk), blk)
        qn = half_norm(q_ref[rows, :].astype(F32), qg_ref[...]) * scale
        kn = half_norm(k_ref[rows, :].astype(F32), kg_ref[...])
        base = pl.multiple_of(i * 2 * blk, 2 * blk)
        qz_s[pl.ds(base, blk), :] = jnp.where(lo, qn, 0.0).astype(BF16)
        qz_s[pl.ds(base + blk, blk), :] = jnp.where(lo, 0.0, qn).astype(BF16)
        kn_s[rows, :] = kn.astype(BF16)
        return carry

    lax.fori_loop(0, nblk, prep, 0)

    lam = (jnp.exp(jnp.sum(lam_ref[0:1, :] * lam_ref[1:2, :], axis=-1, keepdims=True))
           - jnp.exp(jnp.sum(lam_ref[2:3, :] * lam_ref[3:4, :], axis=-1, keepdims=True))
           + LAM_INIT)
    c_far = tbl_ref[far_bucket, h]

    def lane_fold(x, op):
        out = x[:, 0:LANES]
        for c in range(1, blk // LANES):
            out = op(out, x[:, c * LANES:(c + 1) * LANES])
        return out

    for i in range(nblk):
        qz = qz_s[i * 2 * blk:(i + 1) * 2 * blk, :]
        m_t = jnp.full((2 * blk, LANES), NEG_BIG, F32)
        for j in range(i + 1):
            keys = slice(j * blk, (j + 1) * blk)
            s = lax.dot_general(qz, kn_s[keys, :], (((1,), (1,)), ((), ())),
                                preferred_element_type=F32)
            s = s + (bias_s[0] if j == i else bias_s[1] if j == i - 1 else c_far)
            s_s[:, keys] = s
            m_t = jnp.maximum(m_t, lane_fold(s, jnp.maximum))
        m = jnp.max(m_t, axis=-1, keepdims=True)
        l_t = jnp.zeros((2 * blk, LANES), F32)
        for j in range(i + 1):
            keys = slice(j * blk, (j + 1) * blk)
            p = jnp.exp(s_s[:, keys] - m)
            l_t = l_t + lane_fold(p, jnp.add)
            p_s[:, keys] = p.astype(BF16)
        l = jnp.sum(l_t, axis=-1, keepdims=True)
        kend = (i + 1) * blk
        o = jnp.dot(p_s[:, 0:kend], v_ref[0:kend, :], preferred_element_type=F32) / l
        att = o[0:blk, :] - lam * o[blk:2 * blk, :]
        ms = jnp.mean(att * att, axis=-1, keepdims=True)
        y = att * lax.rsqrt(ms + EPS) * sg_ref[...] * (1.0 - LAM_INIT)
        o_ref[i * blk:(i + 1) * blk, :] = y.astype(o_ref.dtype)


def _t5_bucket(rel):
    nb = NUM_BUCKETS // 2
    max_exact = nb // 2
    base = jnp.where(rel > 0, nb, 0)
    n = jnp.abs(rel)
    large = max_exact + (jnp.log(jnp.maximum(n, 1).astype(jnp.float32) / max_exact)
                         / math.log(MAX_DISTANCE / max_exact) * (nb - max_exact)).astype(jnp.int32)
    large = jnp.minimum(large, nb - 1)
    return base + jnp.where(n < max_exact, n, large)


def _diff_call(proj, rel_table, qg, kg, lam_vecs, sg, batch, seq, blk=SEQ_BLK):
    nh = DIFF_HEADS
    first = 4 * RET_HEADS
    r = jnp.arange(blk, dtype=I32)
    rel0 = r[None, :] - r[:, None]
    bidx = jnp.stack([_t5_bucket(rel0), _t5_bucket(rel0 - blk)]).astype(I32) & (NUM_BUCKETS - 1)

    def head_spec(base):
        return pl.BlockSpec((None, seq, LANES), lambda h, b: (base + h, b, 0))

    def vec_spec():
        return pl.BlockSpec((1, LANES), lambda h, b: (0, 0))

    return pl.pallas_call(
        functools.partial(_diff_kernel, blk=blk, nblk=seq // blk),
        grid=(nh, batch),
        in_specs=[pl.BlockSpec(memory_space=pltpu.SMEM),
                  head_spec(first), head_spec(first + nh), head_spec(first + 2 * nh),
                  vec_spec(), vec_spec(),
                  pl.BlockSpec((4, DIFF_DK), lambda h, b: (0, 0)),
                  vec_spec(),
                  pl.BlockSpec((2, blk, blk), lambda h, b: (0, 0, 0))],
        out_specs=pl.BlockSpec((None, seq, LANES), lambda h, b: (h, b, 0)),
        out_shape=jax.ShapeDtypeStruct((nh, batch * seq, LANES), BF16),
        scratch_shapes=[pltpu.VMEM((2 * seq, LANES), BF16),
                        pltpu.VMEM((seq, LANES), BF16),
                        pltpu.VMEM((2, 2 * blk, blk), F32),
                        pltpu.VMEM((2 * blk, seq), F32),
                        pltpu.VMEM((2 * blk, seq), BF16)],
        compiler_params=_cparams(("arbitrary", "arbitrary")),
        name="diff_attn",
    )(rel_table, proj, proj, proj,
      jnp.tile(qg, 2).reshape(1, LANES), jnp.tile(kg, 2).reshape(1, LANES),
      lam_vecs, sg.reshape(1, LANES), bidx)


def _outproj_kernel(yr_ref, yd_ref, x_ref, w_hbm, g_ref, wr_ref, br_ref,
                    x1_ref, h2p_ref, lt_ref, wb_s, wstage_s, wr2_s, wsem,
                    *, rows_per_cast, sub_blocks=2):
    ne = lt_ref.shape[0]
    nchunk = wb_s.shape[0] // rows_per_cast

    def w_copy(c):
        return pltpu.make_async_copy(w_hbm.at[pl.ds(c * rows_per_cast, rows_per_cast)],
                                     wstage_s.at[c % 2], wsem.at[c % 2])

    @pl.when(pl.program_id(0) == 0)
    def _stage_weights():
        w_copy(0).start()
        for c in range(nchunk):
            if c + 1 < nchunk:
                w_copy(c + 1).start()
            w_copy(c).wait()
            wb_s[c * rows_per_cast:(c + 1) * rows_per_cast, :] = wstage_s[c % 2].astype(BF16)
        wr = wr_ref[...]
        wr_hi = wr.astype(BF16)
        wr2_s[:, 0:LANES] = wr_hi
        wr2_s[:, LANES:2 * LANES] = (wr - wr_hi.astype(F32)).astype(BF16)

    tm = x_ref.shape[0]
    sub = tm // sub_blocks
    for sb in range(sub_blocks):
        rs = slice(sb * sub, (sb + 1) * sub)
        y = jnp.concatenate([yr_ref[j, rs, :] for j in range(yr_ref.shape[0])]
                            + [yd_ref[j, rs, :] for j in range(yd_ref.shape[0])], axis=-1)
        x1 = x_ref[rs, :] + jnp.dot(y, wb_s[...], preferred_element_type=F32)
        x1_ref[rs, :] = x1
        ms = jnp.mean(x1 * x1, axis=-1, keepdims=True)
        h2 = x1 * lax.rsqrt(ms + EPS) * g_ref[...]
        h_hi = h2.astype(BF16)
        h_lo = (h2 - h_hi.astype(F32)).astype(BF16)
        parts = (jnp.dot(h_hi, wr2_s[...], preferred_element_type=F32)
                 + jnp.dot(h_lo, wr2_s[...], preferred_element_type=F32))
        logits = parts[:, 0:LANES] + parts[:, LANES:2 * LANES]
        lt_ref[:, rs] = logits.T[0:ne, :] + br_ref[...]
        half = h2.shape[1] // 2
        packed = pltpu.pack_elementwise([h2[:, :half], h2[:, half:]], packed_dtype=BF16)
        h2p_ref[rs] = packed.reshape((sub,) + h2p_ref.shape[1:])


def _outproj_call(y_ret, y_diff, x2, w_out, g, w_router, b_router, tm=512):
    t, d = x2.shape
    ne = w_router.shape[1]
    nhr, nhd = y_ret.shape[0], y_diff.shape[0]
    rows_per_cast = min(CAST_ROWS, d)
    wr_pad = jnp.pad(w_router, ((0, 0), (0, LANES - ne)))
    return pl.pallas_call(
        functools.partial(_outproj_kernel, rows_per_cast=rows_per_cast),
        grid=(t // tm,),
        in_specs=[pl.BlockSpec((nhr, tm, LANES), lambda i: (0, i, 0)),
                  pl.BlockSpec((nhd, tm, LANES), lambda i: (0, i, 0)),
                  pl.BlockSpec((tm, d), lambda i: (i, 0)),
                  pl.BlockSpec(memory_space=pl.ANY),
                  pl.BlockSpec((1, d), lambda i: (0, 0)),
                  pl.BlockSpec((d, LANES), lambda i: (0, 0)),
                  pl.BlockSpec((ne, 1), lambda i: (0, 0))],
        out_specs=[pl.BlockSpec((tm, d), lambda i: (i, 0)),
                   pl.BlockSpec((tm, d // 2 // LANES, LANES), lambda i: (i, 0, 0)),
                   pl.BlockSpec((ne, tm), lambda i: (0, i))],
        out_shape=[jax.ShapeDtypeStruct((t, d), F32),
                   jax.ShapeDtypeStruct((t, d // 2 // LANES, LANES), U32),
                   jax.ShapeDtypeStruct((ne, t), F32)],
        scratch_shapes=[pltpu.VMEM((d, d), BF16),
                        pltpu.VMEM((2, rows_per_cast, d), F32),
                        pltpu.VMEM((d, 2 * LANES), BF16),
                        pltpu.SemaphoreType.DMA((2,))],
        compiler_params=_cparams(("arbitrary",)),
        name="out_proj_router",
    )(y_ret, y_diff, x2, w_out, g.reshape(1, d), wr_pad, b_router.reshape(ne, 1))


def _route_kernel(lt_ref, pos_ref, gate_ref, cnt_ref, nt_ref, ot_ref, idx_s, rank_s, *, tb, row_tile):
    ne, t = lt_ref.shape
    e_iota = lax.broadcasted_iota(I32, (ne, tb), 0)
    upper = (lax.broadcasted_iota(I32, (tb, tb), 0)
             < lax.broadcasted_iota(I32, (tb, tb), 1)).astype(BF16)

    def pass_a(i, running):
        cols = pl.ds(pl.multiple_of(i * tb, tb), tb)
        l = lt_ref[:, cols]
        tops, hots = [], []
        for k in range(TOP_K):
            m = jnp.max(l, axis=0, keepdims=True)
            idx = jnp.min(jnp.where(l == m, e_iota, ne), axis=0, keepdims=True)
            hot = e_iota == idx
            l = jnp.where(hot, -jnp.inf, l)
            idx_s[k:k + 1, cols] = idx
            tops.append(m)
            hots.append(hot)
        exps = [jnp.exp(m - tops[0]) for m in tops]
        denom = exps[0] + exps[1] + exps[2] + exps[3]
        for k in range(TOP_K):
            gate_ref[k:k + 1, cols] = exps[k] / denom
        hot_all = jnp.zeros((ne, tb), F32)
        for hot in hots:
            hot_all = hot_all + hot.astype(F32)
        before = running + jnp.dot(hot_all.astype(BF16), upper, preferred_element_type=F32)
        for k in range(TOP_K):
            rank_s[k:k + 1, cols] = jnp.sum(jnp.where(hots[k], before, 0.0), axis=0, keepdims=True)
        return running + jnp.sum(hot_all, axis=1, keepdims=True)

    cnt = lax.fori_loop(0, t // tb, pass_a, jnp.zeros((ne, 1), F32))
    ntile = jnp.floor((cnt + (row_tile - 1.0)) * (1.0 / row_tile))
    lower = (lax.broadcasted_iota(I32, (ne, ne), 1)
             < lax.broadcasted_iota(I32, (ne, ne), 0)).astype(BF16)
    otile = jnp.dot(lower, jnp.broadcast_to(ntile, (ne, LANES)).astype(BF16),
                    preferred_element_type=F32)
    cnt_ref[...] = jnp.broadcast_to(cnt, (ne, LANES)).astype(I32)
    nt_ref[...] = jnp.broadcast_to(ntile, (ne, LANES)).astype(I32)
    ot_ref[...] = otile.astype(I32)
    off_rows = otile[:, 0:1] * float(row_tile)

    def pass_b(i, carry):
        cols = pl.ds(pl.multiple_of(i * tb, tb), tb)
        for k in range(TOP_K):
            hot = e_iota == idx_s[k:k + 1, cols]
            off = jnp.sum(jnp.where(hot, off_rows, 0.0), axis=0, keepdims=True)
            pos_ref[k:k + 1, cols] = (rank_s[k:k + 1, cols] + off).astype(I32)
        return carry

    lax.fori_loop(0, t // tb, pass_b, 0)


def _route_call(logits_t, tb=256, row_tile=ROW_TILE):
    ne, t = logits_t.shape
    return pl.pallas_call(
        functools.partial(_route_kernel, tb=tb, row_tile=row_tile),
        out_shape=[jax.ShapeDtypeStruct((TOP_K, t), I32),
                   jax.ShapeDtypeStruct((TOP_K, t), F32),
                   jax.ShapeDtypeStruct((ne, LANES), I32),
                   jax.ShapeDtypeStruct((ne, LANES), I32),
                   jax.ShapeDtypeStruct((ne, LANES), I32)],
        scratch_shapes=[pltpu.VMEM((TOP_K, t), I32), pltpu.VMEM((TOP_K, t), F32)],
        compiler_params=_cparams(None),
        name="route",
    )(logits_t)


_PAD_PIECES = tuple(ROW_TILE >> (k + 1) for k in range(ROW_TILE.bit_length() - 1))


def _dispatch_kernel(pos_ref, pad_start_ref, pad_n_ref, src_ref, dst_ref, stage_s, zero_s,
                     lsem, ssem, zsem, *, tb, ne):
    i = pl.program_id(0)
    nsteps = pl.num_programs(0)
    slot = i % 2

    def load(step, s):
        return pltpu.make_async_copy(src_ref.at[pl.ds(step * tb, tb)], stage_s.at[s], lsem.at[s])

    def wait_rows(s):
        for k in range(TOP_K):
            pltpu.make_async_copy(stage_s.at[s], dst_ref.at[pl.ds(0, tb)], ssem.at[s]).wait()

    def pad_copies(e):
        n = pad_n_ref[e]
        start = pad_start_ref[e]
        out = []
        for piece in _PAD_PIECES:
            at = start + (n & ~(2 * piece - 1))
            out.append(((n & piece) != 0, pltpu.make_async_copy(
                zero_s.at[pl.ds(0, piece)], dst_ref.at[pl.ds(at, piece)], zsem)))
        return out

    @pl.when(i == 0)
    def _zero_pads():
        zero_s[...] = jnp.zeros(zero_s.shape, zero_s.dtype)

        def issue(e, c):
            for cond, cp in pad_copies(e):
                @pl.when(cond)
                def _():
                    cp.start()
            return c

        def drain(e, c):
            for cond, cp in pad_copies(e):
                @pl.when(cond)
                def _():
                    cp.wait()
            return c

        lax.fori_loop(0, ne, issue, 0)
        lax.fori_loop(0, ne, drain, 0)
        load(0, 0).start()

    @pl.when(i > 0)
    def _():
        wait_rows(1 - slot)

    @pl.when(i + 1 < nsteps)
    def _():
        load(i + 1, 1 - slot).start()

    load(i, slot).wait()

    def issue_rows(tl, c):
        for k in range(TOP_K):
            pltpu.make_async_copy(stage_s.at[slot, tl], dst_ref.at[pos_ref[k, tl]],
                                  ssem.at[slot]).start(priority=k % 2)
        return c

    lax.fori_loop(0, tb, issue_rows, 0, unroll=4)

    @pl.when(i == nsteps - 1)
    def _():
        wait_rows(slot)


def _dispatch_call(pos, pad_start, pad_n, h2p, n_slots, tb=512):
    t = h2p.shape[0]
    row = h2p.shape[1:]
    ne = pad_n.shape[0]
    return pl.pallas_call(
        functools.partial(_dispatch_kernel, tb=tb, ne=ne),
        grid=(t // tb,),
        in_specs=[pl.BlockSpec((TOP_K, tb), lambda i: (0, i), memory_space=pltpu.SMEM),
                  pl.BlockSpec(memory_space=pltpu.SMEM),
                  pl.BlockSpec(memory_space=pltpu.SMEM),
                  pl.BlockSpec(memory_space=pl.ANY)],
        out_specs=pl.BlockSpec(memory_space=pl.ANY),
        out_shape=jax.ShapeDtypeStruct((n_slots, *row), U32),
        scratch_shapes=[pltpu.VMEM((2, tb, *row), U32),
                        pltpu.VMEM((_PAD_PIECES[0], *row), U32),
                        pltpu.SemaphoreType.DMA((2,)),
                        pltpu.SemaphoreType.DMA((2,)),
                        pltpu.SemaphoreType.DMA(())],
        compiler_params=pltpu.CompilerParams(dimension_semantics=("arbitrary",),
                                             vmem_limit_bytes=VMEM_LIMIT, has_side_effects=True),
        name="dispatch",
    )(pos, pad_start, pad_n, h2p)


def _moe_kernel(we_ref, wt_ref, wn_ref, xs_ref, win_ref, bin_ref, wout_ref, bo_ref,
                ys_ref, xu_s, xb_s, acc_s, wg_st, wu_st, wo_st, yst_s, xsem, wsem, ysem,
                *, tmx, fc, nj):
    w = pl.program_id(0)
    nw = pl.num_programs(0)
    nt = wn_ref[w]
    t0 = wt_ref[w]
    half = xu_s.shape[-2] * xu_s.shape[-1]
    de = nj * fc
    nxt = jnp.minimum(w + 1, nw - 1)
    has_next = jnp.logical_and(w + 1 < nw, wn_ref[nxt] > 0)

    def local_rows(r, n=1):
        return pl.ds(pl.multiple_of(r * tmx, tmx), n * tmx)

    def x_copy(item_t0, r):
        src = pl.ds(pl.multiple_of((item_t0 + r) * tmx, tmx), tmx)
        return pltpu.make_async_copy(xs_ref.at[src], xu_s.at[local_rows(r)], xsem)

    def y_copy(r, slot):
        dst = pl.ds(pl.multiple_of((t0 + r) * tmx, tmx), tmx)
        return pltpu.make_async_copy(yst_s.at[slot], ys_ref.at[dst], ysem.at[slot])

    def w_copies(expert, j, slot):
        c0 = pl.multiple_of(j * fc, fc)
        return (pltpu.make_async_copy(win_ref.at[expert, :, pl.ds(c0, fc)], wg_st.at[slot], wsem.at[slot]),
                pltpu.make_async_copy(win_ref.at[expert, :, pl.ds(de + c0, fc)], wu_st.at[slot],
                                      wsem.at[slot]),
                pltpu.make_async_copy(wout_ref.at[expert, pl.ds(c0, fc), :], wo_st.at[slot],
                                      wsem.at[slot]))

    def start_weights(expert, j, slot):
        for cp in w_copies(expert, j, slot):
            cp.start()

    def start_rows(item_t0, item_nt):
        def go(r, c):
            x_copy(item_t0, r).start()
            return c
        lax.fori_loop(0, item_nt, go, 0)

    @pl.when(w == 0)
    def _prologue():
        start_weights(we_ref[0], 0, 0)
        start_rows(t0, nt)

    @pl.when(nt > 0)
    def _work():
        def chunk(j, carry):
            slot = j % 2

            @pl.when(j + 1 < nj)
            def _():
                start_weights(we_ref[w], j + 1, 1 - slot)

            @pl.when(jnp.logical_and(j + 1 == nj, has_next))
            def _():
                start_weights(we_ref[nxt], 0, 1 - slot)

            for cp in w_copies(we_ref[w], j, slot):
                cp.wait()
            stage = (wg_st.at[slot], wu_st.at[slot], wo_st.at[slot])
            biases = (bin_ref[j], bin_ref[nj + j])

            @pl.when(j == 0)
            def _rows_ready():
                def finish(r, c):
                    x_copy(t0, r).wait()
                    return c
                lax.fori_loop(0, nt, finish, 0)

            @pl.when(jnp.logical_and(j == 1, has_next))
            def _():
                start_rows(wt_ref[nxt], wn_ref[nxt])

            @pl.when(j == 0)
            def _():
                _moe_passes(xu_s, xb_s, acc_s, stage, biases, nt, local_rows, half, True)

            @pl.when(j > 0)
            def _():
                _moe_passes(xu_s, xb_s, acc_s, stage, biases, nt, local_rows, half, False)

            return carry

        lax.fori_loop(0, nj, chunk, 0)

        def y_wait(slot):
            y_copy(0, slot).wait()

        prev_nt = jnp.where(w > 0, wn_ref[jnp.maximum(w - 1, 0)], 0)

        @pl.when(prev_nt >= 1)
        def _():
            y_wait((prev_nt - 1) % 2)

        @pl.when(prev_nt >= 2)
        def _():
            y_wait(prev_nt % 2)

        def emit(r, c):
            slot = r % 2

            @pl.when(r >= 2)
            def _():
                y_wait(slot)

            y = acc_s[local_rows(r), :] + bo_ref[...]
            yst_s[slot] = pltpu.pack_elementwise([y[:, :half], y[:, half:]],
                                                 packed_dtype=BF16).reshape(yst_s.shape[1:])
            y_copy(r, slot).start()
            return c

        lax.fori_loop(0, nt, emit, 0)

        @pl.when(jnp.logical_not(has_next))
        def _drain():
            @pl.when(nt >= 2)
            def _():
                y_wait(nt % 2)

            y_wait((nt - 1) % 2)


def _moe_passes(xu_ref, xb_ref, acc_s, stage, biases, nt, local_rows, half, first):
    wg_ref, wu_ref, wo_ref = stage
    bg, bu = biases

    def proj(x_lo, x_hi, w_ref, b):
        return (jnp.dot(x_lo, w_ref[0:half, :].astype(BF16), preferred_element_type=F32)
                + jnp.dot(x_hi, w_ref[half:2 * half, :].astype(BF16), preferred_element_type=F32) + b)

    def rows_step(rows):
        if first:
            xw = xu_ref[rows]
            xw = xw.reshape(xw.shape[0], half)
            x_lo = pltpu.unpack_elementwise(xw, index=0, packed_dtype=BF16,
                                            unpacked_dtype=F32).astype(BF16)
            x_hi = pltpu.unpack_elementwise(xw, index=1, packed_dtype=BF16,
                                            unpacked_dtype=F32).astype(BF16)
            xb_ref[rows, 0:half] = x_lo
            xb_ref[rows, half:2 * half] = x_hi
        else:
            x_lo = xb_ref[rows, 0:half]
            x_hi = xb_ref[rows, half:2 * half]
        gg = jnp.minimum(proj(x_lo, x_hi, wg_ref, bg), SWIGLU_LIMIT)
        uu = jnp.clip(proj(x_lo, x_hi, wu_ref, bu), -SWIGLU_LIMIT, SWIGLU_LIMIT)
        act = (uu + 1.0) * (gg * jax.nn.sigmoid(SWIGLU_ALPHA * gg))
        y = jnp.dot(act.astype(BF16), wo_ref[...].astype(BF16), preferred_element_type=F32)
        if first:
            acc_s[rows, :] = y
        else:
            acc_s[rows, :] += y

    def big(pi, c):
        rows_step(local_rows(PASS_TILES * pi, PASS_TILES))
        return c

    lax.fori_loop(0, nt // PASS_TILES, big, 0)
    piece = PASS_TILES // 2
    while piece >= 1:
        @pl.when((nt & piece) != 0)
        def _(piece=piece):
            rows_step(local_rows(nt & ~(2 * piece - 1), piece))
        piece //= 2


def _moe_call(work_e, work_t0, work_nt, xs, w_in, b_in, w_out, b_out,
              tmx=ROW_TILE, group=GROUP_TILES, fc=F_CHUNK):
    ne, d, de2 = w_in.shape
    de = de2 // 2
    nj = de // fc
    assert nj % 2 == 0, "weight stage slots alternate per chunk and restart at 0 per work item"
    n_work = work_e.shape[0]
    n_slots = xs.shape[0]
    row = xs.shape[1:]

    grid_spec = pltpu.PrefetchScalarGridSpec(
        num_scalar_prefetch=3,
        grid=(n_work,),
        in_specs=[
            pl.BlockSpec(memory_space=pl.ANY),
            pl.BlockSpec(memory_space=pl.ANY),
            pl.BlockSpec((None, 2 * nj, 1, fc), lambda w, we, wt, wn: (we[w], 0, 0, 0)),
            pl.BlockSpec(memory_space=pl.ANY),
            pl.BlockSpec((None, 1, d), lambda w, we, wt, wn: (we[w], 0, 0)),
        ],
        out_specs=pl.BlockSpec(memory_space=pl.ANY),
        scratch_shapes=[pltpu.VMEM((group * tmx, *row), U32),
                        pltpu.VMEM((group * tmx, d), BF16),
                        pltpu.VMEM((group * tmx, d), F32),
                        pltpu.VMEM((2, d, fc), F32),
                        pltpu.VMEM((2, d, fc), F32),
                        pltpu.VMEM((2, fc, d), F32),
                        pltpu.VMEM((2, tmx, *row), U32),
                        pltpu.SemaphoreType.DMA(()),
                        pltpu.SemaphoreType.DMA((2,)),
                        pltpu.SemaphoreType.DMA((2,))],
    )
    return pl.pallas_call(
        functools.partial(_moe_kernel, tmx=tmx, fc=fc, nj=nj),
        grid_spec=grid_spec,
        out_shape=jax.ShapeDtypeStruct((n_slots, *row), U32),
        compiler_params=pltpu.CompilerParams(dimension_semantics=("arbitrary",),
                                             vmem_limit_bytes=VMEM_LIMIT, has_side_effects=True),
        name="moe_experts",
    )(work_e, work_t0, work_nt, xs, w_in, b_in.reshape(ne, 2 * nj, 1, fc), w_out,
      b_out.reshape(ne, 1, d))


def _combine_kernel(pos_ref, pos_next_ref, x1_ref, gate_ref, ys_ref, o_ref, ybuf, sem, *, tb):
    half = ybuf.shape[-2] * ybuf.shape[-1]
    i = pl.program_id(0)
    slot = i % 2

    def gather(p_ref, s):
        def issue(tl, c):
            for k in range(TOP_K):
                pltpu.make_async_copy(ys_ref.at[p_ref[k, tl]], ybuf.at[s, k, tl],
                                      sem.at[s]).start(priority=k % 2)
            return c
        lax.fori_loop(0, tb, issue, 0, unroll=4)

    @pl.when(i == 0)
    def _():
        gather(pos_ref, 0)

    @pl.when(i + 1 < pl.num_programs(0))
    def _():
        gather(pos_next_ref, 1 - slot)

    for k in range(TOP_K):
        pltpu.make_async_copy(ys_ref.at[pl.ds(0, tb)], ybuf.at[slot, k], sem.at[slot]).wait()

    lo = x1_ref[:, 0:half]
    hi = x1_ref[:, half:2 * half]
    for k in range(TOP_K):
        g = gate_ref[:, k:k + 1]
        yw = ybuf[slot, k].reshape(tb, half)
        lo = lo + g * pltpu.unpack_elementwise(yw, index=0, packed_dtype=BF16, unpacked_dtype=F32)
        hi = hi + g * pltpu.unpack_elementwise(yw, index=1, packed_dtype=BF16, unpacked_dtype=F32)
    o_ref[:, 0:half] = lo
    o_ref[:, half:2 * half] = hi


def _combine_call(pos, x1, gate_tk, ys, tb=256):
    t, d = x1.shape
    row = ys.shape[1:]
    last = t // tb - 1
    return pl.pallas_call(
        functools.partial(_combine_kernel, tb=tb),
        grid=(t // tb,),
        in_specs=[pl.BlockSpec((TOP_K, tb), lambda i: (0, i), memory_space=pltpu.SMEM),
                  pl.BlockSpec((TOP_K, tb), lambda i: (0, jnp.minimum(i + 1, last)),
                               memory_space=pltpu.SMEM),
                  pl.BlockSpec((tb, d), lambda i: (i, 0)),
                  pl.BlockSpec((tb, TOP_K), lambda i: (i, 0)),
                  pl.BlockSpec(memory_space=pl.ANY)],
        out_specs=pl.BlockSpec((tb, d), lambda i: (i, 0)),
        out_shape=jax.ShapeDtypeStruct((t, d), F32),
        scratch_shapes=[pltpu.VMEM((2, TOP_K, tb, *row), U32),
                        pltpu.SemaphoreType.DMA((2,))],
        compiler_params=_cparams(("arbitrary",)),
        name="combine",
    )(pos, pos, x1, gate_tk, ys)


def _work_list(ntile, otile, n_work, group):
    ne = ntile.shape[0]
    items = (ntile + group - 1) // group
    ends = jnp.cumsum(items)
    total = ends[-1]
    w = jnp.arange(n_work, dtype=I32)
    valid = w < total
    e_w = jnp.clip(jnp.sum((ends[None, :] <= w[:, None]).astype(I32), axis=1), 0, ne - 1)
    local = w - (ends - items)[e_w]
    t0_w = otile[e_w] + local * group
    nt_w = jnp.where(valid, jnp.clip(ntile[e_w] - local * group, 0, group), 0)
    e_last = e_w[jnp.maximum(total - 1, 0)]
    return (jnp.where(valid, e_w, e_last).astype(I32), jnp.where(valid, t0_w, 0).astype(I32),
            nt_w.astype(I32))


def _rotary_tables(seq, dk):
    inv_freq = ROPE_BASE ** (-jnp.arange(0, dk, 2, dtype=F32) / dk)
    ang = jnp.arange(seq, dtype=I32).astype(F32)[:, None] * inv_freq[None, :]
    cos = jnp.repeat(jnp.cos(ang), 2, axis=1)
    sin = jnp.stack([-jnp.sin(ang), jnp.sin(ang)], axis=-1).reshape(seq, dk)
    return cos, sin


def kernel(x, norm_mix_g, w_mix_in, ret_gn_g, q_norm_g, k_norm_g, lambda_q1, lambda_k1, lambda_q2, lambda_k2, diff_subln_g, rel_bias_table, w_mix_out, norm_ffn_g, w_router, b_router, w_exp_in, b_exp_in, w_exp_out, b_exp_out):
    batch, seq, d = x.shape
    t = batch * seq
    depth = norm_mix_g.shape[0]
    ne = w_router.shape[-1]
    n_tiles_max = (t * TOP_K) // ROW_TILE + ne
    n_work = ne + (n_tiles_max - ne) // GROUP_TILES
    cos, sin = _rotary_tables(seq, HEAD_W)
    log_gamma = jnp.log1p(-(2.0 ** (-5.0 - jnp.arange(RET_HEADS, dtype=F32))))

    x2 = x.reshape(t, d)
    for l in range(depth):
        proj = _inproj_call(x2, norm_mix_g[l], w_mix_in[l])
        y_ret = _ret_call(proj, cos, sin, log_gamma, ret_gn_g[l], batch, seq)
        lam_vecs = jnp.stack([lambda_q1[l], lambda_k1[l], lambda_q2[l], lambda_k2[l]])
        y_diff = _diff_call(proj, rel_bias_table, q_norm_g[l], k_norm_g[l], lam_vecs,
                            diff_subln_g[l], batch, seq)
        x1, h2p, logits_t = _outproj_call(y_ret, y_diff, x2, w_mix_out[l], norm_ffn_g[l],
                                          w_router[l], b_router[l])
        pos, gate, cnt, ntile, otile = _route_call(logits_t)
        cnt, ntile, otile = cnt[:, 0], ntile[:, 0], otile[:, 0]
        work_e, work_t0, work_nt = _work_list(ntile, otile, n_work, GROUP_TILES)
        xs = _dispatch_call(pos, otile * ROW_TILE + cnt, ntile * ROW_TILE - cnt, h2p,
                            n_tiles_max * ROW_TILE)
        ys = _moe_call(work_e, work_t0, work_nt, xs, w_exp_in[l], b_exp_in[l], w_exp_out[l],
                       b_exp_out[l])
        x2 = _combine_call(pos, x1, gate.T, ys)
    return x2.reshape(batch, seq, d)
```

```python
import functools
import math

import jax
import jax.numpy as jnp
from jax import lax
from jax.experimental import pallas as pl
from jax.experimental.pallas import tpu as pltpu

F32 = jnp.float32
BF16 = jnp.bfloat16
I32 = jnp.int32
U32 = jnp.uint32

EPS = 1e-6
CHUNK = 64
RET_HEADS = 8
DIFF_HEADS = 8
HEAD_W = 128
DIFF_DK = 64
ROPE_BASE = 10000.0
NUM_BUCKETS = 32
MAX_DISTANCE = 128
TOP_K = 4
SWIGLU_LIMIT = 7.0
SWIGLU_ALPHA = 1.702
LAM_INIT = 0.8 - 0.6 * math.exp(-0.3 * 0)

LANES = 128
NEG_BIG = -1e30
VMEM_LIMIT = 60 * 1024 * 1024

SEQ_BLK = 256
RET_BLK = 512
ROW_TILE = 256
GROUP_TILES = 6
PASS_TILES = 4
F_CHUNK = 512
CAST_ROWS = 256


def _cparams(sem, vmem=VMEM_LIMIT):
    return pltpu.CompilerParams(dimension_semantics=sem, vmem_limit_bytes=vmem)


def _inproj_kernel(x_ref, g_ref, w_ref, o_ref, wb_ref, *, rows_per_cast, sub_blocks=2):
    @pl.when(pl.program_id(1) == 0)
    def _():
        d = w_ref.shape[0]
        for c in range(d // rows_per_cast):
            sl = slice(c * rows_per_cast, (c + 1) * rows_per_cast)
            wb_ref[sl, :] = w_ref[sl, :].astype(BF16)

    sub = x_ref.shape[0] // sub_blocks
    for sb in range(sub_blocks):
        rs = slice(sb * sub, (sb + 1) * sub)
        x = x_ref[rs, :]
        ms = jnp.mean(x * x, axis=-1, keepdims=True)
        h = (x * lax.rsqrt(ms + EPS) * g_ref[...]).astype(BF16)
        acc = jnp.dot(h, wb_ref[...], preferred_element_type=F32)
        for j in range(o_ref.shape[0]):
            o_ref[j, rs, :] = acc[:, j * LANES:(j + 1) * LANES].astype(o_ref.dtype)


def _inproj_call(x2, g, w, tm=1024, tn=1024):
    t, d = x2.shape
    n = w.shape[1]
    return pl.pallas_call(
        functools.partial(_inproj_kernel, rows_per_cast=min(CAST_ROWS, d)),
        grid=(n // tn, t // tm),
        in_specs=[pl.BlockSpec((tm, d), lambda j, i: (i, 0)),
                  pl.BlockSpec((1, d), lambda j, i: (0, 0)),
                  pl.BlockSpec((d, tn), lambda j, i: (0, j))],
        out_specs=pl.BlockSpec((tn // LANES, tm, LANES), lambda j, i: (j, i, 0)),
        out_shape=jax.ShapeDtypeStruct((n // LANES, t, LANES), BF16),
        scratch_shapes=[pltpu.VMEM((d, tn), BF16)],
        compiler_params=_cparams(("arbitrary", "arbitrary")),
        name="in_proj",
    )(x2, g.reshape(1, d), w)


def _ret_kernel(lg_ref, q_ref, k_ref, v_ref, g_ref, cos_ref, sin_ref, gn_ref, o_ref, *, blk, nblk):
    dk = q_ref.shape[-1]
    lg = lg_ref[pl.program_id(1)]
    row = lax.broadcasted_iota(I32, (blk, blk), 0)
    col = lax.broadcasted_iota(I32, (blk, blk), 1)
    dist = jnp.abs(row - col).astype(F32)
    visible = (col // CHUNK) <= (row // CHUNK)
    dmask = jnp.where(visible, jnp.exp(lg * dist), 0.0)
    rr = lax.broadcasted_iota(I32, (blk, dk), 0).astype(F32)
    qdec = jnp.exp(lg * (rr + 1.0))
    kdec = jnp.exp(lg * (blk - 1.0 - rr))
    bdec = jnp.exp(lg * jnp.full((1, HEAD_W), float(blk), F32))
    even = (lax.broadcasted_iota(I32, (blk, dk), 1) & 1) == 0
    scale = dk ** -0.5

    def body(i, state):
        rows = pl.ds(pl.multiple_of(i * blk, blk), blk)
        cos = cos_ref[rows, :]
        sin = sin_ref[rows, :]

        def rot(x):
            partner = jnp.where(even, pltpu.roll(x, dk - 1, 1), pltpu.roll(x, 1, 1))
            return x * cos + partner * sin

        qr = rot(q_ref[rows, :].astype(F32)) * scale
        kr = rot(k_ref[rows, :].astype(F32))
        v = v_ref[rows, :]
        s = lax.dot_general(qr.astype(BF16), kr.astype(BF16), (((1,), (1,)), ((), ())),
                            preferred_element_type=F32) * dmask
        out = jnp.dot(s.astype(BF16), v, preferred_element_type=F32)
        out = out + jnp.dot((qr * qdec).astype(BF16), state.astype(BF16),
                            preferred_element_type=F32)
        kv = lax.dot_general((kr * kdec).astype(BF16), v, (((0,), (0,)), ((), ())),
                             preferred_element_type=F32)
        state = state * bdec + kv
        ms = jnp.mean(out * out, axis=-1, keepdims=True)
        normed = out * lax.rsqrt(ms + EPS) * gn_ref[...]
        g = g_ref[rows, :].astype(F32)
        o_ref[rows, :] = (g * jax.nn.sigmoid(g) * normed).astype(o_ref.dtype)
        return state

    lax.fori_loop(0, nblk, body, jnp.zeros((dk, HEAD_W), F32), unroll=True)


def _ret_call(proj, cos, sin, log_gamma, gn_g, batch, seq, blk=RET_BLK):
    nh = RET_HEADS

    def head_spec(base):
        return pl.BlockSpec((None, seq, LANES), lambda b, h: (base + h, b, 0))

    return pl.pallas_call(
        functools.partial(_ret_kernel, blk=blk, nblk=seq // blk),
        grid=(batch, nh),
        in_specs=[pl.BlockSpec(memory_space=pltpu.SMEM),
                  head_spec(0), head_spec(nh), head_spec(2 * nh), head_spec(3 * nh),
                  pl.BlockSpec((seq, LANES), lambda b, h: (0, 0)),
                  pl.BlockSpec((seq, LANES), lambda b, h: (0, 0)),
                  pl.BlockSpec((None, 1, LANES), lambda b, h: (h, 0, 0))],
        out_specs=pl.BlockSpec((None, seq, LANES), lambda b, h: (h, b, 0)),
        out_shape=jax.ShapeDtypeStruct((nh, batch * seq, LANES), BF16),
        compiler_params=_cparams(("arbitrary", "arbitrary")),
        name="retention",
    )(log_gamma, proj, proj, proj, proj, cos, sin, gn_g.reshape(nh, 1, LANES))


def _diff_kernel(tbl_ref, q_ref, k_ref, v_ref, qg_ref, kg_ref, lam_ref, sg_ref, bidx_ref, o_ref,
                 qz_s, kn_s, bias_s, s_s, p_s, *, blk, nblk):
    h = pl.program_id(0)
    b = pl.program_id(1)
    far_bucket = NUM_BUCKETS // 2 - 1

    @pl.when(b == 0)
    def _build_bias():
        row = lax.broadcasted_iota(I32, (blk, blk), 0)
        col = lax.broadcasted_iota(I32, (blk, blk), 1)
        visible = (col // CHUNK) <= (row // CHUNK)
        for d in range(2):
            idx = bidx_ref[d]
            bias = jnp.zeros((blk, blk), F32)
            for bucket in range(NUM_BUCKETS):
                bias = jnp.where(idx == bucket, tbl_ref[bucket, h], bias)
            if d == 0:
                bias = jnp.where(visible, bias, NEG_BIG)
            bias_s[d, 0:blk, :] = bias
            bias_s[d, blk:2 * blk, :] = bias

    lo = lax.broadcasted_iota(I32, (blk, HEAD_W), 1) < DIFF_DK
    scale = DIFF_DK ** -0.5
    same_half = ((lax.broadcasted_iota(I32, (HEAD_W, HEAD_W), 0) // DIFF_DK)
                 == (lax.broadcasted_iota(I32, (HEAD_W, HEAD_W), 1) // DIFF_DK)).astype(BF16)

    def half_norm(x, g):
        ms = jnp.dot((x * x).astype(BF16), same_half, preferred_element_type=F32) * (1.0 / DIFF_DK)
        return x * lax.rsqrt(ms + EPS) * g

    def prep(i, carry):
        rows = pl.ds(pl.multiple_of(i * blk, blk), blk)
        qn = half_norm(q_ref[rows, :].astype(F32), qg_ref[...]) * scale
        kn = half_norm(k_ref[rows, :].astype(F32), kg_ref[...])
        base = pl.multiple_of(i * 2 * blk, 2 * blk)
        qz_s[pl.ds(base, blk), :] = jnp.where(lo, qn, 0.0).astype(BF16)
        qz_s[pl.ds(base + blk, blk), :] = jnp.where(lo, 0.0, qn).astype(BF16)
        kn_s[rows, :] = kn.astype(BF16)
        return carry

    lax.fori_loop(0, nblk, prep, 0, unroll=True)

    lam = (jnp.exp(jnp.sum(lam_ref[0:1, :] * lam_ref[1:2, :], axis=-1, keepdims=True))
           - jnp.exp(jnp.sum(lam_ref[2:3, :] * lam_ref[3:4, :], axis=-1, keepdims=True))
           + LAM_INIT)
    c_far = tbl_ref[far_bucket, h]

    def lane_fold(x, op):
        out = x[:, 0:LANES]
        for c in range(1, blk // LANES):
            out = op(out, x[:, c * LANES:(c + 1) * LANES])
        return out

    for i in range(nblk):
        qz = qz_s[i * 2 * blk:(i + 1) * 2 * blk, :]
        m_t = jnp.full((2 * blk, LANES), NEG_BIG, F32)
        for j in range(i + 1):
            keys = slice(j * blk, (j + 1) * blk)
            s = lax.dot_general(qz, kn_s[keys, :], (((1,), (1,)), ((), ())),
                                preferred_element_type=F32)
            s = s + (bias_s[0] if j == i else bias_s[1] if j == i - 1 else c_far)
            s_s[:, keys] = s
            m_t = jnp.maximum(m_t, lane_fold(s, jnp.maximum))
        m = jnp.max(m_t, axis=-1, keepdims=True)
        l_t = jnp.zeros((2 * blk, LANES), F32)
        for j in range(i + 1):
            keys = slice(j * blk, (j + 1) * blk)
            p = jnp.exp(s_s[:, keys] - m)
            l_t = l_t + lane_fold(p, jnp.add)
            p_s[:, keys] = p.astype(BF16)
        l = jnp.sum(l_t, axis=-1, keepdims=True)
        kend = (i + 1) * blk
        o = jnp.dot(p_s[:, 0:kend], v_ref[0:kend, :], preferred_element_type=F32) / l
        att = o[0:blk, :] - lam * o[blk:2 * blk, :]
        ms = jnp.mean(att * att, axis=-1, keepdims=True)
        y = att * lax.rsqrt(ms + EPS) * sg_ref[...] * (1.0 - LAM_INIT)
        o_ref[i * blk:(i + 1) * blk, :] = y.astype(o_ref.dtype)


def _t5_bucket(rel):
    nb = NUM_BUCKETS // 2
    max_exact = nb // 2
    base = jnp.where(rel > 0, nb, 0)
    n = jnp.abs(rel)
    large = max_exact + (jnp.log(jnp.maximum(n, 1).astype(jnp.float32) / max_exact)
                         / math.log(MAX_DISTANCE / max_exact) * (nb - max_exact)).astype(jnp.int32)
    large = jnp.minimum(large, nb - 1)
    return base + jnp.where(n < max_exact, n, large)


def _diff_call(proj, rel_table, qg, kg, lam_vecs, sg, batch, seq, blk=SEQ_BLK):
    nh = DIFF_HEADS
    first = 4 * RET_HEADS
    r = jnp.arange(blk, dtype=I32)
    rel0 = r[None, :] - r[:, None]
    bidx = jnp.stack([_t5_bucket(rel0), _t5_bucket(rel0 - blk)]).astype(I32) & (NUM_BUCKETS - 1)

    def head_spec(base):
        return pl.BlockSpec((None, seq, LANES), lambda h, b: (base + h, b, 0))

    def vec_spec():
        return pl.BlockSpec((1, LANES), lambda h, b: (0, 0))

    return pl.pallas_call(
        functools.partial(_diff_kernel, blk=blk, nblk=seq // blk),
        grid=(nh, batch),
        in_specs=[pl.BlockSpec(memory_space=pltpu.SMEM),
                  head_spec(first), head_spec(first + nh), head_spec(first + 2 * nh),
                  vec_spec(), vec_spec(),
                  pl.BlockSpec((4, DIFF_DK), lambda h, b: (0, 0)),
                  vec_spec(),
                  pl.BlockSpec((2, blk, blk), lambda h, b: (0, 0, 0))],
        out_specs=pl.BlockSpec((None, seq, LANES), lambda h, b: (h, b, 0)),
        out_shape=jax.ShapeDtypeStruct((nh, batch * seq, LANES), BF16),
        scratch_shapes=[pltpu.VMEM((2 * seq, LANES), BF16),
                        pltpu.VMEM((seq, LANES), BF16),
                        pltpu.VMEM((2, 2 * blk, blk), F32),
                        pltpu.VMEM((2 * blk, seq), F32),
                        pltpu.VMEM((2 * blk, seq), BF16)],
        compiler_params=_cparams(("arbitrary", "arbitrary")),
        name="diff_attn",
    )(rel_table, proj, proj, proj,
      jnp.tile(qg, 2).reshape(1, LANES), jnp.tile(kg, 2).reshape(1, LANES),
      lam_vecs, sg.reshape(1, LANES), bidx)


def _outproj_kernel(yr_ref, yd_ref, x_ref, w_hbm, g_ref, wr_ref, br_ref,
                    x1_ref, h2p_ref, lt_ref, wb_s, wstage_s, wr2_s, wsem,
                    *, rows_per_cast, sub_blocks=2):
    ne = lt_ref.shape[0]
    nchunk = wb_s.shape[0] // rows_per_cast

    def w_copy(c):
        return pltpu.make_async_copy(w_hbm.at[pl.ds(c * rows_per_cast, rows_per_cast)],
                                     wstage_s.at[c % 2], wsem.at[c % 2])

    @pl.when(pl.program_id(0) == 0)
    def _stage_weights():
        w_copy(0).start()
        for c in range(nchunk):
            if c + 1 < nchunk:
                w_copy(c + 1).start()
            w_copy(c).wait()
            wb_s[c * rows_per_cast:(c + 1) * rows_per_cast, :] = wstage_s[c % 2].astype(BF16)
        wr = wr_ref[...]
        wr_hi = wr.astype(BF16)
        wr2_s[:, 0:LANES] = wr_hi
        wr2_s[:, LANES:2 * LANES] = (wr - wr_hi.astype(F32)).astype(BF16)

    tm = x_ref.shape[0]
    sub = tm // sub_blocks
    for sb in range(sub_blocks):
        rs = slice(sb * sub, (sb + 1) * sub)
        y = jnp.concatenate([yr_ref[j, rs, :] for j in range(yr_ref.shape[0])]
                            + [yd_ref[j, rs, :] for j in range(yd_ref.shape[0])], axis=-1)
        x1 = x_ref[rs, :] + jnp.dot(y, wb_s[...], preferred_element_type=F32)
        x1_ref[rs, :] = x1
        ms = jnp.mean(x1 * x1, axis=-1, keepdims=True)
        h2 = x1 * lax.rsqrt(ms + EPS) * g_ref[...]
        h_hi = h2.astype(BF16)
        h_lo = (h2 - h_hi.astype(F32)).astype(BF16)
        parts = (jnp.dot(h_hi, wr2_s[...], preferred_element_type=F32)
                 + jnp.dot(h_lo, wr2_s[...], preferred_element_type=F32))
        logits = parts[:, 0:LANES] + parts[:, LANES:2 * LANES]
        lt_ref[:, rs] = logits.T[0:ne, :] + br_ref[...]
        half = h2.shape[1] // 2
        packed = pltpu.pack_elementwise([h2[:, :half], h2[:, half:]], packed_dtype=BF16)
        h2p_ref[rs] = packed.reshape((sub,) + h2p_ref.shape[1:])


def _outproj_call(y_ret, y_diff, x2, w_out, g, w_router, b_router, tm=512):
    t, d = x2.shape
    ne = w_router.shape[1]
    nhr, nhd = y_ret.shape[0], y_diff.shape[0]
    rows_per_cast = min(CAST_ROWS, d)
    wr_pad = jnp.pad(w_router, ((0, 0), (0, LANES - ne)))
    return pl.pallas_call(
        functools.partial(_outproj_kernel, rows_per_cast=rows_per_cast),
        grid=(t // tm,),
        in_specs=[pl.BlockSpec((nhr, tm, LANES), lambda i: (0, i, 0)),
                  pl.BlockSpec((nhd, tm, LANES), lambda i: (0, i, 0)),
                  pl.BlockSpec((tm, d), lambda i: (i, 0)),
                  pl.BlockSpec(memory_space=pl.ANY),
                  pl.BlockSpec((1, d), lambda i: (0, 0)),
                  pl.BlockSpec((d, LANES), lambda i: (0, 0)),
                  pl.BlockSpec((ne, 1), lambda i: (0, 0))],
        out_specs=[pl.BlockSpec((tm, d), lambda i: (i, 0)),
                   pl.BlockSpec((tm, d // 2 // LANES, LANES), lambda i: (i, 0, 0)),
                   pl.BlockSpec((ne, tm), lambda i: (0, i))],
        out_shape=[jax.ShapeDtypeStruct((t, d), F32),
                   jax.ShapeDtypeStruct((t, d // 2 // LANES, LANES), U32),
                   jax.ShapeDtypeStruct((ne, t), F32)],
        scratch_shapes=[pltpu.VMEM((d, d), BF16),
                        pltpu.VMEM((2, rows_per_cast, d), F32),
                        pltpu.VMEM((d, 2 * LANES), BF16),
                        pltpu.SemaphoreType.DMA((2,))],
        compiler_params=_cparams(("arbitrary",)),
        name="out_proj_router",
    )(y_ret, y_diff, x2, w_out, g.reshape(1, d), wr_pad, b_router.reshape(ne, 1))


def _route_kernel(lt_ref, pos_ref, gate_ref, cnt_ref, nt_ref, ot_ref, idx_s, rank_s, *, tb, row_tile):
    ne, t = lt_ref.shape
    e_iota = lax.broadcasted_iota(I32, (ne, tb), 0)
    upper = (lax.broadcasted_iota(I32, (tb, tb), 0)
             < lax.broadcasted_iota(I32, (tb, tb), 1)).astype(BF16)

    def pass_a(i, running):
        cols = pl.ds(pl.multiple_of(i * tb, tb), tb)
        l = lt_ref[:, cols]
        tops, hots = [], []
        for k in range(TOP_K):
            m = jnp.max(l, axis=0, keepdims=True)
            idx = jnp.min(jnp.where(l == m, e_iota, ne), axis=0, keepdims=True)
            hot = e_iota == idx
            l = jnp.where(hot, -jnp.inf, l)
            idx_s[k:k + 1, cols] = idx
            tops.append(m)
            hots.append(hot)
        exps = [jnp.exp(m - tops[0]) for m in tops]
        denom = exps[0] + exps[1] + exps[2] + exps[3]
        for k in range(TOP_K):
            gate_ref[k:k + 1, cols] = exps[k] / denom
        hot_all = jnp.zeros((ne, tb), F32)
        for hot in hots:
            hot_all = hot_all + hot.astype(F32)
        before = running + jnp.dot(hot_all.astype(BF16), upper, preferred_element_type=F32)
        for k in range(TOP_K):
            rank_s[k:k + 1, cols] = jnp.sum(jnp.where(hots[k], before, 0.0), axis=0, keepdims=True)
        return running + jnp.sum(hot_all, axis=1, keepdims=True)

    cnt = lax.fori_loop(0, t // tb, pass_a, jnp.zeros((ne, 1), F32))
    ntile = jnp.floor((cnt + (row_tile - 1.0)) * (1.0 / row_tile))
    lower = (lax.broadcasted_iota(I32, (ne, ne), 1)
             < lax.broadcasted_iota(I32, (ne, ne), 0)).astype(BF16)
    otile = jnp.dot(lower, jnp.broadcast_to(ntile, (ne, LANES)).astype(BF16),
                    preferred_element_type=F32)
    cnt_ref[...] = jnp.broadcast_to(cnt, (ne, LANES)).astype(I32)
    nt_ref[...] = jnp.broadcast_to(ntile, (ne, LANES)).astype(I32)
    ot_ref[...] = otile.astype(I32)
    off_rows = otile[:, 0:1] * float(row_tile)

    def pass_b(i, carry):
        cols = pl.ds(pl.multiple_of(i * tb, tb), tb)
        for k in range(TOP_K):
            hot = e_iota == idx_s[k:k + 1, cols]
            off = jnp.sum(jnp.where(hot, off_rows, 0.0), axis=0, keepdims=True)
            pos_ref[k:k + 1, cols] = (rank_s[k:k + 1, cols] + off).astype(I32)
        return carry

    lax.fori_loop(0, t // tb, pass_b, 0)


def _route_call(logits_t, tb=256, row_tile=ROW_TILE):
    ne, t = logits_t.shape
    return pl.pallas_call(
        functools.partial(_route_kernel, tb=tb, row_tile=row_tile),
        out_shape=[jax.ShapeDtypeStruct((TOP_K, t), I32),
                   jax.ShapeDtypeStruct((TOP_K, t), F32),
                   jax.ShapeDtypeStruct((ne, LANES), I32),
                   jax.ShapeDtypeStruct((ne, LANES), I32),
                   jax.ShapeDtypeStruct((ne, LANES), I32)],
        scratch_shapes=[pltpu.VMEM((TOP_K, t), I32), pltpu.VMEM((TOP_K, t), F32)],
        compiler_params=_cparams(None),
        name="route",
    )(logits_t)


_PAD_PIECES = tuple(ROW_TILE >> (k + 1) for k in range(ROW_TILE.bit_length() - 1))


def _dispatch_kernel(pos_ref, pad_start_ref, pad_n_ref, src_ref, dst_ref, stage_s, zero_s,
                     lsem, ssem, zsem, *, tb, ne):
    i = pl.program_id(0)
    nsteps = pl.num_programs(0)
    slot = i % 2

    def load(step, s):
        return pltpu.make_async_copy(src_ref.at[pl.ds(step * tb, tb)], stage_s.at[s], lsem.at[s])

    def wait_rows(s):
        for k in range(TOP_K):
            pltpu.make_async_copy(stage_s.at[s], dst_ref.at[pl.ds(0, tb)], ssem.at[s]).wait()

    def pad_copies(e):
        n = pad_n_ref[e]
        start = pad_start_ref[e]
        out = []
        for piece in _PAD_PIECES:
            at = start + (n & ~(2 * piece - 1))
            out.append(((n & piece) != 0, pltpu.make_async_copy(
                zero_s.at[pl.ds(0, piece)], dst_ref.at[pl.ds(at, piece)], zsem)))
        return out

    @pl.when(i == 0)
    def _zero_pads():
        zero_s[...] = jnp.zeros(zero_s.shape, zero_s.dtype)

        def issue(e, c):
            for cond, cp in pad_copies(e):
                @pl.when(cond)
                def _():
                    cp.start()
            return c

        def drain(e, c):
            for cond, cp in pad_copies(e):
                @pl.when(cond)
                def _():
                    cp.wait()
            return c

        lax.fori_loop(0, ne, issue, 0)
        lax.fori_loop(0, ne, drain, 0)
        load(0, 0).start()

    @pl.when(i > 0)
    def _():
        wait_rows(1 - slot)

    @pl.when(i + 1 < nsteps)
    def _():
        load(i + 1, 1 - slot).start()

    load(i, slot).wait()

    def issue_rows(tl, c):
        for k in range(TOP_K):
            pltpu.make_async_copy(stage_s.at[slot, tl], dst_ref.at[pos_ref[k, tl]],
                                  ssem.at[slot]).start(priority=k % 2)
        return c

    lax.fori_loop(0, tb, issue_rows, 0, unroll=4)

    @pl.when(i == nsteps - 1)
    def _():
        wait_rows(slot)


def _dispatch_call(pos, pad_start, pad_n, h2p, n_slots, tb=512):
    t = h2p.shape[0]
    row = h2p.shape[1:]
    ne = pad_n.shape[0]
    return pl.pallas_call(
        functools.partial(_dispatch_kernel, tb=tb, ne=ne),
        grid=(t // tb,),
        in_specs=[pl.BlockSpec((TOP_K, tb), lambda i: (0, i), memory_space=pltpu.SMEM),
                  pl.BlockSpec(memory_space=pltpu.SMEM),
                  pl.BlockSpec(memory_space=pltpu.SMEM),
                  pl.BlockSpec(memory_space=pl.ANY)],
        out_specs=pl.BlockSpec(memory_space=pl.ANY),
        out_shape=jax.ShapeDtypeStruct((n_slots, *row), U32),
        scratch_shapes=[pltpu.VMEM((2, tb, *row), U32),
                        pltpu.VMEM((_PAD_PIECES[0], *row), U32),
                        pltpu.SemaphoreType.DMA((2,)),
                        pltpu.SemaphoreType.DMA((2,)),
                        pltpu.SemaphoreType.DMA(())],
        compiler_params=pltpu.CompilerParams(dimension_semantics=("arbitrary",),
                                             vmem_limit_bytes=VMEM_LIMIT, has_side_effects=True),
        name="dispatch",
    )(pos, pad_start, pad_n, h2p)


def _moe_kernel(we_ref, wt_ref, wn_ref, xs_ref, win_ref, bin_ref, wout_ref, bo_ref,
                ys_ref, xu_s, xb_s, acc_s, wg_st, wu_st, wo_st, yst_s, xsem, wsem, ysem,
                *, tmx, fc, nj):
    w = pl.program_id(0)
    nw = pl.num_programs(0)
    nt = wn_ref[w]
    t0 = wt_ref[w]
    half = xu_s.shape[-2] * xu_s.shape[-1]
    de = nj * fc
    nxt = jnp.minimum(w + 1, nw - 1)
    has_next = jnp.logical_and(w + 1 < nw, wn_ref[nxt] > 0)

    def local_rows(r, n=1):
        return pl.ds(pl.multiple_of(r * tmx, tmx), n * tmx)

    def x_copy(item_t0, r):
        src = pl.ds(pl.multiple_of((item_t0 + r) * tmx, tmx), tmx)
        return pltpu.make_async_copy(xs_ref.at[src], xu_s.at[local_rows(r)], xsem)

    def y_copy(r, slot):
        dst = pl.ds(pl.multiple_of((t0 + r) * tmx, tmx), tmx)
        return pltpu.make_async_copy(yst_s.at[slot], ys_ref.at[dst], ysem.at[slot])

    def w_copies(expert, j, slot):
        c0 = pl.multiple_of(j * fc, fc)
        return (pltpu.make_async_copy(win_ref.at[expert, :, pl.ds(c0, fc)], wg_st.at[slot], wsem.at[slot]),
                pltpu.make_async_copy(win_ref.at[expert, :, pl.ds(de + c0, fc)], wu_st.at[slot],
                                      wsem.at[slot]),
                pltpu.make_async_copy(wout_ref.at[expert, pl.ds(c0, fc), :], wo_st.at[slot],
                                      wsem.at[slot]))

    def start_weights(expert, j, slot):
        for cp in w_copies(expert, j, slot):
            cp.start()

    def start_rows(item_t0, item_nt):
        def go(r, c):
            x_copy(item_t0, r).start()
            return c
        lax.fori_loop(0, item_nt, go, 0)

    @pl.when(w == 0)
    def _prologue():
        start_weights(we_ref[0], 0, 0)
        start_rows(t0, nt)

    @pl.when(nt > 0)
    def _work():
        def chunk(j, carry):
            slot = j % 2

            @pl.when(j + 1 < nj)
            def _():
                start_weights(we_ref[w], j + 1, 1 - slot)

            @pl.when(jnp.logical_and(j + 1 == nj, has_next))
            def _():
                start_weights(we_ref[nxt], 0, 1 - slot)

            for cp in w_copies(we_ref[w], j, slot):
                cp.wait()
            stage = (wg_st.at[slot], wu_st.at[slot], wo_st.at[slot])
            biases = (bin_ref[j], bin_ref[nj + j])

            @pl.when(j == 0)
            def _rows_ready():
                def finish(r, c):
                    x_copy(t0, r).wait()
                    return c
                lax.fori_loop(0, nt, finish, 0)

            @pl.when(jnp.logical_and(j == 1, has_next))
            def _():
                start_rows(wt_ref[nxt], wn_ref[nxt])

            @pl.when(j == 0)
            def _():
                _moe_passes(xu_s, xb_s, acc_s, stage, biases, nt, local_rows, half, True)

            @pl.when(j > 0)
            def _():
                _moe_passes(xu_s, xb_s, acc_s, stage, biases, nt, local_rows, half, False)

            return carry

        lax.fori_loop(0, nj, chunk, 0)

        def y_wait(slot):
            y_copy(0, slot).wait()

        prev_nt = jnp.where(w > 0, wn_ref[jnp.maximum(w - 1, 0)], 0)

        @pl.when(prev_nt >= 1)
        def _():
            y_wait((prev_nt - 1) % 2)

        @pl.when(prev_nt >= 2)
        def _():
            y_wait(prev_nt % 2)

        def emit(r, c):
            slot = r % 2

            @pl.when(r >= 2)
            def _():
                y_wait(slot)

            y = acc_s[local_rows(r), :] + bo_ref[...]
            yst_s[slot] = pltpu.pack_elementwise([y[:, :half], y[:, half:]],
                                                 packed_dtype=BF16).reshape(yst_s.shape[1:])
            y_copy(r, slot).start()
            return c

        lax.fori_loop(0, nt, emit, 0)

        @pl.when(jnp.logical_not(has_next))
        def _drain():
            @pl.when(nt >= 2)
            def _():
                y_wait(nt % 2)

            y_wait((nt - 1) % 2)


def _moe_passes(xu_ref, xb_ref, acc_s, stage, biases, nt, local_rows, half, first):
    wg_ref, wu_ref, wo_ref = stage
    bg, bu = biases

    def proj(x_lo, x_hi, w_ref, b):
        return (jnp.dot(x_lo, w_ref[0:half, :].astype(BF16), preferred_element_type=F32)
                + jnp.dot(x_hi, w_ref[half:2 * half, :].astype(BF16), preferred_element_type=F32) + b)

    def rows_step(rows):
        if first:
            xw = xu_ref[rows]
            xw = xw.reshape(xw.shape[0], half)
            x_lo = pltpu.unpack_elementwise(xw, index=0, packed_dtype=BF16,
                                            unpacked_dtype=F32).astype(BF16)
            x_hi = pltpu.unpack_elementwise(xw, index=1, packed_dtype=BF16,
                                            unpacked_dtype=F32).astype(BF16)
            xb_ref[rows, 0:half] = x_lo
            xb_ref[rows, half:2 * half] = x_hi
        else:
            x_lo = xb_ref[rows, 0:half]
            x_hi = xb_ref[rows, half:2 * half]
        gg = jnp.minimum(proj(x_lo, x_hi, wg_ref, bg), SWIGLU_LIMIT)
        uu = jnp.clip(proj(x_lo, x_hi, wu_ref, bu), -SWIGLU_LIMIT, SWIGLU_LIMIT)
        act = (uu + 1.0) * (gg * jax.nn.sigmoid(SWIGLU_ALPHA * gg))
        y = jnp.dot(act.astype(BF16), wo_ref[...].astype(BF16), preferred_element_type=F32)
        if first:
            acc_s[rows, :] = y
        else:
            acc_s[rows, :] += y

    def big(pi, c):
        rows_step(local_rows(PASS_TILES * pi, PASS_TILES))
        return c

    lax.fori_loop(0, nt // PASS_TILES, big, 0)
    piece = PASS_TILES // 2
    while piece >= 1:
        @pl.when((nt & piece) != 0)
        def _(piece=piece):
            rows_step(local_rows(nt & ~(2 * piece - 1), piece))
        piece //= 2


def _moe_call(work_e, work_t0, work_nt, xs, w_in, b_in, w_out, b_out,
              tmx=ROW_TILE, group=GROUP_TILES, fc=F_CHUNK):
    ne, d, de2 = w_in.shape
    de = de2 // 2
    nj = de // fc
    assert nj % 2 == 0, "weight stage slots alternate per chunk and restart at 0 per work item"
    n_work = work_e.shape[0]
    n_slots = xs.shape[0]
    row = xs.shape[1:]

    grid_spec = pltpu.PrefetchScalarGridSpec(
        num_scalar_prefetch=3,
        grid=(n_work,),
        in_specs=[
            pl.BlockSpec(memory_space=pl.ANY),
            pl.BlockSpec(memory_space=pl.ANY),
            pl.BlockSpec((None, 2 * nj, 1, fc), lambda w, we, wt, wn: (we[w], 0, 0, 0)),
            pl.BlockSpec(memory_space=pl.ANY),
            pl.BlockSpec((None, 1, d), lambda w, we, wt, wn: (we[w], 0, 0)),
        ],
        out_specs=pl.BlockSpec(memory_space=pl.ANY),
        scratch_shapes=[pltpu.VMEM((group * tmx, *row), U32),
                        pltpu.VMEM((group * tmx, d), BF16),
                        pltpu.VMEM((group * tmx, d), F32),
                        pltpu.VMEM((2, d, fc), F32),
                        pltpu.VMEM((2, d, fc), F32),
                        pltpu.VMEM((2, fc, d), F32),
                        pltpu.VMEM((2, tmx, *row), U32),
                        pltpu.SemaphoreType.DMA(()),
                        pltpu.SemaphoreType.DMA((2,)),
                        pltpu.SemaphoreType.DMA((2,))],
    )
    return pl.pallas_call(
        functools.partial(_moe_kernel, tmx=tmx, fc=fc, nj=nj),
        grid_spec=grid_spec,
        out_shape=jax.ShapeDtypeStruct((n_slots, *row), U32),
        compiler_params=pltpu.CompilerParams(dimension_semantics=("arbitrary",),
                                             vmem_limit_bytes=VMEM_LIMIT, has_side_effects=True),
        name="moe_experts",
    )(work_e, work_t0, work_nt, xs, w_in, b_in.reshape(ne, 2 * nj, 1, fc), w_out,
      b_out.reshape(ne, 1, d))


def _combine_kernel(pos_ref, pos_next_ref, x1_ref, gate_ref, ys_ref, o_ref, ybuf, sem, *, tb):
    half = ybuf.shape[-2] * ybuf.shape[-1]
    i = pl.program_id(0)
    slot = i % 2

    def gather(p_ref, s):
        def issue(tl, c):
            for k in range(TOP_K):
                pltpu.make_async_copy(ys_ref.at[p_ref[k, tl]], ybuf.at[s, k, tl],
                                      sem.at[s]).start(priority=k % 2)
            return c
        lax.fori_loop(0, tb, issue, 0, unroll=4)

    @pl.when(i == 0)
    def _():
        gather(pos_ref, 0)

    @pl.when(i + 1 < pl.num_programs(0))
    def _():
        gather(pos_next_ref, 1 - slot)

    for k in range(TOP_K):
        pltpu.make_async_copy(ys_ref.at[pl.ds(0, tb)], ybuf.at[slot, k], sem.at[slot]).wait()

    lo = x1_ref[:, 0:half]
    hi = x1_ref[:, half:2 * half]
    for k in range(TOP_K):
        g = gate_ref[:, k:k + 1]
        yw = ybuf[slot, k].reshape(tb, half)
        lo = lo + g * pltpu.unpack_elementwise(yw, index=0, packed_dtype=BF16, unpacked_dtype=F32)
        hi = hi + g * pltpu.unpack_elementwise(yw, index=1, packed_dtype=BF16, unpacked_dtype=F32)
    o_ref[:, 0:half] = lo
    o_ref[:, half:2 * half] = hi


def _combine_call(pos, x1, gate_tk, ys, tb=256):
    t, d = x1.shape
    row = ys.shape[1:]
    last = t // tb - 1
    return pl.pallas_call(
        functools.partial(_combine_kernel, tb=tb),
        grid=(t // tb,),
        in_specs=[pl.BlockSpec((TOP_K, tb), lambda i: (0, i), memory_space=pltpu.SMEM),
                  pl.BlockSpec((TOP_K, tb), lambda i: (0, jnp.minimum(i + 1, last)),
                               memory_space=pltpu.SMEM),
                  pl.BlockSpec((tb, d), lambda i: (i, 0)),
                  pl.BlockSpec((tb, TOP_K), lambda i: (i, 0)),
                  pl.BlockSpec(memory_space=pl.ANY)],
        out_specs=pl.BlockSpec((tb, d), lambda i: (i, 0)),
        out_shape=jax.ShapeDtypeStruct((t, d), F32),
        scratch_shapes=[pltpu.VMEM((2, TOP_K, tb, *row), U32),
                        pltpu.SemaphoreType.DMA((2,))],
        compiler_params=_cparams(("arbitrary",)),
        name="combine",
    )(pos, pos, x1, gate_tk, ys)


def _work_list(ntile, otile, n_work, group):
    ne = ntile.shape[0]
    items = (ntile + group - 1) // group
    ends = jnp.cumsum(items)
    total = ends[-1]
    w = jnp.arange(n_work, dtype=I32)
    valid = w < total
    e_w = jnp.clip(jnp.sum((ends[None, :] <= w[:, None]).astype(I32), axis=1), 0, ne - 1)
    local = w - (ends - items)[e_w]
    t0_w = otile[e_w] + local * group
    nt_w = jnp.where(valid, jnp.clip(ntile[e_w] - local * group, 0, group), 0)
    e_last = e_w[jnp.maximum(total - 1, 0)]
    return (jnp.where(valid, e_w, e_last).astype(I32), jnp.where(valid, t0_w, 0).astype(I32),
            nt_w.astype(I32))


def _rotary_tables(seq, dk):
    inv_freq = ROPE_BASE ** (-jnp.arange(0, dk, 2, dtype=F32) / dk)
    ang = jnp.arange(seq, dtype=I32).astype(F32)[:, None] * inv_freq[None, :]
    cos = jnp.repeat(jnp.cos(ang), 2, axis=1)
    sin = jnp.stack([-jnp.sin(ang), jnp.sin(ang)], axis=-1).reshape(seq, dk)
    return cos, sin


def kernel(x, norm_mix_g, w_mix_in, ret_gn_g, q_norm_g, k_norm_g, lambda_q1, lambda_k1, lambda_q2, lambda_k2, diff_subln_g, rel_bias_table, w_mix_out, norm_ffn_g, w_router, b_router, w_exp_in, b_exp_in, w_exp_out, b_exp_out):
    batch, seq, d = x.shape
    t = batch * seq
    depth = norm_mix_g.shape[0]
    ne = w_router.shape[-1]
    n_tiles_max = (t * TOP_K) // ROW_TILE + ne
    n_work = ne + (n_tiles_max - ne) // GROUP_TILES
    cos, sin = _rotary_tables(seq, HEAD_W)
    log_gamma = jnp.log1p(-(2.0 ** (-5.0 - jnp.arange(RET_HEADS, dtype=F32))))

    x2 = x.reshape(t, d)
    for l in range(depth):
        proj = _inproj_call(x2, norm_mix_g[l], w_mix_in[l])
        y_ret = _ret_call(proj, cos, sin, log_gamma, ret_gn_g[l], batch, seq)
        lam_vecs = jnp.stack([lambda_q1[l], lambda_k1[l], lambda_q2[l], lambda_k2[l]])
        y_diff = _diff_call(proj, rel_bias_table, q_norm_g[l], k_norm_g[l], lam_vecs,
                            diff_subln_g[l], batch, seq)
        x1, h2p, logits_t = _outproj_call(y_ret, y_diff, x2, w_mix_out[l], norm_ffn_g[l],
                                          w_router[l], b_router[l])
        pos, gate, cnt, ntile, otile = _route_call(logits_t)
        cnt, ntile, otile = cnt[:, 0], ntile[:, 0], otile[:, 0]
        work_e, work_t0, work_nt = _work_list(ntile, otile, n_work, GROUP_TILES)
        xs = _dispatch_call(pos, otile * ROW_TILE + cnt, ntile * ROW_TILE - cnt, h2p,
                            n_tiles_max * ROW_TILE)
        ys = _moe_call(work_e, work_t0, work_nt, xs, w_exp_in[l], b_exp_in[l], w_exp_out[l],
                       b_exp_out[l])
        x2 = _combine_call(pos, x1, gate.T, ys)
    return x2.reshape(batch, seq, d)
```

```python
import functools
import math

import jax
import jax.numpy as jnp
from jax import lax
from jax.experimental import pallas as pl
from jax.experimental.pallas import tpu as pltpu

F32 = jnp.float32
BF16 = jnp.bfloat16
I32 = jnp.int32
U32 = jnp.uint32

EPS = 1e-6
CHUNK = 64
RET_HEADS = 8
DIFF_HEADS = 8
HEAD_W = 128
DIFF_DK = 64
ROPE_BASE = 10000.0
NUM_BUCKETS = 32
MAX_DISTANCE = 128
TOP_K = 4
SWIGLU_LIMIT = 7.0
SWIGLU_ALPHA = 1.702
LAM_INIT = 0.8 - 0.6 * math.exp(-0.3 * 0)

LANES = 128
NEG_BIG = -1e30
VMEM_LIMIT = 60 * 1024 * 1024

SEQ_BLK = 256
RET_BLK = 512
ROW_TILE = 256
GROUP_TILES = 6
PASS_TILES = 4
F_CHUNK = 512
CAST_ROWS = 256


def _cparams(sem, vmem=VMEM_LIMIT):
    return pltpu.CompilerParams(dimension_semantics=sem, vmem_limit_bytes=vmem)


def _inproj_kernel(x_ref, g_ref, w_ref, o_ref, wb_ref, *, rows_per_cast, sub_blocks=4):
    @pl.when(pl.program_id(1) == 0)
    def _():
        d = w_ref.shape[0]
        for c in range(d // rows_per_cast):
            sl = slice(c * rows_per_cast, (c + 1) * rows_per_cast)
            wb_ref[sl, :] = w_ref[sl, :].astype(BF16)

    sub = x_ref.shape[0] // sub_blocks
    for sb in range(sub_blocks):
        rs = slice(sb * sub, (sb + 1) * sub)
        x = x_ref[rs, :]
        ms = jnp.mean(x * x, axis=-1, keepdims=True)
        h = (x * lax.rsqrt(ms + EPS) * g_ref[...]).astype(BF16)
        acc = jnp.dot(h, wb_ref[...], preferred_element_type=F32)
        for j in range(o_ref.shape[0]):
            o_ref[j, rs, :] = acc[:, j * LANES:(j + 1) * LANES].astype(o_ref.dtype)


def _inproj_call(x2, g, w, tm=1024, tn=1024):
    t, d = x2.shape
    n = w.shape[1]
    return pl.pallas_call(
        functools.partial(_inproj_kernel, rows_per_cast=min(CAST_ROWS, d)),
        grid=(n // tn, t // tm),
        in_specs=[pl.BlockSpec((tm, d), lambda j, i: (i, 0)),
                  pl.BlockSpec((1, d), lambda j, i: (0, 0)),
                  pl.BlockSpec((d, tn), lambda j, i: (0, j))],
        out_specs=pl.BlockSpec((tn // LANES, tm, LANES), lambda j, i: (j, i, 0)),
        out_shape=jax.ShapeDtypeStruct((n // LANES, t, LANES), BF16),
        scratch_shapes=[pltpu.VMEM((d, tn), BF16)],
        compiler_params=_cparams(("arbitrary", "arbitrary")),
        name="in_proj",
    )(x2, g.reshape(1, d), w)


def _ret_kernel(lg_ref, q_ref, k_ref, v_ref, g_ref, cos_ref, sin_ref, gn_ref, o_ref, *, blk, nblk):
    dk = q_ref.shape[-1]
    lg = lg_ref[pl.program_id(1)]
    row = lax.broadcasted_iota(I32, (blk, blk), 0)
    col = lax.broadcasted_iota(I32, (blk, blk), 1)
    dist = jnp.abs(row - col).astype(F32)
    visible = (col // CHUNK) <= (row // CHUNK)
    dmask = jnp.where(visible, jnp.exp(lg * dist), 0.0)
    rr = lax.broadcasted_iota(I32, (blk, dk), 0).astype(F32)
    qdec = jnp.exp(lg * (rr + 1.0))
    kdec = jnp.exp(lg * (blk - 1.0 - rr))
    bdec = jnp.exp(lg * jnp.full((1, HEAD_W), float(blk), F32))
    even = (lax.broadcasted_iota(I32, (blk, dk), 1) & 1) == 0
    scale = dk ** -0.5

    def body(i, state):
        rows = pl.ds(pl.multiple_of(i * blk, blk), blk)
        cos = cos_ref[rows, :]
        sin = sin_ref[rows, :]

        def rot(x):
            partner = jnp.where(even, pltpu.roll(x, dk - 1, 1), pltpu.roll(x, 1, 1))
            return x * cos + partner * sin

        qr = rot(q_ref[rows, :].astype(F32)) * scale
        kr = rot(k_ref[rows, :].astype(F32))
        v = v_ref[rows, :]
        s = lax.dot_general(qr.astype(BF16), kr.astype(BF16), (((1,), (1,)), ((), ())),
                            preferred_element_type=F32) * dmask
        out = jnp.dot(s.astype(BF16), v, preferred_element_type=F32)
        out = out + jnp.dot((qr * qdec).astype(BF16), state.astype(BF16),
                            preferred_element_type=F32)
        kv = lax.dot_general((kr * kdec).astype(BF16), v, (((0,), (0,)), ((), ())),
                             preferred_element_type=F32)
        state = state * bdec + kv
        ms = jnp.mean(out * out, axis=-1, keepdims=True)
        normed = out * lax.rsqrt(ms + EPS) * gn_ref[...]
        g = g_ref[rows, :].astype(F32)
        o_ref[rows, :] = (g * jax.nn.sigmoid(g) * normed).astype(o_ref.dtype)
        return state

    lax.fori_loop(0, nblk, body, jnp.zeros((dk, HEAD_W), F32), unroll=True)


def _ret_call(proj, cos, sin, log_gamma, gn_g, batch, seq, blk=RET_BLK):
    nh = RET_HEADS

    def head_spec(base):
        return pl.BlockSpec((None, seq, LANES), lambda b, h: (base + h, b, 0))

    return pl.pallas_call(
        functools.partial(_ret_kernel, blk=blk, nblk=seq // blk),
        grid=(batch, nh),
        in_specs=[pl.BlockSpec(memory_space=pltpu.SMEM),
                  head_spec(0), head_spec(nh), head_spec(2 * nh), head_spec(3 * nh),
                  pl.BlockSpec((seq, LANES), lambda b, h: (0, 0)),
                  pl.BlockSpec((seq, LANES), lambda b, h: (0, 0)),
                  pl.BlockSpec((None, 1, LANES), lambda b, h: (h, 0, 0))],
        out_specs=pl.BlockSpec((None, seq, LANES), lambda b, h: (h, b, 0)),
        out_shape=jax.ShapeDtypeStruct((nh, batch * seq, LANES), BF16),
        compiler_params=_cparams(("arbitrary", "arbitrary")),
        name="retention",
    )(log_gamma, proj, proj, proj, proj, cos, sin, gn_g.reshape(nh, 1, LANES))


def _diff_kernel(tbl_ref, q_ref, k_ref, v_ref, qg_ref, kg_ref, lam_ref, sg_ref, bidx_ref, o_ref,
                 qz_s, kn_s, bias_s, s_s, p_s, *, blk, nblk):
    h = pl.program_id(0)
    b = pl.program_id(1)
    far_bucket = NUM_BUCKETS // 2 - 1

    @pl.when(b == 0)
    def _build_bias():
        row = lax.broadcasted_iota(I32, (blk, blk), 0)
        col = lax.broadcasted_iota(I32, (blk, blk), 1)
        visible = (col // CHUNK) <= (row // CHUNK)
        for d in range(2):
            idx = bidx_ref[d]
            bias = jnp.zeros((blk, blk), F32)
            for bucket in range(NUM_BUCKETS):
                bias = jnp.where(idx == bucket, tbl_ref[bucket, h], bias)
            if d == 0:
                bias = jnp.where(visible, bias, NEG_BIG)
            bias_s[d, 0:blk, :] = bias
            bias_s[d, blk:2 * blk, :] = bias

    lo = lax.broadcasted_iota(I32, (blk, HEAD_W), 1) < DIFF_DK
    scale = DIFF_DK ** -0.5
    same_half = ((lax.broadcasted_iota(I32, (HEAD_W, HEAD_W), 0) // DIFF_DK)
                 == (lax.broadcasted_iota(I32, (HEAD_W, HEAD_W), 1) // DIFF_DK)).astype(BF16)

    def half_norm(x, g):
        ms = jnp.dot((x * x).astype(BF16), same_half, preferred_element_type=F32) * (1.0 / DIFF_DK)
        return x * lax.rsqrt(ms + EPS) * g

    def prep(i, carry):
        rows = pl.ds(pl.multiple_of(i * blk, blk), blk)
        qn = half_norm(q_ref[rows, :].astype(F32), qg_ref[...]) * scale
        kn = half_norm(k_ref[rows, :].astype(F32), kg_ref[...])
        base = pl.multiple_of(i * 2 * blk, 2 * blk)
        qz_s[pl.ds(base, blk), :] = jnp.where(lo, qn, 0.0).astype(BF16)
        qz_s[pl.ds(base + blk, blk), :] = jnp.where(lo, 0.0, qn).astype(BF16)
        kn_s[rows, :] = kn.astype(BF16)
        return carry

    lax.fori_loop(0, nblk, prep, 0, unroll=True)

    lam = (jnp.exp(jnp.sum(lam_ref[0:1, :] * lam_ref[1:2, :], axis=-1, keepdims=True))
           - jnp.exp(jnp.sum(lam_ref[2:3, :] * lam_ref[3:4, :], axis=-1, keepdims=True))
           + LAM_INIT)
    c_far = tbl_ref[far_bucket, h]

    def lane_fold(x, op):
        out = x[:, 0:LANES]
        for c in range(1, blk // LANES):
            out = op(out, x[:, c * LANES:(c + 1) * LANES])
        return out

    for i in range(nblk):
        qz = qz_s[i * 2 * blk:(i + 1) * 2 * blk, :]
        m_t = jnp.full((2 * blk, LANES), NEG_BIG, F32)
        for j in range(i + 1):
            keys = slice(j * blk, (j + 1) * blk)
            s = lax.dot_general(qz, kn_s[keys, :], (((1,), (1,)), ((), ())),
                                preferred_element_type=F32)
            s = s + (bias_s[0] if j == i else bias_s[1] if j == i - 1 else c_far)
            s_s[:, keys] = s
            m_t = jnp.maximum(m_t, lane_fold(s, jnp.maximum))
        m = jnp.max(m_t, axis=-1, keepdims=True)
        l_t = jnp.zeros((2 * blk, LANES), F32)
        for j in range(i + 1):
            keys = slice(j * blk, (j + 1) * blk)
            p = jnp.exp(s_s[:, keys] - m)
            l_t = l_t + lane_fold(p, jnp.add)
            p_s[:, keys] = p.astype(BF16)
        l = jnp.sum(l_t, axis=-1, keepdims=True)
        kend = (i + 1) * blk
        o = jnp.dot(p_s[:, 0:kend], v_ref[0:kend, :], preferred_element_type=F32) / l
        att = o[0:blk, :] - lam * o[blk:2 * blk, :]
        ms = jnp.mean(att * att, axis=-1, keepdims=True)
        y = att * lax.rsqrt(ms + EPS) * sg_ref[...] * (1.0 - LAM_INIT)
        o_ref[i * blk:(i + 1) * blk, :] = y.astype(o_ref.dtype)


def _t5_bucket(rel):
    nb = NUM_BUCKETS // 2
    max_exact = nb // 2
    base = jnp.where(rel > 0, nb, 0)
    n = jnp.abs(rel)
    large = max_exact + (jnp.log(jnp.maximum(n, 1).astype(jnp.float32) / max_exact)
                         / math.log(MAX_DISTANCE / max_exact) * (nb - max_exact)).astype(jnp.int32)
    large = jnp.minimum(large, nb - 1)
    return base + jnp.where(n < max_exact, n, large)


def _diff_call(proj, rel_table, qg, kg, lam_vecs, sg, batch, seq, blk=SEQ_BLK):
    nh = DIFF_HEADS
    first = 4 * RET_HEADS
    r = jnp.arange(blk, dtype=I32)
    rel0 = r[None, :] - r[:, None]
    bidx = jnp.stack([_t5_bucket(rel0), _t5_bucket(rel0 - blk)]).astype(I32) & (NUM_BUCKETS - 1)

    def head_spec(base):
        return pl.BlockSpec((None, seq, LANES), lambda h, b: (base + h, b, 0))

    def vec_spec():
        return pl.BlockSpec((1, LANES), lambda h, b: (0, 0))

    return pl.pallas_call(
        functools.partial(_diff_kernel, blk=blk, nblk=seq // blk),
        grid=(nh, batch),
        in_specs=[pl.BlockSpec(memory_space=pltpu.SMEM),
                  head_spec(first), head_spec(first + nh), head_spec(first + 2 * nh),
                  vec_spec(), vec_spec(),
                  pl.BlockSpec((4, DIFF_DK), lambda h, b: (0, 0)),
                  vec_spec(),
                  pl.BlockSpec((2, blk, blk), lambda h, b: (0, 0, 0))],
        out_specs=pl.BlockSpec((None, seq, LANES), lambda h, b: (h, b, 0)),
        out_shape=jax.ShapeDtypeStruct((nh, batch * seq, LANES), BF16),
        scratch_shapes=[pltpu.VMEM((2 * seq, LANES), BF16),
                        pltpu.VMEM((seq, LANES), BF16),
                        pltpu.VMEM((2, 2 * blk, blk), F32),
                        pltpu.VMEM((2 * blk, seq), F32),
                        pltpu.VMEM((2 * blk, seq), BF16)],
        compiler_params=_cparams(("arbitrary", "arbitrary")),
        name="diff_attn",
    )(rel_table, proj, proj, proj,
      jnp.tile(qg, 2).reshape(1, LANES), jnp.tile(kg, 2).reshape(1, LANES),
      lam_vecs, sg.reshape(1, LANES), bidx)


def _outproj_kernel(yr_ref, yd_ref, x_ref, w_hbm, g_ref, wr_ref, br_ref,
                    x1_ref, h2p_ref, lt_ref, wb_s, wstage_s, wr2_s, wsem,
                    *, rows_per_cast, sub_blocks=2):
    ne = lt_ref.shape[0]
    nchunk = wb_s.shape[0] // rows_per_cast

    def w_copy(c):
        return pltpu.make_async_copy(w_hbm.at[pl.ds(c * rows_per_cast, rows_per_cast)],
                                     wstage_s.at[c % 2], wsem.at[c % 2])

    @pl.when(pl.program_id(0) == 0)
    def _stage_weights():
        w_copy(0).start()
        for c in range(nchunk):
            if c + 1 < nchunk:
                w_copy(c + 1).start()
            w_copy(c).wait()
            wb_s[c * rows_per_cast:(c + 1) * rows_per_cast, :] = wstage_s[c % 2].astype(BF16)
        wr = wr_ref[...]
        wr_hi = wr.astype(BF16)
        wr2_s[:, 0:LANES] = wr_hi
        wr2_s[:, LANES:2 * LANES] = (wr - wr_hi.astype(F32)).astype(BF16)

    tm = x_ref.shape[0]
    sub = tm // sub_blocks
    for sb in range(sub_blocks):
        rs = slice(sb * sub, (sb + 1) * sub)
        y = jnp.concatenate([yr_ref[j, rs, :] for j in range(yr_ref.shape[0])]
                            + [yd_ref[j, rs, :] for j in range(yd_ref.shape[0])], axis=-1)
        x1 = x_ref[rs, :] + jnp.dot(y, wb_s[...], preferred_element_type=F32)
        x1_ref[rs, :] = x1
        ms = jnp.mean(x1 * x1, axis=-1, keepdims=True)
        h2 = x1 * lax.rsqrt(ms + EPS) * g_ref[...]
        h_hi = h2.astype(BF16)
        h_lo = (h2 - h_hi.astype(F32)).astype(BF16)
        parts = (jnp.dot(h_hi, wr2_s[...], preferred_element_type=F32)
                 + jnp.dot(h_lo, wr2_s[...], preferred_element_type=F32))
        logits = parts[:, 0:LANES] + parts[:, LANES:2 * LANES]
        lt_ref[:, rs] = logits.T[0:ne, :] + br_ref[...]
        half = h2.shape[1] // 2
        packed = pltpu.pack_elementwise([h2[:, :half], h2[:, half:]], packed_dtype=BF16)
        h2p_ref[rs] = packed.reshape((sub,) + h2p_ref.shape[1:])


def _outproj_call(y_ret, y_diff, x2, w_out, g, w_router, b_router, tm=512):
    t, d = x2.shape
    ne = w_router.shape[1]
    nhr, nhd = y_ret.shape[0], y_diff.shape[0]
    rows_per_cast = min(CAST_ROWS, d)
    wr_pad = jnp.pad(w_router, ((0, 0), (0, LANES - ne)))
    return pl.pallas_call(
        functools.partial(_outproj_kernel, rows_per_cast=rows_per_cast),
        grid=(t // tm,),
        in_specs=[pl.BlockSpec((nhr, tm, LANES), lambda i: (0, i, 0)),
                  pl.BlockSpec((nhd, tm, LANES), lambda i: (0, i, 0)),
                  pl.BlockSpec((tm, d), lambda i: (i, 0)),
                  pl.BlockSpec(memory_space=pl.ANY),
                  pl.BlockSpec((1, d), lambda i: (0, 0)),
                  pl.BlockSpec((d, LANES), lambda i: (0, 0)),
                  pl.BlockSpec((ne, 1), lambda i: (0, 0))],
        out_specs=[pl.BlockSpec((tm, d), lambda i: (i, 0)),
                   pl.BlockSpec((tm, d // 2 // LANES, LANES), lambda i: (i, 0, 0)),
                   pl.BlockSpec((ne, tm), lambda i: (0, i))],
        out_shape=[jax.ShapeDtypeStruct((t, d), F32),
                   jax.ShapeDtypeStruct((t, d // 2 // LANES, LANES), U32),
                   jax.ShapeDtypeStruct((ne, t), F32)],
        scratch_shapes=[pltpu.VMEM((d, d), BF16),
                        pltpu.VMEM((2, rows_per_cast, d), F32),
                        pltpu.VMEM((d, 2 * LANES), BF16),
                        pltpu.SemaphoreType.DMA((2,))],
        compiler_params=_cparams(("arbitrary",)),
        name="out_proj_router",
    )(y_ret, y_diff, x2, w_out, g.reshape(1, d), wr_pad, b_router.reshape(ne, 1))


def _route_kernel(lt_ref, pos_ref, gate_ref, cnt_ref, nt_ref, ot_ref, idx_s, rank_s, *, tb, row_tile):
    ne, t = lt_ref.shape
    e_iota = lax.broadcasted_iota(I32, (ne, tb), 0)
    upper = (lax.broadcasted_iota(I32, (tb, tb), 0)
             < lax.broadcasted_iota(I32, (tb, tb), 1)).astype(BF16)

    def pass_a(i, running):
        cols = pl.ds(pl.multiple_of(i * tb, tb), tb)
        l = lt_ref[:, cols]
        tops, hots = [], []
        for k in range(TOP_K):
            m = jnp.max(l, axis=0, keepdims=True)
            idx = jnp.min(jnp.where(l == m, e_iota, ne), axis=0, keepdims=True)
            hot = e_iota == idx
            l = jnp.where(hot, -jnp.inf, l)
            idx_s[k:k + 1, cols] = idx
            tops.append(m)
            hots.append(hot)
        exps = [jnp.exp(m - tops[0]) for m in tops]
        denom = exps[0] + exps[1] + exps[2] + exps[3]
        for k in range(TOP_K):
            gate_ref[k:k + 1, cols] = exps[k] / denom
        hot_all = jnp.zeros((ne, tb), F32)
        for hot in hots:
            hot_all = hot_all + hot.astype(F32)
        before = running + jnp.dot(hot_all.astype(BF16), upper, preferred_element_type=F32)
        for k in range(TOP_K):
            rank_s[k:k + 1, cols] = jnp.sum(jnp.where(hots[k], before, 0.0), axis=0, keepdims=True)
        return running + jnp.sum(hot_all, axis=1, keepdims=True)

    cnt = lax.fori_loop(0, t // tb, pass_a, jnp.zeros((ne, 1), F32))
    ntile = jnp.floor((cnt + (row_tile - 1.0)) * (1.0 / row_tile))
    lower = (lax.broadcasted_iota(I32, (ne, ne), 1)
             < lax.broadcasted_iota(I32, (ne, ne), 0)).astype(BF16)
    otile = jnp.dot(lower, jnp.broadcast_to(ntile, (ne, LANES)).astype(BF16),
                    preferred_element_type=F32)
    cnt_ref[...] = jnp.broadcast_to(cnt, (ne, LANES)).astype(I32)
    nt_ref[...] = jnp.broadcast_to(ntile, (ne, LANES)).astype(I32)
    ot_ref[...] = otile.astype(I32)
    off_rows = otile[:, 0:1] * float(row_tile)

    def pass_b(i, carry):
        cols = pl.ds(pl.multiple_of(i * tb, tb), tb)
        for k in range(TOP_K):
            hot = e_iota == idx_s[k:k + 1, cols]
            off = jnp.sum(jnp.where(hot, off_rows, 0.0), axis=0, keepdims=True)
            pos_ref[k:k + 1, cols] = (rank_s[k:k + 1, cols] + off).astype(I32)
        return carry

    lax.fori_loop(0, t // tb, pass_b, 0)


def _route_call(logits_t, tb=256, row_tile=ROW_TILE):
    ne, t = logits_t.shape
    return pl.pallas_call(
        functools.partial(_route_kernel, tb=tb, row_tile=row_tile),
        out_shape=[jax.ShapeDtypeStruct((TOP_K, t), I32),
                   jax.ShapeDtypeStruct((TOP_K, t), F32),
                   jax.ShapeDtypeStruct((ne, LANES), I32),
                   jax.ShapeDtypeStruct((ne, LANES), I32),
                   jax.ShapeDtypeStruct((ne, LANES), I32)],
        scratch_shapes=[pltpu.VMEM((TOP_K, t), I32), pltpu.VMEM((TOP_K, t), F32)],
        compiler_params=_cparams(None),
        name="route",
    )(logits_t)


_PAD_PIECES = tuple(ROW_TILE >> (k + 1) for k in range(ROW_TILE.bit_length() - 1))


def _dispatch_kernel(pos_ref, pad_start_ref, pad_n_ref, src_ref, dst_ref, stage_s, zero_s,
                     lsem, ssem, zsem, *, tb, ne):
    i = pl.program_id(0)
    nsteps = pl.num_programs(0)
    slot = i % 2

    def load(step, s):
        return pltpu.make_async_copy(src_ref.at[pl.ds(step * tb, tb)], stage_s.at[s], lsem.at[s])

    def wait_rows(s):
        for k in range(TOP_K):
            pltpu.make_async_copy(stage_s.at[s], dst_ref.at[pl.ds(0, tb)], ssem.at[s]).wait()

    def pad_copies(e):
        n = pad_n_ref[e]
        start = pad_start_ref[e]
        out = []
        for piece in _PAD_PIECES:
            at = start + (n & ~(2 * piece - 1))
            out.append(((n & piece) != 0, pltpu.make_async_copy(
                zero_s.at[pl.ds(0, piece)], dst_ref.at[pl.ds(at, piece)], zsem)))
        return out

    @pl.when(i == 0)
    def _zero_pads():
        zero_s[...] = jnp.zeros(zero_s.shape, zero_s.dtype)

        def issue(e, c):
            for cond, cp in pad_copies(e):
                @pl.when(cond)
                def _():
                    cp.start()
            return c

        def drain(e, c):
            for cond, cp in pad_copies(e):
                @pl.when(cond)
                def _():
                    cp.wait()
            return c

        lax.fori_loop(0, ne, issue, 0)
        lax.fori_loop(0, ne, drain, 0)
        load(0, 0).start()

    @pl.when(i > 0)
    def _():
        wait_rows(1 - slot)

    @pl.when(i + 1 < nsteps)
    def _():
        load(i + 1, 1 - slot).start()

    load(i, slot).wait()

    def issue_rows(tl, c):
        for k in range(TOP_K):
            pltpu.make_async_copy(stage_s.at[slot, tl], dst_ref.at[pos_ref[k, tl]],
                                  ssem.at[slot]).start(priority=k % 2)
        return c

    lax.fori_loop(0, tb, issue_rows, 0, unroll=4)

    @pl.when(i == nsteps - 1)
    def _():
        wait_rows(slot)


def _dispatch_call(pos, pad_start, pad_n, h2p, n_slots, tb=512):
    t = h2p.shape[0]
    row = h2p.shape[1:]
    ne = pad_n.shape[0]
    return pl.pallas_call(
        functools.partial(_dispatch_kernel, tb=tb, ne=ne),
        grid=(t // tb,),
        in_specs=[pl.BlockSpec((TOP_K, tb), lambda i: (0, i), memory_space=pltpu.SMEM),
                  pl.BlockSpec(memory_space=pltpu.SMEM),
                  pl.BlockSpec(memory_space=pltpu.SMEM),
                  pl.BlockSpec(memory_space=pl.ANY)],
        out_specs=pl.BlockSpec(memory_space=pl.ANY),
        out_shape=jax.ShapeDtypeStruct((n_slots, *row), U32),
        scratch_shapes=[pltpu.VMEM((2, tb, *row), U32),
                        pltpu.VMEM((_PAD_PIECES[0], *row), U32),
                        pltpu.SemaphoreType.DMA((2,)),
                        pltpu.SemaphoreType.DMA((2,)),
                        pltpu.SemaphoreType.DMA(())],
        compiler_params=pltpu.CompilerParams(dimension_semantics=("arbitrary",),
                                             vmem_limit_bytes=VMEM_LIMIT, has_side_effects=True),
        name="dispatch",
    )(pos, pad_start, pad_n, h2p)


def _moe_kernel(we_ref, wt_ref, wn_ref, xs_ref, win_ref, bin_ref, wout_ref, bo_ref,
                ys_ref, xu_s, xb_s, acc_s, wg_st, wu_st, wo_st, yst_s, xsem, wsem, ysem,
                *, tmx, fc, nj):
    w = pl.program_id(0)
    nw = pl.num_programs(0)
    nt = wn_ref[w]
    t0 = wt_ref[w]
    half = xu_s.shape[-2] * xu_s.shape[-1]
    de = nj * fc
    nxt = jnp.minimum(w + 1, nw - 1)
    has_next = jnp.logical_and(w + 1 < nw, wn_ref[nxt] > 0)

    def local_rows(r, n=1):
        return pl.ds(pl.multiple_of(r * tmx, tmx), n * tmx)

    def x_copy(item_t0, r):
        src = pl.ds(pl.multiple_of((item_t0 + r) * tmx, tmx), tmx)
        return pltpu.make_async_copy(xs_ref.at[src], xu_s.at[local_rows(r)], xsem)

    def y_copy(r, slot):
        dst = pl.ds(pl.multiple_of((t0 + r) * tmx, tmx), tmx)
        return pltpu.make_async_copy(yst_s.at[slot], ys_ref.at[dst], ysem.at[slot])

    def w_copies(expert, j, slot):
        c0 = pl.multiple_of(j * fc, fc)
        return (pltpu.make_async_copy(win_ref.at[expert, :, pl.ds(c0, fc)], wg_st.at[slot], wsem.at[slot]),
                pltpu.make_async_copy(win_ref.at[expert, :, pl.ds(de + c0, fc)], wu_st.at[slot],
                                      wsem.at[slot]),
                pltpu.make_async_copy(wout_ref.at[expert, pl.ds(c0, fc), :], wo_st.at[slot],
                                      wsem.at[slot]))

    def start_weights(expert, j, slot):
        for cp in w_copies(expert, j, slot):
            cp.start()

    def start_rows(item_t0, item_nt):
        def go(r, c):
            x_copy(item_t0, r).start()
            return c
        lax.fori_loop(0, item_nt, go, 0)

    @pl.when(w == 0)
    def _prologue():
        start_weights(we_ref[0], 0, 0)
        start_rows(t0, nt)

    @pl.when(nt > 0)
    def _work():
        def chunk(j, carry):
            slot = j % 2

            @pl.when(j + 1 < nj)
            def _():
                start_weights(we_ref[w], j + 1, 1 - slot)

            @pl.when(jnp.logical_and(j + 1 == nj, has_next))
            def _():
                start_weights(we_ref[nxt], 0, 1 - slot)

            for cp in w_copies(we_ref[w], j, slot):
                cp.wait()
            stage = (wg_st.at[slot], wu_st.at[slot], wo_st.at[slot])
            biases = (bin_ref[j], bin_ref[nj + j])

            @pl.when(j == 0)
            def _rows_ready():
                def finish(r, c):
                    x_copy(t0, r).wait()
                    return c
                lax.fori_loop(0, nt, finish, 0)

            @pl.when(jnp.logical_and(j == 1, has_next))
            def _():
                start_rows(wt_ref[nxt], wn_ref[nxt])

            @pl.when(j == 0)
            def _():
                _moe_passes(xu_s, xb_s, acc_s, stage, biases, nt, local_rows, half, True)

            @pl.when(j > 0)
            def _():
                _moe_passes(xu_s, xb_s, acc_s, stage, biases, nt, local_rows, half, False)

            return carry

        lax.fori_loop(0, nj, chunk, 0)

        def y_wait(slot):
            y_copy(0, slot).wait()

        prev_nt = jnp.where(w > 0, wn_ref[jnp.maximum(w - 1, 0)], 0)

        @pl.when(prev_nt >= 1)
        def _():
            y_wait((prev_nt - 1) % 2)

        @pl.when(prev_nt >= 2)
        def _():
            y_wait(prev_nt % 2)

        def emit(r, c):
            slot = r % 2

            @pl.when(r >= 2)
            def _():
                y_wait(slot)

            y = acc_s[local_rows(r), :] + bo_ref[...]
            yst_s[slot] = pltpu.pack_elementwise([y[:, :half], y[:, half:]],
                                                 packed_dtype=BF16).reshape(yst_s.shape[1:])
            y_copy(r, slot).start()
            return c

        lax.fori_loop(0, nt, emit, 0)

        @pl.when(jnp.logical_not(has_next))
        def _drain():
            @pl.when(nt >= 2)
            def _():
                y_wait(nt % 2)

            y_wait((nt - 1) % 2)


def _moe_passes(xu_ref, xb_ref, acc_s, stage, biases, nt, local_rows, half, first):
    wg_ref, wu_ref, wo_ref = stage
    bg, bu = biases

    def proj(x_lo, x_hi, w_ref, b):
        return (jnp.dot(x_lo, w_ref[0:half, :].astype(BF16), preferred_element_type=F32)
                + jnp.dot(x_hi, w_ref[half:2 * half, :].astype(BF16), preferred_element_type=F32) + b)

    def rows_step(rows):
        if first:
            xw = xu_ref[rows]
            xw = xw.reshape(xw.shape[0], half)
            x_lo = pltpu.unpack_elementwise(xw, index=0, packed_dtype=BF16,
                                            unpacked_dtype=F32).astype(BF16)
            x_hi = pltpu.unpack_elementwise(xw, index=1, packed_dtype=BF16,
                                            unpacked_dtype=F32).astype(BF16)
            xb_ref[rows, 0:half] = x_lo
            xb_ref[rows, half:2 * half] = x_hi
        else:
            x_lo = xb_ref[rows, 0:half]
            x_hi = xb_ref[rows, half:2 * half]
        gg = jnp.minimum(proj(x_lo, x_hi, wg_ref, bg), SWIGLU_LIMIT)
        uu = jnp.clip(proj(x_lo, x_hi, wu_ref, bu), -SWIGLU_LIMIT, SWIGLU_LIMIT)
        act = (uu + 1.0) * (gg * jax.nn.sigmoid(SWIGLU_ALPHA * gg))
        y = jnp.dot(act.astype(BF16), wo_ref[...].astype(BF16), preferred_element_type=F32)
        if first:
            acc_s[rows, :] = y
        else:
            acc_s[rows, :] += y

    def big(pi, c):
        rows_step(local_rows(PASS_TILES * pi, PASS_TILES))
        return c

    lax.fori_loop(0, nt // PASS_TILES, big, 0)
    piece = PASS_TILES // 2
    while piece >= 1:
        @pl.when((nt & piece) != 0)
        def _(piece=piece):
            rows_step(local_rows(nt & ~(2 * piece - 1), piece))
        piece //= 2


def _moe_call(work_e, work_t0, work_nt, xs, w_in, b_in, w_out, b_out,
              tmx=ROW_TILE, group=GROUP_TILES, fc=F_CHUNK):
    ne, d, de2 = w_in.shape
    de = de2 // 2
    nj = de // fc
    assert nj % 2 == 0, "weight stage slots alternate per chunk and restart at 0 per work item"
    n_work = work_e.shape[0]
    n_slots = xs.shape[0]
    row = xs.shape[1:]

    grid_spec = pltpu.PrefetchScalarGridSpec(
        num_scalar_prefetch=3,
        grid=(n_work,),
        in_specs=[
            pl.BlockSpec(memory_space=pl.ANY),
            pl.BlockSpec(memory_space=pl.ANY),
            pl.BlockSpec((None, 2 * nj, 1, fc), lambda w, we, wt, wn: (we[w], 0, 0, 0)),
            pl.BlockSpec(memory_space=pl.ANY),
            pl.BlockSpec((None, 1, d), lambda w, we, wt, wn: (we[w], 0, 0)),
        ],
        out_specs=pl.BlockSpec(memory_space=pl.ANY),
        scratch_shapes=[pltpu.VMEM((group * tmx, *row), U32),
                        pltpu.VMEM((group * tmx, d), BF16),
                        pltpu.VMEM((group * tmx, d), F32),
                        pltpu.VMEM((2, d, fc), F32),
                        pltpu.VMEM((2, d, fc), F32),
                        pltpu.VMEM((2, fc, d), F32),
                        pltpu.VMEM((2, tmx, *row), U32),
                        pltpu.SemaphoreType.DMA(()),
                        pltpu.SemaphoreType.DMA((2,)),
                        pltpu.SemaphoreType.DMA((2,))],
    )
    return pl.pallas_call(
        functools.partial(_moe_kernel, tmx=tmx, fc=fc, nj=nj),
        grid_spec=grid_spec,
        out_shape=jax.ShapeDtypeStruct((n_slots, *row), U32),
        compiler_params=pltpu.CompilerParams(dimension_semantics=("arbitrary",),
                                             vmem_limit_bytes=VMEM_LIMIT, has_side_effects=True),
        name="moe_experts",
    )(work_e, work_t0, work_nt, xs, w_in, b_in.reshape(ne, 2 * nj, 1, fc), w_out,
      b_out.reshape(ne, 1, d))


def _combine_kernel(pos_ref, pos_next_ref, x1_ref, gate_ref, ys_ref, o_ref, ybuf, sem, *, tb):
    half = ybuf.shape[-2] * ybuf.shape[-1]
    i = pl.program_id(0)
    slot = i % 2

    def gather(p_ref, s):
        def issue(tl, c):
            for k in range(TOP_K):
                pltpu.make_async_copy(ys_ref.at[p_ref[k, tl]], ybuf.at[s, k, tl],
                                      sem.at[s]).start(priority=k % 2)
            return c
        lax.fori_loop(0, tb, issue, 0, unroll=4)

    @pl.when(i == 0)
    def _():
        gather(pos_ref, 0)

    @pl.when(i + 1 < pl.num_programs(0))
    def _():
        gather(pos_next_ref, 1 - slot)

    for k in range(TOP_K):
        pltpu.make_async_copy(ys_ref.at[pl.ds(0, tb)], ybuf.at[slot, k], sem.at[slot]).wait()

    lo = x1_ref[:, 0:half]
    hi = x1_ref[:, half:2 * half]
    for k in range(TOP_K):
        g = gate_ref[:, k:k + 1]
        yw = ybuf[slot, k].reshape(tb, half)
        lo = lo + g * pltpu.unpack_elementwise(yw, index=0, packed_dtype=BF16, unpacked_dtype=F32)
        hi = hi + g * pltpu.unpack_elementwise(yw, index=1, packed_dtype=BF16, unpacked_dtype=F32)
    o_ref[:, 0:half] = lo
    o_ref[:, half:2 * half] = hi


def _combine_call(pos, x1, gate_tk, ys, tb=256):
    t, d = x1.shape
    row = ys.shape[1:]
    last = t // tb - 1
    return pl.pallas_call(
        functools.partial(_combine_kernel, tb=tb),
        grid=(t // tb,),
        in_specs=[pl.BlockSpec((TOP_K, tb), lambda i: (0, i), memory_space=pltpu.SMEM),
                  pl.BlockSpec((TOP_K, tb), lambda i: (0, jnp.minimum(i + 1, last)),
                               memory_space=pltpu.SMEM),
                  pl.BlockSpec((tb, d), lambda i: (i, 0)),
                  pl.BlockSpec((tb, TOP_K), lambda i: (i, 0)),
                  pl.BlockSpec(memory_space=pl.ANY)],
        out_specs=pl.BlockSpec((tb, d), lambda i: (i, 0)),
        out_shape=jax.ShapeDtypeStruct((t, d), F32),
        scratch_shapes=[pltpu.VMEM((2, TOP_K, tb, *row), U32),
                        pltpu.SemaphoreType.DMA((2,))],
        compiler_params=_cparams(("arbitrary",)),
        name="combine",
    )(pos, pos, x1, gate_tk, ys)


def _work_list(ntile, otile, n_work, group):
    ne = ntile.shape[0]
    items = (ntile + group - 1) // group
    ends = jnp.cumsum(items)
    total = ends[-1]
    w = jnp.arange(n_work, dtype=I32)
    valid = w < total
    e_w = jnp.clip(jnp.sum((ends[None, :] <= w[:, None]).astype(I32), axis=1), 0, ne - 1)
    local = w - (ends - items)[e_w]
    t0_w = otile[e_w] + local * group
    nt_w = jnp.where(valid, jnp.clip(ntile[e_w] - local * group, 0, group), 0)
    e_last = e_w[jnp.maximum(total - 1, 0)]
    return (jnp.where(valid, e_w, e_last).astype(I32), jnp.where(valid, t0_w, 0).astype(I32),
            nt_w.astype(I32))


def _rotary_tables(seq, dk):
    inv_freq = ROPE_BASE ** (-jnp.arange(0, dk, 2, dtype=F32) / dk)
    ang = jnp.arange(seq, dtype=I32).astype(F32)[:, None] * inv_freq[None, :]
    cos = jnp.repeat(jnp.cos(ang), 2, axis=1)
    sin = jnp.stack([-jnp.sin(ang), jnp.sin(ang)], axis=-1).reshape(seq, dk)
    return cos, sin


def kernel(x, norm_mix_g, w_mix_in, ret_gn_g, q_norm_g, k_norm_g, lambda_q1, lambda_k1, lambda_q2, lambda_k2, diff_subln_g, rel_bias_table, w_mix_out, norm_ffn_g, w_router, b_router, w_exp_in, b_exp_in, w_exp_out, b_exp_out):
    batch, seq, d = x.shape
    t = batch * seq
    depth = norm_mix_g.shape[0]
    ne = w_router.shape[-1]
    n_tiles_max = (t * TOP_K) // ROW_TILE + ne
    n_work = ne + (n_tiles_max - ne) // GROUP_TILES
    cos, sin = _rotary_tables(seq, HEAD_W)
    log_gamma = jnp.log1p(-(2.0 ** (-5.0 - jnp.arange(RET_HEADS, dtype=F32))))

    x2 = x.reshape(t, d)
    for l in range(depth):
        proj = _inproj_call(x2, norm_mix_g[l], w_mix_in[l])
        y_ret = _ret_call(proj, cos, sin, log_gamma, ret_gn_g[l], batch, seq)
        lam_vecs = jnp.stack([lambda_q1[l], lambda_k1[l], lambda_q2[l], lambda_k2[l]])
        y_diff = _diff_call(proj, rel_bias_table, q_norm_g[l], k_norm_g[l], lam_vecs,
                            diff_subln_g[l], batch, seq)
        x1, h2p, logits_t = _outproj_call(y_ret, y_diff, x2, w_mix_out[l], norm_ffn_g[l],
                                          w_router[l], b_router[l])
        pos, gate, cnt, ntile, otile = _route_call(logits_t)
        cnt, ntile, otile = cnt[:, 0], ntile[:, 0], otile[:, 0]
        work_e, work_t0, work_nt = _work_list(ntile, otile, n_work, GROUP_TILES)
        xs = _dispatch_call(pos, otile * ROW_TILE + cnt, ntile * ROW_TILE - cnt, h2p,
                            n_tiles_max * ROW_TILE)
        ys = _moe_call(work_e, work_t0, work_nt, xs, w_exp_in[l], b_exp_in[l], w_exp_out[l],
                       b_exp_out[l])
        x2 = _combine_call(pos, x1, gate.T, ys)
    return x2.reshape(batch, seq, d)
```

```python
import functools
import math

import jax
import jax.numpy as jnp
from jax import lax
from jax.experimental import pallas as pl
from jax.experimental.pallas import tpu as pltpu

F32 = jnp.float32
BF16 = jnp.bfloat16
I32 = jnp.int32
U32 = jnp.uint32

EPS = 1e-6
CHUNK = 64
RET_HEADS = 8
DIFF_HEADS = 8
HEAD_W = 128
DIFF_DK = 64
ROPE_BASE = 10000.0
NUM_BUCKETS = 32
MAX_DISTANCE = 128
TOP_K = 4
SWIGLU_LIMIT = 7.0
SWIGLU_ALPHA = 1.702
LAM_INIT = 0.8 - 0.6 * math.exp(-0.3 * 0)

LANES = 128
NEG_BIG = -1e30
VMEM_LIMIT = 60 * 1024 * 1024

SEQ_BLK = 256
RET_BLK = 512
ROW_TILE = 256
GROUP_TILES = 6
PASS_TILES = 4
F_CHUNK = 512
CAST_ROWS = 256


def _cparams(sem, vmem=VMEM_LIMIT):
    return pltpu.CompilerParams(dimension_semantics=sem, vmem_limit_bytes=vmem)


def _inproj_kernel(x_ref, g_ref, w_ref, o_ref, wb_ref, *, rows_per_cast, sub_blocks=4):
    @pl.when(pl.program_id(1) == 0)
    def _():
        d = w_ref.shape[0]
        for c in range(d // rows_per_cast):
            sl = slice(c * rows_per_cast, (c + 1) * rows_per_cast)
            wb_ref[sl, :] = w_ref[sl, :].astype(BF16)

    sub = x_ref.shape[0] // sub_blocks
    for sb in range(sub_blocks):
        rs = slice(sb * sub, (sb + 1) * sub)
        x = x_ref[rs, :]
        ms = jnp.mean(x * x, axis=-1, keepdims=True)
        h = (x * lax.rsqrt(ms + EPS) * g_ref[...]).astype(BF16)
        acc = jnp.dot(h, wb_ref[...], preferred_element_type=F32)
        for j in range(o_ref.shape[0]):
            o_ref[j, rs, :] = acc[:, j * LANES:(j + 1) * LANES].astype(o_ref.dtype)


def _inproj_call(x2, g, w, tm=1024, tn=1024):
    t, d = x2.shape
    n = w.shape[1]
    return pl.pallas_call(
        functools.partial(_inproj_kernel, rows_per_cast=min(CAST_ROWS, d)),
        grid=(n // tn, t // tm),
        in_specs=[pl.BlockSpec((tm, d), lambda j, i: (i, 0)),
                  pl.BlockSpec((1, d), lambda j, i: (0, 0)),
                  pl.BlockSpec((d, tn), lambda j, i: (0, j))],
        out_specs=pl.BlockSpec((tn // LANES, tm, LANES), lambda j, i: (j, i, 0)),
        out_shape=jax.ShapeDtypeStruct((n // LANES, t, LANES), BF16),
        scratch_shapes=[pltpu.VMEM((d, tn), BF16)],
        compiler_params=_cparams(("arbitrary", "arbitrary")),
        name="in_proj",
    )(x2, g.reshape(1, d), w)


def _ret_kernel(lg_ref, q_ref, k_ref, v_ref, g_ref, cos_ref, sin_ref, gn_ref, o_ref, *, blk, nblk):
    dk = q_ref.shape[-1]
    lg = lg_ref[pl.program_id(1)]
    row = lax.broadcasted_iota(I32, (blk, blk), 0)
    col = lax.broadcasted_iota(I32, (blk, blk), 1)
    dist = jnp.abs(row - col).astype(F32)
    visible = (col // CHUNK) <= (row // CHUNK)
    dmask = jnp.where(visible, jnp.exp(lg * dist), 0.0)
    rr = lax.broadcasted_iota(I32, (blk, dk), 0).astype(F32)
    qdec = jnp.exp(lg * (rr + 1.0))
    kdec = jnp.exp(lg * (blk - 1.0 - rr))
    bdec = jnp.exp(lg * jnp.full((1, HEAD_W), float(blk), F32))
    even = (lax.broadcasted_iota(I32, (blk, dk), 1) & 1) == 0
    scale = dk ** -0.5

    def body(i, state):
        rows = pl.ds(pl.multiple_of(i * blk, blk), blk)
        cos = cos_ref[rows, :]
        sin = sin_ref[rows, :]

        def rot(x):
            partner = jnp.where(even, pltpu.roll(x, dk - 1, 1), pltpu.roll(x, 1, 1))
            return x * cos + partner * sin

        qr = rot(q_ref[rows, :].astype(F32)) * scale
        kr = rot(k_ref[rows, :].astype(F32))
        v = v_ref[rows, :]
        s = lax.dot_general(qr.astype(BF16), kr.astype(BF16), (((1,), (1,)), ((), ())),
                            preferred_element_type=F32) * dmask
        out = jnp.dot(s.astype(BF16), v, preferred_element_type=F32)
        out = out + jnp.dot((qr * qdec).astype(BF16), state.astype(BF16),
                            preferred_element_type=F32)
        kv = lax.dot_general((kr * kdec).astype(BF16), v, (((0,), (0,)), ((), ())),
                             preferred_element_type=F32)
        state = state * bdec + kv
        ms = jnp.mean(out * out, axis=-1, keepdims=True)
        normed = out * lax.rsqrt(ms + EPS) * gn_ref[...]
        g = g_ref[rows, :].astype(F32)
        o_ref[rows, :] = (g * jax.nn.sigmoid(g) * normed).astype(o_ref.dtype)
        return state

    lax.fori_loop(0, nblk, body, jnp.zeros((dk, HEAD_W), F32), unroll=True)


def _ret_call(proj, cos, sin, log_gamma, gn_g, batch, seq, blk=RET_BLK):
    nh = RET_HEADS

    def head_spec(base):
        return pl.BlockSpec((None, seq, LANES), lambda b, h: (base + h, b, 0))

    return pl.pallas_call(
        functools.partial(_ret_kernel, blk=blk, nblk=seq // blk),
        grid=(batch, nh),
        in_specs=[pl.BlockSpec(memory_space=pltpu.SMEM),
                  head_spec(0), head_spec(nh), head_spec(2 * nh), head_spec(3 * nh),
                  pl.BlockSpec((seq, LANES), lambda b, h: (0, 0)),
                  pl.BlockSpec((seq, LANES), lambda b, h: (0, 0)),
                  pl.BlockSpec((None, 1, LANES), lambda b, h: (h, 0, 0))],
        out_specs=pl.BlockSpec((None, seq, LANES), lambda b, h: (h, b, 0)),
        out_shape=jax.ShapeDtypeStruct((nh, batch * seq, LANES), BF16),
        compiler_params=_cparams(("arbitrary", "arbitrary")),
        name="retention",
    )(log_gamma, proj, proj, proj, proj, cos, sin, gn_g.reshape(nh, 1, LANES))


def _diff_kernel(tbl_ref, q_ref, k_ref, v_ref, qg_ref, kg_ref, lam_ref, sg_ref, bidx_ref, o_ref,
                 qz_s, kn_s, bias_s, s_s, p_s, *, blk, nblk):
    h = pl.program_id(0)
    b = pl.program_id(1)
    far_bucket = NUM_BUCKETS // 2 - 1

    @pl.when(b == 0)
    def _build_bias():
        row = lax.broadcasted_iota(I32, (blk, blk), 0)
        col = lax.broadcasted_iota(I32, (blk, blk), 1)
        visible = (col // CHUNK) <= (row // CHUNK)
        for d in range(2):
            idx = bidx_ref[d]
            bias = jnp.zeros((blk, blk), F32)
            for bucket in range(NUM_BUCKETS):
                bias = jnp.where(idx == bucket, tbl_ref[bucket, h], bias)
            if d == 0:
                bias = jnp.where(visible, bias, NEG_BIG)
            bias_s[d, 0:blk, :] = bias
            bias_s[d, blk:2 * blk, :] = bias

    lo = lax.broadcasted_iota(I32, (blk, HEAD_W), 1) < DIFF_DK
    scale = DIFF_DK ** -0.5
    same_half = ((lax.broadcasted_iota(I32, (HEAD_W, HEAD_W), 0) // DIFF_DK)
                 == (lax.broadcasted_iota(I32, (HEAD_W, HEAD_W), 1) // DIFF_DK)).astype(BF16)

    def half_norm(x, g):
        ms = jnp.dot((x * x).astype(BF16), same_half, preferred_element_type=F32) * (1.0 / DIFF_DK)
        return x * lax.rsqrt(ms + EPS) * g

    def prep(i, carry):
        rows = pl.ds(pl.multiple_of(i * blk, blk), blk)
        qn = half_norm(q_ref[rows, :].astype(F32), qg_ref[...]) * scale
        kn = half_norm(k_ref[rows, :].astype(F32), kg_ref[...])
        base = pl.multiple_of(i * 2 * blk, 2 * blk)
        qz_s[pl.ds(base, blk), :] = jnp.where(lo, qn, 0.0).astype(BF16)
        qz_s[pl.ds(base + blk, blk), :] = jnp.where(lo, 0.0, qn).astype(BF16)
        kn_s[rows, :] = kn.astype(BF16)
        return carry

    lax.fori_loop(0, nblk, prep, 0, unroll=True)

    lam = (jnp.exp(jnp.sum(lam_ref[0:1, :] * lam_ref[1:2, :], axis=-1, keepdims=True))
           - jnp.exp(jnp.sum(lam_ref[2:3, :] * lam_ref[3:4, :], axis=-1, keepdims=True))
           + LAM_INIT)
    c_far = tbl_ref[far_bucket, h]

    def lane_fold(x, op):
        out = x[:, 0:LANES]
        for c in range(1, blk // LANES):
            out = op(out, x[:, c * LANES:(c + 1) * LANES])
        return out

    for i in range(nblk):
        qz = qz_s[i * 2 * blk:(i + 1) * 2 * blk, :]
        m_t = jnp.full((2 * blk, LANES), NEG_BIG, F32)
        for j in range(i + 1):
            keys = slice(j * blk, (j + 1) * blk)
            s = lax.dot_general(qz, kn_s[keys, :], (((1,), (1,)), ((), ())),
                                preferred_element_type=F32)
            s = s + (bias_s[0] if j == i else bias_s[1] if j == i - 1 else c_far)
            s_s[:, keys] = s
            m_t = jnp.maximum(m_t, lane_fold(s, jnp.maximum))
        m = jnp.max(m_t, axis=-1, keepdims=True)
        l_t = jnp.zeros((2 * blk, LANES), F32)
        for j in range(i + 1):
            keys = slice(j * blk, (j + 1) * blk)
            p = jnp.exp(s_s[:, keys] - m)
            l_t = l_t + lane_fold(p, jnp.add)
            p_s[:, keys] = p.astype(BF16)
        l = jnp.sum(l_t, axis=-1, keepdims=True)
        kend = (i + 1) * blk
        o = jnp.dot(p_s[:, 0:kend], v_ref[0:kend, :], preferred_element_type=F32) / l
        att = o[0:blk, :] - lam * o[blk:2 * blk, :]
        ms = jnp.mean(att * att, axis=-1, keepdims=True)
        y = att * lax.rsqrt(ms + EPS) * sg_ref[...] * (1.0 - LAM_INIT)
        o_ref[i * blk:(i + 1) * blk, :] = y.astype(o_ref.dtype)


def _t5_bucket(rel):
    nb = NUM_BUCKETS // 2
    max_exact = nb // 2
    base = jnp.where(rel > 0, nb, 0)
    n = jnp.abs(rel)
    large = max_exact + (jnp.log(jnp.maximum(n, 1).astype(jnp.float32) / max_exact)
                         / math.log(MAX_DISTANCE / max_exact) * (nb - max_exact)).astype(jnp.int32)
    large = jnp.minimum(large, nb - 1)
    return base + jnp.where(n < max_exact, n, large)


def _diff_call(proj, rel_table, qg, kg, lam_vecs, sg, batch, seq, blk=SEQ_BLK):
    nh = DIFF_HEADS
    first = 4 * RET_HEADS
    r = jnp.arange(blk, dtype=I32)
    rel0 = r[None, :] - r[:, None]
    bidx = jnp.stack([_t5_bucket(rel0), _t5_bucket(rel0 - blk)]).astype(I32) & (NUM_BUCKETS - 1)

    def head_spec(base):
        return pl.BlockSpec((None, seq, LANES), lambda h, b: (base + h, b, 0))

    def vec_spec():
        return pl.BlockSpec((1, LANES), lambda h, b: (0, 0))

    return pl.pallas_call(
        functools.partial(_diff_kernel, blk=blk, nblk=seq // blk),
        grid=(nh, batch),
        in_specs=[pl.BlockSpec(memory_space=pltpu.SMEM),
                  head_spec(first), head_spec(first + nh), head_spec(first + 2 * nh),
                  vec_spec(), vec_spec(),
                  pl.BlockSpec((4, DIFF_DK), lambda h, b: (0, 0)),
                  vec_spec(),
                  pl.BlockSpec((2, blk, blk), lambda h, b: (0, 0, 0))],
        out_specs=pl.BlockSpec((None, seq, LANES), lambda h, b: (h, b, 0)),
        out_shape=jax.ShapeDtypeStruct((nh, batch * seq, LANES), BF16),
        scratch_shapes=[pltpu.VMEM((2 * seq, LANES), BF16),
                        pltpu.VMEM((seq, LANES), BF16),
                        pltpu.VMEM((2, 2 * blk, blk), F32),
                        pltpu.VMEM((2 * blk, seq), F32),
                        pltpu.VMEM((2 * blk, seq), BF16)],
        compiler_params=_cparams(("arbitrary", "arbitrary")),
        name="diff_attn",
    )(rel_table, proj, proj, proj,
      jnp.tile(qg, 2).reshape(1, LANES), jnp.tile(kg, 2).reshape(1, LANES),
      lam_vecs, sg.reshape(1, LANES), bidx)


def _outproj_kernel(yr_ref, yd_ref, x_ref, w_hbm, g_ref, wr_ref, br_ref,
                    x1_ref, h2p_ref, lt_ref, wb_s, wstage_s, wr2_s, wsem,
                    *, rows_per_cast, sub_blocks=2):
    ne = lt_ref.shape[0]
    nchunk = wb_s.shape[0] // rows_per_cast

    def w_copy(c):
        return pltpu.make_async_copy(w_hbm.at[pl.ds(c * rows_per_cast, rows_per_cast)],
                                     wstage_s.at[c % 2], wsem.at[c % 2])

    @pl.when(pl.program_id(0) == 0)
    def _stage_weights():
        w_copy(0).start()
        for c in range(nchunk):
            if c + 1 < nchunk:
                w_copy(c + 1).start()
            w_copy(c).wait()
            wb_s[c * rows_per_cast:(c + 1) * rows_per_cast, :] = wstage_s[c % 2].astype(BF16)
        wr = wr_ref[...]
        wr_hi = wr.astype(BF16)
        wr2_s[:, 0:LANES] = wr_hi
        wr2_s[:, LANES:2 * LANES] = (wr - wr_hi.astype(F32)).astype(BF16)

    tm = x_ref.shape[0]
    sub = tm // sub_blocks
    for sb in range(sub_blocks):
        rs = slice(sb * sub, (sb + 1) * sub)
        y = jnp.concatenate([yr_ref[j, rs, :] for j in range(yr_ref.shape[0])]
                            + [yd_ref[j, rs, :] for j in range(yd_ref.shape[0])], axis=-1)
        x1 = x_ref[rs, :] + jnp.dot(y, wb_s[...], preferred_element_type=F32)
        x1_ref[rs, :] = x1
        ms = jnp.mean(x1 * x1, axis=-1, keepdims=True)
        h2 = x1 * lax.rsqrt(ms + EPS) * g_ref[...]
        h_hi = h2.astype(BF16)
        h_lo = (h2 - h_hi.astype(F32)).astype(BF16)
        parts = (jnp.dot(h_hi, wr2_s[...], preferred_element_type=F32)
                 + jnp.dot(h_lo, wr2_s[...], preferred_element_type=F32))
        logits = parts[:, 0:LANES] + parts[:, LANES:2 * LANES]
        lt_ref[:, rs] = logits.T[0:ne, :] + br_ref[...]
        half = h2.shape[1] // 2
        packed = pltpu.pack_elementwise([h2[:, :half], h2[:, half:]], packed_dtype=BF16)
        h2p_ref[rs] = packed.reshape((sub,) + h2p_ref.shape[1:])


def _outproj_call(y_ret, y_diff, x2, w_out, g, w_router, b_router, tm=512):
    t, d = x2.shape
    ne = w_router.shape[1]
    nhr, nhd = y_ret.shape[0], y_diff.shape[0]
    rows_per_cast = min(CAST_ROWS, d)
    wr_pad = jnp.pad(w_router, ((0, 0), (0, LANES - ne)))
    return pl.pallas_call(
        functools.partial(_outproj_kernel, rows_per_cast=rows_per_cast),
        grid=(t // tm,),
        in_specs=[pl.BlockSpec((nhr, tm, LANES), lambda i: (0, i, 0)),
                  pl.BlockSpec((nhd, tm, LANES), lambda i: (0, i, 0)),
                  pl.BlockSpec((tm, d), lambda i: (i, 0)),
                  pl.BlockSpec(memory_space=pl.ANY),
                  pl.BlockSpec((1, d), lambda i: (0, 0)),
                  pl.BlockSpec((d, LANES), lambda i: (0, 0)),
                  pl.BlockSpec((ne, 1), lambda i: (0, 0))],
        out_specs=[pl.BlockSpec((tm, d), lambda i: (i, 0)),
                   pl.BlockSpec((tm, d // 2 // LANES, LANES), lambda i: (i, 0, 0)),
                   pl.BlockSpec((ne, tm), lambda i: (0, i))],
        out_shape=[jax.ShapeDtypeStruct((t, d), F32),
                   jax.ShapeDtypeStruct((t, d // 2 // LANES, LANES), U32),
                   jax.ShapeDtypeStruct((ne, t), F32)],
        scratch_shapes=[pltpu.VMEM((d, d), BF16),
                        pltpu.VMEM((2, rows_per_cast, d), F32),
                        pltpu.VMEM((d, 2 * LANES), BF16),
                        pltpu.SemaphoreType.DMA((2,))],
        compiler_params=_cparams(("arbitrary",)),
        name="out_proj_router",
    )(y_ret, y_diff, x2, w_out, g.reshape(1, d), wr_pad, b_router.reshape(ne, 1))


def _route_kernel(lt_ref, pos_ref, gate_ref, cnt_ref, nt_ref, ot_ref, idx_s, rank_s, *, tb, row_tile):
    ne, t = lt_ref.shape
    e_iota = lax.broadcasted_iota(I32, (ne, tb), 0)
    upper = (lax.broadcasted_iota(I32, (tb, tb), 0)
             < lax.broadcasted_iota(I32, (tb, tb), 1)).astype(BF16)

    def pass_a(i, running):
        cols = pl.ds(pl.multiple_of(i * tb, tb), tb)
        l = lt_ref[:, cols]
        tops, hots = [], []
        for k in range(TOP_K):
            m = jnp.max(l, axis=0, keepdims=True)
            idx = jnp.min(jnp.where(l == m, e_iota, ne), axis=0, keepdims=True)
            hot = e_iota == idx
            l = jnp.where(hot, -jnp.inf, l)
            idx_s[k:k + 1, cols] = idx
            tops.append(m)
            hots.append(hot)
        exps = [jnp.exp(m - tops[0]) for m in tops]
        denom = exps[0] + exps[1] + exps[2] + exps[3]
        for k in range(TOP_K):
            gate_ref[k:k + 1, cols] = exps[k] / denom
        hot_all = jnp.zeros((ne, tb), F32)
        for hot in hots:
            hot_all = hot_all + hot.astype(F32)
        before = running + jnp.dot(hot_all.astype(BF16), upper, preferred_element_type=F32)
        for k in range(TOP_K):
            rank_s[k:k + 1, cols] = jnp.sum(jnp.where(hots[k], before, 0.0), axis=0, keepdims=True)
        return running + jnp.sum(hot_all, axis=1, keepdims=True)

    cnt = lax.fori_loop(0, t // tb, pass_a, jnp.zeros((ne, 1), F32))
    ntile = jnp.floor((cnt + (row_tile - 1.0)) * (1.0 / row_tile))
    lower = (lax.broadcasted_iota(I32, (ne, ne), 1)
             < lax.broadcasted_iota(I32, (ne, ne), 0)).astype(BF16)
    otile = jnp.dot(lower, jnp.broadcast_to(ntile, (ne, LANES)).astype(BF16),
                    preferred_element_type=F32)
    cnt_ref[...] = jnp.broadcast_to(cnt, (ne, LANES)).astype(I32)
    nt_ref[...] = jnp.broadcast_to(ntile, (ne, LANES)).astype(I32)
    ot_ref[...] = otile.astype(I32)
    off_rows = otile[:, 0:1] * float(row_tile)

    def pass_b(i, carry):
        cols = pl.ds(pl.multiple_of(i * tb, tb), tb)
        for k in range(TOP_K):
            hot = e_iota == idx_s[k:k + 1, cols]
            off = jnp.sum(jnp.where(hot, off_rows, 0.0), axis=0, keepdims=True)
            pos_ref[k:k + 1, cols] = (rank_s[k:k + 1, cols] + off).astype(I32)
        return carry

    lax.fori_loop(0, t // tb, pass_b, 0)


def _route_call(logits_t, tb=256, row_tile=ROW_TILE):
    ne, t = logits_t.shape
    return pl.pallas_call(
        functools.partial(_route_kernel, tb=tb, row_tile=row_tile),
        out_shape=[jax.ShapeDtypeStruct((TOP_K, t), I32),
                   jax.ShapeDtypeStruct((TOP_K, t), F32),
                   jax.ShapeDtypeStruct((ne, LANES), I32),
                   jax.ShapeDtypeStruct((ne, LANES), I32),
                   jax.ShapeDtypeStruct((ne, LANES), I32)],
        scratch_shapes=[pltpu.VMEM((TOP_K, t), I32), pltpu.VMEM((TOP_K, t), F32)],
        compiler_params=_cparams(None),
        name="route",
    )(logits_t)


_PAD_PIECES = tuple(ROW_TILE >> (k + 1) for k in range(ROW_TILE.bit_length() - 1))


def _dispatch_kernel(pos_ref, pad_start_ref, pad_n_ref, src_ref, dst_ref, stage_s, zero_s,
                     lsem, ssem, zsem, *, tb, ne):
    i = pl.program_id(0)
    nsteps = pl.num_programs(0)
    slot = i % 2

    def load(step, s):
        return pltpu.make_async_copy(src_ref.at[pl.ds(step * tb, tb)], stage_s.at[s], lsem.at[s])

    def wait_rows(s):
        for k in range(TOP_K):
            pltpu.make_async_copy(stage_s.at[s], dst_ref.at[pl.ds(0, tb)], ssem.at[s]).wait()

    def pad_copies(e):
        n = pad_n_ref[e]
        start = pad_start_ref[e]
        out = []
        for piece in _PAD_PIECES:
            at = start + (n & ~(2 * piece - 1))
            out.append(((n & piece) != 0, pltpu.make_async_copy(
                zero_s.at[pl.ds(0, piece)], dst_ref.at[pl.ds(at, piece)], zsem)))
        return out

    @pl.when(i == 0)
    def _zero_pads():
        zero_s[...] = jnp.zeros(zero_s.shape, zero_s.dtype)

        def issue(e, c):
            for cond, cp in pad_copies(e):
                @pl.when(cond)
                def _():
                    cp.start()
            return c

        def drain(e, c):
            for cond, cp in pad_copies(e):
                @pl.when(cond)
                def _():
                    cp.wait()
            return c

        lax.fori_loop(0, ne, issue, 0)
        lax.fori_loop(0, ne, drain, 0)
        load(0, 0).start()

    @pl.when(i > 0)
    def _():
        wait_rows(1 - slot)

    @pl.when(i + 1 < nsteps)
    def _():
        load(i + 1, 1 - slot).start()

    load(i, slot).wait()

    def issue_rows(tl, c):
        for k in range(TOP_K):
            pltpu.make_async_copy(stage_s.at[slot, tl], dst_ref.at[pos_ref[k, tl]],
                                  ssem.at[slot]).start(priority=k % 2)
        return c

    lax.fori_loop(0, tb, issue_rows, 0, unroll=4)

    @pl.when(i == nsteps - 1)
    def _():
        wait_rows(slot)


def _dispatch_call(pos, pad_start, pad_n, h2p, n_slots, tb=1024):
    t = h2p.shape[0]
    row = h2p.shape[1:]
    ne = pad_n.shape[0]
    return pl.pallas_call(
        functools.partial(_dispatch_kernel, tb=tb, ne=ne),
        grid=(t // tb,),
        in_specs=[pl.BlockSpec((TOP_K, tb), lambda i: (0, i), memory_space=pltpu.SMEM),
                  pl.BlockSpec(memory_space=pltpu.SMEM),
                  pl.BlockSpec(memory_space=pltpu.SMEM),
                  pl.BlockSpec(memory_space=pl.ANY)],
        out_specs=pl.BlockSpec(memory_space=pl.ANY),
        out_shape=jax.ShapeDtypeStruct((n_slots, *row), U32),
        scratch_shapes=[pltpu.VMEM((2, tb, *row), U32),
                        pltpu.VMEM((_PAD_PIECES[0], *row), U32),
                        pltpu.SemaphoreType.DMA((2,)),
                        pltpu.SemaphoreType.DMA((2,)),
                        pltpu.SemaphoreType.DMA(())],
        compiler_params=pltpu.CompilerParams(dimension_semantics=("arbitrary",),
                                             vmem_limit_bytes=VMEM_LIMIT, has_side_effects=True),
        name="dispatch",
    )(pos, pad_start, pad_n, h2p)


def _moe_kernel(we_ref, wt_ref, wn_ref, xs_ref, win_ref, bin_ref, wout_ref, bo_ref,
                ys_ref, xu_s, xb_s, acc_s, wg_st, wu_st, wo_st, yst_s, xsem, wsem, ysem,
                *, tmx, fc, nj):
    w = pl.program_id(0)
    nw = pl.num_programs(0)
    nt = wn_ref[w]
    t0 = wt_ref[w]
    half = xu_s.shape[-2] * xu_s.shape[-1]
    de = nj * fc
    nxt = jnp.minimum(w + 1, nw - 1)
    has_next = jnp.logical_and(w + 1 < nw, wn_ref[nxt] > 0)

    def local_rows(r, n=1):
        return pl.ds(pl.multiple_of(r * tmx, tmx), n * tmx)

    def x_copy(item_t0, r):
        src = pl.ds(pl.multiple_of((item_t0 + r) * tmx, tmx), tmx)
        return pltpu.make_async_copy(xs_ref.at[src], xu_s.at[local_rows(r)], xsem)

    def y_copy(r, slot):
        dst = pl.ds(pl.multiple_of((t0 + r) * tmx, tmx), tmx)
        return pltpu.make_async_copy(yst_s.at[slot], ys_ref.at[dst], ysem.at[slot])

    def w_copies(expert, j, slot):
        c0 = pl.multiple_of(j * fc, fc)
        return (pltpu.make_async_copy(win_ref.at[expert, :, pl.ds(c0, fc)], wg_st.at[slot], wsem.at[slot]),
                pltpu.make_async_copy(win_ref.at[expert, :, pl.ds(de + c0, fc)], wu_st.at[slot],
                                      wsem.at[slot]),
                pltpu.make_async_copy(wout_ref.at[expert, pl.ds(c0, fc), :], wo_st.at[slot],
                                      wsem.at[slot]))

    def start_weights(expert, j, slot):
        for cp in w_copies(expert, j, slot):
            cp.start()

    def start_rows(item_t0, item_nt):
        def go(r, c):
            x_copy(item_t0, r).start()
            return c
        lax.fori_loop(0, item_nt, go, 0)

    @pl.when(w == 0)
    def _prologue():
        start_weights(we_ref[0], 0, 0)
        start_rows(t0, nt)

    @pl.when(nt > 0)
    def _work():
        def chunk(j, carry):
            slot = j % 2

            @pl.when(j + 1 < nj)
            def _():
                start_weights(we_ref[w], j + 1, 1 - slot)

            @pl.when(jnp.logical_and(j + 1 == nj, has_next))
            def _():
                start_weights(we_ref[nxt], 0, 1 - slot)

            for cp in w_copies(we_ref[w], j, slot):
                cp.wait()
            stage = (wg_st.at[slot], wu_st.at[slot], wo_st.at[slot])
            biases = (bin_ref[j], bin_ref[nj + j])

            @pl.when(j == 0)
            def _rows_ready():
                def finish(r, c):
                    x_copy(t0, r).wait()
                    return c
                lax.fori_loop(0, nt, finish, 0)

            @pl.when(jnp.logical_and(j == 1, has_next))
            def _():
                start_rows(wt_ref[nxt], wn_ref[nxt])

            @pl.when(j == 0)
            def _():
                _moe_passes(xu_s, xb_s, acc_s, stage, biases, nt, local_rows, half, True)

            @pl.when(j > 0)
            def _():
                _moe_passes(xu_s, xb_s, acc_s, stage, biases, nt, local_rows, half, False)

            return carry

        lax.fori_loop(0, nj, chunk, 0)

        def y_wait(slot):
            y_copy(0, slot).wait()

        prev_nt = jnp.where(w > 0, wn_ref[jnp.maximum(w - 1, 0)], 0)

        @pl.when(prev_nt >= 1)
        def _():
            y_wait((prev_nt - 1) % 2)

        @pl.when(prev_nt >= 2)
        def _():
            y_wait(prev_nt % 2)

        def emit(r, c):
            slot = r % 2

            @pl.when(r >= 2)
            def _():
                y_wait(slot)

            y = acc_s[local_rows(r), :] + bo_ref[...]
            yst_s[slot] = pltpu.pack_elementwise([y[:, :half], y[:, half:]],
                                                 packed_dtype=BF16).reshape(yst_s.shape[1:])
            y_copy(r, slot).start()
            return c

        lax.fori_loop(0, nt, emit, 0)

        @pl.when(jnp.logical_not(has_next))
        def _drain():
            @pl.when(nt >= 2)
            def _():
                y_wait(nt % 2)

            y_wait((nt - 1) % 2)


def _moe_passes(xu_ref, xb_ref, acc_s, stage, biases, nt, local_rows, half, first):
    wg_ref, wu_ref, wo_ref = stage
    bg, bu = biases

    def proj(x_lo, x_hi, w_ref, b):
        return (jnp.dot(x_lo, w_ref[0:half, :].astype(BF16), preferred_element_type=F32)
                + jnp.dot(x_hi, w_ref[half:2 * half, :].astype(BF16), preferred_element_type=F32) + b)

    def rows_step(rows):
        if first:
            xw = xu_ref[rows]
            xw = xw.reshape(xw.shape[0], half)
            x_lo = pltpu.unpack_elementwise(xw, index=0, packed_dtype=BF16,
                                            unpacked_dtype=F32).astype(BF16)
            x_hi = pltpu.unpack_elementwise(xw, index=1, packed_dtype=BF16,
                                            unpacked_dtype=F32).astype(BF16)
            xb_ref[rows, 0:half] = x_lo
            xb_ref[rows, half:2 * half] = x_hi
        else:
            x_lo = xb_ref[rows, 0:half]
            x_hi = xb_ref[rows, half:2 * half]
        gg = jnp.minimum(proj(x_lo, x_hi, wg_ref, bg), SWIGLU_LIMIT)
        uu = jnp.clip(proj(x_lo, x_hi, wu_ref, bu), -SWIGLU_LIMIT, SWIGLU_LIMIT)
        act = (uu + 1.0) * (gg * jax.nn.sigmoid(SWIGLU_ALPHA * gg))
        y = jnp.dot(act.astype(BF16), wo_ref[...].astype(BF16), preferred_element_type=F32)
        if first:
            acc_s[rows, :] = y
        else:
            acc_s[rows, :] += y

    def big(pi, c):
        rows_step(local_rows(PASS_TILES * pi, PASS_TILES))
        return c

    lax.fori_loop(0, nt // PASS_TILES, big, 0)
    piece = PASS_TILES // 2
    while piece >= 1:
        @pl.when((nt & piece) != 0)
        def _(piece=piece):
            rows_step(local_rows(nt & ~(2 * piece - 1), piece))
        piece //= 2


def _moe_call(work_e, work_t0, work_nt, xs, w_in, b_in, w_out, b_out,
              tmx=ROW_TILE, group=GROUP_TILES, fc=F_CHUNK):
    ne, d, de2 = w_in.shape
    de = de2 // 2
    nj = de // fc
    assert nj % 2 == 0, "weight stage slots alternate per chunk and restart at 0 per work item"
    n_work = work_e.shape[0]
    n_slots = xs.shape[0]
    row = xs.shape[1:]

    grid_spec = pltpu.PrefetchScalarGridSpec(
        num_scalar_prefetch=3,
        grid=(n_work,),
        in_specs=[
            pl.BlockSpec(memory_space=pl.ANY),
            pl.BlockSpec(memory_space=pl.ANY),
            pl.BlockSpec((None, 2 * nj, 1, fc), lambda w, we, wt, wn: (we[w], 0, 0, 0)),
            pl.BlockSpec(memory_space=pl.ANY),
            pl.BlockSpec((None, 1, d), lambda w, we, wt, wn: (we[w], 0, 0)),
        ],
        out_specs=pl.BlockSpec(memory_space=pl.ANY),
        scratch_shapes=[pltpu.VMEM((group * tmx, *row), U32),
                        pltpu.VMEM((group * tmx, d), BF16),
                        pltpu.VMEM((group * tmx, d), F32),
                        pltpu.VMEM((2, d, fc), F32),
                        pltpu.VMEM((2, d, fc), F32),
                        pltpu.VMEM((2, fc, d), F32),
                        pltpu.VMEM((2, tmx, *row), U32),
                        pltpu.SemaphoreType.DMA(()),
                        pltpu.SemaphoreType.DMA((2,)),
                        pltpu.SemaphoreType.DMA((2,))],
    )
    return pl.pallas_call(
        functools.partial(_moe_kernel, tmx=tmx, fc=fc, nj=nj),
        grid_spec=grid_spec,
        out_shape=jax.ShapeDtypeStruct((n_slots, *row), U32),
        compiler_params=pltpu.CompilerParams(dimension_semantics=("arbitrary",),
                                             vmem_limit_bytes=VMEM_LIMIT, has_side_effects=True),
        name="moe_experts",
    )(work_e, work_t0, work_nt, xs, w_in, b_in.reshape(ne, 2 * nj, 1, fc), w_out,
      b_out.reshape(ne, 1, d))


def _combine_kernel(pos_ref, pos_next_ref, x1_ref, gate_ref, ys_ref, o_ref, ybuf, sem, *, tb):
    half = ybuf.shape[-2] * ybuf.shape[-1]
    i = pl.program_id(0)
    slot = i % 2

    def gather(p_ref, s):
        def issue(tl, c):
            for k in range(TOP_K):
                pltpu.make_async_copy(ys_ref.at[p_ref[k, tl]], ybuf.at[s, k, tl],
                                      sem.at[s]).start(priority=k % 2)
            return c
        lax.fori_loop(0, tb, issue, 0, unroll=4)

    @pl.when(i == 0)
    def _():
        gather(pos_ref, 0)

    @pl.when(i + 1 < pl.num_programs(0))
    def _():
        gather(pos_next_ref, 1 - slot)

    for k in range(TOP_K):
        pltpu.make_async_copy(ys_ref.at[pl.ds(0, tb)], ybuf.at[slot, k], sem.at[slot]).wait()

    lo = x1_ref[:, 0:half]
    hi = x1_ref[:, half:2 * half]
    for k in range(TOP_K):
        g = gate_ref[:, k:k + 1]
        yw = ybuf[slot, k].reshape(tb, half)
        lo = lo + g * pltpu.unpack_elementwise(yw, index=0, packed_dtype=BF16, unpacked_dtype=F32)
        hi = hi + g * pltpu.unpack_elementwise(yw, index=1, packed_dtype=BF16, unpacked_dtype=F32)
    o_ref[:, 0:half] = lo
    o_ref[:, half:2 * half] = hi


def _combine_call(pos, x1, gate_tk, ys, tb=512):
    t, d = x1.shape
    row = ys.shape[1:]
    last = t // tb - 1
    return pl.pallas_call(
        functools.partial(_combine_kernel, tb=tb),
        grid=(t // tb,),
        in_specs=[pl.BlockSpec((TOP_K, tb), lambda i: (0, i), memory_space=pltpu.SMEM),
                  pl.BlockSpec((TOP_K, tb), lambda i: (0, jnp.minimum(i + 1, last)),
                               memory_space=pltpu.SMEM),
                  pl.BlockSpec((tb, d), lambda i: (i, 0)),
                  pl.BlockSpec((tb, TOP_K), lambda i: (i, 0)),
                  pl.BlockSpec(memory_space=pl.ANY)],
        out_specs=pl.BlockSpec((tb, d), lambda i: (i, 0)),
        out_shape=jax.ShapeDtypeStruct((t, d), F32),
        scratch_shapes=[pltpu.VMEM((2, TOP_K, tb, *row), U32),
                        pltpu.SemaphoreType.DMA((2,))],
        compiler_params=_cparams(("arbitrary",)),
        name="combine",
    )(pos, pos, x1, gate_tk, ys)


def _work_list(ntile, otile, n_work, group):
    ne = ntile.shape[0]
    items = (ntile + group - 1) // group
    ends = jnp.cumsum(items)
    total = ends[-1]
    w = jnp.arange(n_work, dtype=I32)
    valid = w < total
    e_w = jnp.clip(jnp.sum((ends[None, :] <= w[:, None]).astype(I32), axis=1), 0, ne - 1)
    local = w - (ends - items)[e_w]
    t0_w = otile[e_w] + local * group
    nt_w = jnp.where(valid, jnp.clip(ntile[e_w] - local * group, 0, group), 0)
    e_last = e_w[jnp.maximum(total - 1, 0)]
    return (jnp.where(valid, e_w, e_last).astype(I32), jnp.where(valid, t0_w, 0).astype(I32),
            nt_w.astype(I32))


def _rotary_tables(seq, dk):
    inv_freq = ROPE_BASE ** (-jnp.arange(0, dk, 2, dtype=F32) / dk)
    ang = jnp.arange(seq, dtype=I32).astype(F32)[:, None] * inv_freq[None, :]
    cos = jnp.repeat(jnp.cos(ang), 2, axis=1)
    sin = jnp.stack([-jnp.sin(ang), jnp.sin(ang)], axis=-1).reshape(seq, dk)
    return cos, sin


def kernel(x, norm_mix_g, w_mix_in, ret_gn_g, q_norm_g, k_norm_g, lambda_q1, lambda_k1, lambda_q2, lambda_k2, diff_subln_g, rel_bias_table, w_mix_out, norm_ffn_g, w_router, b_router, w_exp_in, b_exp_in, w_exp_out, b_exp_out):
    batch, seq, d = x.shape
    t = batch * seq
    depth = norm_mix_g.shape[0]
    ne = w_router.shape[-1]
    n_tiles_max = (t * TOP_K) // ROW_TILE + ne
    n_work = ne + (n_tiles_max - ne) // GROUP_TILES
    cos, sin = _rotary_tables(seq, HEAD_W)
    log_gamma = jnp.log1p(-(2.0 ** (-5.0 - jnp.arange(RET_HEADS, dtype=F32))))

    x2 = x.reshape(t, d)
    for l in range(depth):
        proj = _inproj_call(x2, norm_mix_g[l], w_mix_in[l])
        y_ret = _ret_call(proj, cos, sin, log_gamma, ret_gn_g[l], batch, seq)
        lam_vecs = jnp.stack([lambda_q1[l], lambda_k1[l], lambda_q2[l], lambda_k2[l]])
        y_diff = _diff_call(proj, rel_bias_table, q_norm_g[l], k_norm_g[l], lam_vecs,
                            diff_subln_g[l], batch, seq)
        x1, h2p, logits_t = _outproj_call(y_ret, y_diff, x2, w_mix_out[l], norm_ffn_g[l],
                                          w_router[l], b_router[l])
        pos, gate, cnt, ntile, otile = _route_call(logits_t)
        cnt, ntile, otile = cnt[:, 0], ntile[:, 0], otile[:, 0]
        work_e, work_t0, work_nt = _work_list(ntile, otile, n_work, GROUP_TILES)
        xs = _dispatch_call(pos, otile * ROW_TILE + cnt, ntile * ROW_TILE - cnt, h2p,
                            n_tiles_max * ROW_TILE)
        ys = _moe_call(work_e, work_t0, work_nt, xs, w_exp_in[l], b_exp_in[l], w_exp_out[l],
                       b_exp_out[l])
        x2 = _combine_call(pos, x1, gate.T, ys)
    return x2.reshape(batch, seq, d)
```
